```python
import jax, jax.numpy as jnp
from jax import lax
import numpy as np

D_MODEL = 2048
BATCH = 8
SEQ = 8192
DEPTH = 1

CHUNK = 64
GDN_HEADS = 16
GDN_DK = 128
GDN_DV = 128
GDN_CONV = 4
QK_W = GDN_HEADS * GDN_DK
V_W = GDN_HEADS * GDN_DV
QKV_W = 2 * QK_W + V_W
POOL_WINDOWS = (2, 4, 8, 16)
POOL_WIDTH = D_MODEL // 2
POOL_GROUP = POOL_WIDTH // 4
XA_HEADS = 4
XA_HEAD_DIM = D_MODEL // XA_HEADS
MEM_LEN = 256
D_FF = 5504
FFN_CONV = 3
EPS = 1e-6
IN_SIZES = (QKV_W, V_W, GDN_HEADS, GDN_HEADS, POOL_WIDTH, D_MODEL, D_MODEL)
D_IN = QKV_W + V_W + 2 * GDN_HEADS + POOL_WIDTH + 2 * D_MODEL

kernel_name = "hybrid_gdn_pool_xattn_convffn_block"


def rms_norm(x, w):
    xf = x.astype(jnp.float32)
    y = xf * lax.rsqrt(jnp.mean(xf * xf, axis=-1, keepdims=True) + EPS)
    return (y * w.astype(jnp.float32)).astype(x.dtype)


def l2_norm(x):
    return x * lax.rsqrt(jnp.sum(x * x, axis=-1, keepdims=True) + EPS)


def causal_dwconv(x, w):
    K = w.shape[0]
    S = x.shape[1]
    xp = jnp.pad(x, ((0, 0), (K - 1, 0), (0, 0)))
    y = xp[:, K - 1:K - 1 + S] * w[K - 1]
    for j in range(K - 1):
        y = y + xp[:, j:j + S] * w[j]
    return y


def gated_delta_rule(q, k, v, g, beta):
    B, S, H, DK = q.shape
    DV = v.shape[-1]
    N = S // CHUNK

    def blocks(t):
        return t.reshape(B, N, CHUNK, H, -1).transpose(0, 3, 1, 2, 4)

    q, k, v = blocks(q), blocks(k), blocks(v)
    g = blocks(g[..., None])[..., 0]
    beta = blocks(beta[..., None])[..., 0]
    G = jnp.cumsum(g, axis=-1)
    pos = jnp.arange(CHUNK)
    incl = pos[:, None] >= pos[None, :]
    strict = pos[:, None] > pos[None, :]
    gap = G[..., :, None] - G[..., None, :]
    decay = jnp.where(incl, jnp.exp(jnp.where(incl, gap, 0.0)), 0.0)
    kk = jnp.einsum('bhnid,bhnjd->bhnij', k, k)
    lower = jnp.where(strict, beta[..., :, None] * decay * kk, 0.0)
    system = lower + jnp.eye(CHUNK, dtype=lower.dtype)
    rhs = jnp.concatenate([beta[..., None] * v, (beta * jnp.exp(G))[..., None] * k], axis=-1)
    sol = lax.linalg.triangular_solve(system, rhs, left_side=True, lower=True,
                                      unit_diagonal=True)
    u_v, w_k = sol[..., :DV], sol[..., DV:]
    attn = decay * jnp.einsum('bhnid,bhnjd->bhnij', q, k)
    q_dec = q * jnp.exp(G)[..., None]
    k_dec = k * jnp.exp(G[..., -1:] - G)[..., None]
    chunk_decay = jnp.exp(G[..., -1])

    def step(state, xs):
        u_c, w_c, a_c, qd_c, kd_c, cd_c = xs
        u = u_c - jnp.einsum('bhck,bhkv->bhcv', w_c, state)
        o = jnp.einsum('bhck,bhkv->bhcv', qd_c, state) + jnp.einsum('bhij,bhjv->bhiv', a_c, u)
        state = cd_c[..., None, None] * state + jnp.einsum('bhck,bhcv->bhkv', kd_c, u)
        return state, o

    xs = tuple(jnp.moveaxis(t, 2, 0) for t in (u_v, w_k, attn, q_dec, k_dec, chunk_decay))
    state0 = jnp.zeros((B, H, DK, DV), jnp.float32)
    _, o = lax.scan(step, state0, xs)
    return o.transpose(1, 0, 3, 2, 4).reshape(B, S, H, DV)


def multi_scale_pool(p, pool_w, pool_scale):
    B, S, _ = p.shape
    pf = p.astype(jnp.float32)
    csum = jnp.pad(jnp.cumsum(pf, axis=1), ((0, 0), (1, 0), (0, 0)))
    xs = jnp.split(pf, len(POOL_WINDOWS), axis=-1)
    cs = jnp.split(csum, len(POOL_WINDOWS), axis=-1)
    t = jnp.arange(S)
    outs = []
    for gi, win in enumerate(POOL_WINDOWS):
        lo = jnp.maximum(t + 1 - win, 0)
        cnt = jnp.minimum(t + 1, win).astype(jnp.float32)
        mean = (cs[gi][:, 1:] - cs[gi][:, lo]) / cnt[None, :, None]
        outs.append(mean - xs[gi])
    y = jnp.stack(outs, axis=2)
    y = jnp.einsum('bsgc,gcd->bsgd', y, pool_w.astype(jnp.float32)).reshape(B, S, POOL_WIDTH)
    return (y * pool_scale.astype(jnp.float32)).astype(p.dtype)


def hybrid_mixer(h, w_in, conv_qkv, a_log, dt_bias, gdn_norm, pool_w, pool_scale,
                 w_branch_a, w_branch_b, w_mix_out):
    B, S, _ = h.shape
    splits = [int(i) for i in np.cumsum(IN_SIZES)[:-1]]
    qkv, z, b_raw, a_raw, p, gate_a, gate_b = jnp.split(h @ w_in, splits, axis=-1)
    qkv = jax.nn.silu(causal_dwconv(qkv, conv_qkv))
    q, k, v = jnp.split(qkv, [QK_W, 2 * QK_W], axis=-1)
    q = l2_norm(q.reshape(B, S, GDN_HEADS, GDN_DK).astype(jnp.float32)) * (GDN_DK ** -0.5)
    k = l2_norm(k.reshape(B, S, GDN_HEADS, GDN_DK).astype(jnp.float32))
    v = v.reshape(B, S, GDN_HEADS, GDN_DV).astype(jnp.float32)
    beta = jax.nn.sigmoid(b_raw.astype(jnp.float32))
    g = -jnp.exp(a_log.astype(jnp.float32)) * jax.nn.softplus(
        a_raw.astype(jnp.float32) + dt_bias.astype(jnp.float32))
    o = gated_delta_rule(q, k, v, g, beta)
    o = rms_norm(o, gdn_norm) * jax.nn.silu(z.reshape(B, S, GDN_HEADS, GDN_DV).astype(jnp.float32))
    y_a = o.reshape(B, S, V_W).astype(h.dtype) @ w_branch_a
    y_b = multi_scale_pool(p, pool_w, pool_scale) @ w_branch_b
    merged = jax.nn.sigmoid(gate_a) * y_a + jax.nn.sigmoid(gate_b) * y_b
    return merged @ w_mix_out


def memory_cross_attention(h, m, w_xq, w_xkv, w_xo):
    B, S, _ = h.shape
    M = m.shape[1]
    q = (h @ w_xq).reshape(B, S, XA_HEADS, XA_HEAD_DIM)
    k, v = jnp.split(m @ w_xkv, 2, axis=-1)
    k = k.reshape(B, M, XA_HEADS, XA_HEAD_DIM)
    v = v.reshape(B, M, XA_HEADS, XA_HEAD_DIM)
    s = jnp.einsum('bshd,bmhd->bhsm', q, k).astype(jnp.float32) * (XA_HEAD_DIM ** -0.5)
    pr = jax.nn.softmax(s, axis=-1).astype(v.dtype)
    o = jnp.einsum('bhsm,bmhd->bshd', pr, v).reshape(B, S, D_MODEL)
    return o @ w_xo


def conv_glu_ffn(h, w_up, ffn_conv_w, ffn_conv_b, w_down):
    u = causal_dwconv(h @ w_up, ffn_conv_w) + ffn_conv_b
    a, b = jnp.split(u, 2, axis=-1)
    return (jax.nn.silu(a) * b) @ w_down


def _fwd_setup_inputs(seed: int = 0) -> dict:
    key = jax.random.key(seed)
    ks = iter(jax.random.split(key, 32))

    def dense(shape, fan_in):
        return jax.random.normal(next(ks), shape, jnp.float32) * (fan_in ** -0.5)

    def gain(shape):
        return 1.0 + 0.05 * jax.random.normal(next(ks), shape, jnp.float32)

    L = DEPTH
    x = jax.random.normal(next(ks), (BATCH, SEQ, D_MODEL), jnp.float32)
    mem = jax.random.normal(next(ks), (BATCH, MEM_LEN, D_MODEL), jnp.float32)
    a_log = jnp.log(jax.random.uniform(next(ks), (L, GDN_HEADS), jnp.float32, 1.0, 16.0))
    dt = jnp.exp(jax.random.uniform(next(ks), (L, GDN_HEADS), jnp.float32,
                                    np.log(1e-3), np.log(1e-1)))
    dt_bias = dt + jnp.log(-jnp.expm1(-dt))
    return {
        "x": x,
        "mem": mem,
        "mix_pre_norm": gain((L, D_MODEL)),
        "w_in": dense((L, D_MODEL, D_IN), D_MODEL),
        "conv_qkv": dense((L, GDN_CONV, QKV_W), GDN_CONV),
        "a_log": a_log,
        "dt_bias": dt_bias,
        "gdn_norm": gain((L, GDN_DV)),
        "pool_w": dense((L, len(POOL_WINDOWS), POOL_GROUP, POOL_GROUP), POOL_GROUP),
        "pool_scale": gain((L, POOL_WIDTH)),
        "w_branch_a": dense((L, V_W, D_MODEL), V_W),
        "w_branch_b": dense((L, POOL_WIDTH, D_MODEL), POOL_WIDTH),
        "w_mix_out": dense((L, D_MODEL, D_MODEL), D_MODEL),
        "mix_post_norm": gain((L, D_MODEL)),
        "xa_pre_norm": gain((L, D_MODEL)),
        "mem_norm": gain((L, D_MODEL)),
        "w_xq": dense((L, D_MODEL, D_MODEL), D_MODEL),
        "w_xkv": dense((L, D_MODEL, 2 * D_MODEL), D_MODEL),
        "w_xo": dense((L, D_MODEL, D_MODEL), D_MODEL),
        "xa_post_norm": gain((L, D_MODEL)),
        "ffn_pre_norm": gain((L, D_MODEL)),
        "w_up": dense((L, D_MODEL, 2 * D_FF), D_MODEL),
        "ffn_conv_w": dense((L, FFN_CONV, 2 * D_FF), FFN_CONV),
        "ffn_conv_b": 0.01 * jax.random.normal(next(ks), (L, 2 * D_FF), jnp.float32),
        "w_down": dense((L, D_FF, D_MODEL), D_FF),
        "ffn_post_norm": gain((L, D_MODEL)),
    }


def _fwd_reference(x, mem, mix_pre_norm, w_in, conv_qkv, a_log, dt_bias, gdn_norm, pool_w,
              pool_scale, w_branch_a, w_branch_b, w_mix_out, mix_post_norm, xa_pre_norm,
              mem_norm, w_xq, w_xkv, w_xo, xa_post_norm, ffn_pre_norm, w_up, ffn_conv_w,
              ffn_conv_b, w_down, ffn_post_norm):
    for l in range(DEPTH):
        h = rms_norm(x, mix_pre_norm[l])
        y = hybrid_mixer(h, w_in[l], conv_qkv[l], a_log[l], dt_bias[l], gdn_norm[l], pool_w[l],
                         pool_scale[l], w_branch_a[l], w_branch_b[l], w_mix_out[l])
        x = x + rms_norm(y, mix_post_norm[l])
        h = rms_norm(x, xa_pre_norm[l])
        m = rms_norm(mem, mem_norm[l])
        y = memory_cross_attention(h, m, w_xq[l], w_xkv[l], w_xo[l])
        x = x + rms_norm(y, xa_post_norm[l])
        h = rms_norm(x, ffn_pre_norm[l])
        y = conv_glu_ffn(h, w_up[l], ffn_conv_w[l], ffn_conv_b[l], w_down[l])
        x = x + rms_norm(y, ffn_post_norm[l])
    return x


import jax as _jax
import jax.numpy as _jnp

TWIN_FORMAT = 'train_step'
FWD_PARAMS = ['x', 'mem', 'mix_pre_norm', 'w_in', 'conv_qkv', 'a_log', 'dt_bias', 'gdn_norm', 'pool_w', 'pool_scale', 'w_branch_a', 'w_branch_b', 'w_mix_out', 'mix_post_norm', 'xa_pre_norm', 'mem_norm', 'w_xq', 'w_xkv', 'w_xo', 'xa_post_norm', 'ffn_pre_norm', 'w_up', 'ffn_conv_w', 'ffn_conv_b', 'w_down', 'ffn_post_norm']
TWIN_WEIGHTS = ['mix_pre_norm', 'w_in', 'conv_qkv', 'a_log', 'dt_bias', 'gdn_norm', 'pool_w', 'pool_scale', 'w_branch_a', 'w_branch_b', 'w_mix_out', 'mix_post_norm', 'xa_pre_norm', 'mem_norm', 'w_xq', 'w_xkv', 'w_xo', 'xa_post_norm', 'ffn_pre_norm', 'w_up', 'ffn_conv_w', 'ffn_conv_b', 'w_down', 'ffn_post_norm']
TWIN_DIFF_INPUT = 'x'
TWIN_INPUTS = ['x', 'mem', 'mix_pre_norm', 'w_in', 'conv_qkv', 'a_log', 'dt_bias', 'gdn_norm', 'pool_w', 'pool_scale', 'w_branch_a', 'w_branch_b', 'w_mix_out', 'mix_post_norm', 'xa_pre_norm', 'mem_norm', 'w_xq', 'w_xkv', 'w_xo', 'xa_post_norm', 'ffn_pre_norm', 'w_up', 'ffn_conv_w', 'ffn_conv_b', 'w_down', 'ffn_post_norm', 'loss_target', 'm_mix_pre_norm', 'm_w_in', 'm_conv_qkv', 'm_a_log', 'm_dt_bias', 'm_gdn_norm', 'm_pool_w', 'm_pool_scale', 'm_w_branch_a', 'm_w_branch_b', 'm_w_mix_out', 'm_mix_post_norm', 'm_xa_pre_norm', 'm_mem_norm', 'm_w_xq', 'm_w_xkv', 'm_w_xo', 'm_xa_post_norm', 'm_ffn_pre_norm', 'm_w_up', 'm_ffn_conv_w', 'm_ffn_conv_b', 'm_w_down', 'm_ffn_post_norm', 'v_mix_pre_norm', 'v_w_in', 'v_conv_qkv', 'v_a_log', 'v_dt_bias', 'v_gdn_norm', 'v_pool_w', 'v_pool_scale', 'v_w_branch_a', 'v_w_branch_b', 'v_w_mix_out', 'v_mix_post_norm', 'v_xa_pre_norm', 'v_mem_norm', 'v_w_xq', 'v_w_xkv', 'v_w_xo', 'v_xa_post_norm', 'v_ffn_pre_norm', 'v_w_up', 'v_ffn_conv_w', 'v_ffn_conv_b', 'v_w_down', 'v_ffn_post_norm']
TWIN_OUTPUTS = ['loss', 'grad_x', 'grad_mix_pre_norm', 'grad_w_in', 'grad_conv_qkv', 'grad_a_log', 'grad_dt_bias', 'grad_gdn_norm', 'grad_pool_w', 'grad_pool_scale', 'grad_w_branch_a', 'grad_w_branch_b', 'grad_w_mix_out', 'grad_mix_post_norm', 'grad_xa_pre_norm', 'grad_mem_norm', 'grad_w_xq', 'grad_w_xkv', 'grad_w_xo', 'grad_xa_post_norm', 'grad_ffn_pre_norm', 'grad_w_up', 'grad_ffn_conv_w', 'grad_ffn_conv_b', 'grad_w_down', 'grad_ffn_post_norm', 'delta_mix_pre_norm', 'delta_w_in', 'delta_conv_qkv', 'delta_a_log', 'delta_dt_bias', 'delta_gdn_norm', 'delta_pool_w', 'delta_pool_scale', 'delta_w_branch_a', 'delta_w_branch_b', 'delta_w_mix_out', 'delta_mix_post_norm', 'delta_xa_pre_norm', 'delta_mem_norm', 'delta_w_xq', 'delta_w_xkv', 'delta_w_xo', 'delta_xa_post_norm', 'delta_ffn_pre_norm', 'delta_w_up', 'delta_ffn_conv_w', 'delta_ffn_conv_b', 'delta_w_down', 'delta_ffn_post_norm', 'new_m_mix_pre_norm', 'new_m_w_in', 'new_m_conv_qkv', 'new_m_a_log', 'new_m_dt_bias', 'new_m_gdn_norm', 'new_m_pool_w', 'new_m_pool_scale', 'new_m_w_branch_a', 'new_m_w_branch_b', 'new_m_w_mix_out', 'new_m_mix_post_norm', 'new_m_xa_pre_norm', 'new_m_mem_norm', 'new_m_w_xq', 'new_m_w_xkv', 'new_m_w_xo', 'new_m_xa_post_norm', 'new_m_ffn_pre_norm', 'new_m_w_up', 'new_m_ffn_conv_w', 'new_m_ffn_conv_b', 'new_m_w_down', 'new_m_ffn_post_norm', 'new_v_mix_pre_norm', 'new_v_w_in', 'new_v_conv_qkv', 'new_v_a_log', 'new_v_dt_bias', 'new_v_gdn_norm', 'new_v_pool_w', 'new_v_pool_scale', 'new_v_w_branch_a', 'new_v_w_branch_b', 'new_v_w_mix_out', 'new_v_mix_post_norm', 'new_v_xa_pre_norm', 'new_v_mem_norm', 'new_v_w_xq', 'new_v_w_xkv', 'new_v_w_xo', 'new_v_xa_post_norm', 'new_v_ffn_pre_norm', 'new_v_w_up', 'new_v_ffn_conv_w', 'new_v_ffn_conv_b', 'new_v_w_down', 'new_v_ffn_post_norm']
TWIN_LEAF_KINDS = {'loss': 'loss', 'grad_x': 'grad_x', 'grad_mix_pre_norm': 'grad_w', 'grad_w_in': 'grad_w', 'grad_conv_qkv': 'grad_w', 'grad_a_log': 'grad_w', 'grad_dt_bias': 'grad_w', 'grad_gdn_norm': 'grad_w', 'grad_pool_w': 'grad_w', 'grad_pool_scale': 'grad_w', 'grad_w_branch_a': 'grad_w', 'grad_w_branch_b': 'grad_w', 'grad_w_mix_out': 'grad_w', 'grad_mix_post_norm': 'grad_w', 'grad_xa_pre_norm': 'grad_w', 'grad_mem_norm': 'grad_w', 'grad_w_xq': 'grad_w', 'grad_w_xkv': 'grad_w', 'grad_w_xo': 'grad_w', 'grad_xa_post_norm': 'grad_w', 'grad_ffn_pre_norm': 'grad_w', 'grad_w_up': 'grad_w', 'grad_ffn_conv_w': 'grad_w', 'grad_ffn_conv_b': 'grad_w', 'grad_w_down': 'grad_w', 'grad_ffn_post_norm': 'grad_w', 'delta_mix_pre_norm': 'delta_w', 'delta_w_in': 'delta_w', 'delta_conv_qkv': 'delta_w', 'delta_a_log': 'delta_w', 'delta_dt_bias': 'delta_w', 'delta_gdn_norm': 'delta_w', 'delta_pool_w': 'delta_w', 'delta_pool_scale': 'delta_w', 'delta_w_branch_a': 'delta_w', 'delta_w_branch_b': 'delta_w', 'delta_w_mix_out': 'delta_w', 'delta_mix_post_norm': 'delta_w', 'delta_xa_pre_norm': 'delta_w', 'delta_mem_norm': 'delta_w', 'delta_w_xq': 'delta_w', 'delta_w_xkv': 'delta_w', 'delta_w_xo': 'delta_w', 'delta_xa_post_norm': 'delta_w', 'delta_ffn_pre_norm': 'delta_w', 'delta_w_up': 'delta_w', 'delta_ffn_conv_w': 'delta_w', 'delta_ffn_conv_b': 'delta_w', 'delta_w_down': 'delta_w', 'delta_ffn_post_norm': 'delta_w', 'new_m_mix_pre_norm': 'new_m', 'new_m_w_in': 'new_m', 'new_m_conv_qkv': 'new_m', 'new_m_a_log': 'new_m', 'new_m_dt_bias': 'new_m', 'new_m_gdn_norm': 'new_m', 'new_m_pool_w': 'new_m', 'new_m_pool_scale': 'new_m', 'new_m_w_branch_a': 'new_m', 'new_m_w_branch_b': 'new_m', 'new_m_w_mix_out': 'new_m', 'new_m_mix_post_norm': 'new_m', 'new_m_xa_pre_norm': 'new_m', 'new_m_mem_norm': 'new_m', 'new_m_w_xq': 'new_m', 'new_m_w_xkv': 'new_m', 'new_m_w_xo': 'new_m', 'new_m_xa_post_norm': 'new_m', 'new_m_ffn_pre_norm': 'new_m', 'new_m_w_up': 'new_m', 'new_m_ffn_conv_w': 'new_m', 'new_m_ffn_conv_b': 'new_m', 'new_m_w_down': 'new_m', 'new_m_ffn_post_norm': 'new_m', 'new_v_mix_pre_norm': 'new_v', 'new_v_w_in': 'new_v', 'new_v_conv_qkv': 'new_v', 'new_v_a_log': 'new_v', 'new_v_dt_bias': 'new_v', 'new_v_gdn_norm': 'new_v', 'new_v_pool_w': 'new_v', 'new_v_pool_scale': 'new_v', 'new_v_w_branch_a': 'new_v', 'new_v_w_branch_b': 'new_v', 'new_v_w_mix_out': 'new_v', 'new_v_mix_post_norm': 'new_v', 'new_v_xa_pre_norm': 'new_v', 'new_v_mem_norm': 'new_v', 'new_v_w_xq': 'new_v', 'new_v_w_xkv': 'new_v', 'new_v_w_xo': 'new_v', 'new_v_xa_post_norm': 'new_v', 'new_v_ffn_pre_norm': 'new_v', 'new_v_w_up': 'new_v', 'new_v_ffn_conv_w': 'new_v', 'new_v_ffn_conv_b': 'new_v', 'new_v_w_down': 'new_v', 'new_v_ffn_post_norm': 'new_v'}


def _forward(args):
    return _fwd_reference(*[args[k] for k in FWD_PARAMS])


def _output_shape():
    def fwd():
        inp = _fwd_setup_inputs(0)
        return _fwd_reference(*[inp[k] for k in FWD_PARAMS])
    out = _jax.eval_shape(fwd)
    return out.shape, out.dtype

N_MICROBATCH = 1
ADAM_LR = 0.001
ADAM_B1 = 0.9
ADAM_B2 = 0.999
ADAM_EPS = 1e-08
ADAM_WD = 0.01
ADAM_STEP = 10
PER_EXAMPLE_BATCH_AXIS = {'x': 0, 'mem': 0, 'loss_target': 0}
SHARED_INPUTS = []
_WEIGHT_DTYPES = {'mix_pre_norm': _jnp.float32, 'w_in': _jnp.float32, 'conv_qkv': _jnp.float32, 'a_log': _jnp.float32, 'dt_bias': _jnp.float32, 'gdn_norm': _jnp.float32, 'pool_w': _jnp.float32, 'pool_scale': _jnp.float32, 'w_branch_a': _jnp.float32, 'w_branch_b': _jnp.float32, 'w_mix_out': _jnp.float32, 'mix_post_norm': _jnp.float32, 'xa_pre_norm': _jnp.float32, 'mem_norm': _jnp.float32, 'w_xq': _jnp.float32, 'w_xkv': _jnp.float32, 'w_xo': _jnp.float32, 'xa_post_norm': _jnp.float32, 'ffn_pre_norm': _jnp.float32, 'w_up': _jnp.float32, 'ffn_conv_w': _jnp.float32, 'ffn_conv_b': _jnp.float32, 'w_down': _jnp.float32, 'ffn_post_norm': _jnp.float32}
MOMENT_SCALE = {'mix_pre_norm': 6.553426e-01, 'w_in': 2.387590e-01, 'conv_qkv': 2.684321e-01, 'a_log': 1.255352e+00, 'dt_bias': 1.231303e+00, 'gdn_norm': 2.226953e+00, 'pool_w': 6.631875e-01, 'pool_scale': 7.052142e-01, 'w_branch_a': 6.387657e-01, 'w_branch_b': 4.904294e-01, 'w_mix_out': 7.984704e-01, 'mix_post_norm': 3.202975e+01, 'xa_pre_norm': 4.236544e-01, 'mem_norm': 1.661896e+00, 'w_xq': 4.074604e-01, 'w_xkv': 1.131848e+00, 'w_xo': 1.714533e+00, 'xa_post_norm': 3.275890e+01, 'ffn_pre_norm': 1.171185e+00, 'w_up': 4.847140e-01, 'ffn_conv_w': 5.576514e-01, 'ffn_conv_b': 1.763541e+00, 'w_down': 1.001771e+00, 'ffn_post_norm': 3.198169e+01}


def _to_microbatches(a, axis):
    t = _jnp.moveaxis(a, axis, 0)
    t = t.reshape((N_MICROBATCH, t.shape[0] // N_MICROBATCH) + t.shape[1:])
    return _jnp.moveaxis(t, 1, axis + 1)


def setup_inputs(seed: int = 0) -> dict:
    inp = _fwd_setup_inputs(seed)
    key = _jax.random.fold_in(_jax.random.key(seed), 7919)
    shape, _ = _output_shape()
    out = dict(inp)
    out["loss_target"] = _jax.random.normal(_jax.random.fold_in(key, 0), shape, _jnp.float32)
    for i, name in enumerate(TWIN_WEIGHTS):
        w = inp[name].astype(_jnp.float32)
        if MOMENT_SCALE is None:
            s = _jnp.sqrt(_jnp.mean(_jnp.square(w)) + 1e-30)
        else:
            s = MOMENT_SCALE[name]
        km, kv = _jax.random.split(_jax.random.fold_in(key, i + 1))
        out[name] = w
        out["m_" + name] = s * _jax.random.normal(km, w.shape, _jnp.float32)
        out["v_" + name] = (s * s) * _jax.random.uniform(kv, w.shape, _jnp.float32, 0.5, 1.5)
    if N_MICROBATCH > 1:
        for name, axis in PER_EXAMPLE_BATCH_AXIS.items():
            out[name] = _to_microbatches(out[name], axis)
    return {'x': out['x'], 'mem': out['mem'], 'mix_pre_norm': out['mix_pre_norm'], 'w_in': out['w_in'], 'conv_qkv': out['conv_qkv'], 'a_log': out['a_log'], 'dt_bias': out['dt_bias'], 'gdn_norm': out['gdn_norm'], 'pool_w': out['pool_w'], 'pool_scale': out['pool_scale'], 'w_branch_a': out['w_branch_a'], 'w_branch_b': out['w_branch_b'], 'w_mix_out': out['w_mix_out'], 'mix_post_norm': out['mix_post_norm'], 'xa_pre_norm': out['xa_pre_norm'], 'mem_norm': out['mem_norm'], 'w_xq': out['w_xq'], 'w_xkv': out['w_xkv'], 'w_xo': out['w_xo'], 'xa_post_norm': out['xa_post_norm'], 'ffn_pre_norm': out['ffn_pre_norm'], 'w_up': out['w_up'], 'ffn_conv_w': out['ffn_conv_w'], 'ffn_conv_b': out['ffn_conv_b'], 'w_down': out['w_down'], 'ffn_post_norm': out['ffn_post_norm'], 'loss_target': out['loss_target'], 'm_mix_pre_norm': out['m_mix_pre_norm'], 'm_w_in': out['m_w_in'], 'm_conv_qkv': out['m_conv_qkv'], 'm_a_log': out['m_a_log'], 'm_dt_bias': out['m_dt_bias'], 'm_gdn_norm': out['m_gdn_norm'], 'm_pool_w': out['m_pool_w'], 'm_pool_scale': out['m_pool_scale'], 'm_w_branch_a': out['m_w_branch_a'], 'm_w_branch_b': out['m_w_branch_b'], 'm_w_mix_out': out['m_w_mix_out'], 'm_mix_post_norm': out['m_mix_post_norm'], 'm_xa_pre_norm': out['m_xa_pre_norm'], 'm_mem_norm': out['m_mem_norm'], 'm_w_xq': out['m_w_xq'], 'm_w_xkv': out['m_w_xkv'], 'm_w_xo': out['m_w_xo'], 'm_xa_post_norm': out['m_xa_post_norm'], 'm_ffn_pre_norm': out['m_ffn_pre_norm'], 'm_w_up': out['m_w_up'], 'm_ffn_conv_w': out['m_ffn_conv_w'], 'm_ffn_conv_b': out['m_ffn_conv_b'], 'm_w_down': out['m_w_down'], 'm_ffn_post_norm': out['m_ffn_post_norm'], 'v_mix_pre_norm': out['v_mix_pre_norm'], 'v_w_in': out['v_w_in'], 'v_conv_qkv': out['v_conv_qkv'], 'v_a_log': out['v_a_log'], 'v_dt_bias': out['v_dt_bias'], 'v_gdn_norm': out['v_gdn_norm'], 'v_pool_w': out['v_pool_w'], 'v_pool_scale': out['v_pool_scale'], 'v_w_branch_a': out['v_w_branch_a'], 'v_w_branch_b': out['v_w_branch_b'], 'v_w_mix_out': out['v_w_mix_out'], 'v_mix_post_norm': out['v_mix_post_norm'], 'v_xa_pre_norm': out['v_xa_pre_norm'], 'v_mem_norm': out['v_mem_norm'], 'v_w_xq': out['v_w_xq'], 'v_w_xkv': out['v_w_xkv'], 'v_w_xo': out['v_w_xo'], 'v_xa_post_norm': out['v_xa_post_norm'], 'v_ffn_pre_norm': out['v_ffn_pre_norm'], 'v_w_up': out['v_w_up'], 'v_ffn_conv_w': out['v_ffn_conv_w'], 'v_ffn_conv_b': out['v_ffn_conv_b'], 'v_w_down': out['v_w_down'], 'v_ffn_post_norm': out['v_ffn_post_norm']}


def _loss(weights, diff, rest, loss_target):
    with _jax.named_scope("forward"):
        args = {**rest, TWIN_DIFF_INPUT: diff, **{k: w.astype(_WEIGHT_DTYPES[k]) for k, w in weights.items()}}
        y = _forward(args)
    with _jax.named_scope("loss_head"):
        err = _jnp.square(y.astype(_jnp.float32) - loss_target)
        return 0.5 * _jnp.sum(_jnp.mean(err, axis=-1)) if err.ndim else 0.5 * err


def _adamw(w, g, m, v):
    m = ADAM_B1 * m + (1.0 - ADAM_B1) * g
    v = ADAM_B2 * v + (1.0 - ADAM_B2) * _jnp.square(g)
    m_hat = m / (1.0 - ADAM_B1 ** ADAM_STEP)
    v_hat = v / (1.0 - ADAM_B2 ** ADAM_STEP)
    delta = -ADAM_LR * (m_hat / (_jnp.sqrt(v_hat) + ADAM_EPS) + ADAM_WD * w)
    return delta, m, v


def reference(x, mem, mix_pre_norm, w_in, conv_qkv, a_log, dt_bias, gdn_norm, pool_w, pool_scale, w_branch_a, w_branch_b, w_mix_out, mix_post_norm, xa_pre_norm, mem_norm, w_xq, w_xkv, w_xo, xa_post_norm, ffn_pre_norm, w_up, ffn_conv_w, ffn_conv_b, w_down, ffn_post_norm, loss_target, m_mix_pre_norm, m_w_in, m_conv_qkv, m_a_log, m_dt_bias, m_gdn_norm, m_pool_w, m_pool_scale, m_w_branch_a, m_w_branch_b, m_w_mix_out, m_mix_post_norm, m_xa_pre_norm, m_mem_norm, m_w_xq, m_w_xkv, m_w_xo, m_xa_post_norm, m_ffn_pre_norm, m_w_up, m_ffn_conv_w, m_ffn_conv_b, m_w_down, m_ffn_post_norm, v_mix_pre_norm, v_w_in, v_conv_qkv, v_a_log, v_dt_bias, v_gdn_norm, v_pool_w, v_pool_scale, v_w_branch_a, v_w_branch_b, v_w_mix_out, v_mix_post_norm, v_xa_pre_norm, v_mem_norm, v_w_xq, v_w_xkv, v_w_xo, v_xa_post_norm, v_ffn_pre_norm, v_w_up, v_ffn_conv_w, v_ffn_conv_b, v_w_down, v_ffn_post_norm):
    given = dict(x=x, mem=mem, mix_pre_norm=mix_pre_norm, w_in=w_in, conv_qkv=conv_qkv, a_log=a_log, dt_bias=dt_bias, gdn_norm=gdn_norm, pool_w=pool_w, pool_scale=pool_scale, w_branch_a=w_branch_a, w_branch_b=w_branch_b, w_mix_out=w_mix_out, mix_post_norm=mix_post_norm, xa_pre_norm=xa_pre_norm, mem_norm=mem_norm, w_xq=w_xq, w_xkv=w_xkv, w_xo=w_xo, xa_post_norm=xa_post_norm, ffn_pre_norm=ffn_pre_norm, w_up=w_up, ffn_conv_w=ffn_conv_w, ffn_conv_b=ffn_conv_b, w_down=w_down, ffn_post_norm=ffn_post_norm, loss_target=loss_target, m_mix_pre_norm=m_mix_pre_norm, m_w_in=m_w_in, m_conv_qkv=m_conv_qkv, m_a_log=m_a_log, m_dt_bias=m_dt_bias, m_gdn_norm=m_gdn_norm, m_pool_w=m_pool_w, m_pool_scale=m_pool_scale, m_w_branch_a=m_w_branch_a, m_w_branch_b=m_w_branch_b, m_w_mix_out=m_w_mix_out, m_mix_post_norm=m_mix_post_norm, m_xa_pre_norm=m_xa_pre_norm, m_mem_norm=m_mem_norm, m_w_xq=m_w_xq, m_w_xkv=m_w_xkv, m_w_xo=m_w_xo, m_xa_post_norm=m_xa_post_norm, m_ffn_pre_norm=m_ffn_pre_norm, m_w_up=m_w_up, m_ffn_conv_w=m_ffn_conv_w, m_ffn_conv_b=m_ffn_conv_b, m_w_down=m_w_down, m_ffn_post_norm=m_ffn_post_norm, v_mix_pre_norm=v_mix_pre_norm, v_w_in=v_w_in, v_conv_qkv=v_conv_qkv, v_a_log=v_a_log, v_dt_bias=v_dt_bias, v_gdn_norm=v_gdn_norm, v_pool_w=v_pool_w, v_pool_scale=v_pool_scale, v_w_branch_a=v_w_branch_a, v_w_branch_b=v_w_branch_b, v_w_mix_out=v_w_mix_out, v_mix_post_norm=v_mix_post_norm, v_xa_pre_norm=v_xa_pre_norm, v_mem_norm=v_mem_norm, v_w_xq=v_w_xq, v_w_xkv=v_w_xkv, v_w_xo=v_w_xo, v_xa_post_norm=v_xa_post_norm, v_ffn_pre_norm=v_ffn_pre_norm, v_w_up=v_w_up, v_ffn_conv_w=v_ffn_conv_w, v_ffn_conv_b=v_ffn_conv_b, v_w_down=v_w_down, v_ffn_post_norm=v_ffn_post_norm)
    weights = {n: given[n] for n in TWIN_WEIGHTS}
    shared = {n: given[n] for n in SHARED_INPUTS}
    per_example = {n: given[n] for n in ['x', 'mem']}
    grad_fn = _jax.value_and_grad(_loss, argnums=(0, 1))

    def one_microbatch(ex, loss_target):
        ex = dict(ex)
        diff = ex.pop(TWIN_DIFF_INPUT)
        return grad_fn(weights, diff, {**shared, **ex}, loss_target)

    if N_MICROBATCH == 1:
        loss, (grad_w, grad_x) = one_microbatch(per_example, given["loss_target"])
    else:
        def body(carry, xs):
            loss_sum, grad_sum = carry
            l_k, (gw_k, gx_k) = one_microbatch(xs[0], xs[1])
            with _jax.named_scope("update"):
                return (loss_sum + l_k, _jax.tree.map(_jnp.add, grad_sum, gw_k)), gx_k

        init = (_jnp.zeros((), _jnp.float32), _jax.tree.map(_jnp.zeros_like, weights))
        (loss, grad_w), grad_x = _jax.lax.scan(body, init, (per_example, given["loss_target"]))
    with _jax.named_scope("update"):
        delta_w, new_m, new_v = {}, {}, {}
        for n in TWIN_WEIGHTS:
            delta_w[n], new_m[n], new_v[n] = _adamw(weights[n], grad_w[n], given["m_" + n], given["v_" + n])
    return (loss, grad_x, *[grad_w[n] for n in TWIN_WEIGHTS], *[delta_w[n] for n in TWIN_WEIGHTS],
            *[new_m[n] for n in TWIN_WEIGHTS], *[new_v[n] for n in TWIN_WEIGHTS])
```

```python
import functools

import jax
import jax.numpy as jnp
from jax import lax
from jax.experimental import pallas as pl
from jax.experimental.pallas import tpu as pltpu

F32 = jnp.float32
BF16 = jnp.bfloat16

D_MODEL = 2048
CHUNK = 64
GDN_HEADS = 16
GDN_DK = 128
POOL_WINDOWS = (2, 4, 8, 16)
XA_HEADS = 4
XA_HEAD_DIM = D_MODEL // XA_HEADS
EPS = 1e-6
HALO = 16
V7X_VMEM_LIMIT = 56 * 1024 * 1024

ADAM_LR, ADAM_B1, ADAM_B2, ADAM_EPS, ADAM_WD, ADAM_STEP = 0.001, 0.9, 0.999, 1e-08, 0.01, 10

_NN = ((1,), (0,))
_NT = ((1,), (1,))
_TN = ((0,), (0,))


def _cparams(sem):
    return pltpu.CompilerParams(dimension_semantics=sem, vmem_limit_bytes=V7X_VMEM_LIMIT)


def _dg(a, b, dims, prec=None):
    return lax.dot_general(a, b, (dims, ((), ())), precision=prec, preferred_element_type=F32)


def _make_dots(cast, prec):
    def raw(dims, a, b):
        return _dg(cast(a), cast(b), dims, prec)

    @jax.custom_vjp
    def nn(a, b):
        return raw(_NN, a, b)

    @jax.custom_vjp
    def nt(a, b):
        return raw(_NT, a, b)

    @jax.custom_vjp
    def tn(a, b):
        return raw(_TN, a, b)

    nn.defvjp(lambda a, b: (raw(_NN, a, b), (a, b)), lambda r, g: (nt(g, r[1]), tn(r[0], g)))
    nt.defvjp(lambda a, b: (raw(_NT, a, b), (a, b)), lambda r, g: (nn(g, r[1]), tn(g, r[0])))
    tn.defvjp(lambda a, b: (raw(_TN, a, b), (a, b)), lambda r, g: (nt(r[1], g), nn(r[0], g)))
    return nn, nt, tn


bdot_nn, bdot_nt, bdot_tn = _make_dots(lambda x: x.astype(BF16), None)
hdot_nn, hdot_nt, hdot_tn = _make_dots(lambda x: x, lax.Precision.HIGHEST)


@functools.partial(jax.custom_vjp, nondiff_argnums=(1,))
def shift_rows(x, k):
    return pltpu.roll(x, k, 0)


def _shift_rows_fwd(x, k):
    return pltpu.roll(x, k, 0), None


def _shift_rows_bwd(k, _, g):
    return (pltpu.roll(g, g.shape[0] - k, 0),)


shift_rows.defvjp(_shift_rows_fwd, _shift_rows_bwd)


@functools.partial(jax.custom_vjp, nondiff_argnums=(1,))
def drop_head(x, h):
    return x[h:]


def _drop_head_fwd(x, h):
    return x[h:], None


def _drop_head_bwd(h, _, g):
    return (jnp.concatenate([jnp.zeros((h,) + g.shape[1:], g.dtype), g], axis=0),)


drop_head.defvjp(_drop_head_fwd, _drop_head_bwd)


@functools.partial(jax.custom_vjp, nondiff_argnums=(1,))
def split_lanes(x, n):
    w = x.shape[-1] // n
    return tuple(x[:, i * w:(i + 1) * w] for i in range(n))


def _split_lanes_fwd(x, n):
    return split_lanes(x, n), None


def _split_lanes_bwd(n, _, gs):
    return (jnp.concatenate(list(gs), axis=-1),)


split_lanes.defvjp(_split_lanes_fwd, _split_lanes_bwd)


@functools.partial(jax.custom_vjp, nondiff_argnums=(1,))
def split_rows(x, n):
    h = x.shape[0] // n
    return tuple(x[i * h:(i + 1) * h] for i in range(n))


def _split_rows_fwd(x, n):
    return split_rows(x, n), None


def _split_rows_bwd(n, _, gs):
    return (jnp.concatenate(list(gs), axis=0),)


split_rows.defvjp(_split_rows_fwd, _split_rows_bwd)


def row_of(w, j):
    rid = lax.broadcasted_iota(jnp.int32, w.shape, 0)
    return jnp.sum(jnp.where(rid == j, w, 0.0), axis=0, keepdims=True)


def sigmoid(x):
    return 1.0 / (1.0 + jnp.exp(-x))


def silu(x):
    return x * sigmoid(x)


def softplus(x):
    return jnp.maximum(x, 0.0) + jnp.log(1.0 + jnp.exp(-jnp.abs(x)))


def rms(x, g):
    return x * lax.rsqrt(jnp.mean(x * x, axis=-1, keepdims=True) + EPS) * g


def mm(name, dims, pairs, out_shape, out_spec, grid):
    nk = grid[2]
    acc_in_out = nk > 1 and out_shape.dtype == F32
    npair = len(pairs)

    def body(*refs):
        o_ref = refs[2 * npair]
        part = None
        for p in range(npair):
            d = _dg(refs[2 * p][...].astype(BF16), refs[2 * p + 1][...].astype(BF16), dims)
            part = d if part is None else part + d
        if nk == 1:
            o_ref[...] = part.astype(o_ref.dtype)
            return
        acc = o_ref if acc_in_out else refs[2 * npair + 1]
        k = pl.program_id(2)

        @pl.when(k == 0)
        def _():
            acc[...] = part

        @pl.when(k > 0)
        def _():
            acc[...] += part

        if not acc_in_out:
            @pl.when(k == nk - 1)
            def _():
                o_ref[...] = acc[...].astype(o_ref.dtype)

    scratch = []
    if nk > 1 and not acc_in_out:
        scratch = [pltpu.VMEM(tuple(d for d in out_spec.block_shape if d is not None), F32)]
    in_specs, operands = [], []
    for a, b, a_spec, b_spec in pairs:
        in_specs += [a_spec, b_spec]
        operands += [a, b]
    return pl.pallas_call(
        body, name=name, grid=grid, in_specs=in_specs, out_specs=out_spec, out_shape=out_shape,
        scratch_shapes=scratch, compiler_params=_cparams(("parallel", "parallel", "arbitrary")),
    )(*operands)


def _bs(shape, fn):
    return pl.BlockSpec(shape, fn)


class Tile:
    def __init__(self, arr, w, cb=0, lead=None, halo=False):
        self.arr, self.w, self.cb, self.lead, self.halo = arr, w, cb, lead, halo


class Out:
    def __init__(self, shape, dtype, w, cb=0, lead=None, into=None):
        self.shape, self.dtype, self.w, self.cb, self.lead, self.into = shape, dtype, w, cb, lead, into


class Par:
    def __init__(self, arr, lead=None):
        self.arr, self.lead = arr, lead


def _spec(rows, w, cb, lead, tile_of):
    if lead is None:
        return pl.BlockSpec((rows, w), lambda o, i: (tile_of(i), cb))
    return pl.BlockSpec((None, rows, w), lambda o, i: (lead(o), tile_of(i), cb))


def _par_spec(p):
    if p.lead is None:
        return pl.BlockSpec(p.arr.shape, lambda o, i: (0, 0))
    return pl.BlockSpec((None,) + p.arr.shape[1:], lambda o, i: (p.lead(o), 0, 0))


def run_stage(name, fn, tm, ins, pars, outs, *, outer=1, cts=None, dins=None):
    T = ins[0].arr.shape[-2]
    nt = T // tm
    bwd = cts is not None
    any_halo = any(t.halo for t in ins)
    step_tile = (lambda i: nt - 1 - i) if bwd else (lambda i: i)
    hb = tm // HALO

    in_specs, operands = [], []
    for t in ins:
        if t.halo:
            in_specs.append(_spec(HALO, t.w, t.cb, t.lead, lambda i: jnp.maximum(step_tile(i) * hb - 1, 0)))
            operands.append(t.arr)
        in_specs.append(_spec(tm, t.w, t.cb, t.lead, step_tile))
        operands.append(t.arr)
    for p in pars:
        in_specs.append(_par_spec(p))
        operands.append(p.arr)
    n_in_refs = len(operands)

    out_descs = list(outs) if not bwd else [d for d in dins if d is not None]
    aliases = {}
    if bwd:
        for c, o in zip(cts, outs):
            in_specs.append(_spec(tm, o.w, o.cb, o.lead, step_tile))
            operands.append(c)
    n_ct = len(operands) - n_in_refs
    for k, o in enumerate(out_descs):
        if o.into is not None:
            aliases[len(operands)] = k
            in_specs.append(pl.BlockSpec(memory_space=pl.ANY))
            operands.append(o.into)
    out_specs = [_spec(tm, o.w, o.cb, o.lead, step_tile) for o in out_descs]
    out_shapes = [jax.ShapeDtypeStruct(o.shape, o.dtype) for o in out_descs]
    if bwd:
        for p in pars:
            out_specs.append(_par_spec(p))
            out_shapes.append(jax.ShapeDtypeStruct(p.arr.shape, F32))
    scratch = []
    if bwd and any_halo:
        scratch = [pltpu.VMEM((HALO, t.w), F32) for t, d in zip(ins, dins) if t.halo and d is not None]

    def body(*refs):
        i = pl.program_id(1)
        tile = step_tile(i)
        row0 = tile * tm
        pos = 0
        tiles = []
        for t in ins:
            if t.halo:
                prev = jnp.where(tile > 0, refs[pos][...].astype(F32), 0.0)
                tiles.append(jnp.concatenate([prev, refs[pos + 1][...].astype(F32)], axis=0))
                pos += 2
            else:
                tiles.append(refs[pos][...].astype(F32))
                pos += 1
        pvals = [refs[pos + k][...].astype(F32) for k in range(len(pars))]
        pos += len(pars)
        if not bwd:
            res = fn(tiles, pvals, row0)
            for o_ref, r in zip(refs[pos:], res):
                o_ref[...] = r.astype(o_ref.dtype)
            return
        ct_vals = [refs[pos + k][...].astype(F32) for k in range(n_ct)]
        pos += n_ct + len(aliases)
        _, vjp_fn = jax.vjp(lambda tt, pp: fn(tt, pp, row0), tiles, pvals)
        d_tiles, d_pars = vjp_fn(ct_vals)
        carries = list(refs[len(refs) - len(scratch):])
        for t, d, dt in zip(ins, dins, d_tiles):
            if d is None:
                continue
            o_ref = refs[pos]
            pos += 1
            if t.halo:
                carry = carries.pop(0)
                main = dt[HALO:]
                tail = main[tm - HALO:] + jnp.where(i > 0, carry[...], 0.0)
                o_ref[...] = jnp.concatenate([main[:tm - HALO], tail], axis=0).astype(o_ref.dtype)
                carry[...] = dt[:HALO]
            else:
                o_ref[...] = dt.astype(o_ref.dtype)
        for dp in d_pars:
            acc = refs[pos]
            pos += 1

            @pl.when(i == 0)
            def _(acc=acc, dp=dp):
                acc[...] = dp

            @pl.when(i > 0)
            def _(acc=acc, dp=dp):
                acc[...] += dp

    res = pl.pallas_call(
        body, name=name, grid=(outer, nt), in_specs=in_specs, out_specs=out_specs, out_shape=out_shapes,
        scratch_shapes=scratch, input_output_aliases=aliases,
        compiler_params=_cparams(("arbitrary", "arbitrary")),
    )(*operands)
    if not bwd:
        return list(res)
    n_d = len(out_descs)
    d_full, it = [], iter(res[:n_d])
    for d in dins:
        d_full.append(None if d is None else next(it))
    return d_full, list(res[n_d:])


def f_prenorm(t, p, row0):
    return [rms(t[0], p[0])]


def f_prenorm_res(t, p, row0):
    return [t[0], rms(t[0], p[0])]


def f_post_pre(t, p, row0):
    x, y = t
    x1 = x + rms(y, p[0])
    return [x1, rms(x1, p[1])]


def _causal_conv(x, w, taps):
    y = x * row_of(w, taps - 1)
    for j in range(taps - 1):
        y = y + shift_rows(x, taps - 1 - j) * row_of(w, j)
    return drop_head(y, HALO)


def _l2(x):
    return x * lax.rsqrt(jnp.sum(x * x, axis=-1, keepdims=True) + EPS)


def make_f_convhead(scale, normalise):
    def f(t, p, row0):
        y = silu(_causal_conv(t[0], p[0], 4))
        if not normalise:
            return [y]
        return [jnp.concatenate([_l2(c) * scale for c in split_lanes(y, GDN_HEADS)], axis=-1)]
    return f


def f_bg(t, p, row0):
    ba = split_lanes(t[0], 4)[0]
    alog, dtb = p
    lane = lax.broadcasted_iota(jnp.int32, ba.shape, 1)
    bg = jnp.where(lane < GDN_HEADS, sigmoid(ba), -jnp.exp(alog) * softplus(ba + dtb))
    return [jnp.where(lane < 2 * GDN_HEADS, bg, 0.0)]


def f_gnorm(t, p, row0):
    o, z = t
    po, pz = split_lanes(o, GDN_HEADS), split_lanes(z, GDN_HEADS)
    return [jnp.concatenate([rms(a, p[0]) * silu(b) for a, b in zip(po, pz)], axis=-1)]


def f_pool(t, p, row0):
    x = t[0]
    pw, psc = p
    tm = x.shape[0] - HALO
    tpos = (row0 + lax.broadcasted_iota(jnp.int32, (tm, 1), 0) + 1).astype(F32)
    outs = []
    for xg, wg, win in zip(split_lanes(x, 4), split_rows(pw, 4), POOL_WINDOWS):
        s, span = xg, 1
        while span < win:
            s = s + shift_rows(s, span)
            span *= 2
        mean = drop_head(s, HALO) / jnp.minimum(tpos, float(win))
        outs.append(bdot_nn(mean - drop_head(xg, HALO), wg))
    return [jnp.concatenate(outs, axis=-1) * psc]


def f_merge(t, p, row0):
    gates, ya, yb = t
    ga, gb = split_lanes(gates, 2)
    return [sigmoid(ga) * ya + sigmoid(gb) * yb]


def f_xattn(t, p, row0):
    k, v = p
    outs = []
    for qh, kh, vh in zip(split_lanes(t[0], XA_HEADS), split_lanes(k, XA_HEADS), split_lanes(v, XA_HEADS)):
        s = bdot_nt(qh, kh) * (XA_HEAD_DIM ** -0.5)
        s = s - jnp.max(s, axis=-1, keepdims=True)
        e = jnp.exp(s)
        outs.append(bdot_nn(e / jnp.sum(e, axis=-1, keepdims=True), vh))
    return [jnp.concatenate(outs, axis=-1)]


def f_convglu(t, p, row0):
    ua, ub = t
    cwa, cwb, ba, bb = p
    return [silu(_causal_conv(ua, cwa, 3) + ba) * (_causal_conv(ub, cwb, 3) + bb)]


def _gdn_local(q, k, v, gcol, bcol):
    C = CHUNK
    r = lax.broadcasted_iota(jnp.int32, (C, C), 0)
    c = lax.broadcasted_iota(jnp.int32, (C, C), 1)
    eye, incl, strict = r == c, r >= c, r > c
    grow = jnp.sum(jnp.where(eye, gcol, 0.0), axis=0, keepdims=True)
    Gcol = jnp.sum(jnp.where(incl, grow, 0.0), axis=1, keepdims=True)
    Grow = jnp.sum(jnp.where(eye, Gcol, 0.0), axis=0, keepdims=True)
    decay = jnp.where(incl, jnp.exp(jnp.where(incl, Gcol - Grow, 0.0)), 0.0)
    X = -jnp.where(strict, bcol * decay * bdot_nt(k, k), 0.0)
    ainv = jnp.where(eye, 1.0, 0.0) + X
    pw = X
    for _ in range(5):
        pw = hdot_nn(pw, pw)
        ainv = ainv + hdot_nn(ainv, pw)
    expg = jnp.exp(Gcol)
    u_v = hdot_nn(ainv, bcol * v)
    w_k = hdot_nn(ainv, (bcol * expg) * k)
    attn = decay * bdot_nt(q, k)
    rid = lax.broadcasted_iota(jnp.int32, (C, 1), 0)
    glast = jnp.sum(jnp.where(rid == C - 1, Gcol, 0.0), axis=0, keepdims=True)
    return u_v, w_k, attn, q * expg, k * jnp.exp(glast - Gcol), jnp.exp(glast)


def _gdn_rec(u_v, w_k, attn, q_dec, k_dec, cd, S):
    u = u_v - bdot_nn(w_k, S)
    o = bdot_nn(q_dec, S) + bdot_nn(attn, u)
    return o, cd * S + bdot_tn(k_dec, u)


def _gdn_chunk(q, k, v, gcol, bcol, S):
    return _gdn_rec(*_gdn_local(q, k, v, gcol, bcol), S)


def _head_cols(bg, h):
    lane = lax.broadcasted_iota(jnp.int32, bg.shape, 1)
    bcol = jnp.sum(jnp.where(lane == h, bg, 0.0), axis=-1, keepdims=True)
    gcol = jnp.sum(jnp.where(lane == GDN_HEADS + h, bg, 0.0), axis=-1, keepdims=True)
    return gcol, bcol


def gdn_forward(q, k, v, bg, span):
    T = q.shape[0]
    ns, nch, H = T // span, span // CHUNK, GDN_HEADS

    def body(q_ref, k_ref, v_ref, bg_ref, o_ref, s0_ref, state, gcol_s, bcol_s):
        s, h = pl.program_id(0), pl.program_id(1)

        @pl.when(s == 0)
        def _():
            state[h] = jnp.zeros((GDN_DK, GDN_DK), F32)

        s0_ref[...] = state[h]
        gcol_s[...], bcol_s[...] = _head_cols(bg_ref[...], h)

        def chunk(ci, S):
            sl = pl.ds(pl.multiple_of(ci * CHUNK, CHUNK), CHUNK)
            o, S = _gdn_chunk(q_ref[sl, :], k_ref[sl, :], v_ref[sl, :], gcol_s[sl, :], bcol_s[sl, :], S)
            o_ref[sl, :] = o
            return S

        state[h] = lax.fori_loop(0, nch, chunk, state[h])

    hd = pl.BlockSpec((span, GDN_DK), lambda s, h: (s, h))
    return pl.pallas_call(
        body, name="gdn_fwd", grid=(ns, H),
        in_specs=[hd, hd, hd, pl.BlockSpec((span, 128), lambda s, h: (s, 0))],
        out_specs=[hd, pl.BlockSpec((None, None, GDN_DK, GDN_DK), lambda s, h: (s, h, 0, 0))],
        out_shape=[jax.ShapeDtypeStruct((T, H * GDN_DK), F32), jax.ShapeDtypeStruct((ns, H, GDN_DK, GDN_DK), F32)],
        scratch_shapes=[pltpu.VMEM((H, GDN_DK, GDN_DK), F32), pltpu.VMEM((span, 1), F32), pltpu.VMEM((span, 1), F32)],
        compiler_params=_cparams(("arbitrary", "arbitrary")),
    )(q, k, v, bg)


def gdn_backward(q, k, v, bg, s0, do, span):
    T = q.shape[0]
    ns, nch, H = T // span, span // CHUNK, GDN_HEADS

    def body(q_ref, k_ref, v_ref, bg_ref, s0_ref, do_ref, dq_ref, dk_ref, dv_ref, dbg_ref, dstate, starts, gcol_s, bcol_s):
        i, h = pl.program_id(0), pl.program_id(1)

        @pl.when(i == 0)
        def _():
            dstate[h] = jnp.zeros((GDN_DK, GDN_DK), F32)

        @pl.when(h == 0)
        def _():
            dbg_ref[...] = jnp.zeros(dbg_ref.shape, F32)

        gcol_s[...], bcol_s[...] = _head_cols(bg_ref[...], h)

        def args_of(ci):
            sl = pl.ds(pl.multiple_of(ci * CHUNK, CHUNK), CHUNK)
            return sl, (q_ref[sl, :], k_ref[sl, :], v_ref[sl, :], gcol_s[sl, :], bcol_s[sl, :])

        def fwd(ci, S):
            starts[ci] = S
            _, a = args_of(ci)
            return _gdn_chunk(*a, S)[1]

        lax.fori_loop(0, nch, fwd, s0_ref[...])

        def bwd(cj, dS):
            ci = nch - 1 - cj
            sl, a = args_of(ci)
            _, vjp_fn = jax.vjp(_gdn_chunk, *a, starts[ci])
            dq, dk, dv, dg, db, dS = vjp_fn((do_ref[sl, :], dS))
            dq_ref[sl, :], dk_ref[sl, :], dv_ref[sl, :] = dq, dk, dv
            lane = lax.broadcasted_iota(jnp.int32, (CHUNK, 128), 1)
            dbg_ref[sl, :] += jnp.where(lane == h, db, 0.0) + jnp.where(lane == GDN_HEADS + h, dg, 0.0)
            return dS

        dstate[h] = lax.fori_loop(0, nch, bwd, dstate[h])

    hd = pl.BlockSpec((span, GDN_DK), lambda i, h: (ns - 1 - i, h))
    bgs = pl.BlockSpec((span, 128), lambda i, h: (ns - 1 - i, 0))
    return pl.pallas_call(
        body, name="gdn_bwd", grid=(ns, H),
        in_specs=[hd, hd, hd, bgs, pl.BlockSpec((None, None, GDN_DK, GDN_DK), lambda i, h: (ns - 1 - i, h, 0, 0)), hd],
        out_specs=[hd, hd, hd, bgs],
        out_shape=[jax.ShapeDtypeStruct((T, H * GDN_DK), F32)] * 3 + [jax.ShapeDtypeStruct((T, 128), F32)],
        scratch_shapes=[pltpu.VMEM((H, GDN_DK, GDN_DK), F32), pltpu.VMEM((nch, GDN_DK, GDN_DK), F32),
                        pltpu.VMEM((span, 1), F32), pltpu.VMEM((span, 1), F32)],
        compiler_params=_cparams(("arbitrary", "arbitrary")),
    )(q, k, v, bg, s0, do)


def loss_stage(x2, y3, tgt, g, tm):
    T, D = x2.shape
    nt = T // tm

    def body(x_ref, y_ref, t_ref, g_ref, loss_ref, dx_ref, dy_ref, dg_ref):
        i = pl.program_id(0)
        tgtv = t_ref[...]

        def f(x, y, gg):
            err = x + rms(y, gg) - tgtv
            return 0.5 * jnp.mean(err * err, axis=-1, keepdims=True)

        rows, vjp_fn = jax.vjp(f, x_ref[...], y_ref[...], g_ref[...])
        dx, dy, dg = vjp_fn(jnp.ones_like(rows))
        dx_ref[...] = dx
        dy_ref[...] = dy.astype(dy_ref.dtype)
        part = jnp.sum(rows, axis=0, keepdims=True)

        @pl.when(i == 0)
        def _():
            loss_ref[...] = part
            dg_ref[...] = dg

        @pl.when(i > 0)
        def _():
            loss_ref[...] += part
            dg_ref[...] += dg

    tile = pl.BlockSpec((tm, D), lambda i: (i, 0))
    gs = pl.BlockSpec((1, D), lambda i: (0, 0))
    return pl.pallas_call(
        body, name="loss_head", grid=(nt,), in_specs=[tile, tile, tile, gs],
        out_specs=[pl.BlockSpec((1, 1), lambda i: (0, 0)), tile, tile, gs],
        out_shape=[jax.ShapeDtypeStruct((1, 1), F32), jax.ShapeDtypeStruct((T, D), F32),
                   jax.ShapeDtypeStruct((T, D), BF16), jax.ShapeDtypeStruct((1, D), F32)],
        compiler_params=_cparams(("arbitrary",)),
    )(x2, y3, tgt, g)


def adamw(name, w, g, m, v):
    R, C = w.shape
    tr = R
    for cand in (256, 128, 64, 32, 16, 8):
        if R % cand == 0 and R > cand and cand * C * 4 <= (2 << 20):
            tr = cand
            break
    c1 = 1.0 / (1.0 - ADAM_B1 ** ADAM_STEP)
    c2 = 1.0 / (1.0 - ADAM_B2 ** ADAM_STEP)

    def body(w_ref, g_ref, m_ref, v_ref, d_ref, nm_ref, nv_ref):
        gg = g_ref[...]
        nm = ADAM_B1 * m_ref[...] + (1.0 - ADAM_B1) * gg
        nv = ADAM_B2 * v_ref[...] + (1.0 - ADAM_B2) * (gg * gg)
        d_ref[...] = -ADAM_LR * ((nm * c1) / (jnp.sqrt(nv * c2) + ADAM_EPS) + ADAM_WD * w_ref[...])
        nm_ref[...] = nm
        nv_ref[...] = nv

    spec = pl.BlockSpec((tr, C), lambda i: (i, 0))
    return pl.pallas_call(
        body, name=name, grid=(R // tr,), in_specs=[spec] * 4, out_specs=[spec] * 3,
        out_shape=[jax.ShapeDtypeStruct((R, C), F32)] * 3, compiler_params=_cparams(("parallel",)),
    )(w, g, m, v)


C_Z, C_GATES, C_POOL, C_BA, N_PROJ = 6144, 8192, 12288, 13312, 13824
FF_BLK = 2816
FF_COLS = 2752


def local_step(x, mem, tgt, W, sp):
    T, D = x.shape
    tm = 256
    tmm = min(512, T)
    span = min(512, T)
    nI = T // tmm
    S = jax.ShapeDtypeStruct
    gw, gs = {}, {}

    def stage(name, fn, ins, pars, outs, **kw):
        return run_stage(name, fn, kw.pop("tm", tm), ins, pars, outs, **kw)

    def dense(name, a, w, out_dtype=F32, tn=512):
        Tq, Kd = a.shape
        N = w.shape[1]
        tq = min(tmm, Tq)
        return mm(name, _NN, [(a, w, _bs((tq, Kd), lambda j, i, k: (i, 0)), _bs((Kd, tn), lambda j, i, k: (0, j)))],
                  S((Tq, N), out_dtype), _bs((tq, tn), lambda j, i, k: (i, j)), (N // tn, Tq // tq, 1))

    def dense_t(name, g, w, out_dtype=F32, tn=512):
        Tq, N = g.shape
        Kd = w.shape[0]
        tq = min(tmm, Tq)
        return mm(name, _NT, [(g, w, _bs((tq, N), lambda j, i, k: (i, 0)), _bs((tn, N), lambda j, i, k: (j, 0)))],
                  S((Tq, Kd), out_dtype), _bs((tq, tn), lambda j, i, k: (i, j)), (Kd // tn, Tq // tq, 1))

    def wgrad(name, a, g, ta=512, tn=1024):
        Tq, Kd = a.shape
        N = g.shape[1]
        tt = min(tmm, Tq)
        return mm(name, _TN, [(a, g, _bs((tt, ta), lambda i, j, k: (k, i)), _bs((tt, tn), lambda i, j, k: (k, j)))],
                  S((Kd, N), F32), _bs((ta, tn), lambda i, j, k: (i, j)), (Kd // ta, N // tn, Tq // tt))

    o2048 = lambda dt: Out((T, D), dt, D)

    (h1,) = stage("pre1", f_prenorm, [Tile(x, D)], [Par(sp["mix_pre_norm"])], [o2048(BF16)])
    tnp = 1536
    P2 = mm("in_proj", _NT, [(h1, W["w_in"], _bs((tmm, D), lambda j, i, k: (i, 0)), _bs((tnp, D), lambda j, i, k: (j, 0)))],
            S((T, N_PROJ), F32), _bs((tmm, tnp), lambda j, i, k: (i, j)), (N_PROJ // tnp, nI, 1))
    cw = sp["conv_qkv"]
    cws = [cw[:, i * D:(i + 1) * D] for i in range(3)]
    f_heads = [make_f_convhead(GDN_DK ** -0.5, True), make_f_convhead(1.0, True), make_f_convhead(1.0, False)]
    qkv = [stage("conv_" + n, f_heads[i], [Tile(P2, D, cb=i, halo=True)], [Par(cws[i])], [o2048(F32)])[0]
           for i, n in enumerate("qkv")]
    ba_tile = Tile(P2, 512, cb=C_BA // 512)
    (bg,) = stage("bg", f_bg, [ba_tile], [Par(sp["a_log"]), Par(sp["dt_bias"])], [Out((T, 128), F32, 128)])
    o, s0 = gdn_forward(qkv[0], qkv[1], qkv[2], bg, span)
    z_tile = Tile(P2, D, cb=C_Z // D)
    (o_n,) = stage("gnorm", f_gnorm, [Tile(o, D), z_tile], [Par(sp["gdn_norm"])], [o2048(BF16)])
    y_a = dense("branch_a", o_n, W["w_branch_a"])
    p_tile = Tile(P2, 1024, cb=C_POOL // 1024, halo=True)
    pool_pars = [Par(sp["pool_w"]), Par(sp["pool_scale"])]
    (pooled,) = stage("pool", f_pool, [p_tile], pool_pars, [Out((T, 1024), BF16, 1024)])
    y_b = mm("branch_b", _NN, [(pooled, W["w_branch_b"], _bs((tmm, 1024), lambda j, i, k: (i, 0)),
                                _bs((None, 1024, 512), lambda j, i, k: (j, 0, 0)))],
             S((T, D), F32), _bs((tmm, 512), lambda j, i, k: (i, j)), (4, nI, 1))
    gate_tile = Tile(P2, 2 * D, cb=C_GATES // (2 * D))
    merge_ins = [gate_tile, Tile(y_a, D), Tile(y_b, D)]
    (merged,) = stage("merge", f_merge, merge_ins, [], [o2048(BF16)])
    y1 = dense("mix_out", merged, W["w_mix_out"])
    pp1 = [Par(sp["mix_post_norm"]), Par(sp["xa_pre_norm"])]
    x1, h2 = stage("post1", f_post_pre, [Tile(x, D), Tile(y1, D)], pp1, [o2048(F32), o2048(BF16)])

    q2 = dense("xq", h2, W["w_xq"])
    (mn,) = stage("mem_norm", f_prenorm, [Tile(mem, D)], [Par(sp["mem_norm"])], [Out(mem.shape, BF16, D)], tm=mem.shape[0])
    M = mem.shape[0]
    kv = mm("xkv", _NN, [(mn, W["w_xkv"], _bs((M, D), lambda j, i, k: (0, 0)), _bs((None, D, 1024), lambda j, i, k: (j, 0, 0)))],
            S((M, 2 * D), F32), _bs((M, 1024), lambda j, i, k: (0, j)), (4, 1, 1))
    k2, v2 = kv[:, :D], kv[:, D:]
    xa_pars = [Par(k2), Par(v2)]
    (o2,) = stage("xattn", f_xattn, [Tile(q2, D)], xa_pars, [o2048(BF16)])
    y2 = dense("xo", o2, W["w_xo"])
    pp2 = [Par(sp["xa_post_norm"]), Par(sp["ffn_pre_norm"])]
    x2, h3 = stage("post2", f_post_pre, [Tile(x1, D), Tile(y2, D)], pp2, [o2048(F32), o2048(BF16)])

    def up(name, off):
        return mm(name, _NN, [(h3, W["w_up"], _bs((tmm, D), lambda j, i, k: (i, 0)),
                               _bs((None, D, FF_BLK), lambda j, i, k: (j + off, 0, 0)))],
                  S((2, T, FF_BLK), F32), _bs((None, tmm, FF_BLK), lambda j, i, k: (j, i, 0)), (2, nI, 1))

    Ua, Ub = up("up_a", 0), up("up_b", 2)
    ffn_ins = [Tile(Ua, FF_BLK, lead=lambda o: o, halo=True), Tile(Ub, FF_BLK, lead=lambda o: o, halo=True)]
    ffn_pars = [Par(sp["ffn_conv_w"], lead=lambda o: o), Par(sp["ffn_conv_w"], lead=lambda o: o + 2),
                Par(sp["ffn_conv_b"], lead=lambda o: o), Par(sp["ffn_conv_b"], lead=lambda o: o + 2)]
    ffn_out = [Out((2, T, FF_BLK), BF16, FF_BLK, lead=lambda o: o)]
    (ff,) = stage("convglu", f_convglu, ffn_ins, ffn_pars, ffn_out, outer=2)
    y3 = mm("down", _NN, [(ff, W["w_down"], _bs((None, tmm, FF_BLK), lambda i, j, k: (k, i, 0)),
                           _bs((None, FF_BLK, 512), lambda i, j, k: (k, 0, j)))],
            S((T, D), F32), _bs((tmm, 512), lambda i, j, k: (i, j)), (nI, D // 512, 2))
    loss, dx2, dy3, gs["ffn_post_norm"] = loss_stage(x2, y3, tgt, sp["ffn_post_norm"], tm)

    dff = mm("down_dx", _NT, [(dy3, W["w_down"], _bs((tmm, D), lambda j, i, k: (i, 0)),
                               _bs((None, FF_BLK, D), lambda j, i, k: (j, 0, 0)))],
             S((2, T, FF_BLK), BF16), _bs((None, tmm, FF_BLK), lambda j, i, k: (j, i, 0)), (2, nI, 1))
    gw["w_down"] = mm("down_dw", _TN, [(ff, dy3, _bs((None, tmm, FF_BLK), lambda b, j, k: (b, k, 0)),
                                        _bs((tmm, 1024), lambda b, j, k: (k, j)))],
                      S((2, FF_BLK, D), F32), _bs((None, FF_BLK, 1024), lambda b, j, k: (b, 0, j)), (2, D // 1024, nI))
    dU_out = [Out((2, T, FF_BLK), BF16, FF_BLK, lead=lambda o: o), Out((2, T, FF_BLK), BF16, FF_BLK, lead=lambda o: o)]
    (dUa, dUb), dffn = stage("convglu_bwd", f_convglu, ffn_ins, ffn_pars, ffn_out, outer=2, cts=[dff], dins=dU_out)
    gs["ffn_conv_w"] = jnp.concatenate([dffn[0][:2], dffn[1][2:]], axis=0)
    gs["ffn_conv_b"] = jnp.concatenate([dffn[2][:2], dffn[3][2:]], axis=0)
    dh3 = mm("up_dx", _NT, [(dUa, W["w_up"], _bs((None, tmm, FF_BLK), lambda i, j, k: (k, i, 0)),
                             _bs((None, 512, FF_BLK), lambda i, j, k: (k, j, 0))),
                            (dUb, W["w_up"], _bs((None, tmm, FF_BLK), lambda i, j, k: (k, i, 0)),
                             _bs((None, 512, FF_BLK), lambda i, j, k: (k + 2, j, 0)))],
             S((T, D), F32), _bs((tmm, 512), lambda i, j, k: (i, j)), (nI, D // 512, 2))

    def up_dw(name, dU):
        return mm(name, _TN, [(h3, dU, _bs((tmm, 512), lambda b, i, k: (k, i)), _bs((None, tmm, FF_BLK), lambda b, i, k: (b, k, 0)))],
                  S((2, D, FF_BLK), F32), _bs((None, 512, FF_BLK), lambda b, i, k: (b, i, 0)), (2, D // 512, nI))

    gw["w_up_a"], gw["w_up_b"] = up_dw("up_dw_a", dUa), up_dw("up_dw_b", dUb)
    (dx1, dy2), dpp2 = stage("post2_bwd", f_post_pre, [Tile(x1, D), Tile(y2, D)], pp2, [o2048(F32), o2048(BF16)],
                             cts=[dx2, dh3], dins=[o2048(F32), o2048(BF16)])
    gs["xa_post_norm"], gs["ffn_pre_norm"] = dpp2

    do2 = dense_t("xo_dx", dy2, W["w_xo"])
    gw["w_xo"] = wgrad("xo_dw", o2, dy2)
    (dq2,), (dk2, dv2) = stage("xattn_bwd", f_xattn, [Tile(q2, D)], xa_pars, [o2048(BF16)], cts=[do2], dins=[o2048(BF16)])
    dh2 = dense_t("xq_dx", dq2, W["w_xq"])
    gw["w_xq"] = wgrad("xq_dw", h2, dq2)
    dkv = jnp.concatenate([dk2, dv2], axis=1).astype(BF16)
    dmn = mm("xkv_dx", _NT, [(dkv, W["w_xkv"], _bs((M, 1024), lambda i, j, k: (0, k)), _bs((None, 512, 1024), lambda i, j, k: (k, j, 0)))],
             S((M, D), F32), _bs((M, 512), lambda i, j, k: (0, j)), (1, D // 512, 4))
    gw["w_xkv"] = mm("xkv_dw", _TN, [(mn, dkv, _bs((M, D), lambda b, j, k: (0, 0)), _bs((M, 1024), lambda b, j, k: (0, b)))],
                     S((4, D, 1024), F32), _bs((None, D, 1024), lambda b, j, k: (b, 0, 0)), (4, 1, 1))
    _, (gs["mem_norm"],) = stage("mem_norm_bwd", f_prenorm, [Tile(mem, D)], [Par(sp["mem_norm"])], [Out(mem.shape, BF16, D)],
                                 tm=M, cts=[dmn], dins=[None])
    (dx0, dy1), dpp1 = stage("post1_bwd", f_post_pre, [Tile(x, D), Tile(y1, D)], pp1, [o2048(F32), o2048(BF16)],
                             cts=[dx1, dh2], dins=[o2048(F32), o2048(BF16)])
    gs["mix_post_norm"], gs["xa_pre_norm"] = dpp1

    dmerged = dense_t("mix_out_dx", dy1, W["w_mix_out"])
    gw["w_mix_out"] = wgrad("mix_out_dw", merged, dy1)
    pshape = (T, N_PROJ)
    (dP2, dya, dyb), _ = stage("merge_bwd", f_merge, merge_ins, [], [o2048(BF16)], cts=[dmerged],
                               dins=[Out(pshape, BF16, 2 * D, cb=C_GATES // (2 * D)), o2048(BF16), o2048(BF16)])
    d_on = dense_t("branch_a_dx", dya, W["w_branch_a"])
    gw["w_branch_a"] = wgrad("branch_a_dw", o_n, dya)
    dpooled = mm("branch_b_dx", _NT, [(dyb, W["w_branch_b"], _bs((tmm, 512), lambda i, j, k: (i, k)),
                                       _bs((None, 1024, 512), lambda i, j, k: (k, 0, 0)))],
                 S((T, 1024), F32), _bs((tmm, 1024), lambda i, j, k: (i, 0)), (nI, 1, 4))
    gw["w_branch_b"] = mm("branch_b_dw", _TN, [(pooled, dyb, _bs((tmm, 1024), lambda b, j, k: (k, 0)), _bs((tmm, 512), lambda b, j, k: (k, b)))],
                          S((4, 1024, 512), F32), _bs((None, 1024, 512), lambda b, j, k: (b, 0, 0)), (4, 1, nI))
    (do, dP2), (gs["gdn_norm"],) = stage("gnorm_bwd", f_gnorm, [Tile(o, D), z_tile], [Par(sp["gdn_norm"])], [o2048(BF16)],
                                         cts=[d_on], dins=[o2048(F32), Out(pshape, BF16, D, cb=C_Z // D, into=dP2)])
    dq, dk, dv, dbg = gdn_backward(qkv[0], qkv[1], qkv[2], bg, s0, do, span)
    dcw = []
    for i, (n, dqq) in enumerate(zip("qkv", (dq, dk, dv))):
        (dP2,), (dc,) = stage("conv_%s_bwd" % n, f_heads[i], [Tile(P2, D, cb=i, halo=True)], [Par(cws[i])], [o2048(F32)],
                              cts=[dqq], dins=[Out(pshape, BF16, D, cb=i, into=dP2)])
        dcw.append(dc)
    gs["conv_qkv"] = jnp.concatenate(dcw, axis=1)
    (dP2,), (gs["a_log"], gs["dt_bias"]) = stage("bg_bwd", f_bg, [ba_tile], [Par(sp["a_log"]), Par(sp["dt_bias"])],
                                                 [Out((T, 128), F32, 128)], cts=[dbg],
                                                 dins=[Out(pshape, BF16, 512, cb=C_BA // 512, into=dP2)])
    (dP2,), (gs["pool_w"], gs["pool_scale"]) = stage("pool_bwd", f_pool, [p_tile], pool_pars, [Out((T, 1024), BF16, 1024)],
                                                     cts=[dpooled], dins=[Out(pshape, BF16, 1024, cb=C_POOL // 1024, into=dP2)])
    tk = 1536
    dh1 = mm("in_proj_dx", _NN, [(dP2, W["w_in"], _bs((tmm, tk), lambda i, j, k: (i, k)), _bs((tk, 512), lambda i, j, k: (k, j)))],
             S((T, D), F32), _bs((tmm, 512), lambda i, j, k: (i, j)), (nI, D // 512, N_PROJ // tk))
    gw["w_in"] = mm("in_proj_dw", _TN, [(dP2, h1, _bs((tmm, tk), lambda i, j, k: (k, i)), _bs((tmm, D), lambda i, j, k: (k, 0)))],
                    S((N_PROJ, D), F32), _bs((tk, D), lambda i, j, k: (i, 0)), (N_PROJ // tk, 1, nI))
    (grad_x,), (gs["mix_pre_norm"],) = stage("pre1_bwd", f_prenorm_res, [Tile(x, D)], [Par(sp["mix_pre_norm"])],
                                             [o2048(F32), o2048(BF16)], cts=[dx0, dh1], dins=[o2048(F32)])
    return loss, grad_x, gw, gs


W_IN_COLS = 13344
GROUPED = {"w_branch_b": 512, "w_xkv": 1024}
ROW_SHARDED = ("w_branch_a", "w_mix_out", "w_xq", "w_xo")


def shard_to_slab(name, w):
    if name == "w_in":
        return w.T.astype(BF16)
    if name == "w_up":
        return jnp.pad(w, ((0, 0), (0, FF_BLK - FF_COLS))).astype(BF16)
    return w.astype(BF16)


def slabs_to_weight(name, g):
    if name == "w_in":
        full = g.astype(F32).reshape(W_IN_COLS, D_MODEL)
        pad = jnp.zeros((N_PROJ - W_IN_COLS, D_MODEL), F32)
        return jnp.concatenate([full[0:8192], full[9248:13344], full[8224:9248], full[8192:8224], pad]).astype(BF16)
    if name == "w_down":
        z = jnp.zeros((2, FF_BLK - FF_COLS, D_MODEL), g.dtype)
        return jnp.concatenate([g.reshape(2, FF_COLS, D_MODEL), z], axis=1)
    if name in ROW_SHARDED:
        return g.reshape(D_MODEL, D_MODEL)
    return g


def grad_to_slabs(name, gw):
    if name == "w_in":
        g = gw["w_in"]
        return jnp.concatenate([g[0:8192], g[13312:13344], g[12288:13312], g[8192:12288]]).reshape(4, 3336, D_MODEL)
    if name == "w_up":
        return jnp.concatenate([gw["w_up_a"], gw["w_up_b"]], axis=0)
    if name == "w_down":
        return gw["w_down"][:, :FF_COLS].reshape(4, FF_COLS // 2, D_MODEL)
    if name in ROW_SHARDED:
        return gw[name].reshape(4, D_MODEL // 4, D_MODEL)
    return gw[name]


def slab_to_shard_grad(name, f):
    if name == "w_in":
        return f.T
    if name == "w_up":
        return f[:, :FF_COLS]
    return f


MESH = pl.DeviceIdType.MESH
ANY = pl.BlockSpec(memory_space=pl.ANY)


def _me():
    x, y, c = lax.axis_index("x"), lax.axis_index("y"), lax.axis_index("c")
    return x, y, c, 2 * x + y


def _chip_dev(t, c):
    return (t // 2, t % 2, c)


def _rcopy(src, dst, ssem, rsem, dev):
    return pltpu.make_async_remote_copy(src_ref=src, dst_ref=dst, send_sem=ssem, recv_sem=rsem, device_id=dev, device_id_type=MESH)


def gather_weights(slabs):
    n = len(slabs)

    def body(*refs):
        src, dst = refs[:n], refs[n:2 * n]
        ici_s, ici_r, fwd_s, fwd_r, loc = refs[2 * n:]
        x, y, c, s = _me()
        sender = c == s // 2
        sib = (x, y, 1 - c)
        local = [pltpu.make_async_copy(src[w], dst[w].at[s], loc.at[w]) for w in range(n)]
        for cp in local:
            cp.start()
        for r in (1, 2, 3):
            @pl.when(sender)
            def _(r=r):
                for w in range(n):
                    _rcopy(src[w], dst[w].at[s], ici_s.at[w, r - 1], ici_r.at[w, r - 1], _chip_dev(s ^ r, c)).start()
        for r in (1, 2, 3):
            t = s ^ r
            here = c == t // 2

            @pl.when(here)
            def _(r=r, t=t):
                for w in range(n):
                    _rcopy(src[w], dst[w].at[t], ici_s.at[w, r - 1], ici_r.at[w, r - 1], sib).wait_recv()
                    _rcopy(dst[w].at[t], dst[w].at[t], fwd_s.at[w, r - 1], fwd_r.at[w, r - 1], sib).start()

            @pl.when(jnp.logical_not(here))
            def _(r=r, t=t):
                for w in range(n):
                    _rcopy(dst[w].at[t], dst[w].at[t], fwd_s.at[w, r - 1], fwd_r.at[w, r - 1], sib).wait_recv()
        for r in (1, 2, 3):
            t = s ^ r

            @pl.when(sender)
            def _(r=r):
                for w in range(n):
                    _rcopy(src[w], dst[w].at[s], ici_s.at[w, r - 1], ici_r.at[w, r - 1], sib).wait_send()

            @pl.when(c == t // 2)
            def _(r=r, t=t):
                for w in range(n):
                    _rcopy(dst[w].at[t], dst[w].at[t], fwd_s.at[w, r - 1], fwd_r.at[w, r - 1], sib).wait_send()
        for cp in local:
            cp.wait()

    return pl.pallas_call(
        body, name="gather_weights", in_specs=[ANY] * n, out_specs=[ANY] * n,
        out_shape=[jax.ShapeDtypeStruct((4,) + a.shape, a.dtype) for a in slabs],
        scratch_shapes=[pltpu.SemaphoreType.DMA((n, 3))] * 4 + [pltpu.SemaphoreType.DMA((n,))],
    )(*slabs)


def pair_exchange(g4):
    n = len(g4)

    def body(*refs):
        src, dst = refs[:n], refs[n:2 * n]
        ssem, rsem = refs[2 * n:]
        x, y, c, s = _me()
        cps = [_rcopy(src[w].at[pl.ds(2 * (1 - c), 2)], dst[w], ssem.at[w], rsem.at[w], (x, y, 1 - c)) for w in range(n)]
        for cp in cps:
            cp.start()
        for cp in cps:
            cp.wait()

    return pl.pallas_call(
        body, name="pair_exchange", in_specs=[ANY] * n, out_specs=[ANY] * n,
        out_shape=[jax.ShapeDtypeStruct((2,) + a.shape[1:], a.dtype) for a in g4],
        scratch_shapes=[pltpu.SemaphoreType.DMA((n,))] * 2,
    )(*g4)


def _col_tile(R, C):
    for tc in (512, 256, 128):
        if C % tc == 0 and R * tc * 4 <= (4 << 20):
            return tc
    return 128


def pair_add(name, g4, gsib, c):
    _, R, C = g4.shape
    tc = _col_tile(R, C)

    def body(c_ref, a_ref, b_ref, of_ref, ob_ref):
        v = a_ref[...] + b_ref[...]
        of_ref[...] = v
        ob_ref[...] = v.astype(BF16)

    blk = lambda f: pl.BlockSpec((None, R, tc), f)
    gsp = pltpu.PrefetchScalarGridSpec(
        num_scalar_prefetch=1, grid=(2, C // tc),
        in_specs=[blk(lambda p, j, cr: (2 * cr[0] + p, 0, j)), blk(lambda p, j, cr: (p, 0, j))],
        out_specs=[blk(lambda p, j, cr: (p, 0, j)), blk(lambda p, j, cr: (p, 0, j))])
    return pl.pallas_call(
        body, name=name, grid_spec=gsp,
        out_shape=[jax.ShapeDtypeStruct((2, R, C), F32), jax.ShapeDtypeStruct((2, R, C), BF16)],
        compiler_params=_cparams(("arbitrary", "arbitrary")),
    )(c.reshape(1).astype(jnp.int32), g4, gsib)


def scatter_partials(rb):
    n = len(rb)

    def body(*refs):
        src, dst = refs[:n], refs[n:2 * n]
        ssem, rsem = refs[2 * n:]
        x, y, c, s = _me()
        for r in (1, 2, 3):
            t = s ^ r

            @pl.when(t // 2 == c)
            def _(r=r, t=t):
                for w in range(n):
                    _rcopy(src[w].at[t % 2], dst[w].at[s], ssem.at[w, r - 1], rsem.at[w, r - 1], _chip_dev(t, c)).start()
        for r in (1, 2, 3):
            t = s ^ r

            @pl.when(s // 2 == c)
            def _(r=r, t=t):
                for w in range(n):
                    _rcopy(src[w].at[0], dst[w].at[t], ssem.at[w, r - 1], rsem.at[w, r - 1], _chip_dev(t, c)).wait_recv()
        for r in (1, 2, 3):
            t = s ^ r

            @pl.when(t // 2 == c)
            def _(r=r, t=t):
                for w in range(n):
                    _rcopy(src[w].at[t % 2], dst[w].at[s], ssem.at[w, r - 1], rsem.at[w, r - 1], _chip_dev(t, c)).wait_send()

    return pl.pallas_call(
        body, name="scatter_partials", in_specs=[ANY] * n, out_specs=[ANY] * n,
        out_shape=[jax.ShapeDtypeStruct((4,) + a.shape[1:], a.dtype) for a in rb],
        scratch_shapes=[pltpu.SemaphoreType.DMA((n, 3))] * 2,
    )(*rb)


def final_sum(name, rf, recv, s):
    _, R, C = rf.shape
    tc = _col_tile(R, C)

    def body(s_ref, own_ref, r0_ref, r1_ref, r2_ref, o_ref):
        o_ref[...] = ((own_ref[...] + r0_ref[...].astype(F32)) + r1_ref[...].astype(F32)) + r2_ref[...].astype(F32)

    blk = lambda f: pl.BlockSpec((None, R, tc), f)
    other = lambda k: (lambda j, sr: (k + (k >= sr[0]).astype(jnp.int32), 0, j))
    gsp = pltpu.PrefetchScalarGridSpec(
        num_scalar_prefetch=1, grid=(C // tc,),
        in_specs=[blk(lambda j, sr: (sr[0] % 2, 0, j)), blk(other(0)), blk(other(1)), blk(other(2))],
        out_specs=pl.BlockSpec((R, tc), lambda j, sr: (0, j)))
    return pl.pallas_call(
        body, name=name, grid_spec=gsp, out_shape=jax.ShapeDtypeStruct((R, C), F32),
        compiler_params=_cparams(("arbitrary",)),
    )(s.reshape(1).astype(jnp.int32), rf, recv, recv, recv)


def share_with_sibling(fs):
    n = len(fs)

    def body(*refs):
        src, dst = refs[:n], refs[n:2 * n]
        ssem, rsem = refs[2 * n:]
        x, y, c, s = _me()
        sib = (x, y, 1 - c)

        @pl.when(s // 2 == c)
        def _():
            cps = [_rcopy(src[w], dst[w], ssem.at[w], rsem.at[w], sib) for w in range(n)]
            for cp in cps:
                cp.start()
            for cp in cps:
                cp.wait_send()

        @pl.when(s // 2 != c)
        def _():
            for w in range(n):
                _rcopy(src[w], dst[w], ssem.at[w], rsem.at[w], sib).wait_recv()

    return pl.pallas_call(
        body, name="share_with_sibling", in_specs=[ANY] * n, out_specs=[ANY] * n,
        out_shape=[jax.ShapeDtypeStruct(a.shape, a.dtype) for a in fs],
        input_output_aliases={w: w for w in range(n)},
        scratch_shapes=[pltpu.SemaphoreType.DMA((n,))] * 2,
    )(*fs)


def allgather_rows(v):
    m_per, ncol = v.shape

    def body(x_ref, out_ref, send_sems, recv_sems, local_sem):
        x, y, c = lax.axis_index("x"), lax.axis_index("y"), lax.axis_index("c")
        me, sibling = (x, y, c), (x, y, 1 - c)
        chips = [(1 - x, y), (x, 1 - y), (1 - x, 1 - y)]

        def rows(px, py, pc):
            return out_ref.at[pl.ds((4 * px + 2 * py + pc) * m_per, m_per), :]

        def copy(k, block, to, src=None):
            return _rcopy(rows(*block) if src is None else src, rows(*block), send_sems.at[k], recv_sems.at[k], to)

        mine = pltpu.make_async_copy(x_ref, rows(*me), local_sem)
        mine.start()
        first = [copy(0, me, sibling, src=x_ref)]
        first += [copy(1 + j, me, (*chip, c), src=x_ref) for j, chip in enumerate(chips)]
        for cp in first:
            cp.start()
        passed = [copy(4 + j, (*chip, c), sibling) for j, chip in enumerate(chips)]
        for j, chip in enumerate(chips):
            copy(1 + j, (*chip, c), me).wait_recv()
            passed[j].start()
        copy(0, sibling, me).wait_recv()
        for j, chip in enumerate(chips):
            copy(4 + j, (*chip, 1 - c), me).wait_recv()
        for cp in first + passed:
            cp.wait_send()
        mine.wait()

    return pl.pallas_call(
        body, name="allgather_rows", out_shape=jax.ShapeDtypeStruct((8 * m_per, ncol), v.dtype),
        in_specs=[pl.BlockSpec(memory_space=pltpu.VMEM)], out_specs=pl.BlockSpec(memory_space=pltpu.VMEM),
        scratch_shapes=[pltpu.SemaphoreType.DMA((7,)), pltpu.SemaphoreType.DMA((7,)), pltpu.SemaphoreType.DMA],
        compiler_params=pltpu.CompilerParams(vmem_limit_bytes=V7X_VMEM_LIMIT),
    )(v)


def sum_blocks(name, a, nblk):
    m = a.shape[0] // nblk

    def body(a_ref, o_ref):
        acc = a_ref[pl.ds(0, m), :]
        for b in range(1, nblk):
            acc = acc + a_ref[pl.ds(b * m, m), :]
        o_ref[...] = acc

    return pl.pallas_call(body, name=name, out_shape=jax.ShapeDtypeStruct((m, a.shape[1]), a.dtype),
                          compiler_params=pltpu.CompilerParams(vmem_limit_bytes=V7X_VMEM_LIMIT))(a)


BIG = ("w_in", "w_branch_a", "w_branch_b", "w_mix_out", "w_xq", "w_xkv", "w_xo", "w_up", "w_down")
GAINS = ("mix_pre_norm", "gdn_norm", "pool_scale", "mix_post_norm", "xa_pre_norm", "mem_norm", "xa_post_norm",
         "ffn_pre_norm", "ffn_post_norm")
WEIGHTS = ("mix_pre_norm", "w_in", "conv_qkv", "a_log", "dt_bias", "gdn_norm", "pool_w", "pool_scale", "w_branch_a",
           "w_branch_b", "w_mix_out", "mix_post_norm", "xa_pre_norm", "mem_norm", "w_xq", "w_xkv", "w_xo", "xa_post_norm",
           "ffn_pre_norm", "w_up", "ffn_conv_w", "ffn_conv_b", "w_down", "ffn_post_norm")


def _rows128(vecs):
    flat = jnp.concatenate([v.reshape(-1) for v in vecs])
    m = -(-flat.shape[0] // 1024) * 8
    return jnp.pad(flat, (0, m * 128 - flat.shape[0])).reshape(m, 128)


def _unrows(a, shapes):
    flat, out, pos = a.reshape(-1), [], 0
    for sh in shapes:
        n = 1
        for d in sh:
            n *= d
        out.append(flat[pos:pos + n].reshape(sh))
        pos += n
    return out


def _lane128(v):
    return jnp.pad(v.reshape(1, GDN_HEADS), ((0, 0), (GDN_HEADS, 128 - 2 * GDN_HEADS)))


def kernel(x, mem, mix_pre_norm, w_in, conv_qkv, a_log, dt_bias, gdn_norm, pool_w, pool_scale, w_branch_a, w_branch_b, w_mix_out, mix_post_norm, xa_pre_norm, mem_norm, w_xq, w_xkv, w_xo, xa_post_norm, ffn_pre_norm, w_up, ffn_conv_w, ffn_conv_b, w_down, ffn_post_norm, loss_target, m_mix_pre_norm, m_w_in, m_conv_qkv, m_a_log, m_dt_bias, m_gdn_norm, m_pool_w, m_pool_scale, m_w_branch_a, m_w_branch_b, m_w_mix_out, m_mix_post_norm, m_xa_pre_norm, m_mem_norm, m_w_xq, m_w_xkv, m_w_xo, m_xa_post_norm, m_ffn_pre_norm, m_w_up, m_ffn_conv_w, m_ffn_conv_b, m_w_down, m_ffn_post_norm, v_mix_pre_norm, v_w_in, v_conv_qkv, v_a_log, v_dt_bias, v_gdn_norm, v_pool_w, v_pool_scale, v_w_branch_a, v_w_branch_b, v_w_mix_out, v_mix_post_norm, v_xa_pre_norm, v_mem_norm, v_w_xq, v_w_xkv, v_w_xo, v_xa_post_norm, v_ffn_pre_norm, v_w_up, v_ffn_conv_w, v_ffn_conv_b, v_w_down, v_ffn_post_norm):
    given = dict(locals())
    w = {n: given[n][0] for n in WEIGHTS}
    cx, cy, cc = lax.axis_index("x"), lax.axis_index("y"), lax.axis_index("c")
    chip = 2 * cx + cy

    gathered = gather_weights([shard_to_slab(n, w[n]) for n in BIG])
    W = {n: slabs_to_weight(n, g) for n, g in zip(BIG, gathered)}
    sharded_small = (w["conv_qkv"], w["ffn_conv_w"], w["pool_w"])
    allv = allgather_rows(_rows128(sharded_small))
    per_chip = allv.reshape(8, -1)[0::2]
    parts = [_unrows(per_chip[t], [a.shape for a in sharded_small]) for t in range(4)]
    sp = {n: w[n].reshape(1, -1) for n in GAINS}
    sp["a_log"], sp["dt_bias"] = _lane128(w["a_log"]), _lane128(w["dt_bias"])
    sp["conv_qkv"] = jnp.concatenate([p[0] for p in parts], axis=1)
    sp["ffn_conv_w"] = jnp.pad(jnp.stack([p[1] for p in parts]), ((0, 0), (0, 0), (0, FF_BLK - FF_COLS)))
    sp["pool_w"] = jnp.concatenate([p[2] for p in parts], axis=1).reshape(4 * 256, 256)
    sp["ffn_conv_b"] = jnp.pad(w["ffn_conv_b"].reshape(4, 1, FF_COLS), ((0, 0), (0, 0), (0, FF_BLK - FF_COLS)))

    loss, grad_x, gw, gs = local_step(x[0], mem[0], loss_target[0], W, sp)
    loss = lax.psum(loss[0, 0], ("x", "y", "c"))

    g4 = [grad_to_slabs(n, gw) for n in BIG]
    gsib = pair_exchange(g4)
    sums = [pair_add("pair_add_" + n, a, b, cc) for n, a, b in zip(BIG, g4, gsib)]
    recv = scatter_partials([sb for _, sb in sums])
    fin = [final_sum("final_sum_" + n, sf, rv, chip) for n, (sf, _), rv in zip(BIG, sums, recv)]
    fin = share_with_sibling(fin)
    grads = {n: slab_to_shard_grad(n, f) for n, f in zip(BIG, fin)}

    small_names = GAINS + ("a_log", "dt_bias", "ffn_conv_b", "conv_qkv", "ffn_conv_w", "pool_w")
    vec = _rows128([gs[n] for n in small_names])
    total = sum_blocks("sum_small", allgather_rows(vec), 8)
    tot = dict(zip(small_names, _unrows(total, [gs[n].shape for n in small_names])))
    for n in GAINS:
        grads[n] = tot[n].reshape(-1)
    grads["a_log"] = tot["a_log"][0, GDN_HEADS:2 * GDN_HEADS]
    grads["dt_bias"] = tot["dt_bias"][0, GDN_HEADS:2 * GDN_HEADS]
    grads["ffn_conv_b"] = tot["ffn_conv_b"][:, 0, :FF_COLS].reshape(-1)
    grads["conv_qkv"] = lax.dynamic_slice_in_dim(tot["conv_qkv"], chip * 1536, 1536, axis=1)
    grads["ffn_conv_w"] = lax.dynamic_index_in_dim(tot["ffn_conv_w"], chip, axis=0, keepdims=False)[:, :FF_COLS]
    grads["pool_w"] = lax.dynamic_slice_in_dim(tot["pool_w"].reshape(4, 256, 256), chip * 64, 64, axis=1)

    delta, new_m, new_v = {}, {}, {}
    for n in WEIGHTS:
        shp = w[n].shape
        two = (lambda a: a.reshape(-1, shp[-1])) if len(shp) > 1 else (lambda a: a.reshape(1, -1))
        d, nm, nv = adamw("adamw_" + n, two(w[n]), two(grads[n]), two(given["m_" + n][0]), two(given["v_" + n][0]))
        delta[n], new_m[n], new_v[n] = (a.reshape((1,) + shp) for a in (d, nm, nv))
    out_g = [grads[n].reshape((1,) + w[n].shape) for n in WEIGHTS]
    return (loss, grad_x[None], *out_g, *[delta[n] for n in WEIGHTS], *[new_m[n] for n in WEIGHTS], *[new_v[n] for n in WEIGHTS])
```

```python
import functools

import jax
import jax.numpy as jnp
from jax import lax
from jax.experimental import pallas as pl
from jax.experimental.pallas import tpu as pltpu

F32 = jnp.float32
BF16 = jnp.bfloat16

D_MODEL = 2048
CHUNK = 64
GDN_HEADS = 16
GDN_DK = 128
POOL_WINDOWS = (2, 4, 8, 16)
XA_HEADS = 4
XA_HEAD_DIM = D_MODEL // XA_HEADS
EPS = 1e-6
GDN_SPAN = 256
GDN_SPAN_BWD = 128
HALO = 16
V7X_VMEM_LIMIT = 56 * 1024 * 1024

ADAM_LR, ADAM_B1, ADAM_B2, ADAM_EPS, ADAM_WD, ADAM_STEP = 0.001, 0.9, 0.999, 1e-08, 0.01, 10

_NN = ((1,), (0,))
_NT = ((1,), (1,))
_TN = ((0,), (0,))


def _cparams(sem):
    return pltpu.CompilerParams(dimension_semantics=sem, vmem_limit_bytes=V7X_VMEM_LIMIT)


def _dg(a, b, dims, prec=None):
    return lax.dot_general(a, b, (dims, ((), ())), precision=prec, preferred_element_type=F32)


def _make_dots(cast, prec):
    def raw(dims, a, b):
        return _dg(cast(a), cast(b), dims, prec)

    @jax.custom_vjp
    def nn(a, b):
        return raw(_NN, a, b)

    @jax.custom_vjp
    def nt(a, b):
        return raw(_NT, a, b)

    @jax.custom_vjp
    def tn(a, b):
        return raw(_TN, a, b)

    nn.defvjp(lambda a, b: (raw(_NN, a, b), (a, b)), lambda r, g: (nt(g, r[1]), tn(r[0], g)))
    nt.defvjp(lambda a, b: (raw(_NT, a, b), (a, b)), lambda r, g: (nn(g, r[1]), tn(g, r[0])))
    tn.defvjp(lambda a, b: (raw(_TN, a, b), (a, b)), lambda r, g: (nt(r[1], g), nn(r[0], g)))
    return nn, nt, tn


bdot_nn, bdot_nt, bdot_tn = _make_dots(lambda x: x.astype(BF16), None)
hdot_nn, hdot_nt, hdot_tn = _make_dots(lambda x: x, lax.Precision.HIGHEST)


@functools.partial(jax.custom_vjp, nondiff_argnums=(1,))
def shift_rows(x, k):
    return pltpu.roll(x, k, 0)


def _shift_rows_fwd(x, k):
    return pltpu.roll(x, k, 0), None


def _shift_rows_bwd(k, _, g):
    return (pltpu.roll(g, g.shape[0] - k, 0),)


shift_rows.defvjp(_shift_rows_fwd, _shift_rows_bwd)


@functools.partial(jax.custom_vjp, nondiff_argnums=(1,))
def drop_head(x, h):
    return x[h:]


def _drop_head_fwd(x, h):
    return x[h:], None


def _drop_head_bwd(h, _, g):
    return (jnp.concatenate([jnp.zeros((h,) + g.shape[1:], g.dtype), g], axis=0),)


drop_head.defvjp(_drop_head_fwd, _drop_head_bwd)


@functools.partial(jax.custom_vjp, nondiff_argnums=(1,))
def split_lanes(x, n):
    w = x.shape[-1] // n
    return tuple(x[:, i * w:(i + 1) * w] for i in range(n))


def _split_lanes_fwd(x, n):
    return split_lanes(x, n), None


def _split_lanes_bwd(n, _, gs):
    return (jnp.concatenate(list(gs), axis=-1),)


split_lanes.defvjp(_split_lanes_fwd, _split_lanes_bwd)


@functools.partial(jax.custom_vjp, nondiff_argnums=(1,))
def split_rows(x, n):
    h = x.shape[0] // n
    return tuple(x[i * h:(i + 1) * h] for i in range(n))


def _split_rows_fwd(x, n):
    return split_rows(x, n), None


def _split_rows_bwd(n, _, gs):
    return (jnp.concatenate(list(gs), axis=0),)


split_rows.defvjp(_split_rows_fwd, _split_rows_bwd)


def row_of(w, j):
    rid = lax.broadcasted_iota(jnp.int32, w.shape, 0)
    return jnp.sum(jnp.where(rid == j, w, 0.0), axis=0, keepdims=True)


def sigmoid(x):
    return 1.0 / (1.0 + jnp.exp(-x))


def silu(x):
    return x * sigmoid(x)


def softplus(x):
    return jnp.maximum(x, 0.0) + jnp.log(1.0 + jnp.exp(-jnp.abs(x)))


def rms(x, g):
    return x * lax.rsqrt(jnp.mean(x * x, axis=-1, keepdims=True) + EPS) * g


def mm(name, dims, pairs, out_shape, out_spec, grid):
    nk = grid[2]
    acc_in_out = nk > 1 and out_shape.dtype == F32
    npair = len(pairs)

    def body(*refs):
        o_ref = refs[2 * npair]
        part = None
        for p in range(npair):
            d = _dg(refs[2 * p][...].astype(BF16), refs[2 * p + 1][...].astype(BF16), dims)
            part = d if part is None else part + d
        if nk == 1:
            o_ref[...] = part.astype(o_ref.dtype)
            return
        acc = o_ref if acc_in_out else refs[2 * npair + 1]
        k = pl.program_id(2)

        @pl.when(k == 0)
        def _():
            acc[...] = part

        @pl.when(k > 0)
        def _():
            acc[...] += part

        if not acc_in_out:
            @pl.when(k == nk - 1)
            def _():
                o_ref[...] = acc[...].astype(o_ref.dtype)

    scratch = []
    if nk > 1 and not acc_in_out:
        scratch = [pltpu.VMEM(tuple(d for d in out_spec.block_shape if d is not None), F32)]
    in_specs, operands = [], []
    for a, b, a_spec, b_spec in pairs:
        in_specs += [a_spec, b_spec]
        operands += [a, b]
    return pl.pallas_call(
        body, name=name, grid=grid, in_specs=in_specs, out_specs=out_spec, out_shape=out_shape,
        scratch_shapes=scratch, compiler_params=_cparams(("parallel", "parallel", "arbitrary")),
    )(*operands)


def _bs(shape, fn):
    return pl.BlockSpec(shape, fn)


class Tile:
    def __init__(self, arr, w, cb=0, lead=None, halo=False):
        self.arr, self.w, self.cb, self.lead, self.halo = arr, w, cb, lead, halo


class Out:
    def __init__(self, shape, dtype, w, cb=0, lead=None, into=None):
        self.shape, self.dtype, self.w, self.cb, self.lead, self.into = shape, dtype, w, cb, lead, into


class Par:
    def __init__(self, arr, lead=None):
        self.arr, self.lead = arr, lead


def _spec(rows, w, cb, lead, tile_of):
    if lead is None:
        return pl.BlockSpec((rows, w), lambda o, i: (tile_of(i), cb))
    return pl.BlockSpec((None, rows, w), lambda o, i: (lead(o), tile_of(i), cb))


def _par_spec(p):
    if p.lead is None:
        return pl.BlockSpec(p.arr.shape, lambda o, i: (0, 0))
    return pl.BlockSpec((None,) + p.arr.shape[1:], lambda o, i: (p.lead(o), 0, 0))


def run_stage(name, fn, tm, ins, pars, outs, *, outer=1, cts=None, dins=None):
    T = ins[0].arr.shape[-2]
    nt = T // tm
    bwd = cts is not None
    any_halo = any(t.halo for t in ins)
    step_tile = (lambda i: nt - 1 - i) if bwd else (lambda i: i)
    hb = tm // HALO

    in_specs, operands = [], []
    for t in ins:
        if t.halo:
            in_specs.append(_spec(HALO, t.w, t.cb, t.lead, lambda i: jnp.maximum(step_tile(i) * hb - 1, 0)))
            operands.append(t.arr)
        in_specs.append(_spec(tm, t.w, t.cb, t.lead, step_tile))
        operands.append(t.arr)
    for p in pars:
        in_specs.append(_par_spec(p))
        operands.append(p.arr)
    n_in_refs = len(operands)

    out_descs = list(outs) if not bwd else [d for d in dins if d is not None]
    aliases = {}
    if bwd:
        for c, o in zip(cts, outs):
            in_specs.append(_spec(tm, o.w, o.cb, o.lead, step_tile))
            operands.append(c)
    n_ct = len(operands) - n_in_refs
    for k, o in enumerate(out_descs):
        if o.into is not None:
            aliases[len(operands)] = k
            in_specs.append(pl.BlockSpec(memory_space=pl.ANY))
            operands.append(o.into)
    out_specs = [_spec(tm, o.w, o.cb, o.lead, step_tile) for o in out_descs]
    out_shapes = [jax.ShapeDtypeStruct(o.shape, o.dtype) for o in out_descs]
    if bwd:
        for p in pars:
            out_specs.append(_par_spec(p))
            out_shapes.append(jax.ShapeDtypeStruct(p.arr.shape, F32))
    scratch = []
    if bwd and any_halo:
        scratch = [pltpu.VMEM((HALO, t.w), F32) for t, d in zip(ins, dins) if t.halo and d is not None]

    def body(*refs):
        i = pl.program_id(1)
        tile = step_tile(i)
        row0 = tile * tm
        pos = 0
        tiles = []
        for t in ins:
            if t.halo:
                prev = jnp.where(tile > 0, refs[pos][...].astype(F32), 0.0)
                tiles.append(jnp.concatenate([prev, refs[pos + 1][...].astype(F32)], axis=0))
                pos += 2
            else:
                tiles.append(refs[pos][...].astype(F32))
                pos += 1
        pvals = [refs[pos + k][...].astype(F32) for k in range(len(pars))]
        pos += len(pars)
        if not bwd:
            res = fn(tiles, pvals, row0)
            for o_ref, r in zip(refs[pos:], res):
                o_ref[...] = r.astype(o_ref.dtype)
            return
        ct_vals = [refs[pos + k][...].astype(F32) for k in range(n_ct)]
        pos += n_ct + len(aliases)
        _, vjp_fn = jax.vjp(lambda tt, pp: fn(tt, pp, row0), tiles, pvals)
        d_tiles, d_pars = vjp_fn(ct_vals)
        carries = list(refs[len(refs) - len(scratch):])
        for t, d, dt in zip(ins, dins, d_tiles):
            if d is None:
                continue
            o_ref = refs[pos]
            pos += 1
            if t.halo:
                carry = carries.pop(0)
                main = dt[HALO:]
                tail = main[tm - HALO:] + jnp.where(i > 0, carry[...], 0.0)
                o_ref[...] = jnp.concatenate([main[:tm - HALO], tail], axis=0).astype(o_ref.dtype)
                carry[...] = dt[:HALO]
            else:
                o_ref[...] = dt.astype(o_ref.dtype)
        for dp in d_pars:
            acc = refs[pos]
            pos += 1

            @pl.when(i == 0)
            def _(acc=acc, dp=dp):
                acc[...] = dp

            @pl.when(i > 0)
            def _(acc=acc, dp=dp):
                acc[...] += dp

    res = pl.pallas_call(
        body, name=name, grid=(outer, nt), in_specs=in_specs, out_specs=out_specs, out_shape=out_shapes,
        scratch_shapes=scratch, input_output_aliases=aliases,
        compiler_params=_cparams(("arbitrary", "arbitrary")),
    )(*operands)
    if not bwd:
        return list(res)
    n_d = len(out_descs)
    d_full, it = [], iter(res[:n_d])
    for d in dins:
        d_full.append(None if d is None else next(it))
    return d_full, list(res[n_d:])


def f_prenorm(t, p, row0):
    return [rms(t[0], p[0])]


def f_prenorm_res(t, p, row0):
    return [t[0], rms(t[0], p[0])]


def f_post_pre(t, p, row0):
    x, y = t
    x1 = x + rms(y, p[0])
    return [x1, rms(x1, p[1])]


def _causal_conv(x, w, taps):
    y = x * row_of(w, taps - 1)
    for j in range(taps - 1):
        y = y + shift_rows(x, taps - 1 - j) * row_of(w, j)
    return drop_head(y, HALO)


def _l2(x):
    return x * lax.rsqrt(jnp.sum(x * x, axis=-1, keepdims=True) + EPS)


def make_f_convhead(scale, normalise):
    def f(t, p, row0):
        y = silu(_causal_conv(t[0], p[0], 4))
        if not normalise:
            return [y]
        return [jnp.concatenate([_l2(c) * scale for c in split_lanes(y, GDN_HEADS)], axis=-1)]
    return f


def f_bg(t, p, row0):
    ba = split_lanes(t[0], 4)[0]
    alog, dtb = p
    lane = lax.broadcasted_iota(jnp.int32, ba.shape, 1)
    bg = jnp.where(lane < GDN_HEADS, sigmoid(ba), -jnp.exp(alog) * softplus(ba + dtb))
    return [jnp.where(lane < 2 * GDN_HEADS, bg, 0.0)]


def f_gnorm(t, p, row0):
    o, z = t
    po, pz = split_lanes(o, GDN_HEADS), split_lanes(z, GDN_HEADS)
    return [jnp.concatenate([rms(a, p[0]) * silu(b) for a, b in zip(po, pz)], axis=-1)]


def f_pool(t, p, row0):
    x = t[0]
    pw, psc = p
    tm = x.shape[0] - HALO
    tpos = (row0 + lax.broadcasted_iota(jnp.int32, (tm, 1), 0) + 1).astype(F32)
    outs = []
    for xg, wg, win in zip(split_lanes(x, 4), split_rows(pw, 4), POOL_WINDOWS):
        s, span = xg, 1
        while span < win:
            s = s + shift_rows(s, span)
            span *= 2
        mean = drop_head(s, HALO) / jnp.minimum(tpos, float(win))
        outs.append(bdot_nn(mean - drop_head(xg, HALO), wg))
    return [jnp.concatenate(outs, axis=-1) * psc]


def f_merge(t, p, row0):
    gates, ya, yb = t
    ga, gb = split_lanes(gates, 2)
    return [sigmoid(ga) * ya + sigmoid(gb) * yb]


def f_xattn(t, p, row0):
    k, v = p
    outs = []
    for qh, kh, vh in zip(split_lanes(t[0], XA_HEADS), split_lanes(k, XA_HEADS), split_lanes(v, XA_HEADS)):
        s = bdot_nt(qh, kh) * (XA_HEAD_DIM ** -0.5)
        s = s - jnp.max(s, axis=-1, keepdims=True)
        e = jnp.exp(s)
        outs.append(bdot_nn(e / jnp.sum(e, axis=-1, keepdims=True), vh))
    return [jnp.concatenate(outs, axis=-1)]


def f_convglu(t, p, row0):
    ua, ub = t
    cwa, cwb, ba, bb = p
    return [silu(_causal_conv(ua, cwa, 3) + ba) * (_causal_conv(ub, cwb, 3) + bb)]


_BNN = (((2,), (1,)), ((0,), (0,)))
_BNT = (((2,), (2,)), ((0,), (0,)))
_BTN = (((1,), (1,)), ((0,), (0,)))


def _make_batched_dots():
    def raw(dims, a, b):
        return lax.dot_general(a.astype(BF16), b.astype(BF16), dims, preferred_element_type=F32)

    @jax.custom_vjp
    def nn(a, b):
        return raw(_BNN, a, b)

    @jax.custom_vjp
    def nt(a, b):
        return raw(_BNT, a, b)

    @jax.custom_vjp
    def tn(a, b):
        return raw(_BTN, a, b)

    nn.defvjp(lambda a, b: (raw(_BNN, a, b), (a, b)), lambda r, g: (nt(g, r[1]), tn(r[0], g)))
    nt.defvjp(lambda a, b: (raw(_BNT, a, b), (a, b)), lambda r, g: (nn(g, r[1]), tn(g, r[0])))
    tn.defvjp(lambda a, b: (raw(_BTN, a, b), (a, b)), lambda r, g: (nt(r[1], g), nn(r[0], g)))
    return nn, nt, tn


bb_nn, bb_nt, bb_tn = _make_batched_dots()


def _gdn_local(q, k, v, gcol, bcol):
    C = CHUNK
    r = lax.broadcasted_iota(jnp.int32, (1, C, C), 1)
    c = lax.broadcasted_iota(jnp.int32, (1, C, C), 2)
    eye, incl, strict = r == c, r >= c, r > c
    grow = jnp.sum(jnp.where(eye, gcol, 0.0), axis=1, keepdims=True)
    Gcol = jnp.sum(jnp.where(incl, grow, 0.0), axis=2, keepdims=True)
    Grow = jnp.sum(jnp.where(eye, Gcol, 0.0), axis=1, keepdims=True)
    decay = jnp.where(incl, jnp.exp(jnp.where(incl, Gcol - Grow, 0.0)), 0.0)
    X = -jnp.where(strict, bcol * decay * bb_nt(k, k), 0.0)
    N, P = X, X
    for _ in range(5):
        P = bb_nn(P, P)
        N = N + P + bb_nn(N, P)
    expg = jnp.exp(Gcol)
    rv, rk = bcol * v, (bcol * expg) * k
    u_v = rv + bb_nn(N, rv)
    w_k = rk + bb_nn(N, rk)
    attn = decay * bb_nt(q, k)
    rid = lax.broadcasted_iota(jnp.int32, (1, C, 1), 1)
    glast = jnp.sum(jnp.where(rid == C - 1, Gcol, 0.0), axis=1, keepdims=True)
    return u_v, w_k, attn, q * expg, k * jnp.exp(glast - Gcol), jnp.exp(glast)


def _gdn_rec(u_v, w_k, attn, q_dec, k_dec, cd, S):
    u = u_v - bb_nn(w_k, S)
    o = bb_nn(q_dec, S) + bb_nn(attn, u)
    return o, cd * S + bb_tn(k_dec, u)


def _gdn_load(q_ref, k_ref, v_ref, bg_ref, nch):
    H = GDN_HEADS

    def batched(ref):
        return jnp.stack([ref[c * CHUNK:(c + 1) * CHUNK, h * GDN_DK:(h + 1) * GDN_DK] for c in range(nch) for h in range(H)])

    bg = bg_ref[...]
    lane = lax.broadcasted_iota(jnp.int32, bg.shape, 1)
    bcols = [jnp.sum(jnp.where(lane == h, bg, 0.0), axis=-1, keepdims=True) for h in range(H)]
    gcols = [jnp.sum(jnp.where(lane == H + h, bg, 0.0), axis=-1, keepdims=True) for h in range(H)]
    pick = lambda cols: jnp.stack([cols[h][c * CHUNK:(c + 1) * CHUNK] for c in range(nch) for h in range(H)])
    return batched(q_ref), batched(k_ref), batched(v_ref), pick(gcols), pick(bcols)


def _gdn_store(ref, val, nch):
    H = GDN_HEADS
    for c in range(nch):
        for h in range(H):
            ref[c * CHUNK:(c + 1) * CHUNK, h * GDN_DK:(h + 1) * GDN_DK] = val[c * H + h]


def gdn_forward(q, k, v, bg, span):
    T = q.shape[0]
    ns, nch, H = T // span, span // CHUNK, GDN_HEADS

    def body(q_ref, k_ref, v_ref, bg_ref, o_ref, s_ref, state):
        @pl.when(pl.program_id(0) == 0)
        def _():
            state[...] = jnp.zeros(state.shape, F32)

        loc = _gdn_local(*_gdn_load(q_ref, k_ref, v_ref, bg_ref, nch))
        S = state[...]
        for c in range(nch):
            s_ref[c] = S
            o, S = _gdn_rec(*[a[c * H:(c + 1) * H] for a in loc], S)
            for h in range(H):
                o_ref[c * CHUNK:(c + 1) * CHUNK, h * GDN_DK:(h + 1) * GDN_DK] = o[h]
        state[...] = S

    wide = pl.BlockSpec((span, H * GDN_DK), lambda s: (s, 0))
    return pl.pallas_call(
        body, name="gdn_fwd", grid=(ns,),
        in_specs=[wide, wide, wide, pl.BlockSpec((span, 128), lambda s: (s, 0))],
        out_specs=[wide, pl.BlockSpec((nch, H, GDN_DK, GDN_DK), lambda s: (s, 0, 0, 0))],
        out_shape=[jax.ShapeDtypeStruct((T, H * GDN_DK), F32), jax.ShapeDtypeStruct((T // CHUNK, H, GDN_DK, GDN_DK), F32)],
        scratch_shapes=[pltpu.VMEM((H, GDN_DK, GDN_DK), F32)],
        compiler_params=_cparams(("arbitrary",)),
    )(q, k, v, bg)


def gdn_backward(q, k, v, bg, starts, do, span):
    T = q.shape[0]
    ns, nch, H = T // span, span // CHUNK, GDN_HEADS

    def body(q_ref, k_ref, v_ref, bg_ref, s_ref, do_ref, dq_ref, dk_ref, dv_ref, dbg_ref, dstate):
        @pl.when(pl.program_id(0) == 0)
        def _():
            dstate[...] = jnp.zeros(dstate.shape, F32)

        loc, vjp_loc = jax.vjp(_gdn_local, *_gdn_load(q_ref, k_ref, v_ref, bg_ref, nch))
        dS = dstate[...]
        dloc = [None] * nch
        for c in reversed(range(nch)):
            _, vjp_rec = jax.vjp(_gdn_rec, *[a[c * H:(c + 1) * H] for a in loc], s_ref[c])
            do_c = jnp.stack([do_ref[c * CHUNK:(c + 1) * CHUNK, h * GDN_DK:(h + 1) * GDN_DK] for h in range(H)])
            *dloc[c], dS = vjp_rec((do_c, dS))
        dstate[...] = dS
        dq, dk, dv, dg, db = vjp_loc(tuple(jnp.concatenate([dloc[c][i] for c in range(nch)], axis=0) for i in range(6)))
        _gdn_store(dq_ref, dq, nch)
        _gdn_store(dk_ref, dk, nch)
        _gdn_store(dv_ref, dv, nch)
        lane = lax.broadcasted_iota(jnp.int32, (CHUNK, 128), 1)
        for c in range(nch):
            acc = jnp.zeros((CHUNK, 128), F32)
            for h in range(H):
                acc = acc + jnp.where(lane == h, db[c * H + h], 0.0) + jnp.where(lane == H + h, dg[c * H + h], 0.0)
            dbg_ref[c * CHUNK:(c + 1) * CHUNK, :] = acc

    wide = pl.BlockSpec((span, H * GDN_DK), lambda i: (ns - 1 - i, 0))
    bgs = pl.BlockSpec((span, 128), lambda i: (ns - 1 - i, 0))
    return pl.pallas_call(
        body, name="gdn_bwd", grid=(ns,),
        in_specs=[wide, wide, wide, bgs, pl.BlockSpec((nch, H, GDN_DK, GDN_DK), lambda i: (ns - 1 - i, 0, 0, 0)), wide],
        out_specs=[wide, wide, wide, bgs],
        out_shape=[jax.ShapeDtypeStruct((T, H * GDN_DK), F32)] * 3 + [jax.ShapeDtypeStruct((T, 128), F32)],
        scratch_shapes=[pltpu.VMEM((H, GDN_DK, GDN_DK), F32)],
        compiler_params=_cparams(("arbitrary",)),
    )(q, k, v, bg, starts, do)


def loss_stage(x2, y3, tgt, g, tm):
    T, D = x2.shape
    nt = T // tm

    def body(x_ref, y_ref, t_ref, g_ref, loss_ref, dx_ref, dy_ref, dg_ref):
        i = pl.program_id(0)
        tgtv = t_ref[...]

        def f(x, y, gg):
            err = x + rms(y, gg) - tgtv
            return 0.5 * jnp.mean(err * err, axis=-1, keepdims=True)

        rows, vjp_fn = jax.vjp(f, x_ref[...], y_ref[...], g_ref[...])
        dx, dy, dg = vjp_fn(jnp.ones_like(rows))
        dx_ref[...] = dx
        dy_ref[...] = dy.astype(dy_ref.dtype)
        part = jnp.sum(rows, axis=0, keepdims=True)

        @pl.when(i == 0)
        def _():
            loss_ref[...] = part
            dg_ref[...] = dg

        @pl.when(i > 0)
        def _():
            loss_ref[...] += part
            dg_ref[...] += dg

    tile = pl.BlockSpec((tm, D), lambda i: (i, 0))
    gs = pl.BlockSpec((1, D), lambda i: (0, 0))
    return pl.pallas_call(
        body, name="loss_head", grid=(nt,), in_specs=[tile, tile, tile, gs],
        out_specs=[pl.BlockSpec((1, 1), lambda i: (0, 0)), tile, tile, gs],
        out_shape=[jax.ShapeDtypeStruct((1, 1), F32), jax.ShapeDtypeStruct((T, D), F32),
                   jax.ShapeDtypeStruct((T, D), BF16), jax.ShapeDtypeStruct((1, D), F32)],
        compiler_params=_cparams(("arbitrary",)),
    )(x2, y3, tgt, g)


def adamw(name, w, g, m, v):
    R, C = w.shape
    tr = R
    for cand in (256, 128, 64, 32, 16, 8):
        if R % cand == 0 and R > cand and cand * C * 4 <= (2 << 20):
            tr = cand
            break
    c1 = 1.0 / (1.0 - ADAM_B1 ** ADAM_STEP)
    c2 = 1.0 / (1.0 - ADAM_B2 ** ADAM_STEP)

    def body(w_ref, g_ref, m_ref, v_ref, d_ref, nm_ref, nv_ref):
        gg = g_ref[...]
        nm = ADAM_B1 * m_ref[...] + (1.0 - ADAM_B1) * gg
        nv = ADAM_B2 * v_ref[...] + (1.0 - ADAM_B2) * (gg * gg)
        d_ref[...] = -ADAM_LR * ((nm * c1) / (jnp.sqrt(nv * c2) + ADAM_EPS) + ADAM_WD * w_ref[...])
        nm_ref[...] = nm
        nv_ref[...] = nv

    spec = pl.BlockSpec((tr, C), lambda i: (i, 0))
    return pl.pallas_call(
        body, name=name, grid=(R // tr,), in_specs=[spec] * 4, out_specs=[spec] * 3,
        out_shape=[jax.ShapeDtypeStruct((R, C), F32)] * 3, compiler_params=_cparams(("parallel",)),
    )(w, g, m, v)


C_Z, C_GATES, C_POOL, C_BA, N_PROJ = 6144, 8192, 12288, 13312, 13824
FF_BLK = 2816
FF_COLS = 2752


def local_step(x, mem, tgt, W, sp):
    T, D = x.shape
    tm = 256
    tmm = min(512, T)
    nI = T // tmm
    S = jax.ShapeDtypeStruct
    gw, gs = {}, {}

    def stage(name, fn, ins, pars, outs, **kw):
        return run_stage(name, fn, kw.pop("tm", tm), ins, pars, outs, **kw)

    def dense(name, a, w, out_dtype=F32, tn=512):
        Tq, Kd = a.shape
        N = w.shape[1]
        tq = min(tmm, Tq)
        return mm(name, _NN, [(a, w, _bs((tq, Kd), lambda j, i, k: (i, 0)), _bs((Kd, tn), lambda j, i, k: (0, j)))],
                  S((Tq, N), out_dtype), _bs((tq, tn), lambda j, i, k: (i, j)), (N // tn, Tq // tq, 1))

    def dense_t(name, g, w, out_dtype=F32, tn=512):
        Tq, N = g.shape
        Kd = w.shape[0]
        tq = min(tmm, Tq)
        return mm(name, _NT, [(g, w, _bs((tq, N), lambda j, i, k: (i, 0)), _bs((tn, N), lambda j, i, k: (j, 0)))],
                  S((Tq, Kd), out_dtype), _bs((tq, tn), lambda j, i, k: (i, j)), (Kd // tn, Tq // tq, 1))

    def wgrad(name, a, g, ta=512, tn=1024):
        Tq, Kd = a.shape
        N = g.shape[1]
        tt = min(tmm, Tq)
        return mm(name, _TN, [(a, g, _bs((tt, ta), lambda i, j, k: (k, i)), _bs((tt, tn), lambda i, j, k: (k, j)))],
                  S((Kd, N), F32), _bs((ta, tn), lambda i, j, k: (i, j)), (Kd // ta, N // tn, Tq // tt))

    o2048 = lambda dt: Out((T, D), dt, D)

    (h1,) = stage("pre1", f_prenorm, [Tile(x, D)], [Par(sp["mix_pre_norm"])], [o2048(BF16)])
    tnp = 1536
    P2 = mm("in_proj", _NT, [(h1, W["w_in"], _bs((tmm, D), lambda j, i, k: (i, 0)), _bs((tnp, D), lambda j, i, k: (j, 0)))],
            S((T, N_PROJ), F32), _bs((tmm, tnp), lambda j, i, k: (i, j)), (N_PROJ // tnp, nI, 1))
    cw = sp["conv_qkv"]
    cws = [cw[:, i * D:(i + 1) * D] for i in range(3)]
    f_heads = [make_f_convhead(GDN_DK ** -0.5, True), make_f_convhead(1.0, True), make_f_convhead(1.0, False)]
    qkv = [stage("conv_" + n, f_heads[i], [Tile(P2, D, cb=i, halo=True)], [Par(cws[i])], [o2048(F32)])[0]
           for i, n in enumerate("qkv")]
    ba_tile = Tile(P2, 512, cb=C_BA // 512)
    (bg,) = stage("bg", f_bg, [ba_tile], [Par(sp["a_log"]), Par(sp["dt_bias"])], [Out((T, 128), F32, 128)])
    o, s0 = gdn_forward(qkv[0], qkv[1], qkv[2], bg, GDN_SPAN)
    z_tile = Tile(P2, D, cb=C_Z // D)
    (o_n,) = stage("gnorm", f_gnorm, [Tile(o, D), z_tile], [Par(sp["gdn_norm"])], [o2048(BF16)])
    y_a = dense("branch_a", o_n, W["w_branch_a"])
    p_tile = Tile(P2, 1024, cb=C_POOL // 1024, halo=True)
    pool_pars = [Par(sp["pool_w"]), Par(sp["pool_scale"])]
    (pooled,) = stage("pool", f_pool, [p_tile], pool_pars, [Out((T, 1024), BF16, 1024)])
    y_b = mm("branch_b", _NN, [(pooled, W["w_branch_b"], _bs((tmm, 1024), lambda j, i, k: (i, 0)),
                                _bs((None, 1024, 512), lambda j, i, k: (j, 0, 0)))],
             S((T, D), F32), _bs((tmm, 512), lambda j, i, k: (i, j)), (4, nI, 1))
    gate_tile = Tile(P2, 2 * D, cb=C_GATES // (2 * D))
    merge_ins = [gate_tile, Tile(y_a, D), Tile(y_b, D)]
    (merged,) = stage("merge", f_merge, merge_ins, [], [o2048(BF16)])
    y1 = dense("mix_out", merged, W["w_mix_out"])
    pp1 = [Par(sp["mix_post_norm"]), Par(sp["xa_pre_norm"])]
    x1, h2 = stage("post1", f_post_pre, [Tile(x, D), Tile(y1, D)], pp1, [o2048(F32), o2048(BF16)])

    q2 = dense("xq", h2, W["w_xq"])
    (mn,) = stage("mem_norm", f_prenorm, [Tile(mem, D)], [Par(sp["mem_norm"])], [Out(mem.shape, BF16, D)], tm=mem.shape[0])
    M = mem.shape[0]
    kv = mm("xkv", _NN, [(mn, W["w_xkv"], _bs((M, D), lambda j, i, k: (0, 0)), _bs((None, D, 1024), lambda j, i, k: (j, 0, 0)))],
            S((M, 2 * D), F32), _bs((M, 1024), lambda j, i, k: (0, j)), (4, 1, 1))
    k2, v2 = kv[:, :D], kv[:, D:]
    xa_pars = [Par(k2), Par(v2)]
    (o2,) = stage("xattn", f_xattn, [Tile(q2, D)], xa_pars, [o2048(BF16)])
    y2 = dense("xo", o2, W["w_xo"])
    pp2 = [Par(sp["xa_post_norm"]), Par(sp["ffn_pre_norm"])]
    x2, h3 = stage("post2", f_post_pre, [Tile(x1, D), Tile(y2, D)], pp2, [o2048(F32), o2048(BF16)])

    def up(name, off):
        return mm(name, _NN, [(h3, W["w_up"], _bs((tmm, D), lambda j, i, k: (i, 0)),
                               _bs((None, D, FF_BLK), lambda j, i, k: (j + off, 0, 0)))],
                  S((2, T, FF_BLK), F32), _bs((None, tmm, FF_BLK), lambda j, i, k: (j, i, 0)), (2, nI, 1))

    Ua, Ub = up("up_a", 0), up("up_b", 2)
    ffn_ins = [Tile(Ua, FF_BLK, lead=lambda o: o, halo=True), Tile(Ub, FF_BLK, lead=lambda o: o, halo=True)]
    ffn_pars = [Par(sp["ffn_conv_w"], lead=lambda o: o), Par(sp["ffn_conv_w"], lead=lambda o: o + 2),
                Par(sp["ffn_conv_b"], lead=lambda o: o), Par(sp["ffn_conv_b"], lead=lambda o: o + 2)]
    ffn_out = [Out((2, T, FF_BLK), BF16, FF_BLK, lead=lambda o: o)]
    (ff,) = stage("convglu", f_convglu, ffn_ins, ffn_pars, ffn_out, outer=2)
    y3 = mm("down", _NN, [(ff, W["w_down"], _bs((None, tmm, FF_BLK), lambda i, j, k: (k, i, 0)),
                           _bs((None, FF_BLK, 512), lambda i, j, k: (k, 0, j)))],
            S((T, D), F32), _bs((tmm, 512), lambda i, j, k: (i, j)), (nI, D // 512, 2))
    loss, dx2, dy3, gs["ffn_post_norm"] = loss_stage(x2, y3, tgt, sp["ffn_post_norm"], tm)

    dff = mm("down_dx", _NT, [(dy3, W["w_down"], _bs((tmm, D), lambda j, i, k: (i, 0)),
                               _bs((None, FF_BLK, D), lambda j, i, k: (j, 0, 0)))],
             S((2, T, FF_BLK), BF16), _bs((None, tmm, FF_BLK), lambda j, i, k: (j, i, 0)), (2, nI, 1))
    gw["w_down"] = mm("down_dw", _TN, [(ff, dy3, _bs((None, tmm, FF_BLK), lambda b, j, k: (b, k, 0)),
                                        _bs((tmm, 1024), lambda b, j, k: (k, j)))],
                      S((2, FF_BLK, D), F32), _bs((None, FF_BLK, 1024), lambda b, j, k: (b, 0, j)), (2, D // 1024, nI))
    dU_out = [Out((2, T, FF_BLK), BF16, FF_BLK, lead=lambda o: o), Out((2, T, FF_BLK), BF16, FF_BLK, lead=lambda o: o)]
    (dUa, dUb), dffn = stage("convglu_bwd", f_convglu, ffn_ins, ffn_pars, ffn_out, outer=2, cts=[dff], dins=dU_out)
    gs["ffn_conv_w"] = jnp.concatenate([dffn[0][:2], dffn[1][2:]], axis=0)
    gs["ffn_conv_b"] = jnp.concatenate([dffn[2][:2], dffn[3][2:]], axis=0)
    dh3 = mm("up_dx", _NT, [(dUa, W["w_up"], _bs((None, tmm, FF_BLK), lambda i, j, k: (k, i, 0)),
                             _bs((None, 512, FF_BLK), lambda i, j, k: (k, j, 0))),
                            (dUb, W["w_up"], _bs((None, tmm, FF_BLK), lambda i, j, k: (k, i, 0)),
                             _bs((None, 512, FF_BLK), lambda i, j, k: (k + 2, j, 0)))],
             S((T, D), F32), _bs((tmm, 512), lambda i, j, k: (i, j)), (nI, D // 512, 2))

    def up_dw(name, dU):
        return mm(name, _TN, [(h3, dU, _bs((tmm, 512), lambda b, i, k: (k, i)), _bs((None, tmm, FF_BLK), lambda b, i, k: (b, k, 0)))],
                  S((2, D, FF_BLK), F32), _bs((None, 512, FF_BLK), lambda b, i, k: (b, i, 0)), (2, D // 512, nI))

    gw["w_up_a"], gw["w_up_b"] = up_dw("up_dw_a", dUa), up_dw("up_dw_b", dUb)
    (dx1, dy2), dpp2 = stage("post2_bwd", f_post_pre, [Tile(x1, D), Tile(y2, D)], pp2, [o2048(F32), o2048(BF16)],
                             cts=[dx2, dh3], dins=[o2048(F32), o2048(BF16)])
    gs["xa_post_norm"], gs["ffn_pre_norm"] = dpp2

    do2 = dense_t("xo_dx", dy2, W["w_xo"])
    gw["w_xo"] = wgrad("xo_dw", o2, dy2)
    (dq2,), (dk2, dv2) = stage("xattn_bwd", f_xattn, [Tile(q2, D)], xa_pars, [o2048(BF16)], cts=[do2], dins=[o2048(BF16)])
    dh2 = dense_t("xq_dx", dq2, W["w_xq"])
    gw["w_xq"] = wgrad("xq_dw", h2, dq2)
    dkv = jnp.concatenate([dk2, dv2], axis=1).astype(BF16)
    dmn = mm("xkv_dx", _NT, [(dkv, W["w_xkv"], _bs((M, 1024), lambda i, j, k: (0, k)), _bs((None, 512, 1024), lambda i, j, k: (k, j, 0)))],
             S((M, D), F32), _bs((M, 512), lambda i, j, k: (0, j)), (1, D // 512, 4))
    gw["w_xkv"] = mm("xkv_dw", _TN, [(mn, dkv, _bs((M, D), lambda b, j, k: (0, 0)), _bs((M, 1024), lambda b, j, k: (0, b)))],
                     S((4, D, 1024), F32), _bs((None, D, 1024), lambda b, j, k: (b, 0, 0)), (4, 1, 1))
    _, (gs["mem_norm"],) = stage("mem_norm_bwd", f_prenorm, [Tile(mem, D)], [Par(sp["mem_norm"])], [Out(mem.shape, BF16, D)],
                                 tm=M, cts=[dmn], dins=[None])
    (dx0, dy1), dpp1 = stage("post1_bwd", f_post_pre, [Tile(x, D), Tile(y1, D)], pp1, [o2048(F32), o2048(BF16)],
                             cts=[dx1, dh2], dins=[o2048(F32), o2048(BF16)])
    gs["mix_post_norm"], gs["xa_pre_norm"] = dpp1

    dmerged = dense_t("mix_out_dx", dy1, W["w_mix_out"])
    gw["w_mix_out"] = wgrad("mix_out_dw", merged, dy1)
    pshape = (T, N_PROJ)
    (dP2, dya, dyb), _ = stage("merge_bwd", f_merge, merge_ins, [], [o2048(BF16)], cts=[dmerged],
                               dins=[Out(pshape, BF16, 2 * D, cb=C_GATES // (2 * D)), o2048(BF16), o2048(BF16)])
    d_on = dense_t("branch_a_dx", dya, W["w_branch_a"])
    gw["w_branch_a"] = wgrad("branch_a_dw", o_n, dya)
    dpooled = mm("branch_b_dx", _NT, [(dyb, W["w_branch_b"], _bs((tmm, 512), lambda i, j, k: (i, k)),
                                       _bs((None, 1024, 512), lambda i, j, k: (k, 0, 0)))],
                 S((T, 1024), F32), _bs((tmm, 1024), lambda i, j, k: (i, 0)), (nI, 1, 4))
    gw["w_branch_b"] = mm("branch_b_dw", _TN, [(pooled, dyb, _bs((tmm, 1024), lambda b, j, k: (k, 0)), _bs((tmm, 512), lambda b, j, k: (k, b)))],
                          S((4, 1024, 512), F32), _bs((None, 1024, 512), lambda b, j, k: (b, 0, 0)), (4, 1, nI))
    (do, dP2), (gs["gdn_norm"],) = stage("gnorm_bwd", f_gnorm, [Tile(o, D), z_tile], [Par(sp["gdn_norm"])], [o2048(BF16)],
                                         cts=[d_on], dins=[o2048(F32), Out(pshape, BF16, D, cb=C_Z // D, into=dP2)])
    dq, dk, dv, dbg = gdn_backward(qkv[0], qkv[1], qkv[2], bg, s0, do, GDN_SPAN_BWD)
    dcw = []
    for i, (n, dqq) in enumerate(zip("qkv", (dq, dk, dv))):
        (dP2,), (dc,) = stage("conv_%s_bwd" % n, f_heads[i], [Tile(P2, D, cb=i, halo=True)], [Par(cws[i])], [o2048(F32)],
                              cts=[dqq], dins=[Out(pshape, BF16, D, cb=i, into=dP2)])
        dcw.append(dc)
    gs["conv_qkv"] = jnp.concatenate(dcw, axis=1)
    (dP2,), (gs["a_log"], gs["dt_bias"]) = stage("bg_bwd", f_bg, [ba_tile], [Par(sp["a_log"]), Par(sp["dt_bias"])],
                                                 [Out((T, 128), F32, 128)], cts=[dbg],
                                                 dins=[Out(pshape, BF16, 512, cb=C_BA // 512, into=dP2)])
    (dP2,), (gs["pool_w"], gs["pool_scale"]) = stage("pool_bwd", f_pool, [p_tile], pool_pars, [Out((T, 1024), BF16, 1024)],
                                                     cts=[dpooled], dins=[Out(pshape, BF16, 1024, cb=C_POOL // 1024, into=dP2)])
    tk = 1536
    dh1 = mm("in_proj_dx", _NN, [(dP2, W["w_in"], _bs((tmm, tk), lambda i, j, k: (i, k)), _bs((tk, 512), lambda i, j, k: (k, j)))],
             S((T, D), F32), _bs((tmm, 512), lambda i, j, k: (i, j)), (nI, D // 512, N_PROJ // tk))
    gw["w_in"] = mm("in_proj_dw", _TN, [(dP2, h1, _bs((tmm, tk), lambda i, j, k: (k, i)), _bs((tmm, D), lambda i, j, k: (k, 0)))],
                    S((N_PROJ, D), F32), _bs((tk, D), lambda i, j, k: (i, 0)), (N_PROJ // tk, 1, nI))
    (grad_x,), (gs["mix_pre_norm"],) = stage("pre1_bwd", f_prenorm_res, [Tile(x, D)], [Par(sp["mix_pre_norm"])],
                                             [o2048(F32), o2048(BF16)], cts=[dx0, dh1], dins=[o2048(F32)])
    return loss, grad_x, gw, gs


W_IN_COLS = 13344
GROUPED = {"w_branch_b": 512, "w_xkv": 1024}
ROW_SHARDED = ("w_branch_a", "w_mix_out", "w_xq", "w_xo")


def shard_to_slab(name, w):
    if name == "w_in":
        return w.T.astype(BF16)
    if name == "w_up":
        return jnp.pad(w, ((0, 0), (0, FF_BLK - FF_COLS))).astype(BF16)
    return w.astype(BF16)


def slabs_to_weight(name, g):
    if name == "w_in":
        full = g.astype(F32).reshape(W_IN_COLS, D_MODEL)
        pad = jnp.zeros((N_PROJ - W_IN_COLS, D_MODEL), F32)
        return jnp.concatenate([full[0:8192], full[9248:13344], full[8224:9248], full[8192:8224], pad]).astype(BF16)
    if name == "w_down":
        z = jnp.zeros((2, FF_BLK - FF_COLS, D_MODEL), g.dtype)
        return jnp.concatenate([g.reshape(2, FF_COLS, D_MODEL), z], axis=1)
    if name in ROW_SHARDED:
        return g.reshape(D_MODEL, D_MODEL)
    return g


def grad_to_slabs(name, gw):
    if name == "w_in":
        g = gw["w_in"]
        return jnp.concatenate([g[0:8192], g[13312:13344], g[12288:13312], g[8192:12288]]).reshape(4, 3336, D_MODEL)
    if name == "w_up":
        return jnp.concatenate([gw["w_up_a"], gw["w_up_b"]], axis=0)
    if name == "w_down":
        return gw["w_down"][:, :FF_COLS].reshape(4, FF_COLS // 2, D_MODEL)
    if name in ROW_SHARDED:
        return gw[name].reshape(4, D_MODEL // 4, D_MODEL)
    return gw[name]


def slab_to_shard_grad(name, f):
    if name == "w_in":
        return f.T
    if name == "w_up":
        return f[:, :FF_COLS]
    return f


MESH = pl.DeviceIdType.MESH
ANY = pl.BlockSpec(memory_space=pl.ANY)


def _me():
    x, y, c = lax.axis_index("x"), lax.axis_index("y"), lax.axis_index("c")
    return x, y, c, 2 * x + y


def _chip_dev(t, c):
    return (t // 2, t % 2, c)


def _rcopy(src, dst, ssem, rsem, dev):
    return pltpu.make_async_remote_copy(src_ref=src, dst_ref=dst, send_sem=ssem, recv_sem=rsem, device_id=dev, device_id_type=MESH)


def gather_weights(slabs):
    n = len(slabs)

    def body(*refs):
        src, dst = refs[:n], refs[n:2 * n]
        ici_s, ici_r, fwd_s, fwd_r, loc = refs[2 * n:]
        x, y, c, s = _me()
        sender = c == s // 2
        sib = (x, y, 1 - c)
        local = [pltpu.make_async_copy(src[w], dst[w].at[s], loc.at[w]) for w in range(n)]
        for cp in local:
            cp.start()
        for r in (1, 2, 3):
            @pl.when(sender)
            def _(r=r):
                for w in range(n):
                    _rcopy(src[w], dst[w].at[s], ici_s.at[w, r - 1], ici_r.at[w, r - 1], _chip_dev(s ^ r, c)).start()
        for r in (1, 2, 3):
            t = s ^ r
            here = c == t // 2

            @pl.when(here)
            def _(r=r, t=t):
                for w in range(n):
                    _rcopy(src[w], dst[w].at[t], ici_s.at[w, r - 1], ici_r.at[w, r - 1], sib).wait_recv()
                    _rcopy(dst[w].at[t], dst[w].at[t], fwd_s.at[w, r - 1], fwd_r.at[w, r - 1], sib).start()

            @pl.when(jnp.logical_not(here))
            def _(r=r, t=t):
                for w in range(n):
                    _rcopy(dst[w].at[t], dst[w].at[t], fwd_s.at[w, r - 1], fwd_r.at[w, r - 1], sib).wait_recv()
        for r in (1, 2, 3):
            t = s ^ r

            @pl.when(sender)
            def _(r=r):
                for w in range(n):
                    _rcopy(src[w], dst[w].at[s], ici_s.at[w, r - 1], ici_r.at[w, r - 1], sib).wait_send()

            @pl.when(c == t // 2)
            def _(r=r, t=t):
                for w in range(n):
                    _rcopy(dst[w].at[t], dst[w].at[t], fwd_s.at[w, r - 1], fwd_r.at[w, r - 1], sib).wait_send()
        for cp in local:
            cp.wait()

    return pl.pallas_call(
        body, name="gather_weights", in_specs=[ANY] * n, out_specs=[ANY] * n,
        out_shape=[jax.ShapeDtypeStruct((4,) + a.shape, a.dtype) for a in slabs],
        scratch_shapes=[pltpu.SemaphoreType.DMA((n, 3))] * 4 + [pltpu.SemaphoreType.DMA((n,))],
    )(*slabs)


def pair_exchange(g4):
    n = len(g4)

    def body(*refs):
        src, dst = refs[:n], refs[n:2 * n]
        ssem, rsem = refs[2 * n:]
        x, y, c, s = _me()
        cps = [_rcopy(src[w].at[pl.ds(2 * (1 - c), 2)], dst[w], ssem.at[w], rsem.at[w], (x, y, 1 - c)) for w in range(n)]
        for cp in cps:
            cp.start()
        for cp in cps:
            cp.wait()

    return pl.pallas_call(
        body, name="pair_exchange", in_specs=[ANY] * n, out_specs=[ANY] * n,
        out_shape=[jax.ShapeDtypeStruct((2,) + a.shape[1:], a.dtype) for a in g4],
        scratch_shapes=[pltpu.SemaphoreType.DMA((n,))] * 2,
    )(*g4)


def _col_tile(R, C):
    for tc in (512, 256, 128):
        if C % tc == 0 and R * tc * 4 <= (4 << 20):
            return tc
    return 128


def pair_add(name, g4, gsib, c):
    _, R, C = g4.shape
    tc = _col_tile(R, C)

    def body(c_ref, a_ref, b_ref, of_ref, ob_ref):
        v = a_ref[...] + b_ref[...]
        of_ref[...] = v
        ob_ref[...] = v.astype(BF16)

    blk = lambda f: pl.BlockSpec((None, R, tc), f)
    gsp = pltpu.PrefetchScalarGridSpec(
        num_scalar_prefetch=1, grid=(2, C // tc),
        in_specs=[blk(lambda p, j, cr: (2 * cr[0] + p, 0, j)), blk(lambda p, j, cr: (p, 0, j))],
        out_specs=[blk(lambda p, j, cr: (p, 0, j)), blk(lambda p, j, cr: (p, 0, j))])
    return pl.pallas_call(
        body, name=name, grid_spec=gsp,
        out_shape=[jax.ShapeDtypeStruct((2, R, C), F32), jax.ShapeDtypeStruct((2, R, C), BF16)],
        compiler_params=_cparams(("arbitrary", "arbitrary")),
    )(c.reshape(1).astype(jnp.int32), g4, gsib)


def scatter_partials(rb):
    n = len(rb)

    def body(*refs):
        src, dst = refs[:n], refs[n:2 * n]
        ssem, rsem = refs[2 * n:]
        x, y, c, s = _me()
        for r in (1, 2, 3):
            t = s ^ r

            @pl.when(t // 2 == c)
            def _(r=r, t=t):
                for w in range(n):
                    _rcopy(src[w].at[t % 2], dst[w].at[s], ssem.at[w, r - 1], rsem.at[w, r - 1], _chip_dev(t, c)).start()
        for r in (1, 2, 3):
            t = s ^ r

            @pl.when(s // 2 == c)
            def _(r=r, t=t):
                for w in range(n):
                    _rcopy(src[w].at[0], dst[w].at[t], ssem.at[w, r - 1], rsem.at[w, r - 1], _chip_dev(t, c)).wait_recv()
        for r in (1, 2, 3):
            t = s ^ r

            @pl.when(t // 2 == c)
            def _(r=r, t=t):
                for w in range(n):
                    _rcopy(src[w].at[t % 2], dst[w].at[s], ssem.at[w, r - 1], rsem.at[w, r - 1], _chip_dev(t, c)).wait_send()

    return pl.pallas_call(
        body, name="scatter_partials", in_specs=[ANY] * n, out_specs=[ANY] * n,
        out_shape=[jax.ShapeDtypeStruct((4,) + a.shape[1:], a.dtype) for a in rb],
        scratch_shapes=[pltpu.SemaphoreType.DMA((n, 3))] * 2,
    )(*rb)


def final_sum(name, rf, recv, s):
    _, R, C = rf.shape
    tc = _col_tile(R, C)

    def body(s_ref, own_ref, r0_ref, r1_ref, r2_ref, o_ref):
        o_ref[...] = ((own_ref[...] + r0_ref[...].astype(F32)) + r1_ref[...].astype(F32)) + r2_ref[...].astype(F32)

    blk = lambda f: pl.BlockSpec((None, R, tc), f)
    other = lambda k: (lambda j, sr: (k + (k >= sr[0]).astype(jnp.int32), 0, j))
    gsp = pltpu.PrefetchScalarGridSpec(
        num_scalar_prefetch=1, grid=(C // tc,),
        in_specs=[blk(lambda j, sr: (sr[0] % 2, 0, j)), blk(other(0)), blk(other(1)), blk(other(2))],
        out_specs=pl.BlockSpec((R, tc), lambda j, sr: (0, j)))
    return pl.pallas_call(
        body, name=name, grid_spec=gsp, out_shape=jax.ShapeDtypeStruct((R, C), F32),
        compiler_params=_cparams(("arbitrary",)),
    )(s.reshape(1).astype(jnp.int32), rf, recv, recv, recv)


def share_with_sibling(fs):
    n = len(fs)

    def body(*refs):
        src, dst = refs[:n], refs[n:2 * n]
        ssem, rsem = refs[2 * n:]
        x, y, c, s = _me()
        sib = (x, y, 1 - c)

        @pl.when(s // 2 == c)
        def _():
            cps = [_rcopy(src[w], dst[w], ssem.at[w], rsem.at[w], sib) for w in range(n)]
            for cp in cps:
                cp.start()
            for cp in cps:
                cp.wait_send()

        @pl.when(s // 2 != c)
        def _():
            for w in range(n):
                _rcopy(src[w], dst[w], ssem.at[w], rsem.at[w], sib).wait_recv()

    return pl.pallas_call(
        body, name="share_with_sibling", in_specs=[ANY] * n, out_specs=[ANY] * n,
        out_shape=[jax.ShapeDtypeStruct(a.shape, a.dtype) for a in fs],
        input_output_aliases={w: w for w in range(n)},
        scratch_shapes=[pltpu.SemaphoreType.DMA((n,))] * 2,
    )(*fs)


def allgather_rows(v):
    m_per, ncol = v.shape

    def body(x_ref, out_ref, send_sems, recv_sems, local_sem):
        x, y, c = lax.axis_index("x"), lax.axis_index("y"), lax.axis_index("c")
        me, sibling = (x, y, c), (x, y, 1 - c)
        chips = [(1 - x, y), (x, 1 - y), (1 - x, 1 - y)]

        def rows(px, py, pc):
            return out_ref.at[pl.ds((4 * px + 2 * py + pc) * m_per, m_per), :]

        def copy(k, block, to, src=None):
            return _rcopy(rows(*block) if src is None else src, rows(*block), send_sems.at[k], recv_sems.at[k], to)

        mine = pltpu.make_async_copy(x_ref, rows(*me), local_sem)
        mine.start()
        first = [copy(0, me, sibling, src=x_ref)]
        first += [copy(1 + j, me, (*chip, c), src=x_ref) for j, chip in enumerate(chips)]
        for cp in first:
            cp.start()
        passed = [copy(4 + j, (*chip, c), sibling) for j, chip in enumerate(chips)]
        for j, chip in enumerate(chips):
            copy(1 + j, (*chip, c), me).wait_recv()
            passed[j].start()
        copy(0, sibling, me).wait_recv()
        for j, chip in enumerate(chips):
            copy(4 + j, (*chip, 1 - c), me).wait_recv()
        for cp in first + passed:
            cp.wait_send()
        mine.wait()

    return pl.pallas_call(
        body, name="allgather_rows", out_shape=jax.ShapeDtypeStruct((8 * m_per, ncol), v.dtype),
        in_specs=[pl.BlockSpec(memory_space=pltpu.VMEM)], out_specs=pl.BlockSpec(memory_space=pltpu.VMEM),
        scratch_shapes=[pltpu.SemaphoreType.DMA((7,)), pltpu.SemaphoreType.DMA((7,)), pltpu.SemaphoreType.DMA],
        compiler_params=pltpu.CompilerParams(vmem_limit_bytes=V7X_VMEM_LIMIT),
    )(v)


def sum_blocks(name, a, nblk):
    m = a.shape[0] // nblk

    def body(a_ref, o_ref):
        acc = a_ref[pl.ds(0, m), :]
        for b in range(1, nblk):
            acc = acc + a_ref[pl.ds(b * m, m), :]
        o_ref[...] = acc

    return pl.pallas_call(body, name=name, out_shape=jax.ShapeDtypeStruct((m, a.shape[1]), a.dtype),
                          compiler_params=pltpu.CompilerParams(vmem_limit_bytes=V7X_VMEM_LIMIT))(a)


BIG = ("w_in", "w_branch_a", "w_branch_b", "w_mix_out", "w_xq", "w_xkv", "w_xo", "w_up", "w_down")
GAINS = ("mix_pre_norm", "gdn_norm", "pool_scale", "mix_post_norm", "xa_pre_norm", "mem_norm", "xa_post_norm",
         "ffn_pre_norm", "ffn_post_norm")
WEIGHTS = ("mix_pre_norm", "w_in", "conv_qkv", "a_log", "dt_bias", "gdn_norm", "pool_w", "pool_scale", "w_branch_a",
           "w_branch_b", "w_mix_out", "mix_post_norm", "xa_pre_norm", "mem_norm", "w_xq", "w_xkv", "w_xo", "xa_post_norm",
           "ffn_pre_norm", "w_up", "ffn_conv_w", "ffn_conv_b", "w_down", "ffn_post_norm")


def _rows128(vecs):
    flat = jnp.concatenate([v.reshape(-1) for v in vecs])
    m = -(-flat.shape[0] // 1024) * 8
    return jnp.pad(flat, (0, m * 128 - flat.shape[0])).reshape(m, 128)


def _unrows(a, shapes):
    flat, out, pos = a.reshape(-1), [], 0
    for sh in shapes:
        n = 1
        for d in sh:
            n *= d
        out.append(flat[pos:pos + n].reshape(sh))
        pos += n
    return out


def _lane128(v):
    return jnp.pad(v.reshape(1, GDN_HEADS), ((0, 0), (GDN_HEADS, 128 - 2 * GDN_HEADS)))


def kernel(x, mem, mix_pre_norm, w_in, conv_qkv, a_log, dt_bias, gdn_norm, pool_w, pool_scale, w_branch_a, w_branch_b, w_mix_out, mix_post_norm, xa_pre_norm, mem_norm, w_xq, w_xkv, w_xo, xa_post_norm, ffn_pre_norm, w_up, ffn_conv_w, ffn_conv_b, w_down, ffn_post_norm, loss_target, m_mix_pre_norm, m_w_in, m_conv_qkv, m_a_log, m_dt_bias, m_gdn_norm, m_pool_w, m_pool_scale, m_w_branch_a, m_w_branch_b, m_w_mix_out, m_mix_post_norm, m_xa_pre_norm, m_mem_norm, m_w_xq, m_w_xkv, m_w_xo, m_xa_post_norm, m_ffn_pre_norm, m_w_up, m_ffn_conv_w, m_ffn_conv_b, m_w_down, m_ffn_post_norm, v_mix_pre_norm, v_w_in, v_conv_qkv, v_a_log, v_dt_bias, v_gdn_norm, v_pool_w, v_pool_scale, v_w_branch_a, v_w_branch_b, v_w_mix_out, v_mix_post_norm, v_xa_pre_norm, v_mem_norm, v_w_xq, v_w_xkv, v_w_xo, v_xa_post_norm, v_ffn_pre_norm, v_w_up, v_ffn_conv_w, v_ffn_conv_b, v_w_down, v_ffn_post_norm):
    given = dict(locals())
    w = {n: given[n][0] for n in WEIGHTS}
    cx, cy, cc = lax.axis_index("x"), lax.axis_index("y"), lax.axis_index("c")
    chip = 2 * cx + cy

    gathered = gather_weights([shard_to_slab(n, w[n]) for n in BIG])
    W = {n: slabs_to_weight(n, g) for n, g in zip(BIG, gathered)}
    sharded_small = (w["conv_qkv"], w["ffn_conv_w"], w["pool_w"])
    allv = allgather_rows(_rows128(sharded_small))
    per_chip = allv.reshape(8, -1)[0::2]
    parts = [_unrows(per_chip[t], [a.shape for a in sharded_small]) for t in range(4)]
    sp = {n: w[n].reshape(1, -1) for n in GAINS}
    sp["a_log"], sp["dt_bias"] = _lane128(w["a_log"]), _lane128(w["dt_bias"])
    sp["conv_qkv"] = jnp.concatenate([p[0] for p in parts], axis=1)
    sp["ffn_conv_w"] = jnp.pad(jnp.stack([p[1] for p in parts]), ((0, 0), (0, 0), (0, FF_BLK - FF_COLS)))
    sp["pool_w"] = jnp.concatenate([p[2] for p in parts], axis=1).reshape(4 * 256, 256)
    sp["ffn_conv_b"] = jnp.pad(w["ffn_conv_b"].reshape(4, 1, FF_COLS), ((0, 0), (0, 0), (0, FF_BLK - FF_COLS)))

    loss, grad_x, gw, gs = local_step(x[0], mem[0], loss_target[0], W, sp)
    loss = lax.psum(loss[0, 0], ("x", "y", "c"))

    g4 = [grad_to_slabs(n, gw) for n in BIG]
    gsib = pair_exchange(g4)
    sums = [pair_add("pair_add_" + n, a, b, cc) for n, a, b in zip(BIG, g4, gsib)]
    recv = scatter_partials([sb for _, sb in sums])
    fin = [final_sum("final_sum_" + n, sf, rv, chip) for n, (sf, _), rv in zip(BIG, sums, recv)]
    fin = share_with_sibling(fin)
    grads = {n: slab_to_shard_grad(n, f) for n, f in zip(BIG, fin)}

    small_names = GAINS + ("a_log", "dt_bias", "ffn_conv_b", "conv_qkv", "ffn_conv_w", "pool_w")
    vec = _rows128([gs[n] for n in small_names])
    total = sum_blocks("sum_small", allgather_rows(vec), 8)
    tot = dict(zip(small_names, _unrows(total, [gs[n].shape for n in small_names])))
    for n in GAINS:
        grads[n] = tot[n].reshape(-1)
    grads["a_log"] = tot["a_log"][0, GDN_HEADS:2 * GDN_HEADS]
    grads["dt_bias"] = tot["dt_bias"][0, GDN_HEADS:2 * GDN_HEADS]
    grads["ffn_conv_b"] = tot["ffn_conv_b"][:, 0, :FF_COLS].reshape(-1)
    grads["conv_qkv"] = lax.dynamic_slice_in_dim(tot["conv_qkv"], chip * 1536, 1536, axis=1)
    grads["ffn_conv_w"] = lax.dynamic_index_in_dim(tot["ffn_conv_w"], chip, axis=0, keepdims=False)[:, :FF_COLS]
    grads["pool_w"] = lax.dynamic_slice_in_dim(tot["pool_w"].reshape(4, 256, 256), chip * 64, 64, axis=1)

    delta, new_m, new_v = {}, {}, {}
    for n in WEIGHTS:
        shp = w[n].shape
        two = (lambda a: a.reshape(-1, shp[-1])) if len(shp) > 1 else (lambda a: a.reshape(1, -1))
        d, nm, nv = adamw("adamw_" + n, two(w[n]), two(grads[n]), two(given["m_" + n][0]), two(given["v_" + n][0]))
        delta[n], new_m[n], new_v[n] = (a.reshape((1,) + shp) for a in (d, nm, nv))
    out_g = [grads[n].reshape((1,) + w[n].shape) for n in WEIGHTS]
    return (loss, grad_x[None], *out_g, *[delta[n] for n in WEIGHTS], *[new_m[n] for n in WEIGHTS], *[new_v[n] for n in WEIGHTS])
```

```python
import functools

import jax
import jax.numpy as jnp
from jax import lax
from jax.experimental import pallas as pl
from jax.experimental.pallas import tpu as pltpu
from jax.experimental.pallas import tpu_sc as plsc

F32 = jnp.float32
BF16 = jnp.bfloat16

D_MODEL = 2048
CHUNK = 64
GDN_HEADS = 16
GDN_DK = 128
POOL_WINDOWS = (2, 4, 8, 16)
XA_HEADS = 4
XA_HEAD_DIM = D_MODEL // XA_HEADS
EPS = 1e-6
GDN_SPAN = 256
GDN_SPAN_BWD = 128
HALO = 16
V7X_VMEM_LIMIT = 56 * 1024 * 1024

ADAM_LR, ADAM_B1, ADAM_B2, ADAM_EPS, ADAM_WD, ADAM_STEP = 0.001, 0.9, 0.999, 1e-08, 0.01, 10

_NN = ((1,), (0,))
_NT = ((1,), (1,))
_TN = ((0,), (0,))


def _cparams(sem):
    return pltpu.CompilerParams(dimension_semantics=sem, vmem_limit_bytes=V7X_VMEM_LIMIT)


def _dg(a, b, dims, prec=None):
    return lax.dot_general(a, b, (dims, ((), ())), precision=prec, preferred_element_type=F32)


def _make_dots(cast, prec):
    def raw(dims, a, b):
        return _dg(cast(a), cast(b), dims, prec)

    @jax.custom_vjp
    def nn(a, b):
        return raw(_NN, a, b)

    @jax.custom_vjp
    def nt(a, b):
        return raw(_NT, a, b)

    @jax.custom_vjp
    def tn(a, b):
        return raw(_TN, a, b)

    nn.defvjp(lambda a, b: (raw(_NN, a, b), (a, b)), lambda r, g: (nt(g, r[1]), tn(r[0], g)))
    nt.defvjp(lambda a, b: (raw(_NT, a, b), (a, b)), lambda r, g: (nn(g, r[1]), tn(g, r[0])))
    tn.defvjp(lambda a, b: (raw(_TN, a, b), (a, b)), lambda r, g: (nt(r[1], g), nn(r[0], g)))
    return nn, nt, tn


bdot_nn, bdot_nt, bdot_tn = _make_dots(lambda x: x.astype(BF16), None)
hdot_nn, hdot_nt, hdot_tn = _make_dots(lambda x: x, lax.Precision.HIGHEST)


@functools.partial(jax.custom_vjp, nondiff_argnums=(1,))
def shift_rows(x, k):
    return pltpu.roll(x, k, 0)


def _shift_rows_fwd(x, k):
    return pltpu.roll(x, k, 0), None


def _shift_rows_bwd(k, _, g):
    return (pltpu.roll(g, g.shape[0] - k, 0),)


shift_rows.defvjp(_shift_rows_fwd, _shift_rows_bwd)


@functools.partial(jax.custom_vjp, nondiff_argnums=(1,))
def drop_head(x, h):
    return x[h:]


def _drop_head_fwd(x, h):
    return x[h:], None


def _drop_head_bwd(h, _, g):
    return (jnp.concatenate([jnp.zeros((h,) + g.shape[1:], g.dtype), g], axis=0),)


drop_head.defvjp(_drop_head_fwd, _drop_head_bwd)


@functools.partial(jax.custom_vjp, nondiff_argnums=(1,))
def split_lanes(x, n):
    w = x.shape[-1] // n
    return tuple(x[:, i * w:(i + 1) * w] for i in range(n))


def _split_lanes_fwd(x, n):
    return split_lanes(x, n), None


def _split_lanes_bwd(n, _, gs):
    return (jnp.concatenate(list(gs), axis=-1),)


split_lanes.defvjp(_split_lanes_fwd, _split_lanes_bwd)


@functools.partial(jax.custom_vjp, nondiff_argnums=(1,))
def split_rows(x, n):
    h = x.shape[0] // n
    return tuple(x[i * h:(i + 1) * h] for i in range(n))


def _split_rows_fwd(x, n):
    return split_rows(x, n), None


def _split_rows_bwd(n, _, gs):
    return (jnp.concatenate(list(gs), axis=0),)


split_rows.defvjp(_split_rows_fwd, _split_rows_bwd)


def row_of(w, j):
    rid = lax.broadcasted_iota(jnp.int32, w.shape, 0)
    return jnp.sum(jnp.where(rid == j, w, 0.0), axis=0, keepdims=True)


def sigmoid(x):
    return 0.5 * jnp.tanh(0.5 * x) + 0.5


def silu(x):
    return x * sigmoid(x)


def softplus(x):
    return jnp.maximum(x, 0.0) + jnp.log(1.0 + jnp.exp(-jnp.abs(x)))


def rms(x, g):
    return x * lax.rsqrt(jnp.mean(x * x, axis=-1, keepdims=True) + EPS) * g


def mm(name, dims, pairs, out_shape, out_spec, grid):
    nk = grid[2]
    acc_in_out = nk > 1 and out_shape.dtype == F32
    npair = len(pairs)

    def body(*refs):
        o_ref = refs[2 * npair]
        part = None
        for p in range(npair):
            d = _dg(refs[2 * p][...].astype(BF16), refs[2 * p + 1][...].astype(BF16), dims)
            part = d if part is None else part + d
        if nk == 1:
            o_ref[...] = part.astype(o_ref.dtype)
            return
        acc = o_ref if acc_in_out else refs[2 * npair + 1]
        k = pl.program_id(2)

        @pl.when(k == 0)
        def _():
            acc[...] = part

        @pl.when(k > 0)
        def _():
            acc[...] += part

        if not acc_in_out:
            @pl.when(k == nk - 1)
            def _():
                o_ref[...] = acc[...].astype(o_ref.dtype)

    scratch = []
    if nk > 1 and not acc_in_out:
        scratch = [pltpu.VMEM(tuple(d for d in out_spec.block_shape if d is not None), F32)]
    in_specs, operands = [], []
    for a, b, a_spec, b_spec in pairs:
        in_specs += [a_spec, b_spec]
        operands += [a, b]
    return pl.pallas_call(
        body, name=name, grid=grid, in_specs=in_specs, out_specs=out_spec, out_shape=out_shape,
        scratch_shapes=scratch, compiler_params=_cparams(("parallel", "parallel", "arbitrary")),
    )(*operands)


def _bs(shape, fn):
    return pl.BlockSpec(shape, fn)


class Tile:
    def __init__(self, arr, w, cb=0, lead=None, halo=False):
        self.arr, self.w, self.cb, self.lead, self.halo = arr, w, cb, lead, halo


class Out:
    def __init__(self, shape, dtype, w, cb=0, lead=None, into=None):
        self.shape, self.dtype, self.w, self.cb, self.lead, self.into = shape, dtype, w, cb, lead, into


class Par:
    def __init__(self, arr, lead=None):
        self.arr, self.lead = arr, lead


def _spec(rows, w, cb, lead, tile_of):
    if lead is None:
        return pl.BlockSpec((rows, w), lambda o, i: (tile_of(i), cb))
    return pl.BlockSpec((None, rows, w), lambda o, i: (lead(o), tile_of(i), cb))


def _par_spec(p):
    if p.lead is None:
        return pl.BlockSpec(p.arr.shape, lambda o, i: (0, 0))
    return pl.BlockSpec((None,) + p.arr.shape[1:], lambda o, i: (p.lead(o), 0, 0))


def run_stage(name, fn, tm, ins, pars, outs, *, outer=1, cts=None, dins=None):
    T = ins[0].arr.shape[-2]
    nt = T // tm
    bwd = cts is not None
    any_halo = any(t.halo for t in ins)
    step_tile = (lambda i: nt - 1 - i) if bwd else (lambda i: i)
    hb = tm // HALO

    in_specs, operands = [], []
    for t in ins:
        if t.halo:
            in_specs.append(_spec(HALO, t.w, t.cb, t.lead, lambda i: jnp.maximum(step_tile(i) * hb - 1, 0)))
            operands.append(t.arr)
        in_specs.append(_spec(tm, t.w, t.cb, t.lead, step_tile))
        operands.append(t.arr)
    for p in pars:
        in_specs.append(_par_spec(p))
        operands.append(p.arr)
    n_in_refs = len(operands)

    out_descs = list(outs) if not bwd else [d for d in dins if d is not None]
    aliases = {}
    if bwd:
        for c, o in zip(cts, outs):
            in_specs.append(_spec(tm, o.w, o.cb, o.lead, step_tile))
            operands.append(c)
    n_ct = len(operands) - n_in_refs
    for k, o in enumerate(out_descs):
        if o.into is not None:
            aliases[len(operands)] = k
            in_specs.append(pl.BlockSpec(memory_space=pl.ANY))
            operands.append(o.into)
    out_specs = [_spec(tm, o.w, o.cb, o.lead, step_tile) for o in out_descs]
    out_shapes = [jax.ShapeDtypeStruct(o.shape, o.dtype) for o in out_descs]
    if bwd:
        for p in pars:
            out_specs.append(_par_spec(p))
            out_shapes.append(jax.ShapeDtypeStruct(p.arr.shape, F32))
    scratch = []
    if bwd and any_halo:
        scratch = [pltpu.VMEM((HALO, t.w), F32) for t, d in zip(ins, dins) if t.halo and d is not None]

    def body(*refs):
        i = pl.program_id(1)
        tile = step_tile(i)
        row0 = tile * tm
        pos = 0
        tiles = []
        for t in ins:
            if t.halo:
                prev = jnp.where(tile > 0, refs[pos][...].astype(F32), 0.0)
                tiles.append(jnp.concatenate([prev, refs[pos + 1][...].astype(F32)], axis=0))
                pos += 2
            else:
                tiles.append(refs[pos][...].astype(F32))
                pos += 1
        pvals = [refs[pos + k][...].astype(F32) for k in range(len(pars))]
        pos += len(pars)
        if not bwd:
            res = fn(tiles, pvals, row0)
            for o_ref, r in zip(refs[pos:], res):
                o_ref[...] = r.astype(o_ref.dtype)
            return
        ct_vals = [refs[pos + k][...].astype(F32) for k in range(n_ct)]
        pos += n_ct + len(aliases)
        _, vjp_fn = jax.vjp(lambda tt, pp: fn(tt, pp, row0), tiles, pvals)
        d_tiles, d_pars = vjp_fn(ct_vals)
        carries = list(refs[len(refs) - len(scratch):])
        for t, d, dt in zip(ins, dins, d_tiles):
            if d is None:
                continue
            o_ref = refs[pos]
            pos += 1
            if t.halo:
                carry = carries.pop(0)
                main = dt[HALO:]
                tail = main[tm - HALO:] + jnp.where(i > 0, carry[...], 0.0)
                o_ref[...] = jnp.concatenate([main[:tm - HALO], tail], axis=0).astype(o_ref.dtype)
                carry[...] = dt[:HALO]
            else:
                o_ref[...] = dt.astype(o_ref.dtype)
        for dp in d_pars:
            acc = refs[pos]
            pos += 1

            @pl.when(i == 0)
            def _(acc=acc, dp=dp):
                acc[...] = dp

            @pl.when(i > 0)
            def _(acc=acc, dp=dp):
                acc[...] += dp

    res = pl.pallas_call(
        body, name=name, grid=(outer, nt), in_specs=in_specs, out_specs=out_specs, out_shape=out_shapes,
        scratch_shapes=scratch, input_output_aliases=aliases,
        compiler_params=_cparams(("arbitrary", "arbitrary")),
    )(*operands)
    if not bwd:
        return list(res)
    n_d = len(out_descs)
    d_full, it = [], iter(res[:n_d])
    for d in dins:
        d_full.append(None if d is None else next(it))
    return d_full, list(res[n_d:])


def f_prenorm(t, p, row0):
    return [rms(t[0], p[0])]


def f_prenorm_res(t, p, row0):
    return [t[0], rms(t[0], p[0])]


def f_post_pre(t, p, row0):
    x, y = t
    x1 = x + rms(y, p[0])
    return [x1, rms(x1, p[1])]


def _causal_conv(x, w, taps):
    y = x * row_of(w, taps - 1)
    for j in range(taps - 1):
        y = y + shift_rows(x, taps - 1 - j) * row_of(w, j)
    return drop_head(y, HALO)


def _l2(x):
    return x * lax.rsqrt(jnp.sum(x * x, axis=-1, keepdims=True) + EPS)


def make_f_convhead(scale, normalise):
    def f(t, p, row0):
        y = silu(_causal_conv(t[0], p[0], 4))
        if not normalise:
            return [y]
        return [jnp.concatenate([_l2(c) * scale for c in split_lanes(y, GDN_HEADS)], axis=-1)]
    return f


def f_bg(t, p, row0):
    ba = split_lanes(t[0], 4)[0]
    alog, dtb = p
    lane = lax.broadcasted_iota(jnp.int32, ba.shape, 1)
    bg = jnp.where(lane < GDN_HEADS, sigmoid(ba), -jnp.exp(alog) * softplus(ba + dtb))
    return [jnp.where(lane < 2 * GDN_HEADS, bg, 0.0)]


def f_gnorm(t, p, row0):
    o, z = t
    po, pz = split_lanes(o, GDN_HEADS), split_lanes(z, GDN_HEADS)
    return [jnp.concatenate([rms(a, p[0]) * silu(b) for a, b in zip(po, pz)], axis=-1)]


def f_pool(t, p, row0):
    x = t[0]
    pw, psc = p
    tm = x.shape[0] - HALO
    tpos = (row0 + lax.broadcasted_iota(jnp.int32, (tm, 1), 0) + 1).astype(F32)
    outs = []
    for xg, wg, win in zip(split_lanes(x, 4), split_rows(pw, 4), POOL_WINDOWS):
        s, span = xg, 1
        while span < win:
            s = s + shift_rows(s, span)
            span *= 2
        mean = drop_head(s, HALO) / jnp.minimum(tpos, float(win))
        outs.append(bdot_nn(mean - drop_head(xg, HALO), wg))
    return [jnp.concatenate(outs, axis=-1) * psc]


def f_merge(t, p, row0):
    gates, ya, yb = t
    ga, gb = split_lanes(gates, 2)
    return [sigmoid(ga) * ya + sigmoid(gb) * yb]


def f_xattn(t, p, row0):
    k, v = p
    outs = []
    for qh, kh, vh in zip(split_lanes(t[0], XA_HEADS), split_lanes(k, XA_HEADS), split_lanes(v, XA_HEADS)):
        s = bdot_nt(qh, kh) * (XA_HEAD_DIM ** -0.5)
        s = s - jnp.max(s, axis=-1, keepdims=True)
        e = jnp.exp(s)
        outs.append(bdot_nn(e / jnp.sum(e, axis=-1, keepdims=True), vh))
    return [jnp.concatenate(outs, axis=-1)]


def f_convglu(t, p, row0):
    ua, ub = t
    cwa, cwb, ba, bb = p
    return [silu(_causal_conv(ua, cwa, 3) + ba) * (_causal_conv(ub, cwb, 3) + bb)]


_BNN = (((2,), (1,)), ((0,), (0,)))
_BNT = (((2,), (2,)), ((0,), (0,)))
_BTN = (((1,), (1,)), ((0,), (0,)))


def _make_batched_dots():
    def raw(dims, a, b):
        return lax.dot_general(a.astype(BF16), b.astype(BF16), dims, preferred_element_type=F32)

    @jax.custom_vjp
    def nn(a, b):
        return raw(_BNN, a, b)

    @jax.custom_vjp
    def nt(a, b):
        return raw(_BNT, a, b)

    @jax.custom_vjp
    def tn(a, b):
        return raw(_BTN, a, b)

    nn.defvjp(lambda a, b: (raw(_BNN, a, b), (a, b)), lambda r, g: (nt(g, r[1]), tn(r[0], g)))
    nt.defvjp(lambda a, b: (raw(_BNT, a, b), (a, b)), lambda r, g: (nn(g, r[1]), tn(g, r[0])))
    tn.defvjp(lambda a, b: (raw(_BTN, a, b), (a, b)), lambda r, g: (nt(r[1], g), nn(r[0], g)))
    return nn, nt, tn


bb_nn, bb_nt, bb_tn = _make_batched_dots()


def _gdn_local(q, k, v, gcol, bcol):
    C = CHUNK
    r = lax.broadcasted_iota(jnp.int32, (1, C, C), 1)
    c = lax.broadcasted_iota(jnp.int32, (1, C, C), 2)
    eye, incl, strict = r == c, r >= c, r > c
    grow = jnp.sum(jnp.where(eye, gcol, 0.0), axis=1, keepdims=True)
    Gcol = jnp.sum(jnp.where(incl, grow, 0.0), axis=2, keepdims=True)
    Grow = jnp.sum(jnp.where(eye, Gcol, 0.0), axis=1, keepdims=True)
    decay = jnp.where(incl, jnp.exp(jnp.where(incl, Gcol - Grow, 0.0)), 0.0)
    X = -jnp.where(strict, bcol * decay * bb_nt(k, k), 0.0)
    N, P = X, X
    for _ in range(5):
        P = bb_nn(P, P)
        N = N + P + bb_nn(N, P)
    expg = jnp.exp(Gcol)
    rv, rk = bcol * v, (bcol * expg) * k
    u_v = rv + bb_nn(N, rv)
    w_k = rk + bb_nn(N, rk)
    attn = decay * bb_nt(q, k)
    rid = lax.broadcasted_iota(jnp.int32, (1, C, 1), 1)
    glast = jnp.sum(jnp.where(rid == C - 1, Gcol, 0.0), axis=1, keepdims=True)
    return u_v, w_k, attn, q * expg, k * jnp.exp(glast - Gcol), jnp.exp(glast)


def _gdn_rec(u_v, w_k, attn, q_dec, k_dec, cd, S):
    u = u_v - bb_nn(w_k, S)
    o = bb_nn(q_dec, S) + bb_nn(attn, u)
    return o, cd * S + bb_tn(k_dec, u)


def _gdn_load(q_ref, k_ref, v_ref, bg_ref, nch):
    H = GDN_HEADS

    def batched(ref):
        return jnp.stack([ref[c * CHUNK:(c + 1) * CHUNK, h * GDN_DK:(h + 1) * GDN_DK] for c in range(nch) for h in range(H)])

    bg = bg_ref[...]
    lane = lax.broadcasted_iota(jnp.int32, bg.shape, 1)
    bcols = [jnp.sum(jnp.where(lane == h, bg, 0.0), axis=-1, keepdims=True) for h in range(H)]
    gcols = [jnp.sum(jnp.where(lane == H + h, bg, 0.0), axis=-1, keepdims=True) for h in range(H)]
    pick = lambda cols: jnp.stack([cols[h][c * CHUNK:(c + 1) * CHUNK] for c in range(nch) for h in range(H)])
    return batched(q_ref), batched(k_ref), batched(v_ref), pick(gcols), pick(bcols)


def _gdn_store(ref, val, nch):
    H = GDN_HEADS
    for c in range(nch):
        for h in range(H):
            ref[c * CHUNK:(c + 1) * CHUNK, h * GDN_DK:(h + 1) * GDN_DK] = val[c * H + h]


def gdn_forward(q, k, v, bg, span):
    T = q.shape[0]
    ns, nch, H = T // span, span // CHUNK, GDN_HEADS

    def body(q_ref, k_ref, v_ref, bg_ref, o_ref, s_ref, state):
        @pl.when(pl.program_id(0) == 0)
        def _():
            state[...] = jnp.zeros(state.shape, F32)

        loc = _gdn_local(*_gdn_load(q_ref, k_ref, v_ref, bg_ref, nch))
        S = state[...]
        for c in range(nch):
            s_ref[c] = S
            o, S = _gdn_rec(*[a[c * H:(c + 1) * H] for a in loc], S)
            for h in range(H):
                o_ref[c * CHUNK:(c + 1) * CHUNK, h * GDN_DK:(h + 1) * GDN_DK] = o[h]
        state[...] = S

    wide = pl.BlockSpec((span, H * GDN_DK), lambda s: (s, 0))
    return pl.pallas_call(
        body, name="gdn_fwd", grid=(ns,),
        in_specs=[wide, wide, wide, pl.BlockSpec((span, 128), lambda s: (s, 0))],
        out_specs=[wide, pl.BlockSpec((nch, H, GDN_DK, GDN_DK), lambda s: (s, 0, 0, 0))],
        out_shape=[jax.ShapeDtypeStruct((T, H * GDN_DK), F32), jax.ShapeDtypeStruct((T // CHUNK, H, GDN_DK, GDN_DK), F32)],
        scratch_shapes=[pltpu.VMEM((H, GDN_DK, GDN_DK), F32)],
        compiler_params=_cparams(("arbitrary",)),
    )(q, k, v, bg)


def gdn_backward(q, k, v, bg, starts, do, span):
    T = q.shape[0]
    ns, nch, H = T // span, span // CHUNK, GDN_HEADS

    def body(q_ref, k_ref, v_ref, bg_ref, s_ref, do_ref, dq_ref, dk_ref, dv_ref, dbg_ref, dstate):
        @pl.when(pl.program_id(0) == 0)
        def _():
            dstate[...] = jnp.zeros(dstate.shape, F32)

        loc, vjp_loc = jax.vjp(_gdn_local, *_gdn_load(q_ref, k_ref, v_ref, bg_ref, nch))
        dS = dstate[...]
        dloc = [None] * nch
        for c in reversed(range(nch)):
            _, vjp_rec = jax.vjp(_gdn_rec, *[a[c * H:(c + 1) * H] for a in loc], s_ref[c])
            do_c = jnp.stack([do_ref[c * CHUNK:(c + 1) * CHUNK, h * GDN_DK:(h + 1) * GDN_DK] for h in range(H)])
            *dloc[c], dS = vjp_rec((do_c, dS))
        dstate[...] = dS
        dq, dk, dv, dg, db = vjp_loc(tuple(jnp.concatenate([dloc[c][i] for c in range(nch)], axis=0) for i in range(6)))
        _gdn_store(dq_ref, dq, nch)
        _gdn_store(dk_ref, dk, nch)
        _gdn_store(dv_ref, dv, nch)
        lane = lax.broadcasted_iota(jnp.int32, (CHUNK, 128), 1)
        for c in range(nch):
            acc = jnp.zeros((CHUNK, 128), F32)
            for h in range(H):
                acc = acc + jnp.where(lane == h, db[c * H + h], 0.0) + jnp.where(lane == H + h, dg[c * H + h], 0.0)
            dbg_ref[c * CHUNK:(c + 1) * CHUNK, :] = acc

    wide = pl.BlockSpec((span, H * GDN_DK), lambda i: (ns - 1 - i, 0))
    bgs = pl.BlockSpec((span, 128), lambda i: (ns - 1 - i, 0))
    return pl.pallas_call(
        body, name="gdn_bwd", grid=(ns,),
        in_specs=[wide, wide, wide, bgs, pl.BlockSpec((nch, H, GDN_DK, GDN_DK), lambda i: (ns - 1 - i, 0, 0, 0)), wide],
        out_specs=[wide, wide, wide, bgs],
        out_shape=[jax.ShapeDtypeStruct((T, H * GDN_DK), F32)] * 3 + [jax.ShapeDtypeStruct((T, 128), F32)],
        scratch_shapes=[pltpu.VMEM((H, GDN_DK, GDN_DK), F32)],
        compiler_params=_cparams(("arbitrary",)),
    )(q, k, v, bg, starts, do)


def loss_stage(x2, y3, tgt, g, tm):
    T, D = x2.shape
    nt = T // tm

    def body(x_ref, y_ref, t_ref, g_ref, loss_ref, dx_ref, dy_ref, dg_ref):
        i = pl.program_id(0)
        tgtv = t_ref[...]

        def f(x, y, gg):
            err = x + rms(y, gg) - tgtv
            return 0.5 * jnp.mean(err * err, axis=-1, keepdims=True)

        rows, vjp_fn = jax.vjp(f, x_ref[...], y_ref[...], g_ref[...])
        dx, dy, dg = vjp_fn(jnp.ones_like(rows))
        dx_ref[...] = dx
        dy_ref[...] = dy.astype(dy_ref.dtype)
        part = jnp.sum(rows, axis=0, keepdims=True)

        @pl.when(i == 0)
        def _():
            loss_ref[...] = part
            dg_ref[...] = dg

        @pl.when(i > 0)
        def _():
            loss_ref[...] += part
            dg_ref[...] += dg

    tile = pl.BlockSpec((tm, D), lambda i: (i, 0))
    gs = pl.BlockSpec((1, D), lambda i: (0, 0))
    return pl.pallas_call(
        body, name="loss_head", grid=(nt,), in_specs=[tile, tile, tile, gs],
        out_specs=[pl.BlockSpec((1, 1), lambda i: (0, 0)), tile, tile, gs],
        out_shape=[jax.ShapeDtypeStruct((1, 1), F32), jax.ShapeDtypeStruct((T, D), F32),
                   jax.ShapeDtypeStruct((T, D), BF16), jax.ShapeDtypeStruct((1, D), F32)],
        compiler_params=_cparams(("arbitrary",)),
    )(x2, y3, tgt, g)


def adamw(name, w, g, m, v):
    R, C = w.shape
    tr = R
    for cand in (256, 128, 64, 32, 16, 8):
        if R % cand == 0 and R > cand and cand * C * 4 <= (2 << 20):
            tr = cand
            break
    c1 = 1.0 / (1.0 - ADAM_B1 ** ADAM_STEP)
    c2 = 1.0 / (1.0 - ADAM_B2 ** ADAM_STEP)

    def body(w_ref, g_ref, m_ref, v_ref, d_ref, nm_ref, nv_ref):
        gg = g_ref[...]
        nm = ADAM_B1 * m_ref[...] + (1.0 - ADAM_B1) * gg
        nv = ADAM_B2 * v_ref[...] + (1.0 - ADAM_B2) * (gg * gg)
        d_ref[...] = -ADAM_LR * ((nm * c1) / (jnp.sqrt(nv * c2) + ADAM_EPS) + ADAM_WD * w_ref[...])
        nm_ref[...] = nm
        nv_ref[...] = nv

    spec = pl.BlockSpec((tr, C), lambda i: (i, 0))
    return pl.pallas_call(
        body, name=name, grid=(R // tr,), in_specs=[spec] * 4, out_specs=[spec] * 3,
        out_shape=[jax.ShapeDtypeStruct((R, C), F32)] * 3, compiler_params=_cparams(("parallel",)),
    )(w, g, m, v)


C_Z, C_GATES, C_POOL, C_BA, N_PROJ = 6144, 8192, 12288, 13312, 13824
FF_BLK = 2816
FF_COLS = 2752


def local_step(x, mem, tgt, W, sp):
    T, D = x.shape
    tm = 256
    tmm = min(512, T)
    nI = T // tmm
    S = jax.ShapeDtypeStruct
    gw, gs = {}, {}

    def stage(name, fn, ins, pars, outs, **kw):
        return run_stage(name, fn, kw.pop("tm", tm), ins, pars, outs, **kw)

    def dense(name, a, w, out_dtype=F32, tn=2048):
        Tq, Kd = a.shape
        N = w.shape[1]
        tq = min(tmm, Tq)
        return mm(name, _NN, [(a, w, _bs((tq, Kd), lambda j, i, k: (i, 0)), _bs((Kd, tn), lambda j, i, k: (0, j)))],
                  S((Tq, N), out_dtype), _bs((tq, tn), lambda j, i, k: (i, j)), (N // tn, Tq // tq, 1))

    def dense_t(name, g, w, out_dtype=F32, tn=2048):
        Tq, N = g.shape
        Kd = w.shape[0]
        tq = min(tmm, Tq)
        return mm(name, _NT, [(g, w, _bs((tq, N), lambda j, i, k: (i, 0)), _bs((tn, N), lambda j, i, k: (j, 0)))],
                  S((Tq, Kd), out_dtype), _bs((tq, tn), lambda j, i, k: (i, j)), (Kd // tn, Tq // tq, 1))

    def wgrad(name, a, g, ta=1024, tn=2048):
        Tq, Kd = a.shape
        N = g.shape[1]
        tt = min(1024, Tq)
        return mm(name, _TN, [(a, g, _bs((tt, ta), lambda i, j, k: (k, i)), _bs((tt, tn), lambda i, j, k: (k, j)))],
                  S((Kd, N), F32), _bs((ta, tn), lambda i, j, k: (i, j)), (Kd // ta, N // tn, Tq // tt))

    o2048 = lambda dt: Out((T, D), dt, D)

    (h1,) = stage("pre1", f_prenorm, [Tile(x, D)], [Par(sp["mix_pre_norm"])], [o2048(BF16)])
    tnp = 1536
    P2 = mm("in_proj", _NT, [(h1, W["w_in"], _bs((tmm, D), lambda j, i, k: (i, 0)), _bs((tnp, D), lambda j, i, k: (j, 0)))],
            S((T, N_PROJ), F32), _bs((tmm, tnp), lambda j, i, k: (i, j)), (N_PROJ // tnp, nI, 1))
    cw = sp["conv_qkv"]
    cws = [cw[:, i * D:(i + 1) * D] for i in range(3)]
    f_heads = [make_f_convhead(GDN_DK ** -0.5, True), make_f_convhead(1.0, True), make_f_convhead(1.0, False)]
    qkv = [stage("conv_" + n, f_heads[i], [Tile(P2, D, cb=i, halo=True)], [Par(cws[i])], [o2048(F32)])[0]
           for i, n in enumerate("qkv")]
    ba_tile = Tile(P2, 512, cb=C_BA // 512)
    (bg,) = stage("bg", f_bg, [ba_tile], [Par(sp["a_log"]), Par(sp["dt_bias"])], [Out((T, 128), F32, 128)])
    o, s0 = gdn_forward(qkv[0], qkv[1], qkv[2], bg, GDN_SPAN)
    z_tile = Tile(P2, D, cb=C_Z // D)
    (o_n,) = stage("gnorm", f_gnorm, [Tile(o, D), z_tile], [Par(sp["gdn_norm"])], [o2048(BF16)])
    y_a = dense("branch_a", o_n, W["w_branch_a"])
    p_tile = Tile(P2, 1024, cb=C_POOL // 1024, halo=True)
    pool_pars = [Par(sp["pool_w"]), Par(sp["pool_scale"])]
    (pooled,) = stage("pool", f_pool, [p_tile], pool_pars, [Out((T, 1024), BF16, 1024)])
    y_b = mm("branch_b", _NN, [(pooled, W["w_branch_b"], _bs((tmm, 1024), lambda j, i, k: (i, 0)),
                                _bs((None, 1024, 512), lambda j, i, k: (j, 0, 0)))],
             S((T, D), F32), _bs((tmm, 512), lambda j, i, k: (i, j)), (4, nI, 1))
    gate_tile = Tile(P2, 2 * D, cb=C_GATES // (2 * D))
    merge_ins = [gate_tile, Tile(y_a, D), Tile(y_b, D)]
    (merged,) = stage("merge", f_merge, merge_ins, [], [o2048(BF16)])
    y1 = dense("mix_out", merged, W["w_mix_out"])
    pp1 = [Par(sp["mix_post_norm"]), Par(sp["xa_pre_norm"])]
    x1, h2 = stage("post1", f_post_pre, [Tile(x, D), Tile(y1, D)], pp1, [o2048(F32), o2048(BF16)])

    q2 = dense("xq", h2, W["w_xq"], out_dtype=BF16)
    (mn,) = stage("mem_norm", f_prenorm, [Tile(mem, D)], [Par(sp["mem_norm"])], [Out(mem.shape, BF16, D)], tm=mem.shape[0])
    M = mem.shape[0]
    kv = mm("xkv", _NN, [(mn, W["w_xkv"], _bs((M, D), lambda j, i, k: (0, 0)), _bs((None, D, 1024), lambda j, i, k: (j, 0, 0)))],
            S((M, 2 * D), F32), _bs((M, 1024), lambda j, i, k: (0, j)), (4, 1, 1))
    k2, v2 = kv[:, :D], kv[:, D:]
    xa_pars = [Par(k2), Par(v2)]
    (o2,) = stage("xattn", f_xattn, [Tile(q2, D)], xa_pars, [o2048(BF16)])
    y2 = dense("xo", o2, W["w_xo"])
    pp2 = [Par(sp["xa_post_norm"]), Par(sp["ffn_pre_norm"])]
    x2, h3 = stage("post2", f_post_pre, [Tile(x1, D), Tile(y2, D)], pp2, [o2048(F32), o2048(BF16)])

    def up(name, off):
        return mm(name, _NN, [(h3, W["w_up"], _bs((tmm, D), lambda j, i, k: (i, 0)),
                               _bs((None, D, FF_BLK), lambda j, i, k: (j + off, 0, 0)))],
                  S((2, T, FF_BLK), F32), _bs((None, tmm, FF_BLK), lambda j, i, k: (j, i, 0)), (2, nI, 1))

    Ua, Ub = up("up_a", 0), up("up_b", 2)
    ffn_ins = [Tile(Ua, FF_BLK, lead=lambda o: o, halo=True), Tile(Ub, FF_BLK, lead=lambda o: o, halo=True)]
    ffn_pars = [Par(sp["ffn_conv_w"], lead=lambda o: o), Par(sp["ffn_conv_w"], lead=lambda o: o + 2),
                Par(sp["ffn_conv_b"], lead=lambda o: o), Par(sp["ffn_conv_b"], lead=lambda o: o + 2)]
    ffn_out = [Out((2, T, FF_BLK), BF16, FF_BLK, lead=lambda o: o)]
    (ff,) = stage("convglu", f_convglu, ffn_ins, ffn_pars, ffn_out, outer=2)
    y3 = mm("down", _NN, [(ff, W["w_down"], _bs((None, tmm, FF_BLK), lambda i, j, k: (k, i, 0)),
                           _bs((None, FF_BLK, D), lambda i, j, k: (k, 0, 0)))],
            S((T, D), F32), _bs((tmm, D), lambda i, j, k: (i, 0)), (nI, 1, 2))
    loss, dx2, dy3, gs["ffn_post_norm"] = loss_stage(x2, y3, tgt, sp["ffn_post_norm"], tm)

    dff = mm("down_dx", _NT, [(dy3, W["w_down"], _bs((tmm, D), lambda j, i, k: (i, 0)),
                               _bs((None, FF_BLK, D), lambda j, i, k: (j, 0, 0)))],
             S((2, T, FF_BLK), BF16), _bs((None, tmm, FF_BLK), lambda j, i, k: (j, i, 0)), (2, nI, 1))
    tbig = min(1024, T)
    gw["w_down"] = mm("down_dw", _TN, [(ff, dy3, _bs((None, tbig, FF_BLK), lambda b, j, k: (b, k, 0)),
                                        _bs((tbig, 1024), lambda b, j, k: (k, j)))],
                      S((2, FF_BLK, D), F32), _bs((None, FF_BLK, 1024), lambda b, j, k: (b, 0, j)), (2, D // 1024, T // tbig))
    dU_out = [Out((2, T, FF_BLK), BF16, FF_BLK, lead=lambda o: o), Out((2, T, FF_BLK), BF16, FF_BLK, lead=lambda o: o)]
    (dUa, dUb), dffn = stage("convglu_bwd", f_convglu, ffn_ins, ffn_pars, ffn_out, outer=2, cts=[dff], dins=dU_out)
    gs["ffn_conv_w"] = jnp.concatenate([dffn[0][:2], dffn[1][2:]], axis=0)
    gs["ffn_conv_b"] = jnp.concatenate([dffn[2][:2], dffn[3][2:]], axis=0)
    dh3 = mm("up_dx", _NT, [(dUa, W["w_up"], _bs((None, tmm, FF_BLK), lambda i, j, k: (k, i, 0)),
                             _bs((None, 1024, FF_BLK), lambda i, j, k: (k, j, 0))),
                            (dUb, W["w_up"], _bs((None, tmm, FF_BLK), lambda i, j, k: (k, i, 0)),
                             _bs((None, 1024, FF_BLK), lambda i, j, k: (k + 2, j, 0)))],
             S((T, D), F32), _bs((tmm, 1024), lambda i, j, k: (i, j)), (nI, D // 1024, 2))

    def up_dw(name, dU):
        tt = min(2048, T)
        return mm(name, _TN, [(h3, dU, _bs((tt, 512), lambda b, i, k: (k, i)), _bs((None, tt, FF_BLK), lambda b, i, k: (b, k, 0)))],
                  S((2, D, FF_BLK), F32), _bs((None, 512, FF_BLK), lambda b, i, k: (b, i, 0)), (2, D // 512, T // tt))

    gw["w_up_a"], gw["w_up_b"] = up_dw("up_dw_a", dUa), up_dw("up_dw_b", dUb)
    (dx1, dy2), dpp2 = stage("post2_bwd", f_post_pre, [Tile(x1, D), Tile(y2, D)], pp2, [o2048(F32), o2048(BF16)],
                             cts=[dx2, dh3], dins=[o2048(F32), o2048(BF16)])
    gs["xa_post_norm"], gs["ffn_pre_norm"] = dpp2

    do2 = dense_t("xo_dx", dy2, W["w_xo"])
    gw["w_xo"] = wgrad("xo_dw", o2, dy2)
    (dq2,), (dk2, dv2) = stage("xattn_bwd", f_xattn, [Tile(q2, D)], xa_pars, [o2048(BF16)], cts=[do2], dins=[o2048(BF16)])
    dh2 = dense_t("xq_dx", dq2, W["w_xq"])
    gw["w_xq"] = wgrad("xq_dw", h2, dq2)
    dkv = jnp.concatenate([dk2, dv2], axis=1).astype(BF16)
    dmn = mm("xkv_dx", _NT, [(dkv, W["w_xkv"], _bs((M, 1024), lambda i, j, k: (0, k)), _bs((None, 512, 1024), lambda i, j, k: (k, j, 0)))],
             S((M, D), F32), _bs((M, 512), lambda i, j, k: (0, j)), (1, D // 512, 4))
    gw["w_xkv"] = mm("xkv_dw", _TN, [(mn, dkv, _bs((M, D), lambda b, j, k: (0, 0)), _bs((M, 1024), lambda b, j, k: (0, b)))],
                     S((4, D, 1024), F32), _bs((None, D, 1024), lambda b, j, k: (b, 0, 0)), (4, 1, 1))
    _, (gs["mem_norm"],) = stage("mem_norm_bwd", f_prenorm, [Tile(mem, D)], [Par(sp["mem_norm"])], [Out(mem.shape, BF16, D)],
                                 tm=M, cts=[dmn], dins=[None])
    (dx0, dy1), dpp1 = stage("post1_bwd", f_post_pre, [Tile(x, D), Tile(y1, D)], pp1, [o2048(F32), o2048(BF16)],
                             cts=[dx1, dh2], dins=[o2048(F32), o2048(BF16)])
    gs["mix_post_norm"], gs["xa_pre_norm"] = dpp1

    dmerged = dense_t("mix_out_dx", dy1, W["w_mix_out"])
    gw["w_mix_out"] = wgrad("mix_out_dw", merged, dy1)
    pshape = (T, N_PROJ)
    (dP2, dya, dyb), _ = stage("merge_bwd", f_merge, merge_ins, [], [o2048(BF16)], cts=[dmerged],
                               dins=[Out(pshape, BF16, 2 * D, cb=C_GATES // (2 * D)), o2048(BF16), o2048(BF16)])
    d_on = dense_t("branch_a_dx", dya, W["w_branch_a"])
    gw["w_branch_a"] = wgrad("branch_a_dw", o_n, dya)
    dpooled = mm("branch_b_dx", _NT, [(dyb, W["w_branch_b"], _bs((tmm, 512), lambda i, j, k: (i, k)),
                                       _bs((None, 1024, 512), lambda i, j, k: (k, 0, 0)))],
                 S((T, 1024), F32), _bs((tmm, 1024), lambda i, j, k: (i, 0)), (nI, 1, 4))
    gw["w_branch_b"] = mm("branch_b_dw", _TN, [(pooled, dyb, _bs((tmm, 1024), lambda b, j, k: (k, 0)), _bs((tmm, 512), lambda b, j, k: (k, b)))],
                          S((4, 1024, 512), F32), _bs((None, 1024, 512), lambda b, j, k: (b, 0, 0)), (4, 1, nI))
    (do, dP2), (gs["gdn_norm"],) = stage("gnorm_bwd", f_gnorm, [Tile(o, D), z_tile], [Par(sp["gdn_norm"])], [o2048(BF16)],
                                         cts=[d_on], dins=[o2048(F32), Out(pshape, BF16, D, cb=C_Z // D, into=dP2)])
    dq, dk, dv, dbg = gdn_backward(qkv[0], qkv[1], qkv[2], bg, s0, do, GDN_SPAN_BWD)
    dcw = []
    for i, (n, dqq) in enumerate(zip("qkv", (dq, dk, dv))):
        (dP2,), (dc,) = stage("conv_%s_bwd" % n, f_heads[i], [Tile(P2, D, cb=i, halo=True)], [Par(cws[i])], [o2048(F32)],
                              cts=[dqq], dins=[Out(pshape, BF16, D, cb=i, into=dP2)])
        dcw.append(dc)
    gs["conv_qkv"] = jnp.concatenate(dcw, axis=1)
    (dP2,), (gs["a_log"], gs["dt_bias"]) = stage("bg_bwd", f_bg, [ba_tile], [Par(sp["a_log"]), Par(sp["dt_bias"])],
                                                 [Out((T, 128), F32, 128)], cts=[dbg],
                                                 dins=[Out(pshape, BF16, 512, cb=C_BA // 512, into=dP2)])
    (dP2,), (gs["pool_w"], gs["pool_scale"]) = stage("pool_bwd", f_pool, [p_tile], pool_pars, [Out((T, 1024), BF16, 1024)],
                                                     cts=[dpooled], dins=[Out(pshape, BF16, 1024, cb=C_POOL // 1024, into=dP2)])
    tk, ta, tt = 2304, 1152, min(2048, T)
    dh1 = mm("in_proj_dx", _NN, [(dP2, W["w_in"], _bs((tbig, tk), lambda i, j, k: (i, k)), _bs((tk, D), lambda i, j, k: (k, 0)))],
             S((T, D), F32), _bs((tbig, D), lambda i, j, k: (i, 0)), (T // tbig, 1, N_PROJ // tk))
    gw["w_in"] = mm("in_proj_dw", _TN, [(dP2, h1, _bs((tt, ta), lambda i, j, k: (k, i)), _bs((tt, D), lambda i, j, k: (k, 0)))],
                    S((N_PROJ, D), F32), _bs((ta, D), lambda i, j, k: (i, 0)), (N_PROJ // ta, 1, T // tt))
    (grad_x,), (gs["mix_pre_norm"],) = stage("pre1_bwd", f_prenorm_res, [Tile(x, D)], [Par(sp["mix_pre_norm"])],
                                             [o2048(F32), o2048(BF16)], cts=[dx0, dh1], dins=[o2048(F32)])
    return loss, grad_x, gw, gs


W_IN_COLS = 13344
GROUPED = {"w_branch_b": 512, "w_xkv": 1024}
ROW_SHARDED = ("w_branch_a", "w_mix_out", "w_xq", "w_xo")


def shard_to_slab(name, w):
    if name == "w_in":
        return w.T.astype(BF16)
    if name == "w_up":
        return jnp.pad(w, ((0, 0), (0, FF_BLK - FF_COLS))).astype(BF16)
    return w.astype(BF16)


def slabs_to_weight(name, g):
    if name == "w_in":
        full = g.astype(F32).reshape(W_IN_COLS, D_MODEL)
        pad = jnp.zeros((N_PROJ - W_IN_COLS, D_MODEL), F32)
        return jnp.concatenate([full[0:8192], full[9248:13344], full[8224:9248], full[8192:8224], pad]).astype(BF16)
    if name == "w_down":
        z = jnp.zeros((2, FF_BLK - FF_COLS, D_MODEL), g.dtype)
        return jnp.concatenate([g.reshape(2, FF_COLS, D_MODEL), z], axis=1)
    if name in ROW_SHARDED:
        return g.reshape(D_MODEL, D_MODEL)
    return g


def grad_to_slabs(name, gw):
    if name == "w_in":
        g = gw["w_in"]
        return jnp.concatenate([g[0:8192], g[13312:13344], g[12288:13312], g[8192:12288]]).reshape(4, 3336, D_MODEL)
    if name == "w_up":
        return jnp.concatenate([gw["w_up_a"], gw["w_up_b"]], axis=0)
    if name == "w_down":
        return gw["w_down"][:, :FF_COLS].reshape(4, FF_COLS // 2, D_MODEL)
    if name in ROW_SHARDED:
        return gw[name].reshape(4, D_MODEL // 4, D_MODEL)
    return gw[name]


def slab_to_shard_grad(name, f):
    if name == "w_in":
        return f.T
    if name == "w_up":
        return f[:, :FF_COLS]
    return f


MESH = pl.DeviceIdType.MESH
ANY = pl.BlockSpec(memory_space=pl.ANY)


def _me():
    x, y, c = lax.axis_index("x"), lax.axis_index("y"), lax.axis_index("c")
    return x, y, c, 2 * x + y


def _chip_dev(t, c):
    return (t // 2, t % 2, c)


def _rcopy(src, dst, ssem, rsem, dev):
    return pltpu.make_async_remote_copy(src_ref=src, dst_ref=dst, send_sem=ssem, recv_sem=rsem, device_id=dev, device_id_type=MESH)


def _handshake_all(x, y, c):
    barrier = pltpu.get_barrier_semaphore()
    for dx in (0, 1):
        for dy in (0, 1):
            for dc in (0, 1):
                if dx or dy or dc:
                    pl.semaphore_signal(barrier, inc=1, device_id=((x + dx) % 2, (y + dy) % 2, (c + dc) % 2), device_id_type=MESH)
    pl.semaphore_wait(barrier, 7)


def gather_weights(slabs, on_sequencer=False, name="gather_weights", collective_id=None):
    n = len(slabs)

    def body(*refs):
        src, dst = refs[:n], refs[n:2 * n]
        ici_s, ici_r, fwd_s, fwd_r, loc = refs[2 * n:]
        x, y, c, s = _me()
        if on_sequencer:
            _handshake_all(x, y, c)
        sender = c == s // 2
        sib = (x, y, 1 - c)
        local = [pltpu.make_async_copy(src[w], dst[w].at[s], loc.at[w]) for w in range(n)]
        for cp in local:
            cp.start()
        for r in (1, 2, 3):
            @pl.when(sender)
            def _(r=r):
                for w in range(n):
                    _rcopy(src[w], dst[w].at[s], ici_s.at[w, r - 1], ici_r.at[w, r - 1], _chip_dev(s ^ r, c)).start()
        for r in (1, 2, 3):
            t = s ^ r
            here = c == t // 2

            @pl.when(here)
            def _(r=r, t=t):
                for w in range(n):
                    _rcopy(src[w], dst[w].at[t], ici_s.at[w, r - 1], ici_r.at[w, r - 1], sib).wait_recv()
                    _rcopy(dst[w].at[t], dst[w].at[t], fwd_s.at[w, r - 1], fwd_r.at[w, r - 1], sib).start()

            @pl.when(jnp.logical_not(here))
            def _(r=r, t=t):
                for w in range(n):
                    _rcopy(dst[w].at[t], dst[w].at[t], fwd_s.at[w, r - 1], fwd_r.at[w, r - 1], sib).wait_recv()
        for r in (1, 2, 3):
            t = s ^ r

            @pl.when(sender)
            def _(r=r):
                for w in range(n):
                    _rcopy(src[w], dst[w].at[s], ici_s.at[w, r - 1], ici_r.at[w, r - 1], sib).wait_send()

            @pl.when(c == t // 2)
            def _(r=r, t=t):
                for w in range(n):
                    _rcopy(dst[w].at[t], dst[w].at[t], fwd_s.at[w, r - 1], fwd_r.at[w, r - 1], sib).wait_send()
        for cp in local:
            cp.wait()

    out_shape = [jax.ShapeDtypeStruct((4,) + a.shape, a.dtype) for a in slabs]
    sems = [pltpu.SemaphoreType.DMA((n, 3))] * 4 + [pltpu.SemaphoreType.DMA((n,))]
    if on_sequencer:
        return pl.kernel(body, out_type=out_shape, mesh=plsc.ScalarSubcoreMesh(axis_name="seq", num_cores=1), name=name,
                         scratch_types=sems, compiler_params=pltpu.CompilerParams(collective_id=collective_id))(*slabs)
    return pl.pallas_call(body, name=name, in_specs=[ANY] * n, out_specs=[ANY] * n, out_shape=out_shape,
                          scratch_shapes=sems)(*slabs)


def pair_exchange(g4):
    n = len(g4)

    def body(*refs):
        src, dst = refs[:n], refs[n:2 * n]
        ssem, rsem = refs[2 * n:]
        x, y, c, s = _me()
        cps = [_rcopy(src[w].at[pl.ds(2 * (1 - c), 2)], dst[w], ssem.at[w], rsem.at[w], (x, y, 1 - c)) for w in range(n)]
        for cp in cps:
            cp.start()
        for cp in cps:
            cp.wait()

    return pl.pallas_call(
        body, name="pair_exchange", in_specs=[ANY] * n, out_specs=[ANY] * n,
        out_shape=[jax.ShapeDtypeStruct((2,) + a.shape[1:], a.dtype) for a in g4],
        scratch_shapes=[pltpu.SemaphoreType.DMA((n,))] * 2,
    )(*g4)


def _col_tile(R, C):
    for tc in (512, 256, 128):
        if C % tc == 0 and R * tc * 4 <= (4 << 20):
            return tc
    return 128


def pair_add(name, g4, gsib, c):
    _, R, C = g4.shape
    tc = _col_tile(R, C)

    def body(c_ref, a_ref, b_ref, of_ref, ob_ref):
        v = a_ref[...] + b_ref[...]
        of_ref[...] = v
        ob_ref[...] = v.astype(BF16)

    blk = lambda f: pl.BlockSpec((None, R, tc), f)
    gsp = pltpu.PrefetchScalarGridSpec(
        num_scalar_prefetch=1, grid=(2, C // tc),
        in_specs=[blk(lambda p, j, cr: (2 * cr[0] + p, 0, j)), blk(lambda p, j, cr: (p, 0, j))],
        out_specs=[blk(lambda p, j, cr: (p, 0, j)), blk(lambda p, j, cr: (p, 0, j))])
    return pl.pallas_call(
        body, name=name, grid_spec=gsp,
        out_shape=[jax.ShapeDtypeStruct((2, R, C), F32), jax.ShapeDtypeStruct((2, R, C), BF16)],
        compiler_params=_cparams(("arbitrary", "arbitrary")),
    )(c.reshape(1).astype(jnp.int32), g4, gsib)


def scatter_partials(rb):
    n = len(rb)

    def body(*refs):
        src, dst = refs[:n], refs[n:2 * n]
        ssem, rsem = refs[2 * n:]
        x, y, c, s = _me()
        for r in (1, 2, 3):
            t = s ^ r

            @pl.when(t // 2 == c)
            def _(r=r, t=t):
                for w in range(n):
                    _rcopy(src[w].at[t % 2], dst[w].at[s], ssem.at[w, r - 1], rsem.at[w, r - 1], _chip_dev(t, c)).start()
        for r in (1, 2, 3):
            t = s ^ r

            @pl.when(s // 2 == c)
            def _(r=r, t=t):
                for w in range(n):
                    _rcopy(src[w].at[0], dst[w].at[t], ssem.at[w, r - 1], rsem.at[w, r - 1], _chip_dev(t, c)).wait_recv()
        for r in (1, 2, 3):
            t = s ^ r

            @pl.when(t // 2 == c)
            def _(r=r, t=t):
                for w in range(n):
                    _rcopy(src[w].at[t % 2], dst[w].at[s], ssem.at[w, r - 1], rsem.at[w, r - 1], _chip_dev(t, c)).wait_send()

    return pl.pallas_call(
        body, name="scatter_partials", in_specs=[ANY] * n, out_specs=[ANY] * n,
        out_shape=[jax.ShapeDtypeStruct((4,) + a.shape[1:], a.dtype) for a in rb],
        scratch_shapes=[pltpu.SemaphoreType.DMA((n, 3))] * 2,
    )(*rb)


def final_sum(name, rf, recv, s):
    _, R, C = rf.shape
    tc = _col_tile(R, C)

    def body(s_ref, own_ref, r0_ref, r1_ref, r2_ref, o_ref):
        o_ref[...] = ((own_ref[...] + r0_ref[...].astype(F32)) + r1_ref[...].astype(F32)) + r2_ref[...].astype(F32)

    blk = lambda f: pl.BlockSpec((None, R, tc), f)
    other = lambda k: (lambda j, sr: (k + (k >= sr[0]).astype(jnp.int32), 0, j))
    gsp = pltpu.PrefetchScalarGridSpec(
        num_scalar_prefetch=1, grid=(C // tc,),
        in_specs=[blk(lambda j, sr: (sr[0] % 2, 0, j)), blk(other(0)), blk(other(1)), blk(other(2))],
        out_specs=pl.BlockSpec((R, tc), lambda j, sr: (0, j)))
    return pl.pallas_call(
        body, name=name, grid_spec=gsp, out_shape=jax.ShapeDtypeStruct((R, C), F32),
        compiler_params=_cparams(("arbitrary",)),
    )(s.reshape(1).astype(jnp.int32), rf, recv, recv, recv)


def share_with_sibling(fs):
    n = len(fs)

    def body(*refs):
        src, dst = refs[:n], refs[n:2 * n]
        ssem, rsem = refs[2 * n:]
        x, y, c, s = _me()
        sib = (x, y, 1 - c)

        @pl.when(s // 2 == c)
        def _():
            cps = [_rcopy(src[w], dst[w], ssem.at[w], rsem.at[w], sib) for w in range(n)]
            for cp in cps:
                cp.start()
            for cp in cps:
                cp.wait_send()

        @pl.when(s // 2 != c)
        def _():
            for w in range(n):
                _rcopy(src[w], dst[w], ssem.at[w], rsem.at[w], sib).wait_recv()

    return pl.pallas_call(
        body, name="share_with_sibling", in_specs=[ANY] * n, out_specs=[ANY] * n,
        out_shape=[jax.ShapeDtypeStruct(a.shape, a.dtype) for a in fs],
        input_output_aliases={w: w for w in range(n)},
        scratch_shapes=[pltpu.SemaphoreType.DMA((n,))] * 2,
    )(*fs)


def allgather_rows(v):
    m_per, ncol = v.shape

    def body(x_ref, out_ref, send_sems, recv_sems, local_sem):
        x, y, c = lax.axis_index("x"), lax.axis_index("y"), lax.axis_index("c")
        me, sibling = (x, y, c), (x, y, 1 - c)
        chips = [(1 - x, y), (x, 1 - y), (1 - x, 1 - y)]

        def rows(px, py, pc):
            return out_ref.at[pl.ds((4 * px + 2 * py + pc) * m_per, m_per), :]

        def copy(k, block, to, src=None):
            return _rcopy(rows(*block) if src is None else src, rows(*block), send_sems.at[k], recv_sems.at[k], to)

        mine = pltpu.make_async_copy(x_ref, rows(*me), local_sem)
        mine.start()
        first = [copy(0, me, sibling, src=x_ref)]
        first += [copy(1 + j, me, (*chip, c), src=x_ref) for j, chip in enumerate(chips)]
        for cp in first:
            cp.start()
        passed = [copy(4 + j, (*chip, c), sibling) for j, chip in enumerate(chips)]
        for j, chip in enumerate(chips):
            copy(1 + j, (*chip, c), me).wait_recv()
            passed[j].start()
        copy(0, sibling, me).wait_recv()
        for j, chip in enumerate(chips):
            copy(4 + j, (*chip, 1 - c), me).wait_recv()
        for cp in first + passed:
            cp.wait_send()
        mine.wait()

    return pl.pallas_call(
        body, name="allgather_rows", out_shape=jax.ShapeDtypeStruct((8 * m_per, ncol), v.dtype),
        in_specs=[pl.BlockSpec(memory_space=pltpu.VMEM)], out_specs=pl.BlockSpec(memory_space=pltpu.VMEM),
        scratch_shapes=[pltpu.SemaphoreType.DMA((7,)), pltpu.SemaphoreType.DMA((7,)), pltpu.SemaphoreType.DMA],
        compiler_params=pltpu.CompilerParams(vmem_limit_bytes=V7X_VMEM_LIMIT),
    )(v)


def sum_blocks(name, a, nblk):
    m = a.shape[0] // nblk

    def body(a_ref, o_ref):
        acc = a_ref[pl.ds(0, m), :]
        for b in range(1, nblk):
            acc = acc + a_ref[pl.ds(b * m, m), :]
        o_ref[...] = acc

    return pl.pallas_call(body, name=name, out_shape=jax.ShapeDtypeStruct((m, a.shape[1]), a.dtype),
                          compiler_params=pltpu.CompilerParams(vmem_limit_bytes=V7X_VMEM_LIMIT))(a)


BIG = ("w_in", "w_branch_a", "w_branch_b", "w_mix_out", "w_xq", "w_xkv", "w_xo", "w_up", "w_down")
GAINS = ("mix_pre_norm", "gdn_norm", "pool_scale", "mix_post_norm", "xa_pre_norm", "mem_norm", "xa_post_norm",
         "ffn_pre_norm", "ffn_post_norm")
WEIGHTS = ("mix_pre_norm", "w_in", "conv_qkv", "a_log", "dt_bias", "gdn_norm", "pool_w", "pool_scale", "w_branch_a",
           "w_branch_b", "w_mix_out", "mix_post_norm", "xa_pre_norm", "mem_norm", "w_xq", "w_xkv", "w_xo", "xa_post_norm",
           "ffn_pre_norm", "w_up", "ffn_conv_w", "ffn_conv_b", "w_down", "ffn_post_norm")


def _rows128(vecs):
    flat = jnp.concatenate([v.reshape(-1) for v in vecs])
    m = -(-flat.shape[0] // 1024) * 8
    return jnp.pad(flat, (0, m * 128 - flat.shape[0])).reshape(m, 128)


def _unrows(a, shapes):
    flat, out, pos = a.reshape(-1), [], 0
    for sh in shapes:
        n = 1
        for d in sh:
            n *= d
        out.append(flat[pos:pos + n].reshape(sh))
        pos += n
    return out


def _lane128(v):
    return jnp.pad(v.reshape(1, GDN_HEADS), ((0, 0), (GDN_HEADS, 128 - 2 * GDN_HEADS)))


def kernel(x, mem, mix_pre_norm, w_in, conv_qkv, a_log, dt_bias, gdn_norm, pool_w, pool_scale, w_branch_a, w_branch_b, w_mix_out, mix_post_norm, xa_pre_norm, mem_norm, w_xq, w_xkv, w_xo, xa_post_norm, ffn_pre_norm, w_up, ffn_conv_w, ffn_conv_b, w_down, ffn_post_norm, loss_target, m_mix_pre_norm, m_w_in, m_conv_qkv, m_a_log, m_dt_bias, m_gdn_norm, m_pool_w, m_pool_scale, m_w_branch_a, m_w_branch_b, m_w_mix_out, m_mix_post_norm, m_xa_pre_norm, m_mem_norm, m_w_xq, m_w_xkv, m_w_xo, m_xa_post_norm, m_ffn_pre_norm, m_w_up, m_ffn_conv_w, m_ffn_conv_b, m_w_down, m_ffn_post_norm, v_mix_pre_norm, v_w_in, v_conv_qkv, v_a_log, v_dt_bias, v_gdn_norm, v_pool_w, v_pool_scale, v_w_branch_a, v_w_branch_b, v_w_mix_out, v_mix_post_norm, v_xa_pre_norm, v_mem_norm, v_w_xq, v_w_xkv, v_w_xo, v_xa_post_norm, v_ffn_pre_norm, v_w_up, v_ffn_conv_w, v_ffn_conv_b, v_w_down, v_ffn_post_norm):
    given = dict(locals())
    w = {n: given[n][0] for n in WEIGHTS}
    cx, cy, cc = lax.axis_index("x"), lax.axis_index("y"), lax.axis_index("c")
    chip = 2 * cx + cy

    rest = gather_weights([shard_to_slab(n, w[n]) for n in BIG[1:]], on_sequencer=True, name="gather_rest", collective_id=1)
    first = gather_weights([shard_to_slab("w_in", w["w_in"])])
    W = {n: slabs_to_weight(n, g) for n, g in zip(BIG, list(first) + list(rest))}
    sharded_small = (w["conv_qkv"], w["ffn_conv_w"], w["pool_w"])
    allv = allgather_rows(_rows128(sharded_small))
    per_chip = allv.reshape(8, -1)[0::2]
    parts = [_unrows(per_chip[t], [a.shape for a in sharded_small]) for t in range(4)]
    sp = {n: w[n].reshape(1, -1) for n in GAINS}
    sp["a_log"], sp["dt_bias"] = _lane128(w["a_log"]), _lane128(w["dt_bias"])
    sp["conv_qkv"] = jnp.concatenate([p[0] for p in parts], axis=1)
    sp["ffn_conv_w"] = jnp.pad(jnp.stack([p[1] for p in parts]), ((0, 0), (0, 0), (0, FF_BLK - FF_COLS)))
    sp["pool_w"] = jnp.concatenate([p[2] for p in parts], axis=1).reshape(4 * 256, 256)
    sp["ffn_conv_b"] = jnp.pad(w["ffn_conv_b"].reshape(4, 1, FF_COLS), ((0, 0), (0, 0), (0, FF_BLK - FF_COLS)))

    loss, grad_x, gw, gs = local_step(x[0], mem[0], loss_target[0], W, sp)
    loss = lax.psum(loss[0, 0], ("x", "y", "c"))

    g4 = [grad_to_slabs(n, gw) for n in BIG]
    gsib = pair_exchange(g4)
    sums = [pair_add("pair_add_" + n, a, b, cc) for n, a, b in zip(BIG, g4, gsib)]
    recv = scatter_partials([sb for _, sb in sums])
    fin = [final_sum("final_sum_" + n, sf, rv, chip) for n, (sf, _), rv in zip(BIG, sums, recv)]
    fin = share_with_sibling(fin)
    grads = {n: slab_to_shard_grad(n, f) for n, f in zip(BIG, fin)}

    small_names = GAINS + ("a_log", "dt_bias", "ffn_conv_b", "conv_qkv", "ffn_conv_w", "pool_w")
    vec = _rows128([gs[n] for n in small_names])
    total = sum_blocks("sum_small", allgather_rows(vec), 8)
    tot = dict(zip(small_names, _unrows(total, [gs[n].shape for n in small_names])))
    for n in GAINS:
        grads[n] = tot[n].reshape(-1)
    grads["a_log"] = tot["a_log"][0, GDN_HEADS:2 * GDN_HEADS]
    grads["dt_bias"] = tot["dt_bias"][0, GDN_HEADS:2 * GDN_HEADS]
    grads["ffn_conv_b"] = tot["ffn_conv_b"][:, 0, :FF_COLS].reshape(-1)
    grads["conv_qkv"] = lax.dynamic_slice_in_dim(tot["conv_qkv"], chip * 1536, 1536, axis=1)
    grads["ffn_conv_w"] = lax.dynamic_index_in_dim(tot["ffn_conv_w"], chip, axis=0, keepdims=False)[:, :FF_COLS]
    grads["pool_w"] = lax.dynamic_slice_in_dim(tot["pool_w"].reshape(4, 256, 256), chip * 64, 64, axis=1)

    delta, new_m, new_v = {}, {}, {}
    for n in WEIGHTS:
        shp = w[n].shape
        two = (lambda a: a.reshape(-1, shp[-1])) if len(shp) > 1 else (lambda a: a.reshape(1, -1))
        d, nm, nv = adamw("adamw_" + n, two(w[n]), two(grads[n]), two(given["m_" + n][0]), two(given["v_" + n][0]))
        delta[n], new_m[n], new_v[n] = (a.reshape((1,) + shp) for a in (d, nm, nv))
    out_g = [grads[n].reshape((1,) + w[n].shape) for n in WEIGHTS]
    return (loss, grad_x[None], *out_g, *[delta[n] for n in WEIGHTS], *[new_m[n] for n in WEIGHTS], *[new_v[n] for n in WEIGHTS])
```

```python
import functools

import jax
import jax.numpy as jnp
from jax import lax
from jax.experimental import pallas as pl
from jax.experimental.pallas import tpu as pltpu
from jax.experimental.pallas import tpu_sc as plsc

F32 = jnp.float32
BF16 = jnp.bfloat16

D_MODEL = 2048
CHUNK = 64
GDN_HEADS = 16
GDN_DK = 128
POOL_WINDOWS = (2, 4, 8, 16)
XA_HEADS = 4
XA_HEAD_DIM = D_MODEL // XA_HEADS
EPS = 1e-6
GDN_SPAN = 256
GDN_SPAN_BWD = 128
HALO = 16
V7X_VMEM_LIMIT = 56 * 1024 * 1024

ADAM_LR, ADAM_B1, ADAM_B2, ADAM_EPS, ADAM_WD, ADAM_STEP = 0.001, 0.9, 0.999, 1e-08, 0.01, 10

_NN = ((1,), (0,))
_NT = ((1,), (1,))
_TN = ((0,), (0,))


def _cparams(sem):
    return pltpu.CompilerParams(dimension_semantics=sem, vmem_limit_bytes=V7X_VMEM_LIMIT)


_LAST = [None]


def _pcall(body, operands, *, in_specs, out_specs, grid=(), num_scalar_prefetch=0, scratch_shapes=(), **kw):
    operands, in_specs = list(operands), list(in_specs)
    if _LAST[0] is not None:
        n = len(operands)
        inner = body

        def body(*refs):
            return inner(*refs[:n], *refs[n + 1:])

        operands.append(_LAST[0])
        in_specs.append(pl.BlockSpec(memory_space=pl.ANY))
    if num_scalar_prefetch:
        kw["grid_spec"] = pltpu.PrefetchScalarGridSpec(num_scalar_prefetch=num_scalar_prefetch, grid=grid, in_specs=in_specs,
                                                       out_specs=out_specs, scratch_shapes=list(scratch_shapes))
    else:
        kw.update(grid=grid, in_specs=in_specs, out_specs=out_specs, scratch_shapes=list(scratch_shapes))
    res = pl.pallas_call(body, **kw)(*operands)
    _LAST[0] = res[0] if isinstance(res, (list, tuple)) else res
    return res


def _dg(a, b, dims, prec=None):
    return lax.dot_general(a, b, (dims, ((), ())), precision=prec, preferred_element_type=F32)


def _make_dots(cast, prec):
    def raw(dims, a, b):
        return _dg(cast(a), cast(b), dims, prec)

    @jax.custom_vjp
    def nn(a, b):
        return raw(_NN, a, b)

    @jax.custom_vjp
    def nt(a, b):
        return raw(_NT, a, b)

    @jax.custom_vjp
    def tn(a, b):
        return raw(_TN, a, b)

    nn.defvjp(lambda a, b: (raw(_NN, a, b), (a, b)), lambda r, g: (nt(g, r[1]), tn(r[0], g)))
    nt.defvjp(lambda a, b: (raw(_NT, a, b), (a, b)), lambda r, g: (nn(g, r[1]), tn(g, r[0])))
    tn.defvjp(lambda a, b: (raw(_TN, a, b), (a, b)), lambda r, g: (nt(r[1], g), nn(r[0], g)))
    return nn, nt, tn


bdot_nn, bdot_nt, bdot_tn = _make_dots(lambda x: x.astype(BF16), None)
hdot_nn, hdot_nt, hdot_tn = _make_dots(lambda x: x, lax.Precision.HIGHEST)


@functools.partial(jax.custom_vjp, nondiff_argnums=(1,))
def shift_rows(x, k):
    return pltpu.roll(x, k, 0)


def _shift_rows_fwd(x, k):
    return pltpu.roll(x, k, 0), None


def _shift_rows_bwd(k, _, g):
    return (pltpu.roll(g, g.shape[0] - k, 0),)


shift_rows.defvjp(_shift_rows_fwd, _shift_rows_bwd)


@functools.partial(jax.custom_vjp, nondiff_argnums=(1,))
def drop_head(x, h):
    return x[h:]


def _drop_head_fwd(x, h):
    return x[h:], None


def _drop_head_bwd(h, _, g):
    return (jnp.concatenate([jnp.zeros((h,) + g.shape[1:], g.dtype), g], axis=0),)


drop_head.defvjp(_drop_head_fwd, _drop_head_bwd)


@functools.partial(jax.custom_vjp, nondiff_argnums=(1,))
def split_lanes(x, n):
    w = x.shape[-1] // n
    return tuple(x[:, i * w:(i + 1) * w] for i in range(n))


def _split_lanes_fwd(x, n):
    return split_lanes(x, n), None


def _split_lanes_bwd(n, _, gs):
    return (jnp.concatenate(list(gs), axis=-1),)


split_lanes.defvjp(_split_lanes_fwd, _split_lanes_bwd)


@functools.partial(jax.custom_vjp, nondiff_argnums=(1,))
def split_rows(x, n):
    h = x.shape[0] // n
    return tuple(x[i * h:(i + 1) * h] for i in range(n))


def _split_rows_fwd(x, n):
    return split_rows(x, n), None


def _split_rows_bwd(n, _, gs):
    return (jnp.concatenate(list(gs), axis=0),)


split_rows.defvjp(_split_rows_fwd, _split_rows_bwd)


def row_of(w, j):
    rid = lax.broadcasted_iota(jnp.int32, w.shape, 0)
    return jnp.sum(jnp.where(rid == j, w, 0.0), axis=0, keepdims=True)


def sigmoid(x):
    return 0.5 * jnp.tanh(0.5 * x) + 0.5


def silu(x):
    return x * sigmoid(x)


def softplus(x):
    return jnp.maximum(x, 0.0) + jnp.log(1.0 + jnp.exp(-jnp.abs(x)))


def rms(x, g):
    return x * lax.rsqrt(jnp.mean(x * x, axis=-1, keepdims=True) + EPS) * g


def mm(name, dims, pairs, out_shape, out_spec, grid):
    nk = grid[2]
    acc_in_out = nk > 1 and out_shape.dtype == F32
    npair = len(pairs)

    def body(*refs):
        o_ref = refs[2 * npair]
        part = None
        for p in range(npair):
            d = _dg(refs[2 * p][...].astype(BF16), refs[2 * p + 1][...].astype(BF16), dims)
            part = d if part is None else part + d
        if nk == 1:
            o_ref[...] = part.astype(o_ref.dtype)
            return
        acc = o_ref if acc_in_out else refs[2 * npair + 1]
        k = pl.program_id(2)

        @pl.when(k == 0)
        def _():
            acc[...] = part

        @pl.when(k > 0)
        def _():
            acc[...] += part

        if not acc_in_out:
            @pl.when(k == nk - 1)
            def _():
                o_ref[...] = acc[...].astype(o_ref.dtype)

    scratch = []
    if nk > 1 and not acc_in_out:
        scratch = [pltpu.VMEM(tuple(d for d in out_spec.block_shape if d is not None), F32)]
    in_specs, operands = [], []
    for a, b, a_spec, b_spec in pairs:
        in_specs += [a_spec, b_spec]
        operands += [a, b]
    return _pcall(body, operands, name=name, grid=grid, in_specs=in_specs, out_specs=out_spec, out_shape=out_shape,
                  scratch_shapes=scratch, compiler_params=_cparams(("parallel", "parallel", "arbitrary")))


def _bs(shape, fn):
    return pl.BlockSpec(shape, fn)


class Tile:
    def __init__(self, arr, w, cb=0, lead=None, halo=False):
        self.arr, self.w, self.cb, self.lead, self.halo = arr, w, cb, lead, halo


class Out:
    def __init__(self, shape, dtype, w, cb=0, lead=None, into=None):
        self.shape, self.dtype, self.w, self.cb, self.lead, self.into = shape, dtype, w, cb, lead, into


class Par:
    def __init__(self, arr, lead=None):
        self.arr, self.lead = arr, lead


def _spec(rows, w, cb, lead, tile_of):
    if lead is None:
        return pl.BlockSpec((rows, w), lambda o, i: (tile_of(i), cb))
    return pl.BlockSpec((None, rows, w), lambda o, i: (lead(o), tile_of(i), cb))


def _par_spec(p):
    if p.lead is None:
        return pl.BlockSpec(p.arr.shape, lambda o, i: (0, 0))
    return pl.BlockSpec((None,) + p.arr.shape[1:], lambda o, i: (p.lead(o), 0, 0))


def run_stage(name, fn, tm, ins, pars, outs, *, outer=1, cts=None, dins=None):
    T = ins[0].arr.shape[-2]
    nt = T // tm
    bwd = cts is not None
    any_halo = any(t.halo for t in ins)
    step_tile = (lambda i: nt - 1 - i) if bwd else (lambda i: i)
    hb = tm // HALO

    in_specs, operands = [], []
    for t in ins:
        if t.halo:
            in_specs.append(_spec(HALO, t.w, t.cb, t.lead, lambda i: jnp.maximum(step_tile(i) * hb - 1, 0)))
            operands.append(t.arr)
        in_specs.append(_spec(tm, t.w, t.cb, t.lead, step_tile))
        operands.append(t.arr)
    for p in pars:
        in_specs.append(_par_spec(p))
        operands.append(p.arr)
    n_in_refs = len(operands)

    out_descs = list(outs) if not bwd else [d for d in dins if d is not None]
    aliases = {}
    if bwd:
        for c, o in zip(cts, outs):
            in_specs.append(_spec(tm, o.w, o.cb, o.lead, step_tile))
            operands.append(c)
    n_ct = len(operands) - n_in_refs
    for k, o in enumerate(out_descs):
        if o.into is not None:
            aliases[len(operands)] = k
            in_specs.append(pl.BlockSpec(memory_space=pl.ANY))
            operands.append(o.into)
    out_specs = [_spec(tm, o.w, o.cb, o.lead, step_tile) for o in out_descs]
    out_shapes = [jax.ShapeDtypeStruct(o.shape, o.dtype) for o in out_descs]
    if bwd:
        for p in pars:
            out_specs.append(_par_spec(p))
            out_shapes.append(jax.ShapeDtypeStruct(p.arr.shape, F32))
    scratch = []
    if bwd and any_halo:
        scratch = [pltpu.VMEM((HALO, t.w), F32) for t, d in zip(ins, dins) if t.halo and d is not None]

    def body(*refs):
        i = pl.program_id(1)
        tile = step_tile(i)
        row0 = tile * tm
        pos = 0
        tiles = []
        for t in ins:
            if t.halo:
                prev = jnp.where(tile > 0, refs[pos][...].astype(F32), 0.0)
                tiles.append(jnp.concatenate([prev, refs[pos + 1][...].astype(F32)], axis=0))
                pos += 2
            else:
                tiles.append(refs[pos][...].astype(F32))
                pos += 1
        pvals = [refs[pos + k][...].astype(F32) for k in range(len(pars))]
        pos += len(pars)
        if not bwd:
            res = fn(tiles, pvals, row0)
            for o_ref, r in zip(refs[pos:], res):
                o_ref[...] = r.astype(o_ref.dtype)
            return
        ct_vals = [refs[pos + k][...].astype(F32) for k in range(n_ct)]
        pos += n_ct + len(aliases)
        _, vjp_fn = jax.vjp(lambda tt, pp: fn(tt, pp, row0), tiles, pvals)
        d_tiles, d_pars = vjp_fn(ct_vals)
        carries = list(refs[len(refs) - len(scratch):])
        for t, d, dt in zip(ins, dins, d_tiles):
            if d is None:
                continue
            o_ref = refs[pos]
            pos += 1
            if t.halo:
                carry = carries.pop(0)
                main = dt[HALO:]
                tail = main[tm - HALO:] + jnp.where(i > 0, carry[...], 0.0)
                o_ref[...] = jnp.concatenate([main[:tm - HALO], tail], axis=0).astype(o_ref.dtype)
                carry[...] = dt[:HALO]
            else:
                o_ref[...] = dt.astype(o_ref.dtype)
        for dp in d_pars:
            acc = refs[pos]
            pos += 1

            @pl.when(i == 0)
            def _(acc=acc, dp=dp):
                acc[...] = dp

            @pl.when(i > 0)
            def _(acc=acc, dp=dp):
                acc[...] += dp

    res = _pcall(body, operands, name=name, grid=(outer, nt), in_specs=in_specs, out_specs=out_specs, out_shape=out_shapes,
                 scratch_shapes=scratch, input_output_aliases=aliases, compiler_params=_cparams(("arbitrary", "arbitrary")))
    if not bwd:
        return list(res)
    n_d = len(out_descs)
    d_full, it = [], iter(res[:n_d])
    for d in dins:
        d_full.append(None if d is None else next(it))
    return d_full, list(res[n_d:])


def f_prenorm(t, p, row0):
    return [rms(t[0], p[0])]


def f_prenorm_res(t, p, row0):
    return [t[0], rms(t[0], p[0])]


def f_post_pre(t, p, row0):
    x, y = t
    x1 = x + rms(y, p[0])
    return [x1, rms(x1, p[1])]


def _causal_conv(x, w, taps):
    y = x * row_of(w, taps - 1)
    for j in range(taps - 1):
        y = y + shift_rows(x, taps - 1 - j) * row_of(w, j)
    return drop_head(y, HALO)


def _l2(x):
    return x * lax.rsqrt(jnp.sum(x * x, axis=-1, keepdims=True) + EPS)


def make_f_convhead(scale, normalise):
    def f(t, p, row0):
        y = silu(_causal_conv(t[0], p[0], 4))
        if not normalise:
            return [y]
        return [jnp.concatenate([_l2(c) * scale for c in split_lanes(y, GDN_HEADS)], axis=-1)]
    return f


def f_bg(t, p, row0):
    ba = split_lanes(t[0], 4)[0]
    alog, dtb = p
    lane = lax.broadcasted_iota(jnp.int32, ba.shape, 1)
    bg = jnp.where(lane < GDN_HEADS, sigmoid(ba), -jnp.exp(alog) * softplus(ba + dtb))
    return [jnp.where(lane < 2 * GDN_HEADS, bg, 0.0)]


def f_gnorm(t, p, row0):
    o, z = t
    po, pz = split_lanes(o, GDN_HEADS), split_lanes(z, GDN_HEADS)
    return [jnp.concatenate([rms(a, p[0]) * silu(b) for a, b in zip(po, pz)], axis=-1)]


def f_pool(t, p, row0):
    x = t[0]
    pw, psc = p
    tm = x.shape[0] - HALO
    tpos = (row0 + lax.broadcasted_iota(jnp.int32, (tm, 1), 0) + 1).astype(F32)
    outs = []
    for xg, wg, win in zip(split_lanes(x, 4), split_rows(pw, 4), POOL_WINDOWS):
        s, span = xg, 1
        while span < win:
            s = s + shift_rows(s, span)
            span *= 2
        mean = drop_head(s, HALO) / jnp.minimum(tpos, float(win))
        outs.append(bdot_nn(mean - drop_head(xg, HALO), wg))
    return [jnp.concatenate(outs, axis=-1) * psc]


def f_merge(t, p, row0):
    gates, ya, yb = t
    ga, gb = split_lanes(gates, 2)
    return [sigmoid(ga) * ya + sigmoid(gb) * yb]


def f_xattn(t, p, row0):
    k, v = p
    outs = []
    for qh, kh, vh in zip(split_lanes(t[0], XA_HEADS), split_lanes(k, XA_HEADS), split_lanes(v, XA_HEADS)):
        s = bdot_nt(qh, kh) * (XA_HEAD_DIM ** -0.5)
        s = s - jnp.max(s, axis=-1, keepdims=True)
        e = jnp.exp(s)
        outs.append(bdot_nn(e / jnp.sum(e, axis=-1, keepdims=True), vh))
    return [jnp.concatenate(outs, axis=-1)]


def f_convglu(t, p, row0):
    ua, ub = t
    cwa, cwb, ba, bb = p
    return [silu(_causal_conv(ua, cwa, 3) + ba) * (_causal_conv(ub, cwb, 3) + bb)]


_BNN = (((2,), (1,)), ((0,), (0,)))
_BNT = (((2,), (2,)), ((0,), (0,)))
_BTN = (((1,), (1,)), ((0,), (0,)))


def _make_batched_dots():
    def raw(dims, a, b):
        return lax.dot_general(a.astype(BF16), b.astype(BF16), dims, preferred_element_type=F32)

    @jax.custom_vjp
    def nn(a, b):
        return raw(_BNN, a, b)

    @jax.custom_vjp
    def nt(a, b):
        return raw(_BNT, a, b)

    @jax.custom_vjp
    def tn(a, b):
        return raw(_BTN, a, b)

    nn.defvjp(lambda a, b: (raw(_BNN, a, b), (a, b)), lambda r, g: (nt(g, r[1]), tn(r[0], g)))
    nt.defvjp(lambda a, b: (raw(_BNT, a, b), (a, b)), lambda r, g: (nn(g, r[1]), tn(g, r[0])))
    tn.defvjp(lambda a, b: (raw(_BTN, a, b), (a, b)), lambda r, g: (nt(r[1], g), nn(r[0], g)))
    return nn, nt, tn


bb_nn, bb_nt, bb_tn = _make_batched_dots()


def _gdn_local(q, k, v, gcol, bcol):
    C = CHUNK
    r = lax.broadcasted_iota(jnp.int32, (1, C, C), 1)
    c = lax.broadcasted_iota(jnp.int32, (1, C, C), 2)
    eye, incl, strict = r == c, r >= c, r > c
    grow = jnp.sum(jnp.where(eye, gcol, 0.0), axis=1, keepdims=True)
    Gcol = jnp.sum(jnp.where(incl, grow, 0.0), axis=2, keepdims=True)
    Grow = jnp.sum(jnp.where(eye, Gcol, 0.0), axis=1, keepdims=True)
    decay = jnp.where(incl, jnp.exp(jnp.where(incl, Gcol - Grow, 0.0)), 0.0)
    X = -jnp.where(strict, bcol * decay * bb_nt(k, k), 0.0)
    N, P = X, X
    for _ in range(5):
        P = bb_nn(P, P)
        N = N + P + bb_nn(N, P)
    expg = jnp.exp(Gcol)
    rv, rk = bcol * v, (bcol * expg) * k
    u_v = rv + bb_nn(N, rv)
    w_k = rk + bb_nn(N, rk)
    attn = decay * bb_nt(q, k)
    rid = lax.broadcasted_iota(jnp.int32, (1, C, 1), 1)
    glast = jnp.sum(jnp.where(rid == C - 1, Gcol, 0.0), axis=1, keepdims=True)
    return u_v, w_k, attn, q * expg, k * jnp.exp(glast - Gcol), jnp.exp(glast)


def _gdn_rec(u_v, w_k, attn, q_dec, k_dec, cd, S):
    u = u_v - bb_nn(w_k, S)
    o = bb_nn(q_dec, S) + bb_nn(attn, u)
    return o, cd * S + bb_tn(k_dec, u)


def _gdn_load(q_ref, k_ref, v_ref, bg_ref, nch):
    H = GDN_HEADS

    def batched(ref):
        return jnp.stack([ref[c * CHUNK:(c + 1) * CHUNK, h * GDN_DK:(h + 1) * GDN_DK] for c in range(nch) for h in range(H)])

    bg = bg_ref[...]
    lane = lax.broadcasted_iota(jnp.int32, bg.shape, 1)
    bcols = [jnp.sum(jnp.where(lane == h, bg, 0.0), axis=-1, keepdims=True) for h in range(H)]
    gcols = [jnp.sum(jnp.where(lane == H + h, bg, 0.0), axis=-1, keepdims=True) for h in range(H)]
    pick = lambda cols: jnp.stack([cols[h][c * CHUNK:(c + 1) * CHUNK] for c in range(nch) for h in range(H)])
    return batched(q_ref), batched(k_ref), batched(v_ref), pick(gcols), pick(bcols)


def _gdn_store(ref, val, nch):
    H = GDN_HEADS
    for c in range(nch):
        for h in range(H):
            ref[c * CHUNK:(c + 1) * CHUNK, h * GDN_DK:(h + 1) * GDN_DK] = val[c * H + h]


def gdn_forward(q, k, v, bg, span):
    T = q.shape[0]
    ns, nch, H = T // span, span // CHUNK, GDN_HEADS

    def body(q_ref, k_ref, v_ref, bg_ref, o_ref, s_ref, state):
        @pl.when(pl.program_id(0) == 0)
        def _():
            state[...] = jnp.zeros(state.shape, F32)

        loc = _gdn_local(*_gdn_load(q_ref, k_ref, v_ref, bg_ref, nch))
        S = state[...]
        for c in range(nch):
            s_ref[c] = S
            o, S = _gdn_rec(*[a[c * H:(c + 1) * H] for a in loc], S)
            for h in range(H):
                o_ref[c * CHUNK:(c + 1) * CHUNK, h * GDN_DK:(h + 1) * GDN_DK] = o[h]
        state[...] = S

    wide = pl.BlockSpec((span, H * GDN_DK), lambda s: (s, 0))
    return _pcall(
        body, (q, k, v, bg), name="gdn_fwd", grid=(ns,),
        in_specs=[wide, wide, wide, pl.BlockSpec((span, 128), lambda s: (s, 0))],
        out_specs=[wide, pl.BlockSpec((nch, H, GDN_DK, GDN_DK), lambda s: (s, 0, 0, 0))],
        out_shape=[jax.ShapeDtypeStruct((T, H * GDN_DK), F32), jax.ShapeDtypeStruct((T // CHUNK, H, GDN_DK, GDN_DK), F32)],
        scratch_shapes=[pltpu.VMEM((H, GDN_DK, GDN_DK), F32)],
        compiler_params=_cparams(("arbitrary",)),
    )


def gdn_backward(q, k, v, bg, starts, do, span):
    T = q.shape[0]
    ns, nch, H = T // span, span // CHUNK, GDN_HEADS

    def body(q_ref, k_ref, v_ref, bg_ref, s_ref, do_ref, dq_ref, dk_ref, dv_ref, dbg_ref, dstate):
        @pl.when(pl.program_id(0) == 0)
        def _():
            dstate[...] = jnp.zeros(dstate.shape, F32)

        loc, vjp_loc = jax.vjp(_gdn_local, *_gdn_load(q_ref, k_ref, v_ref, bg_ref, nch))
        dS = dstate[...]
        dloc = [None] * nch
        for c in reversed(range(nch)):
            _, vjp_rec = jax.vjp(_gdn_rec, *[a[c * H:(c + 1) * H] for a in loc], s_ref[c])
            do_c = jnp.stack([do_ref[c * CHUNK:(c + 1) * CHUNK, h * GDN_DK:(h + 1) * GDN_DK] for h in range(H)])
            *dloc[c], dS = vjp_rec((do_c, dS))
        dstate[...] = dS
        dq, dk, dv, dg, db = vjp_loc(tuple(jnp.concatenate([dloc[c][i] for c in range(nch)], axis=0) for i in range(6)))
        _gdn_store(dq_ref, dq, nch)
        _gdn_store(dk_ref, dk, nch)
        _gdn_store(dv_ref, dv, nch)
        lane = lax.broadcasted_iota(jnp.int32, (CHUNK, 128), 1)
        for c in range(nch):
            acc = jnp.zeros((CHUNK, 128), F32)
            for h in range(H):
                acc = acc + jnp.where(lane == h, db[c * H + h], 0.0) + jnp.where(lane == H + h, dg[c * H + h], 0.0)
            dbg_ref[c * CHUNK:(c + 1) * CHUNK, :] = acc

    wide = pl.BlockSpec((span, H * GDN_DK), lambda i: (ns - 1 - i, 0))
    bgs = pl.BlockSpec((span, 128), lambda i: (ns - 1 - i, 0))
    return _pcall(
        body, (q, k, v, bg, starts, do), name="gdn_bwd", grid=(ns,),
        in_specs=[wide, wide, wide, bgs, pl.BlockSpec((nch, H, GDN_DK, GDN_DK), lambda i: (ns - 1 - i, 0, 0, 0)), wide],
        out_specs=[wide, wide, wide, bgs],
        out_shape=[jax.ShapeDtypeStruct((T, H * GDN_DK), F32)] * 3 + [jax.ShapeDtypeStruct((T, 128), F32)],
        scratch_shapes=[pltpu.VMEM((H, GDN_DK, GDN_DK), F32)],
        compiler_params=_cparams(("arbitrary",)),
    )


def loss_stage(x2, y3, tgt, g, tm):
    T, D = x2.shape
    nt = T // tm

    def body(x_ref, y_ref, t_ref, g_ref, loss_ref, dx_ref, dy_ref, dg_ref):
        i = pl.program_id(0)
        tgtv = t_ref[...]

        def f(x, y, gg):
            err = x + rms(y, gg) - tgtv
            return 0.5 * jnp.mean(err * err, axis=-1, keepdims=True)

        rows, vjp_fn = jax.vjp(f, x_ref[...], y_ref[...], g_ref[...])
        dx, dy, dg = vjp_fn(jnp.ones_like(rows))
        dx_ref[...] = dx
        dy_ref[...] = dy.astype(dy_ref.dtype)
        part = jnp.sum(rows, axis=0, keepdims=True)

        @pl.when(i == 0)
        def _():
            loss_ref[...] = part
            dg_ref[...] = dg

        @pl.when(i > 0)
        def _():
            loss_ref[...] += part
            dg_ref[...] += dg

    tile = pl.BlockSpec((tm, D), lambda i: (i, 0))
    gs = pl.BlockSpec((1, D), lambda i: (0, 0))
    return _pcall(
        body, (x2, y3, tgt, g), name="loss_head", grid=(nt,), in_specs=[tile, tile, tile, gs],
        out_specs=[pl.BlockSpec((1, 1), lambda i: (0, 0)), tile, tile, gs],
        out_shape=[jax.ShapeDtypeStruct((1, 1), F32), jax.ShapeDtypeStruct((T, D), F32),
                   jax.ShapeDtypeStruct((T, D), BF16), jax.ShapeDtypeStruct((1, D), F32)],
        compiler_params=_cparams(("arbitrary",)),
    )


def adamw(name, w, g, m, v):
    R, C = w.shape
    tr = R
    for cand in (256, 128, 64, 32, 16, 8):
        if R % cand == 0 and R > cand and cand * C * 4 <= (2 << 20):
            tr = cand
            break
    c1 = 1.0 / (1.0 - ADAM_B1 ** ADAM_STEP)
    c2 = 1.0 / (1.0 - ADAM_B2 ** ADAM_STEP)

    def body(w_ref, g_ref, m_ref, v_ref, d_ref, nm_ref, nv_ref):
        gg = g_ref[...]
        nm = ADAM_B1 * m_ref[...] + (1.0 - ADAM_B1) * gg
        nv = ADAM_B2 * v_ref[...] + (1.0 - ADAM_B2) * (gg * gg)
        d_ref[...] = -ADAM_LR * ((nm * c1) / (jnp.sqrt(nv * c2) + ADAM_EPS) + ADAM_WD * w_ref[...])
        nm_ref[...] = nm
        nv_ref[...] = nv

    spec = pl.BlockSpec((tr, C), lambda i: (i, 0))
    return _pcall(
        body, (w, g, m, v), name=name, grid=(R // tr,), in_specs=[spec] * 4, out_specs=[spec] * 3,
        out_shape=[jax.ShapeDtypeStruct((R, C), F32)] * 3, compiler_params=_cparams(("parallel",)),
    )


C_Z, C_GATES, C_POOL, C_BA, N_PROJ = 6144, 8192, 12288, 13312, 13824
FF_BLK = 2816
FF_COLS = 2752


def local_step(x, mem, tgt, W, sp, hook=lambda event, gw: None):
    T, D = x.shape
    tm = 256
    tmm = min(512, T)
    nI = T // tmm
    S = jax.ShapeDtypeStruct
    gw, gs = {}, {}

    def stage(name, fn, ins, pars, outs, **kw):
        return run_stage(name, fn, kw.pop("tm", tm), ins, pars, outs, **kw)

    def dense(name, a, w, out_dtype=F32, tn=2048):
        Tq, Kd = a.shape
        N = w.shape[1]
        tq = min(tmm, Tq)
        return mm(name, _NN, [(a, w, _bs((tq, Kd), lambda j, i, k: (i, 0)), _bs((Kd, tn), lambda j, i, k: (0, j)))],
                  S((Tq, N), out_dtype), _bs((tq, tn), lambda j, i, k: (i, j)), (N // tn, Tq // tq, 1))

    def dense_t(name, g, w, out_dtype=F32, tn=2048):
        Tq, N = g.shape
        Kd = w.shape[0]
        tq = min(tmm, Tq)
        return mm(name, _NT, [(g, w, _bs((tq, N), lambda j, i, k: (i, 0)), _bs((tn, N), lambda j, i, k: (j, 0)))],
                  S((Tq, Kd), out_dtype), _bs((tq, tn), lambda j, i, k: (i, j)), (Kd // tn, Tq // tq, 1))

    def wgrad(name, a, g, ta=1024, tn=2048):
        Tq, Kd = a.shape
        N = g.shape[1]
        tt = min(1024, Tq)
        return mm(name, _TN, [(a, g, _bs((tt, ta), lambda i, j, k: (k, i)), _bs((tt, tn), lambda i, j, k: (k, j)))],
                  S((Kd, N), F32), _bs((ta, tn), lambda i, j, k: (i, j)), (Kd // ta, N // tn, Tq // tt))

    o2048 = lambda dt: Out((T, D), dt, D)

    (h1,) = stage("pre1", f_prenorm, [Tile(x, D)], [Par(sp["mix_pre_norm"])], [o2048(BF16)])
    tnp = 1536
    P2 = mm("in_proj", _NT, [(h1, W["w_in"], _bs((tmm, D), lambda j, i, k: (i, 0)), _bs((tnp, D), lambda j, i, k: (j, 0)))],
            S((T, N_PROJ), F32), _bs((tmm, tnp), lambda j, i, k: (i, j)), (N_PROJ // tnp, nI, 1))
    cw = sp["conv_qkv"]
    cws = [cw[:, i * D:(i + 1) * D] for i in range(3)]
    f_heads = [make_f_convhead(GDN_DK ** -0.5, True), make_f_convhead(1.0, True), make_f_convhead(1.0, False)]
    qkv = [stage("conv_" + n, f_heads[i], [Tile(P2, D, cb=i, halo=True)], [Par(cws[i])], [o2048(F32)])[0]
           for i, n in enumerate("qkv")]
    ba_tile = Tile(P2, 512, cb=C_BA // 512)
    (bg,) = stage("bg", f_bg, [ba_tile], [Par(sp["a_log"]), Par(sp["dt_bias"])], [Out((T, 128), F32, 128)])
    o, s0 = gdn_forward(qkv[0], qkv[1], qkv[2], bg, GDN_SPAN)
    z_tile = Tile(P2, D, cb=C_Z // D)
    (o_n,) = stage("gnorm", f_gnorm, [Tile(o, D), z_tile], [Par(sp["gdn_norm"])], [o2048(BF16)])
    y_a = dense("branch_a", o_n, W["w_branch_a"])
    p_tile = Tile(P2, 1024, cb=C_POOL // 1024, halo=True)
    pool_pars = [Par(sp["pool_w"]), Par(sp["pool_scale"])]
    (pooled,) = stage("pool", f_pool, [p_tile], pool_pars, [Out((T, 1024), BF16, 1024)])
    y_b = mm("branch_b", _NN, [(pooled, W["w_branch_b"], _bs((tmm, 1024), lambda j, i, k: (i, 0)),
                                _bs((None, 1024, 512), lambda j, i, k: (j, 0, 0)))],
             S((T, D), F32), _bs((tmm, 512), lambda j, i, k: (i, j)), (4, nI, 1))
    gate_tile = Tile(P2, 2 * D, cb=C_GATES // (2 * D))
    merge_ins = [gate_tile, Tile(y_a, D), Tile(y_b, D)]
    (merged,) = stage("merge", f_merge, merge_ins, [], [o2048(BF16)])
    y1 = dense("mix_out", merged, W["w_mix_out"])
    pp1 = [Par(sp["mix_post_norm"]), Par(sp["xa_pre_norm"])]
    x1, h2 = stage("post1", f_post_pre, [Tile(x, D), Tile(y1, D)], pp1, [o2048(F32), o2048(BF16)])

    q2 = dense("xq", h2, W["w_xq"], out_dtype=BF16)
    (mn,) = stage("mem_norm", f_prenorm, [Tile(mem, D)], [Par(sp["mem_norm"])], [Out(mem.shape, BF16, D)], tm=mem.shape[0])
    M = mem.shape[0]
    kv = mm("xkv", _NN, [(mn, W["w_xkv"], _bs((M, D), lambda j, i, k: (0, 0)), _bs((None, D, 1024), lambda j, i, k: (j, 0, 0)))],
            S((M, 2 * D), F32), _bs((M, 1024), lambda j, i, k: (0, j)), (4, 1, 1))
    k2, v2 = kv[:, :D], kv[:, D:]
    xa_pars = [Par(k2), Par(v2)]
    (o2,) = stage("xattn", f_xattn, [Tile(q2, D)], xa_pars, [o2048(BF16)])
    y2 = dense("xo", o2, W["w_xo"])
    pp2 = [Par(sp["xa_post_norm"]), Par(sp["ffn_pre_norm"])]
    x2, h3 = stage("post2", f_post_pre, [Tile(x1, D), Tile(y2, D)], pp2, [o2048(F32), o2048(BF16)])

    def up(name, off):
        return mm(name, _NN, [(h3, W["w_up"], _bs((tmm, D), lambda j, i, k: (i, 0)),
                               _bs((None, D, FF_BLK), lambda j, i, k: (j + off, 0, 0)))],
                  S((2, T, FF_BLK), F32), _bs((None, tmm, FF_BLK), lambda j, i, k: (j, i, 0)), (2, nI, 1))

    Ua, Ub = up("up_a", 0), up("up_b", 2)
    ffn_ins = [Tile(Ua, FF_BLK, lead=lambda o: o, halo=True), Tile(Ub, FF_BLK, lead=lambda o: o, halo=True)]
    ffn_pars = [Par(sp["ffn_conv_w"], lead=lambda o: o), Par(sp["ffn_conv_w"], lead=lambda o: o + 2),
                Par(sp["ffn_conv_b"], lead=lambda o: o), Par(sp["ffn_conv_b"], lead=lambda o: o + 2)]
    ffn_out = [Out((2, T, FF_BLK), BF16, FF_BLK, lead=lambda o: o)]
    (ff,) = stage("convglu", f_convglu, ffn_ins, ffn_pars, ffn_out, outer=2)
    y3 = mm("down", _NN, [(ff, W["w_down"], _bs((None, tmm, FF_BLK), lambda i, j, k: (k, i, 0)),
                           _bs((None, FF_BLK, D), lambda i, j, k: (k, 0, 0)))],
            S((T, D), F32), _bs((tmm, D), lambda i, j, k: (i, 0)), (nI, 1, 2))
    loss, dx2, dy3, gs["ffn_post_norm"] = loss_stage(x2, y3, tgt, sp["ffn_post_norm"], tm)

    dff = mm("down_dx", _NT, [(dy3, W["w_down"], _bs((tmm, D), lambda j, i, k: (i, 0)),
                               _bs((None, FF_BLK, D), lambda j, i, k: (j, 0, 0)))],
             S((2, T, FF_BLK), BF16), _bs((None, tmm, FF_BLK), lambda j, i, k: (j, i, 0)), (2, nI, 1))
    tbig = min(1024, T)
    gw["w_down"] = mm("down_dw", _TN, [(ff, dy3, _bs((None, tbig, FF_BLK), lambda b, j, k: (b, k, 0)),
                                        _bs((tbig, 1024), lambda b, j, k: (k, j)))],
                      S((2, FF_BLK, D), F32), _bs((None, FF_BLK, 1024), lambda b, j, k: (b, 0, j)), (2, D // 1024, T // tbig))
    dU_out = [Out((2, T, FF_BLK), BF16, FF_BLK, lead=lambda o: o), Out((2, T, FF_BLK), BF16, FF_BLK, lead=lambda o: o)]
    (dUa, dUb), dffn = stage("convglu_bwd", f_convglu, ffn_ins, ffn_pars, ffn_out, outer=2, cts=[dff], dins=dU_out)
    gs["ffn_conv_w"] = jnp.concatenate([dffn[0][:2], dffn[1][2:]], axis=0)
    gs["ffn_conv_b"] = jnp.concatenate([dffn[2][:2], dffn[3][2:]], axis=0)
    dh3 = mm("up_dx", _NT, [(dUa, W["w_up"], _bs((None, tmm, FF_BLK), lambda i, j, k: (k, i, 0)),
                             _bs((None, 1024, FF_BLK), lambda i, j, k: (k, j, 0))),
                            (dUb, W["w_up"], _bs((None, tmm, FF_BLK), lambda i, j, k: (k, i, 0)),
                             _bs((None, 1024, FF_BLK), lambda i, j, k: (k + 2, j, 0)))],
             S((T, D), F32), _bs((tmm, 1024), lambda i, j, k: (i, j)), (nI, D // 1024, 2))

    def up_dw(name, dU):
        tt = min(2048, T)
        return mm(name, _TN, [(h3, dU, _bs((tt, 512), lambda b, i, k: (k, i)), _bs((None, tt, FF_BLK), lambda b, i, k: (b, k, 0)))],
                  S((2, D, FF_BLK), F32), _bs((None, 512, FF_BLK), lambda b, i, k: (b, i, 0)), (2, D // 512, T // tt))

    gw["w_up_a"], gw["w_up_b"] = up_dw("up_dw_a", dUa), up_dw("up_dw_b", dUb)
    hook("ffn", gw)
    (dx1, dy2), dpp2 = stage("post2_bwd", f_post_pre, [Tile(x1, D), Tile(y2, D)], pp2, [o2048(F32), o2048(BF16)],
                             cts=[dx2, dh3], dins=[o2048(F32), o2048(BF16)])
    gs["xa_post_norm"], gs["ffn_pre_norm"] = dpp2

    do2 = dense_t("xo_dx", dy2, W["w_xo"])
    gw["w_xo"] = wgrad("xo_dw", o2, dy2)
    (dq2,), (dk2, dv2) = stage("xattn_bwd", f_xattn, [Tile(q2, D)], xa_pars, [o2048(BF16)], cts=[do2], dins=[o2048(BF16)])
    dh2 = dense_t("xq_dx", dq2, W["w_xq"])
    gw["w_xq"] = wgrad("xq_dw", h2, dq2)
    dkv = jnp.concatenate([dk2, dv2], axis=1).astype(BF16)
    dmn = mm("xkv_dx", _NT, [(dkv, W["w_xkv"], _bs((M, 1024), lambda i, j, k: (0, k)), _bs((None, 512, 1024), lambda i, j, k: (k, j, 0)))],
             S((M, D), F32), _bs((M, 512), lambda i, j, k: (0, j)), (1, D // 512, 4))
    gw["w_xkv"] = mm("xkv_dw", _TN, [(mn, dkv, _bs((M, D), lambda b, j, k: (0, 0)), _bs((M, 1024), lambda b, j, k: (0, b)))],
                     S((4, D, 1024), F32), _bs((None, D, 1024), lambda b, j, k: (b, 0, 0)), (4, 1, 1))
    hook("xattn", gw)
    _, (gs["mem_norm"],) = stage("mem_norm_bwd", f_prenorm, [Tile(mem, D)], [Par(sp["mem_norm"])], [Out(mem.shape, BF16, D)],
                                 tm=M, cts=[dmn], dins=[None])
    (dx0, dy1), dpp1 = stage("post1_bwd", f_post_pre, [Tile(x, D), Tile(y1, D)], pp1, [o2048(F32), o2048(BF16)],
                             cts=[dx1, dh2], dins=[o2048(F32), o2048(BF16)])
    gs["mix_post_norm"], gs["xa_pre_norm"] = dpp1

    dmerged = dense_t("mix_out_dx", dy1, W["w_mix_out"])
    gw["w_mix_out"] = wgrad("mix_out_dw", merged, dy1)
    pshape = (T, N_PROJ)
    (dP2, dya, dyb), _ = stage("merge_bwd", f_merge, merge_ins, [], [o2048(BF16)], cts=[dmerged],
                               dins=[Out(pshape, BF16, 2 * D, cb=C_GATES // (2 * D)), o2048(BF16), o2048(BF16)])
    d_on = dense_t("branch_a_dx", dya, W["w_branch_a"])
    gw["w_branch_a"] = wgrad("branch_a_dw", o_n, dya)
    dpooled = mm("branch_b_dx", _NT, [(dyb, W["w_branch_b"], _bs((tmm, 512), lambda i, j, k: (i, k)),
                                       _bs((None, 1024, 512), lambda i, j, k: (k, 0, 0)))],
                 S((T, 1024), F32), _bs((tmm, 1024), lambda i, j, k: (i, 0)), (nI, 1, 4))
    gw["w_branch_b"] = mm("branch_b_dw", _TN, [(pooled, dyb, _bs((tmm, 1024), lambda b, j, k: (k, 0)), _bs((tmm, 512), lambda b, j, k: (k, b)))],
                          S((4, 1024, 512), F32), _bs((None, 1024, 512), lambda b, j, k: (b, 0, 0)), (4, 1, nI))
    hook("mixer", gw)
    (do, dP2), (gs["gdn_norm"],) = stage("gnorm_bwd", f_gnorm, [Tile(o, D), z_tile], [Par(sp["gdn_norm"])], [o2048(BF16)],
                                         cts=[d_on], dins=[o2048(F32), Out(pshape, BF16, D, cb=C_Z // D, into=dP2)])
    dq, dk, dv, dbg = gdn_backward(qkv[0], qkv[1], qkv[2], bg, s0, do, GDN_SPAN_BWD)
    dcw = []
    for i, (n, dqq) in enumerate(zip("qkv", (dq, dk, dv))):
        (dP2,), (dc,) = stage("conv_%s_bwd" % n, f_heads[i], [Tile(P2, D, cb=i, halo=True)], [Par(cws[i])], [o2048(F32)],
                              cts=[dqq], dins=[Out(pshape, BF16, D, cb=i, into=dP2)])
        dcw.append(dc)
    gs["conv_qkv"] = jnp.concatenate(dcw, axis=1)
    (dP2,), (gs["a_log"], gs["dt_bias"]) = stage("bg_bwd", f_bg, [ba_tile], [Par(sp["a_log"]), Par(sp["dt_bias"])],
                                                 [Out((T, 128), F32, 128)], cts=[dbg],
                                                 dins=[Out(pshape, BF16, 512, cb=C_BA // 512, into=dP2)])
    (dP2,), (gs["pool_w"], gs["pool_scale"]) = stage("pool_bwd", f_pool, [p_tile], pool_pars, [Out((T, 1024), BF16, 1024)],
                                                     cts=[dpooled], dins=[Out(pshape, BF16, 1024, cb=C_POOL // 1024, into=dP2)])
    tk, ta, tt = 2304, 1152, min(2048, T)
    gw["w_in"] = mm("in_proj_dw", _TN, [(dP2, h1, _bs((tt, ta), lambda i, j, k: (k, i)), _bs((tt, D), lambda i, j, k: (k, 0)))],
                    S((N_PROJ, D), F32), _bs((ta, D), lambda i, j, k: (i, 0)), (N_PROJ // ta, 1, T // tt))
    hook("in_proj", gw)
    dh1 = mm("in_proj_dx", _NN, [(dP2, W["w_in"], _bs((tbig, tk), lambda i, j, k: (i, k)), _bs((tk, D), lambda i, j, k: (k, 0)))],
             S((T, D), F32), _bs((tbig, D), lambda i, j, k: (i, 0)), (T // tbig, 1, N_PROJ // tk))
    (grad_x,), (gs["mix_pre_norm"],) = stage("pre1_bwd", f_prenorm_res, [Tile(x, D)], [Par(sp["mix_pre_norm"])],
                                             [o2048(F32), o2048(BF16)], cts=[dx0, dh1], dins=[o2048(F32)])
    return loss, grad_x, gw, gs


W_IN_COLS = 13344
GROUPED = {"w_branch_b": 512, "w_xkv": 1024}
ROW_SHARDED = ("w_branch_a", "w_mix_out", "w_xq", "w_xo")


def shard_to_slab(name, w):
    if name == "w_in":
        return w.T.astype(BF16)
    if name == "w_up":
        return jnp.pad(w, ((0, 0), (0, FF_BLK - FF_COLS))).astype(BF16)
    return w.astype(BF16)


def slabs_to_weight(name, g):
    if name == "w_in":
        full = g.astype(F32).reshape(W_IN_COLS, D_MODEL)
        pad = jnp.zeros((N_PROJ - W_IN_COLS, D_MODEL), F32)
        return jnp.concatenate([full[0:8192], full[9248:13344], full[8224:9248], full[8192:8224], pad]).astype(BF16)
    if name == "w_down":
        z = jnp.zeros((2, FF_BLK - FF_COLS, D_MODEL), g.dtype)
        return jnp.concatenate([g.reshape(2, FF_COLS, D_MODEL), z], axis=1)
    if name in ROW_SHARDED:
        return g.reshape(D_MODEL, D_MODEL)
    return g


def grad_to_slabs(name, gw):
    if name == "w_in":
        g = gw["w_in"]
        return jnp.concatenate([g[0:8192], g[13312:13344], g[12288:13312], g[8192:12288]]).reshape(4, 3336, D_MODEL)
    if name == "w_up":
        return jnp.concatenate([gw["w_up_a"], gw["w_up_b"]], axis=0)
    if name == "w_down":
        return gw["w_down"][:, :FF_COLS].reshape(4, FF_COLS // 2, D_MODEL)
    if name in ROW_SHARDED:
        return gw[name].reshape(4, D_MODEL // 4, D_MODEL)
    return gw[name]


def slab_to_shard_grad(name, f):
    if name == "w_in":
        return f.T
    if name == "w_up":
        return f[:, :FF_COLS]
    return f


MESH = pl.DeviceIdType.MESH
ANY = pl.BlockSpec(memory_space=pl.ANY)


def _me():
    x, y, c = lax.axis_index("x"), lax.axis_index("y"), lax.axis_index("c")
    return x, y, c, 2 * x + y


def _chip_dev(t, c):
    return (t // 2, t % 2, c)


def _rcopy(src, dst, ssem, rsem, dev):
    return pltpu.make_async_remote_copy(src_ref=src, dst_ref=dst, send_sem=ssem, recv_sem=rsem, device_id=dev, device_id_type=MESH)


def _handshake_all(x, y, c):
    barrier = pltpu.get_barrier_semaphore()
    for dx in (0, 1):
        for dy in (0, 1):
            for dc in (0, 1):
                if dx or dy or dc:
                    pl.semaphore_signal(barrier, inc=1, device_id=((x + dx) % 2, (y + dy) % 2, (c + dc) % 2), device_id_type=MESH)
    pl.semaphore_wait(barrier, 7)


def _comm_call(body, name, operands, out_shape, sems, collective_id):
    if collective_id is not None:
        return pl.kernel(body, out_type=out_shape, mesh=plsc.ScalarSubcoreMesh(axis_name="seq", num_cores=1), name=name,
                         scratch_types=sems, compiler_params=pltpu.CompilerParams(collective_id=collective_id))(*operands)
    n_in, n_out = len(operands), len(out_shape)
    return pl.pallas_call(body, name=name, in_specs=[ANY] * n_in, out_specs=[ANY] * n_out, out_shape=out_shape,
                          scratch_shapes=sems)(*operands)


def gather_weights(slabs, name="gather_weights", collective_id=None):
    n = len(slabs)

    def body(*refs):
        src, dst = refs[:n], refs[n:2 * n]
        ici_s, ici_r, fwd_s, fwd_r, loc = refs[2 * n:]
        x, y, c, s = _me()
        if collective_id is not None:
            _handshake_all(x, y, c)
        sender = c == s // 2
        sib = (x, y, 1 - c)
        local = [pltpu.make_async_copy(src[w], dst[w].at[s], loc.at[w]) for w in range(n)]
        for cp in local:
            cp.start()
        for r in (1, 2, 3):
            @pl.when(sender)
            def _(r=r):
                for w in range(n):
                    _rcopy(src[w], dst[w].at[s], ici_s.at[w, r - 1], ici_r.at[w, r - 1], _chip_dev(s ^ r, c)).start()
        for r in (1, 2, 3):
            t = s ^ r
            here = c == t // 2

            @pl.when(here)
            def _(r=r, t=t):
                for w in range(n):
                    _rcopy(src[w], dst[w].at[t], ici_s.at[w, r - 1], ici_r.at[w, r - 1], sib).wait_recv()
                    _rcopy(dst[w].at[t], dst[w].at[t], fwd_s.at[w, r - 1], fwd_r.at[w, r - 1], sib).start()

            @pl.when(jnp.logical_not(here))
            def _(r=r, t=t):
                for w in range(n):
                    _rcopy(dst[w].at[t], dst[w].at[t], fwd_s.at[w, r - 1], fwd_r.at[w, r - 1], sib).wait_recv()
        for r in (1, 2, 3):
            t = s ^ r

            @pl.when(sender)
            def _(r=r):
                for w in range(n):
                    _rcopy(src[w], dst[w].at[s], ici_s.at[w, r - 1], ici_r.at[w, r - 1], sib).wait_send()

            @pl.when(c == t // 2)
            def _(r=r, t=t):
                for w in range(n):
                    _rcopy(dst[w].at[t], dst[w].at[t], fwd_s.at[w, r - 1], fwd_r.at[w, r - 1], sib).wait_send()
        for cp in local:
            cp.wait()

    out_shape = [jax.ShapeDtypeStruct((4,) + a.shape, a.dtype) for a in slabs]
    sems = [pltpu.SemaphoreType.DMA((n, 3))] * 4 + [pltpu.SemaphoreType.DMA((n,))]
    return _comm_call(body, name, slabs, out_shape, sems, collective_id)


def pair_exchange(g4, name="pair_exchange", collective_id=None):
    n = len(g4)

    def body(*refs):
        src, dst = refs[:n], refs[n:2 * n]
        ssem, rsem = refs[2 * n:]
        x, y, c, s = _me()
        if collective_id is not None:
            _handshake_all(x, y, c)
        cps = [_rcopy(src[w].at[pl.ds(2 * (1 - c), 2)], dst[w], ssem.at[w], rsem.at[w], (x, y, 1 - c)) for w in range(n)]
        for cp in cps:
            cp.start()
        for cp in cps:
            cp.wait()

    return _comm_call(body, name, g4, [jax.ShapeDtypeStruct((2,) + a.shape[1:], a.dtype) for a in g4],
                      [pltpu.SemaphoreType.DMA((n,))] * 2, collective_id)


def _col_tile(R, C):
    for tc in (512, 256, 128):
        if C % tc == 0 and R * tc * 4 <= (4 << 20):
            return tc
    return 128


def pair_add(name, g4, gsib, c):
    _, R, C = g4.shape
    tc = _col_tile(R, C)

    def body(c_ref, a_ref, b_ref, of_ref, ob_ref):
        v = a_ref[...] + b_ref[...]
        of_ref[...] = v
        ob_ref[...] = v.astype(BF16)

    blk = lambda f: pl.BlockSpec((None, R, tc), f)
    return _pcall(
        body, (c.reshape(1).astype(jnp.int32), g4, gsib), name=name, num_scalar_prefetch=1, grid=(2, C // tc),
        in_specs=[blk(lambda p, j, cr: (2 * cr[0] + p, 0, j)), blk(lambda p, j, cr: (p, 0, j))],
        out_specs=[blk(lambda p, j, cr: (p, 0, j)), blk(lambda p, j, cr: (p, 0, j))],
        out_shape=[jax.ShapeDtypeStruct((2, R, C), F32), jax.ShapeDtypeStruct((2, R, C), BF16)],
        compiler_params=_cparams(("arbitrary", "arbitrary")),
    )


def scatter_partials(rb, name="scatter_partials", collective_id=None):
    n = len(rb)

    def body(*refs):
        src, dst = refs[:n], refs[n:2 * n]
        ssem, rsem = refs[2 * n:]
        x, y, c, s = _me()
        if collective_id is not None:
            _handshake_all(x, y, c)
        for r in (1, 2, 3):
            t = s ^ r

            @pl.when(t // 2 == c)
            def _(r=r, t=t):
                for w in range(n):
                    _rcopy(src[w].at[t % 2], dst[w].at[s], ssem.at[w, r - 1], rsem.at[w, r - 1], _chip_dev(t, c)).start()
        for r in (1, 2, 3):
            t = s ^ r

            @pl.when(s // 2 == c)
            def _(r=r, t=t):
                for w in range(n):
                    _rcopy(src[w].at[0], dst[w].at[t], ssem.at[w, r - 1], rsem.at[w, r - 1], _chip_dev(t, c)).wait_recv()
        for r in (1, 2, 3):
            t = s ^ r

            @pl.when(t // 2 == c)
            def _(r=r, t=t):
                for w in range(n):
                    _rcopy(src[w].at[t % 2], dst[w].at[s], ssem.at[w, r - 1], rsem.at[w, r - 1], _chip_dev(t, c)).wait_send()

    return _comm_call(body, name, rb, [jax.ShapeDtypeStruct((4,) + a.shape[1:], a.dtype) for a in rb],
                      [pltpu.SemaphoreType.DMA((n, 3))] * 2, collective_id)


def final_sum(name, rf, recv, s):
    _, R, C = rf.shape
    tc = _col_tile(R, C)

    def body(s_ref, own_ref, r0_ref, r1_ref, r2_ref, o_ref):
        o_ref[...] = ((own_ref[...] + r0_ref[...].astype(F32)) + r1_ref[...].astype(F32)) + r2_ref[...].astype(F32)

    blk = lambda f: pl.BlockSpec((None, R, tc), f)
    other = lambda k: (lambda j, sr: (k + (k >= sr[0]).astype(jnp.int32), 0, j))
    return _pcall(
        body, (s.reshape(1).astype(jnp.int32), rf, recv, recv, recv), name=name, num_scalar_prefetch=1, grid=(C // tc,),
        in_specs=[blk(lambda j, sr: (sr[0] % 2, 0, j)), blk(other(0)), blk(other(1)), blk(other(2))],
        out_specs=pl.BlockSpec((R, tc), lambda j, sr: (0, j)), out_shape=jax.ShapeDtypeStruct((R, C), F32),
        compiler_params=_cparams(("arbitrary",)),
    )


def share_with_sibling(fs, name="share_with_sibling", collective_id=None):
    n = len(fs)

    def body(*refs):
        src, dst = refs[:n], refs[n:2 * n]
        ssem, rsem, loc = refs[2 * n:]
        x, y, c, s = _me()
        if collective_id is not None:
            _handshake_all(x, y, c)
        sib = (x, y, 1 - c)

        @pl.when(s // 2 == c)
        def _():
            cps = [_rcopy(src[w], dst[w], ssem.at[w], rsem.at[w], sib) for w in range(n)]
            own = [pltpu.make_async_copy(src[w], dst[w], loc.at[w]) for w in range(n)]
            for cp in cps + own:
                cp.start()
            for cp in cps:
                cp.wait_send()
            for cp in own:
                cp.wait()

        @pl.when(s // 2 != c)
        def _():
            for w in range(n):
                _rcopy(src[w], dst[w], ssem.at[w], rsem.at[w], sib).wait_recv()

    return _comm_call(body, name, fs, [jax.ShapeDtypeStruct(a.shape, a.dtype) for a in fs],
                      [pltpu.SemaphoreType.DMA((n,))] * 3, collective_id)


def allgather_rows(v):
    m_per, ncol = v.shape

    def body(x_ref, out_ref, send_sems, recv_sems, local_sem):
        x, y, c = lax.axis_index("x"), lax.axis_index("y"), lax.axis_index("c")
        me, sibling = (x, y, c), (x, y, 1 - c)
        chips = [(1 - x, y), (x, 1 - y), (1 - x, 1 - y)]

        def rows(px, py, pc):
            return out_ref.at[pl.ds((4 * px + 2 * py + pc) * m_per, m_per), :]

        def copy(k, block, to, src=None):
            return _rcopy(rows(*block) if src is None else src, rows(*block), send_sems.at[k], recv_sems.at[k], to)

        mine = pltpu.make_async_copy(x_ref, rows(*me), local_sem)
        mine.start()
        first = [copy(0, me, sibling, src=x_ref)]
        first += [copy(1 + j, me, (*chip, c), src=x_ref) for j, chip in enumerate(chips)]
        for cp in first:
            cp.start()
        passed = [copy(4 + j, (*chip, c), sibling) for j, chip in enumerate(chips)]
        for j, chip in enumerate(chips):
            copy(1 + j, (*chip, c), me).wait_recv()
            passed[j].start()
        copy(0, sibling, me).wait_recv()
        for j, chip in enumerate(chips):
            copy(4 + j, (*chip, 1 - c), me).wait_recv()
        for cp in first + passed:
            cp.wait_send()
        mine.wait()

    return pl.pallas_call(
        body, name="allgather_rows", out_shape=jax.ShapeDtypeStruct((8 * m_per, ncol), v.dtype),
        in_specs=[pl.BlockSpec(memory_space=pltpu.VMEM)], out_specs=pl.BlockSpec(memory_space=pltpu.VMEM),
        scratch_shapes=[pltpu.SemaphoreType.DMA((7,)), pltpu.SemaphoreType.DMA((7,)), pltpu.SemaphoreType.DMA],
        compiler_params=pltpu.CompilerParams(vmem_limit_bytes=V7X_VMEM_LIMIT),
    )(v)


def sum_blocks(name, a, nblk):
    m = a.shape[0] // nblk

    def body(a_ref, o_ref):
        acc = a_ref[pl.ds(0, m), :]
        for b in range(1, nblk):
            acc = acc + a_ref[pl.ds(b * m, m), :]
        o_ref[...] = acc

    return pl.pallas_call(body, name=name, out_shape=jax.ShapeDtypeStruct((m, a.shape[1]), a.dtype),
                          compiler_params=pltpu.CompilerParams(vmem_limit_bytes=V7X_VMEM_LIMIT))(a)


BIG = ("w_in", "w_branch_a", "w_branch_b", "w_mix_out", "w_xq", "w_xkv", "w_xo", "w_up", "w_down")
GAINS = ("mix_pre_norm", "gdn_norm", "pool_scale", "mix_post_norm", "xa_pre_norm", "mem_norm", "xa_post_norm",
         "ffn_pre_norm", "ffn_post_norm")
WEIGHTS = ("mix_pre_norm", "w_in", "conv_qkv", "a_log", "dt_bias", "gdn_norm", "pool_w", "pool_scale", "w_branch_a",
           "w_branch_b", "w_mix_out", "mix_post_norm", "xa_pre_norm", "mem_norm", "w_xq", "w_xkv", "w_xo", "xa_post_norm",
           "ffn_pre_norm", "w_up", "ffn_conv_w", "ffn_conv_b", "w_down", "ffn_post_norm")


RS_GROUPS = {"ffn": (("w_up", "w_down"), (3, 4, 5)), "xattn": (("w_xo", "w_xq", "w_xkv"), (6, 7, 8)),
             "mixer": (("w_mix_out", "w_branch_a", "w_branch_b"), (9, 10, 11)), "in_proj": (("w_in",), (12, 13, 14))}


def _rows128(vecs):
    flat = jnp.concatenate([v.reshape(-1) for v in vecs])
    m = -(-flat.shape[0] // 1024) * 8
    return jnp.pad(flat, (0, m * 128 - flat.shape[0])).reshape(m, 128)


def _unrows(a, shapes):
    flat, out, pos = a.reshape(-1), [], 0
    for sh in shapes:
        n = 1
        for d in sh:
            n *= d
        out.append(flat[pos:pos + n].reshape(sh))
        pos += n
    return out


def _lane128(v):
    return jnp.pad(v.reshape(1, GDN_HEADS), ((0, 0), (GDN_HEADS, 128 - 2 * GDN_HEADS)))


def kernel(x, mem, mix_pre_norm, w_in, conv_qkv, a_log, dt_bias, gdn_norm, pool_w, pool_scale, w_branch_a, w_branch_b, w_mix_out, mix_post_norm, xa_pre_norm, mem_norm, w_xq, w_xkv, w_xo, xa_post_norm, ffn_pre_norm, w_up, ffn_conv_w, ffn_conv_b, w_down, ffn_post_norm, loss_target, m_mix_pre_norm, m_w_in, m_conv_qkv, m_a_log, m_dt_bias, m_gdn_norm, m_pool_w, m_pool_scale, m_w_branch_a, m_w_branch_b, m_w_mix_out, m_mix_post_norm, m_xa_pre_norm, m_mem_norm, m_w_xq, m_w_xkv, m_w_xo, m_xa_post_norm, m_ffn_pre_norm, m_w_up, m_ffn_conv_w, m_ffn_conv_b, m_w_down, m_ffn_post_norm, v_mix_pre_norm, v_w_in, v_conv_qkv, v_a_log, v_dt_bias, v_gdn_norm, v_pool_w, v_pool_scale, v_w_branch_a, v_w_branch_b, v_w_mix_out, v_mix_post_norm, v_xa_pre_norm, v_mem_norm, v_w_xq, v_w_xkv, v_w_xo, v_xa_post_norm, v_ffn_pre_norm, v_w_up, v_ffn_conv_w, v_ffn_conv_b, v_w_down, v_ffn_post_norm):
    given = dict(locals())
    _LAST[0] = None
    w = {n: given[n][0] for n in WEIGHTS}
    cx, cy, cc = lax.axis_index("x"), lax.axis_index("y"), lax.axis_index("c")
    chip = 2 * cx + cy

    first = gather_weights([shard_to_slab("w_in", w["w_in"])], name="gather_w_in", collective_id=1)
    rest = gather_weights([shard_to_slab(n, w[n]) for n in BIG[1:]], name="gather_rest", collective_id=2)
    W = {n: slabs_to_weight(n, g) for n, g in zip(BIG, list(first) + list(rest))}
    sharded_small = (w["conv_qkv"], w["ffn_conv_w"], w["pool_w"])
    allv = allgather_rows(_rows128(sharded_small))
    per_chip = allv.reshape(8, -1)[0::2]
    parts = [_unrows(per_chip[t], [a.shape for a in sharded_small]) for t in range(4)]
    sp = {n: w[n].reshape(1, -1) for n in GAINS}
    sp["a_log"], sp["dt_bias"] = _lane128(w["a_log"]), _lane128(w["dt_bias"])
    sp["conv_qkv"] = jnp.concatenate([p[0] for p in parts], axis=1)
    sp["ffn_conv_w"] = jnp.pad(jnp.stack([p[1] for p in parts]), ((0, 0), (0, 0), (0, FF_BLK - FF_COLS)))
    sp["pool_w"] = jnp.concatenate([p[2] for p in parts], axis=1).reshape(4 * 256, 256)
    sp["ffn_conv_b"] = jnp.pad(w["ffn_conv_b"].reshape(4, 1, FF_COLS), ((0, 0), (0, 0), (0, FF_BLK - FF_COLS)))

    grads, todo = {}, []

    def rs_steps(names, ids, gw):
        g4 = [grad_to_slabs(n, gw) for n in names]
        tag = names[0]
        gsib = pair_exchange(g4, "pair_exchange_" + tag, ids[0])
        yield
        sums = [pair_add("pair_add_" + n, a, b, cc) for n, a, b in zip(names, g4, gsib)]
        recv = scatter_partials([sb for _, sb in sums], "scatter_partials_" + tag, ids[1])
        yield
        fin = [final_sum("final_sum_" + n, sf, rv, chip) for n, (sf, _), rv in zip(names, sums, recv)]
        fin = share_with_sibling(fin, "share_" + tag, ids[2])
        for n, f in zip(names, fin):
            grads[n] = slab_to_shard_grad(n, f)

    def advance():
        for it in list(todo):
            if next(it, "done") == "done":
                todo.remove(it)

    def hook(event, gw):
        new = rs_steps(*RS_GROUPS[event], gw)
        next(new)
        advance()
        todo.append(new)
        if event == "in_proj":
            next(new)

    loss, grad_x, gw, gs = local_step(x[0], mem[0], loss_target[0], W, sp, hook)

    small_names = GAINS + ("a_log", "dt_bias", "ffn_conv_b", "conv_qkv", "ffn_conv_w", "pool_w")
    vec = _rows128([gs[n] for n in small_names])
    total = sum_blocks("sum_small", allgather_rows(vec), 8)
    while todo:
        advance()
    loss = lax.psum(loss[0, 0], ("x", "y", "c"))
    tot = dict(zip(small_names, _unrows(total, [gs[n].shape for n in small_names])))
    for n in GAINS:
        grads[n] = tot[n].reshape(-1)
    grads["a_log"] = tot["a_log"][0, GDN_HEADS:2 * GDN_HEADS]
    grads["dt_bias"] = tot["dt_bias"][0, GDN_HEADS:2 * GDN_HEADS]
    grads["ffn_conv_b"] = tot["ffn_conv_b"][:, 0, :FF_COLS].reshape(-1)
    grads["conv_qkv"] = lax.dynamic_slice_in_dim(tot["conv_qkv"], chip * 1536, 1536, axis=1)
    grads["ffn_conv_w"] = lax.dynamic_index_in_dim(tot["ffn_conv_w"], chip, axis=0, keepdims=False)[:, :FF_COLS]
    grads["pool_w"] = lax.dynamic_slice_in_dim(tot["pool_w"].reshape(4, 256, 256), chip * 64, 64, axis=1)

    delta, new_m, new_v = {}, {}, {}
    for n in WEIGHTS:
        shp = w[n].shape
        two = (lambda a: a.reshape(-1, shp[-1])) if len(shp) > 1 else (lambda a: a.reshape(1, -1))
        d, nm, nv = adamw("adamw_" + n, two(w[n]), two(grads[n]), two(given["m_" + n][0]), two(given["v_" + n][0]))
        delta[n], new_m[n], new_v[n] = (a.reshape((1,) + shp) for a in (d, nm, nv))
    out_g = [grads[n].reshape((1,) + w[n].shape) for n in WEIGHTS]
    return (loss, grad_x[None], *out_g, *[delta[n] for n in WEIGHTS], *[new_m[n] for n in WEIGHTS], *[new_v[n] for n in WEIGHTS])
```

```python
import functools

import jax
import jax.numpy as jnp
from jax import lax
from jax.experimental import pallas as pl
from jax.experimental.pallas import tpu as pltpu
from jax.experimental.pallas import tpu_sc as plsc

F32 = jnp.float32
BF16 = jnp.bfloat16

D_MODEL = 2048
CHUNK = 64
GDN_HEADS = 16
GDN_DK = 128
POOL_WINDOWS = (2, 4, 8, 16)
XA_HEADS = 4
XA_HEAD_DIM = D_MODEL // XA_HEADS
EPS = 1e-6
GDN_SPAN = 256
GDN_SPAN_BWD = 128
HALO = 16
V7X_VMEM_LIMIT = 56 * 1024 * 1024

ADAM_LR, ADAM_B1, ADAM_B2, ADAM_EPS, ADAM_WD, ADAM_STEP = 0.001, 0.9, 0.999, 1e-08, 0.01, 10

_NN = ((1,), (0,))
_NT = ((1,), (1,))
_TN = ((0,), (0,))


def _cparams(sem):
    return pltpu.CompilerParams(dimension_semantics=sem, vmem_limit_bytes=V7X_VMEM_LIMIT)


_LAST = [None]


def _pcall(body, operands, *, in_specs, out_specs, grid=(), num_scalar_prefetch=0, scratch_shapes=(), **kw):
    operands, in_specs = list(operands), list(in_specs)
    if _LAST[0] is not None and not any(o is _LAST[0] for o in operands):
        n = len(operands)
        inner = body

        def body(*refs):
            return inner(*refs[:n], *refs[n + 1:])

        operands.append(_LAST[0])
        in_specs.append(pl.BlockSpec(memory_space=pl.ANY))
    if num_scalar_prefetch:
        kw["grid_spec"] = pltpu.PrefetchScalarGridSpec(num_scalar_prefetch=num_scalar_prefetch, grid=grid, in_specs=in_specs,
                                                       out_specs=out_specs, scratch_shapes=list(scratch_shapes))
    else:
        kw.update(grid=grid, in_specs=in_specs, out_specs=out_specs, scratch_shapes=list(scratch_shapes))
    res = pl.pallas_call(body, **kw)(*operands)
    _LAST[0] = res[0] if isinstance(res, (list, tuple)) else res
    return res


def _dg(a, b, dims, prec=None):
    return lax.dot_general(a, b, (dims, ((), ())), precision=prec, preferred_element_type=F32)


def _make_dots(cast, prec):
    def raw(dims, a, b):
        return _dg(cast(a), cast(b), dims, prec)

    @jax.custom_vjp
    def nn(a, b):
        return raw(_NN, a, b)

    @jax.custom_vjp
    def nt(a, b):
        return raw(_NT, a, b)

    @jax.custom_vjp
    def tn(a, b):
        return raw(_TN, a, b)

    nn.defvjp(lambda a, b: (raw(_NN, a, b), (a, b)), lambda r, g: (nt(g, r[1]), tn(r[0], g)))
    nt.defvjp(lambda a, b: (raw(_NT, a, b), (a, b)), lambda r, g: (nn(g, r[1]), tn(g, r[0])))
    tn.defvjp(lambda a, b: (raw(_TN, a, b), (a, b)), lambda r, g: (nt(r[1], g), nn(r[0], g)))
    return nn, nt, tn


bdot_nn, bdot_nt, bdot_tn = _make_dots(lambda x: x.astype(BF16), None)
hdot_nn, hdot_nt, hdot_tn = _make_dots(lambda x: x, lax.Precision.HIGHEST)


@functools.partial(jax.custom_vjp, nondiff_argnums=(1,))
def shift_rows(x, k):
    return pltpu.roll(x, k, 0)


def _shift_rows_fwd(x, k):
    return pltpu.roll(x, k, 0), None


def _shift_rows_bwd(k, _, g):
    return (pltpu.roll(g, g.shape[0] - k, 0),)


shift_rows.defvjp(_shift_rows_fwd, _shift_rows_bwd)


@functools.partial(jax.custom_vjp, nondiff_argnums=(1,))
def drop_head(x, h):
    return x[h:]


def _drop_head_fwd(x, h):
    return x[h:], None


def _drop_head_bwd(h, _, g):
    return (jnp.concatenate([jnp.zeros((h,) + g.shape[1:], g.dtype), g], axis=0),)


drop_head.defvjp(_drop_head_fwd, _drop_head_bwd)


@functools.partial(jax.custom_vjp, nondiff_argnums=(1,))
def split_lanes(x, n):
    w = x.shape[-1] // n
    return tuple(x[:, i * w:(i + 1) * w] for i in range(n))


def _split_lanes_fwd(x, n):
    return split_lanes(x, n), None


def _split_lanes_bwd(n, _, gs):
    return (jnp.concatenate(list(gs), axis=-1),)


split_lanes.defvjp(_split_lanes_fwd, _split_lanes_bwd)


@functools.partial(jax.custom_vjp, nondiff_argnums=(1,))
def split_rows(x, n):
    h = x.shape[0] // n
    return tuple(x[i * h:(i + 1) * h] for i in range(n))


def _split_rows_fwd(x, n):
    return split_rows(x, n), None


def _split_rows_bwd(n, _, gs):
    return (jnp.concatenate(list(gs), axis=0),)


split_rows.defvjp(_split_rows_fwd, _split_rows_bwd)


def row_of(w, j):
    rid = lax.broadcasted_iota(jnp.int32, w.shape, 0)
    return jnp.sum(jnp.where(rid == j, w, 0.0), axis=0, keepdims=True)


def sigmoid(x):
    return 0.5 * jnp.tanh(0.5 * x) + 0.5


def silu(x):
    return x * sigmoid(x)


def softplus(x):
    return jnp.maximum(x, 0.0) + jnp.log(1.0 + jnp.exp(-jnp.abs(x)))


def rms(x, g):
    return x * lax.rsqrt(jnp.mean(x * x, axis=-1, keepdims=True) + EPS) * g


def mm(name, dims, pairs, out_shape, out_spec, grid):
    nk = grid[2]
    acc_in_out = nk > 1 and out_shape.dtype == F32
    npair = len(pairs)

    def body(*refs):
        o_ref = refs[2 * npair]
        part = None
        for p in range(npair):
            d = _dg(refs[2 * p][...].astype(BF16), refs[2 * p + 1][...].astype(BF16), dims)
            part = d if part is None else part + d
        if nk == 1:
            o_ref[...] = part.astype(o_ref.dtype)
            return
        acc = o_ref if acc_in_out else refs[2 * npair + 1]
        k = pl.program_id(2)

        @pl.when(k == 0)
        def _():
            acc[...] = part

        @pl.when(k > 0)
        def _():
            acc[...] += part

        if not acc_in_out:
            @pl.when(k == nk - 1)
            def _():
                o_ref[...] = acc[...].astype(o_ref.dtype)

    scratch = []
    if nk > 1 and not acc_in_out:
        scratch = [pltpu.VMEM(tuple(d for d in out_spec.block_shape if d is not None), F32)]
    in_specs, operands = [], []
    for a, b, a_spec, b_spec in pairs:
        in_specs += [a_spec, b_spec]
        operands += [a, b]
    return _pcall(body, operands, name=name, grid=grid, in_specs=in_specs, out_specs=out_spec, out_shape=out_shape,
                  scratch_shapes=scratch, compiler_params=_cparams(("parallel", "parallel", "arbitrary")))


def _bs(shape, fn):
    return pl.BlockSpec(shape, fn)


class Tile:
    def __init__(self, arr, w, cb=0, lead=None, halo=False):
        self.arr, self.w, self.cb, self.lead, self.halo = arr, w, cb, lead, halo


class Out:
    def __init__(self, shape, dtype, w, cb=0, lead=None, into=None):
        self.shape, self.dtype, self.w, self.cb, self.lead, self.into = shape, dtype, w, cb, lead, into


class Par:
    def __init__(self, arr, lead=None):
        self.arr, self.lead = arr, lead


def _spec(rows, w, cb, lead, tile_of):
    if lead is None:
        return pl.BlockSpec((rows, w), lambda o, i: (tile_of(i), cb))
    return pl.BlockSpec((None, rows, w), lambda o, i: (lead(o), tile_of(i), cb))


def _par_spec(p):
    if p.lead is None:
        return pl.BlockSpec(p.arr.shape, lambda o, i: (0, 0))
    return pl.BlockSpec((None,) + p.arr.shape[1:], lambda o, i: (p.lead(o), 0, 0))


def run_stage(name, fn, tm, ins, pars, outs, *, outer=1, cts=None, dins=None):
    T = ins[0].arr.shape[-2]
    nt = T // tm
    bwd = cts is not None
    any_halo = any(t.halo for t in ins)
    step_tile = (lambda i: nt - 1 - i) if bwd else (lambda i: i)
    hb = tm // HALO

    in_specs, operands = [], []
    for t in ins:
        if t.halo:
            in_specs.append(_spec(HALO, t.w, t.cb, t.lead, lambda i: jnp.maximum(step_tile(i) * hb - 1, 0)))
            operands.append(t.arr)
        in_specs.append(_spec(tm, t.w, t.cb, t.lead, step_tile))
        operands.append(t.arr)
    for p in pars:
        in_specs.append(_par_spec(p))
        operands.append(p.arr)
    n_in_refs = len(operands)

    out_descs = list(outs) if not bwd else [d for d in dins if d is not None]
    aliases = {}
    if bwd:
        for c, o in zip(cts, outs):
            in_specs.append(_spec(tm, o.w, o.cb, o.lead, step_tile))
            operands.append(c)
    n_ct = len(operands) - n_in_refs
    for k, o in enumerate(out_descs):
        if o.into is not None:
            aliases[len(operands)] = k
            in_specs.append(pl.BlockSpec(memory_space=pl.ANY))
            operands.append(o.into)
    out_specs = [_spec(tm, o.w, o.cb, o.lead, step_tile) for o in out_descs]
    out_shapes = [jax.ShapeDtypeStruct(o.shape, o.dtype) for o in out_descs]
    if bwd:
        for p in pars:
            out_specs.append(_par_spec(p))
            out_shapes.append(jax.ShapeDtypeStruct(p.arr.shape, F32))
    scratch = []
    if bwd and any_halo:
        scratch = [pltpu.VMEM((HALO, t.w), F32) for t, d in zip(ins, dins) if t.halo and d is not None]

    def body(*refs):
        i = pl.program_id(1)
        tile = step_tile(i)
        row0 = tile * tm
        pos = 0
        tiles = []
        for t in ins:
            if t.halo:
                prev = jnp.where(tile > 0, refs[pos][...].astype(F32), 0.0)
                tiles.append(jnp.concatenate([prev, refs[pos + 1][...].astype(F32)], axis=0))
                pos += 2
            else:
                tiles.append(refs[pos][...].astype(F32))
                pos += 1
        pvals = [refs[pos + k][...].astype(F32) for k in range(len(pars))]
        pos += len(pars)
        if not bwd:
            res = fn(tiles, pvals, row0)
            for o_ref, r in zip(refs[pos:], res):
                o_ref[...] = r.astype(o_ref.dtype)
            return
        ct_vals = [refs[pos + k][...].astype(F32) for k in range(n_ct)]
        pos += n_ct + len(aliases)
        _, vjp_fn = jax.vjp(lambda tt, pp: fn(tt, pp, row0), tiles, pvals)
        d_tiles, d_pars = vjp_fn(ct_vals)
        carries = list(refs[len(refs) - len(scratch):])
        for t, d, dt in zip(ins, dins, d_tiles):
            if d is None:
                continue
            o_ref = refs[pos]
            pos += 1
            if t.halo:
                carry = carries.pop(0)
                main = dt[HALO:]
                tail = main[tm - HALO:] + jnp.where(i > 0, carry[...], 0.0)
                o_ref[...] = jnp.concatenate([main[:tm - HALO], tail], axis=0).astype(o_ref.dtype)
                carry[...] = dt[:HALO]
            else:
                o_ref[...] = dt.astype(o_ref.dtype)
        for dp in d_pars:
            acc = refs[pos]
            pos += 1

            @pl.when(i == 0)
            def _(acc=acc, dp=dp):
                acc[...] = dp

            @pl.when(i > 0)
            def _(acc=acc, dp=dp):
                acc[...] += dp

    res = _pcall(body, operands, name=name, grid=(outer, nt), in_specs=in_specs, out_specs=out_specs, out_shape=out_shapes,
                 scratch_shapes=scratch, input_output_aliases=aliases, compiler_params=_cparams(("arbitrary", "arbitrary")))
    if not bwd:
        return list(res)
    n_d = len(out_descs)
    d_full, it = [], iter(res[:n_d])
    for d in dins:
        d_full.append(None if d is None else next(it))
    return d_full, list(res[n_d:])


def f_prenorm(t, p, row0):
    return [rms(t[0], p[0])]


def f_prenorm_res(t, p, row0):
    return [t[0], rms(t[0], p[0])]


def f_post_pre(t, p, row0):
    x, y = t
    x1 = x + rms(y, p[0])
    return [x1, rms(x1, p[1])]


def _causal_conv(x, w, taps):
    y = x * row_of(w, taps - 1)
    for j in range(taps - 1):
        y = y + shift_rows(x, taps - 1 - j) * row_of(w, j)
    return drop_head(y, HALO)


def _l2(x):
    return x * lax.rsqrt(jnp.sum(x * x, axis=-1, keepdims=True) + EPS)


def make_f_convhead(scale, normalise):
    def f(t, p, row0):
        y = silu(_causal_conv(t[0], p[0], 4))
        if not normalise:
            return [y]
        return [jnp.concatenate([_l2(c) * scale for c in split_lanes(y, GDN_HEADS)], axis=-1)]
    return f


def f_bg(t, p, row0):
    ba = split_lanes(t[0], 4)[0]
    alog, dtb = p
    lane = lax.broadcasted_iota(jnp.int32, ba.shape, 1)
    bg = jnp.where(lane < GDN_HEADS, sigmoid(ba), -jnp.exp(alog) * softplus(ba + dtb))
    return [jnp.where(lane < 2 * GDN_HEADS, bg, 0.0)]


def f_gnorm(t, p, row0):
    o, z = t
    po, pz = split_lanes(o, GDN_HEADS), split_lanes(z, GDN_HEADS)
    return [jnp.concatenate([rms(a, p[0]) * silu(b) for a, b in zip(po, pz)], axis=-1)]


def f_pool(t, p, row0):
    x = t[0]
    pw, psc = p
    tm = x.shape[0] - HALO
    tpos = (row0 + lax.broadcasted_iota(jnp.int32, (tm, 1), 0) + 1).astype(F32)
    outs = []
    for xg, wg, win in zip(split_lanes(x, 4), split_rows(pw, 4), POOL_WINDOWS):
        s, span = xg, 1
        while span < win:
            s = s + shift_rows(s, span)
            span *= 2
        mean = drop_head(s, HALO) / jnp.minimum(tpos, float(win))
        outs.append(bdot_nn(mean - drop_head(xg, HALO), wg))
    return [jnp.concatenate(outs, axis=-1) * psc]


def f_merge(t, p, row0):
    gates, ya, yb = t
    ga, gb = split_lanes(gates, 2)
    return [sigmoid(ga) * ya + sigmoid(gb) * yb]


def f_xattn(t, p, row0):
    k, v = p
    outs = []
    for qh, kh, vh in zip(split_lanes(t[0], XA_HEADS), split_lanes(k, XA_HEADS), split_lanes(v, XA_HEADS)):
        s = bdot_nt(qh, kh) * (XA_HEAD_DIM ** -0.5)
        s = s - jnp.max(s, axis=-1, keepdims=True)
        e = jnp.exp(s)
        outs.append(bdot_nn(e / jnp.sum(e, axis=-1, keepdims=True), vh))
    return [jnp.concatenate(outs, axis=-1)]


def f_convglu(t, p, row0):
    ua, ub = t
    cwa, cwb, ba, bb = p
    return [silu(_causal_conv(ua, cwa, 3) + ba) * (_causal_conv(ub, cwb, 3) + bb)]


_BNN = (((2,), (1,)), ((0,), (0,)))
_BNT = (((2,), (2,)), ((0,), (0,)))
_BTN = (((1,), (1,)), ((0,), (0,)))


def _make_batched_dots():
    def raw(dims, a, b):
        return lax.dot_general(a.astype(BF16), b.astype(BF16), dims, preferred_element_type=F32)

    @jax.custom_vjp
    def nn(a, b):
        return raw(_BNN, a, b)

    @jax.custom_vjp
    def nt(a, b):
        return raw(_BNT, a, b)

    @jax.custom_vjp
    def tn(a, b):
        return raw(_BTN, a, b)

    nn.defvjp(lambda a, b: (raw(_BNN, a, b), (a, b)), lambda r, g: (nt(g, r[1]), tn(r[0], g)))
    nt.defvjp(lambda a, b: (raw(_BNT, a, b), (a, b)), lambda r, g: (nn(g, r[1]), tn(g, r[0])))
    tn.defvjp(lambda a, b: (raw(_BTN, a, b), (a, b)), lambda r, g: (nt(r[1], g), nn(r[0], g)))
    return nn, nt, tn


bb_nn, bb_nt, bb_tn = _make_batched_dots()


def _gdn_local(q, k, v, gcol, bcol):
    C = CHUNK
    r = lax.broadcasted_iota(jnp.int32, (1, C, C), 1)
    c = lax.broadcasted_iota(jnp.int32, (1, C, C), 2)
    eye, incl, strict = r == c, r >= c, r > c
    grow = jnp.sum(jnp.where(eye, gcol, 0.0), axis=1, keepdims=True)
    Gcol = jnp.sum(jnp.where(incl, grow, 0.0), axis=2, keepdims=True)
    Grow = jnp.sum(jnp.where(eye, Gcol, 0.0), axis=1, keepdims=True)
    decay = jnp.where(incl, jnp.exp(jnp.where(incl, Gcol - Grow, 0.0)), 0.0)
    X = -jnp.where(strict, bcol * decay * bb_nt(k, k), 0.0)
    N, P = X, X
    for _ in range(5):
        P = bb_nn(P, P)
        N = N + P + bb_nn(N, P)
    expg = jnp.exp(Gcol)
    rv, rk = bcol * v, (bcol * expg) * k
    u_v = rv + bb_nn(N, rv)
    w_k = rk + bb_nn(N, rk)
    attn = decay * bb_nt(q, k)
    rid = lax.broadcasted_iota(jnp.int32, (1, C, 1), 1)
    glast = jnp.sum(jnp.where(rid == C - 1, Gcol, 0.0), axis=1, keepdims=True)
    return u_v, w_k, attn, q * expg, k * jnp.exp(glast - Gcol), jnp.exp(glast)


def _gdn_rec(u_v, w_k, attn, q_dec, k_dec, cd, S):
    u = u_v - bb_nn(w_k, S)
    o = bb_nn(q_dec, S) + bb_nn(attn, u)
    return o, cd * S + bb_tn(k_dec, u)


def _gdn_load(q_ref, k_ref, v_ref, bg_ref, nch):
    H = GDN_HEADS

    def batched(ref):
        return jnp.stack([ref[c * CHUNK:(c + 1) * CHUNK, h * GDN_DK:(h + 1) * GDN_DK] for c in range(nch) for h in range(H)])

    bg = bg_ref[...]
    lane = lax.broadcasted_iota(jnp.int32, bg.shape, 1)
    bcols = [jnp.sum(jnp.where(lane == h, bg, 0.0), axis=-1, keepdims=True) for h in range(H)]
    gcols = [jnp.sum(jnp.where(lane == H + h, bg, 0.0), axis=-1, keepdims=True) for h in range(H)]
    pick = lambda cols: jnp.stack([cols[h][c * CHUNK:(c + 1) * CHUNK] for c in range(nch) for h in range(H)])
    return batched(q_ref), batched(k_ref), batched(v_ref), pick(gcols), pick(bcols)


def _gdn_store(ref, val, nch):
    H = GDN_HEADS
    for c in range(nch):
        for h in range(H):
            ref[c * CHUNK:(c + 1) * CHUNK, h * GDN_DK:(h + 1) * GDN_DK] = val[c * H + h]


def gdn_forward(q, k, v, bg, span):
    T = q.shape[0]
    ns, nch, H = T // span, span // CHUNK, GDN_HEADS

    def body(q_ref, k_ref, v_ref, bg_ref, o_ref, s_ref, state):
        @pl.when(pl.program_id(0) == 0)
        def _():
            state[...] = jnp.zeros(state.shape, F32)

        loc = _gdn_local(*_gdn_load(q_ref, k_ref, v_ref, bg_ref, nch))
        S = state[...]
        for c in range(nch):
            s_ref[c] = S
            o, S = _gdn_rec(*[a[c * H:(c + 1) * H] for a in loc], S)
            for h in range(H):
                o_ref[c * CHUNK:(c + 1) * CHUNK, h * GDN_DK:(h + 1) * GDN_DK] = o[h]
        state[...] = S

    wide = pl.BlockSpec((span, H * GDN_DK), lambda s: (s, 0))
    return _pcall(
        body, (q, k, v, bg), name="gdn_fwd", grid=(ns,),
        in_specs=[wide, wide, wide, pl.BlockSpec((span, 128), lambda s: (s, 0))],
        out_specs=[wide, pl.BlockSpec((nch, H, GDN_DK, GDN_DK), lambda s: (s, 0, 0, 0))],
        out_shape=[jax.ShapeDtypeStruct((T, H * GDN_DK), F32), jax.ShapeDtypeStruct((T // CHUNK, H, GDN_DK, GDN_DK), F32)],
        scratch_shapes=[pltpu.VMEM((H, GDN_DK, GDN_DK), F32)],
        compiler_params=_cparams(("arbitrary",)),
    )


def gdn_backward(q, k, v, bg, starts, do, span):
    T = q.shape[0]
    ns, nch, H = T // span, span // CHUNK, GDN_HEADS

    def body(q_ref, k_ref, v_ref, bg_ref, s_ref, do_ref, dq_ref, dk_ref, dv_ref, dbg_ref, dstate):
        @pl.when(pl.program_id(0) == 0)
        def _():
            dstate[...] = jnp.zeros(dstate.shape, F32)

        loc, vjp_loc = jax.vjp(_gdn_local, *_gdn_load(q_ref, k_ref, v_ref, bg_ref, nch))
        dS = dstate[...]
        dloc = [None] * nch
        for c in reversed(range(nch)):
            _, vjp_rec = jax.vjp(_gdn_rec, *[a[c * H:(c + 1) * H] for a in loc], s_ref[c])
            do_c = jnp.stack([do_ref[c * CHUNK:(c + 1) * CHUNK, h * GDN_DK:(h + 1) * GDN_DK] for h in range(H)])
            *dloc[c], dS = vjp_rec((do_c, dS))
        dstate[...] = dS
        dq, dk, dv, dg, db = vjp_loc(tuple(jnp.concatenate([dloc[c][i] for c in range(nch)], axis=0) for i in range(6)))
        _gdn_store(dq_ref, dq, nch)
        _gdn_store(dk_ref, dk, nch)
        _gdn_store(dv_ref, dv, nch)
        lane = lax.broadcasted_iota(jnp.int32, (CHUNK, 128), 1)
        for c in range(nch):
            acc = jnp.zeros((CHUNK, 128), F32)
            for h in range(H):
                acc = acc + jnp.where(lane == h, db[c * H + h], 0.0) + jnp.where(lane == H + h, dg[c * H + h], 0.0)
            dbg_ref[c * CHUNK:(c + 1) * CHUNK, :] = acc

    wide = pl.BlockSpec((span, H * GDN_DK), lambda i: (ns - 1 - i, 0))
    bgs = pl.BlockSpec((span, 128), lambda i: (ns - 1 - i, 0))
    return _pcall(
        body, (q, k, v, bg, starts, do), name="gdn_bwd", grid=(ns,),
        in_specs=[wide, wide, wide, bgs, pl.BlockSpec((nch, H, GDN_DK, GDN_DK), lambda i: (ns - 1 - i, 0, 0, 0)), wide],
        out_specs=[wide, wide, wide, bgs],
        out_shape=[jax.ShapeDtypeStruct((T, H * GDN_DK), F32)] * 3 + [jax.ShapeDtypeStruct((T, 128), F32)],
        scratch_shapes=[pltpu.VMEM((H, GDN_DK, GDN_DK), F32)],
        compiler_params=_cparams(("arbitrary",)),
    )


def loss_stage(x2, y3, tgt, g, tm):
    T, D = x2.shape
    nt = T // tm

    def body(x_ref, y_ref, t_ref, g_ref, loss_ref, dx_ref, dy_ref, dg_ref):
        i = pl.program_id(0)
        tgtv = t_ref[...]

        def f(x, y, gg):
            err = x + rms(y, gg) - tgtv
            return 0.5 * jnp.mean(err * err, axis=-1, keepdims=True)

        rows, vjp_fn = jax.vjp(f, x_ref[...], y_ref[...], g_ref[...])
        dx, dy, dg = vjp_fn(jnp.ones_like(rows))
        dx_ref[...] = dx
        dy_ref[...] = dy.astype(dy_ref.dtype)
        part = jnp.sum(rows, axis=0, keepdims=True)

        @pl.when(i == 0)
        def _():
            loss_ref[...] = part
            dg_ref[...] = dg

        @pl.when(i > 0)
        def _():
            loss_ref[...] += part
            dg_ref[...] += dg

    tile = pl.BlockSpec((tm, D), lambda i: (i, 0))
    gs = pl.BlockSpec((1, D), lambda i: (0, 0))
    return _pcall(
        body, (x2, y3, tgt, g), name="loss_head", grid=(nt,), in_specs=[tile, tile, tile, gs],
        out_specs=[pl.BlockSpec((1, 1), lambda i: (0, 0)), tile, tile, gs],
        out_shape=[jax.ShapeDtypeStruct((1, 1), F32), jax.ShapeDtypeStruct((T, D), F32),
                   jax.ShapeDtypeStruct((T, D), BF16), jax.ShapeDtypeStruct((1, D), F32)],
        compiler_params=_cparams(("arbitrary",)),
    )


def adamw(name, w, g, m, v):
    R, C = w.shape
    tr = R
    for cand in (256, 128, 64, 32, 16, 8):
        if R % cand == 0 and R > cand and cand * C * 4 <= (2 << 20):
            tr = cand
            break
    c1 = 1.0 / (1.0 - ADAM_B1 ** ADAM_STEP)
    c2 = 1.0 / (1.0 - ADAM_B2 ** ADAM_STEP)

    def body(w_ref, g_ref, m_ref, v_ref, d_ref, nm_ref, nv_ref):
        gg = g_ref[...]
        nm = ADAM_B1 * m_ref[...] + (1.0 - ADAM_B1) * gg
        nv = ADAM_B2 * v_ref[...] + (1.0 - ADAM_B2) * (gg * gg)
        d_ref[...] = -ADAM_LR * ((nm * c1) / (jnp.sqrt(nv * c2) + ADAM_EPS) + ADAM_WD * w_ref[...])
        nm_ref[...] = nm
        nv_ref[...] = nv

    spec = pl.BlockSpec((tr, C), lambda i: (i, 0))
    return _pcall(
        body, (w, g, m, v), name=name, grid=(R // tr,), in_specs=[spec] * 4, out_specs=[spec] * 3,
        out_shape=[jax.ShapeDtypeStruct((R, C), F32)] * 3, compiler_params=_cparams(("parallel",)),
    )


C_Z, C_GATES, C_POOL, C_BA, N_PROJ = 6144, 8192, 12288, 13312, 13824
FF_BLK = 2816
FF_COLS = 2752


def local_step(x, mem, tgt, W, sp, hook=lambda event, gw: None):
    T, D = x.shape
    tm = 256
    tmm = min(512, T)
    nI = T // tmm
    S = jax.ShapeDtypeStruct
    gw, gs = {}, {}

    def stage(name, fn, ins, pars, outs, **kw):
        return run_stage(name, fn, kw.pop("tm", tm), ins, pars, outs, **kw)

    def dense(name, a, w, out_dtype=F32, tn=2048):
        Tq, Kd = a.shape
        N = w.shape[1]
        tq = min(tmm, Tq)
        return mm(name, _NN, [(a, w, _bs((tq, Kd), lambda j, i, k: (i, 0)), _bs((Kd, tn), lambda j, i, k: (0, j)))],
                  S((Tq, N), out_dtype), _bs((tq, tn), lambda j, i, k: (i, j)), (N // tn, Tq // tq, 1))

    def dense_t(name, g, w, out_dtype=F32, tn=2048):
        Tq, N = g.shape
        Kd = w.shape[0]
        tq = min(tmm, Tq)
        return mm(name, _NT, [(g, w, _bs((tq, N), lambda j, i, k: (i, 0)), _bs((tn, N), lambda j, i, k: (j, 0)))],
                  S((Tq, Kd), out_dtype), _bs((tq, tn), lambda j, i, k: (i, j)), (Kd // tn, Tq // tq, 1))

    def wgrad(name, a, g, ta=1024, tn=2048):
        Tq, Kd = a.shape
        N = g.shape[1]
        tt = min(1024, Tq)
        return mm(name, _TN, [(a, g, _bs((tt, ta), lambda i, j, k: (k, i)), _bs((tt, tn), lambda i, j, k: (k, j)))],
                  S((Kd, N), F32), _bs((ta, tn), lambda i, j, k: (i, j)), (Kd // ta, N // tn, Tq // tt))

    o2048 = lambda dt: Out((T, D), dt, D)

    (h1,) = stage("pre1", f_prenorm, [Tile(x, D)], [Par(sp["mix_pre_norm"])], [o2048(BF16)])
    tnp = 1536
    P2 = mm("in_proj", _NT, [(h1, W["w_in"], _bs((tmm, D), lambda j, i, k: (i, 0)), _bs((tnp, D), lambda j, i, k: (j, 0)))],
            S((T, N_PROJ), F32), _bs((tmm, tnp), lambda j, i, k: (i, j)), (N_PROJ // tnp, nI, 1))
    cw = sp["conv_qkv"]
    cws = [cw[:, i * D:(i + 1) * D] for i in range(3)]
    f_heads = [make_f_convhead(GDN_DK ** -0.5, True), make_f_convhead(1.0, True), make_f_convhead(1.0, False)]
    qkv = [stage("conv_" + n, f_heads[i], [Tile(P2, D, cb=i, halo=True)], [Par(cws[i])], [o2048(F32)])[0]
           for i, n in enumerate("qkv")]
    ba_tile = Tile(P2, 512, cb=C_BA // 512)
    (bg,) = stage("bg", f_bg, [ba_tile], [Par(sp["a_log"]), Par(sp["dt_bias"])], [Out((T, 128), F32, 128)])
    o, s0 = gdn_forward(qkv[0], qkv[1], qkv[2], bg, GDN_SPAN)
    z_tile = Tile(P2, D, cb=C_Z // D)
    (o_n,) = stage("gnorm", f_gnorm, [Tile(o, D), z_tile], [Par(sp["gdn_norm"])], [o2048(BF16)])
    y_a = dense("branch_a", o_n, W["w_branch_a"])
    p_tile = Tile(P2, 1024, cb=C_POOL // 1024, halo=True)
    pool_pars = [Par(sp["pool_w"]), Par(sp["pool_scale"])]
    (pooled,) = stage("pool", f_pool, [p_tile], pool_pars, [Out((T, 1024), BF16, 1024)])
    y_b = mm("branch_b", _NN, [(pooled, W["w_branch_b"], _bs((tmm, 1024), lambda j, i, k: (i, 0)),
                                _bs((None, 1024, 512), lambda j, i, k: (j, 0, 0)))],
             S((T, D), F32), _bs((tmm, 512), lambda j, i, k: (i, j)), (4, nI, 1))
    gate_tile = Tile(P2, 2 * D, cb=C_GATES // (2 * D))
    merge_ins = [gate_tile, Tile(y_a, D), Tile(y_b, D)]
    (merged,) = stage("merge", f_merge, merge_ins, [], [o2048(BF16)])
    y1 = dense("mix_out", merged, W["w_mix_out"])
    pp1 = [Par(sp["mix_post_norm"]), Par(sp["xa_pre_norm"])]
    x1, h2 = stage("post1", f_post_pre, [Tile(x, D), Tile(y1, D)], pp1, [o2048(F32), o2048(BF16)])

    q2 = dense("xq", h2, W["w_xq"], out_dtype=BF16)
    (mn,) = stage("mem_norm", f_prenorm, [Tile(mem, D)], [Par(sp["mem_norm"])], [Out(mem.shape, BF16, D)], tm=mem.shape[0])
    M = mem.shape[0]
    kv = mm("xkv", _NN, [(mn, W["w_xkv"], _bs((M, D), lambda j, i, k: (0, 0)), _bs((None, D, 1024), lambda j, i, k: (j, 0, 0)))],
            S((M, 2 * D), F32), _bs((M, 1024), lambda j, i, k: (0, j)), (4, 1, 1))
    k2, v2 = kv[:, :D], kv[:, D:]
    xa_pars = [Par(k2), Par(v2)]
    (o2,) = stage("xattn", f_xattn, [Tile(q2, D)], xa_pars, [o2048(BF16)])
    y2 = dense("xo", o2, W["w_xo"])
    pp2 = [Par(sp["xa_post_norm"]), Par(sp["ffn_pre_norm"])]
    x2, h3 = stage("post2", f_post_pre, [Tile(x1, D), Tile(y2, D)], pp2, [o2048(F32), o2048(BF16)])

    def up(name, off):
        return mm(name, _NN, [(h3, W["w_up"], _bs((tmm, D), lambda j, i, k: (i, 0)),
                               _bs((None, D, FF_BLK), lambda j, i, k: (j + off, 0, 0)))],
                  S((2, T, FF_BLK), F32), _bs((None, tmm, FF_BLK), lambda j, i, k: (j, i, 0)), (2, nI, 1))

    Ua, Ub = up("up_a", 0), up("up_b", 2)
    ffn_ins = [Tile(Ua, FF_BLK, lead=lambda o: o, halo=True), Tile(Ub, FF_BLK, lead=lambda o: o, halo=True)]
    ffn_pars = [Par(sp["ffn_conv_w"], lead=lambda o: o), Par(sp["ffn_conv_w"], lead=lambda o: o + 2),
                Par(sp["ffn_conv_b"], lead=lambda o: o), Par(sp["ffn_conv_b"], lead=lambda o: o + 2)]
    ffn_out = [Out((2, T, FF_BLK), BF16, FF_BLK, lead=lambda o: o)]
    (ff,) = stage("convglu", f_convglu, ffn_ins, ffn_pars, ffn_out, outer=2)
    y3 = mm("down", _NN, [(ff, W["w_down"], _bs((None, tmm, FF_BLK), lambda i, j, k: (k, i, 0)),
                           _bs((None, FF_BLK, D), lambda i, j, k: (k, 0, 0)))],
            S((T, D), F32), _bs((tmm, D), lambda i, j, k: (i, 0)), (nI, 1, 2))
    loss, dx2, dy3, gs["ffn_post_norm"] = loss_stage(x2, y3, tgt, sp["ffn_post_norm"], tm)

    dff = mm("down_dx", _NT, [(dy3, W["w_down"], _bs((tmm, D), lambda j, i, k: (i, 0)),
                               _bs((None, FF_BLK, D), lambda j, i, k: (j, 0, 0)))],
             S((2, T, FF_BLK), BF16), _bs((None, tmm, FF_BLK), lambda j, i, k: (j, i, 0)), (2, nI, 1))
    tbig = min(1024, T)
    gw["w_down"] = mm("down_dw", _TN, [(ff, dy3, _bs((None, tbig, FF_BLK), lambda b, j, k: (b, k, 0)),
                                        _bs((tbig, 1024), lambda b, j, k: (k, j)))],
                      S((2, FF_BLK, D), F32), _bs((None, FF_BLK, 1024), lambda b, j, k: (b, 0, j)), (2, D // 1024, T // tbig))
    dU_out = [Out((2, T, FF_BLK), BF16, FF_BLK, lead=lambda o: o), Out((2, T, FF_BLK), BF16, FF_BLK, lead=lambda o: o)]
    (dUa, dUb), dffn = stage("convglu_bwd", f_convglu, ffn_ins, ffn_pars, ffn_out, outer=2, cts=[dff], dins=dU_out)
    gs["ffn_conv_w"] = jnp.concatenate([dffn[0][:2], dffn[1][2:]], axis=0)
    gs["ffn_conv_b"] = jnp.concatenate([dffn[2][:2], dffn[3][2:]], axis=0)
    dh3 = mm("up_dx", _NT, [(dUa, W["w_up"], _bs((None, tmm, FF_BLK), lambda i, j, k: (k, i, 0)),
                             _bs((None, 1024, FF_BLK), lambda i, j, k: (k, j, 0))),
                            (dUb, W["w_up"], _bs((None, tmm, FF_BLK), lambda i, j, k: (k, i, 0)),
                             _bs((None, 1024, FF_BLK), lambda i, j, k: (k + 2, j, 0)))],
             S((T, D), F32), _bs((tmm, 1024), lambda i, j, k: (i, j)), (nI, D // 1024, 2))

    def up_dw(name, dU):
        tt = min(2048, T)
        return mm(name, _TN, [(h3, dU, _bs((tt, 512), lambda b, i, k: (k, i)), _bs((None, tt, FF_BLK), lambda b, i, k: (b, k, 0)))],
                  S((2, D, FF_BLK), F32), _bs((None, 512, FF_BLK), lambda b, i, k: (b, i, 0)), (2, D // 512, T // tt))

    gw["w_up_a"], gw["w_up_b"] = up_dw("up_dw_a", dUa), up_dw("up_dw_b", dUb)
    hook("ffn", gw)
    (dx1, dy2), dpp2 = stage("post2_bwd", f_post_pre, [Tile(x1, D), Tile(y2, D)], pp2, [o2048(F32), o2048(BF16)],
                             cts=[dx2, dh3], dins=[o2048(F32), o2048(BF16)])
    gs["xa_post_norm"], gs["ffn_pre_norm"] = dpp2

    do2 = dense_t("xo_dx", dy2, W["w_xo"])
    gw["w_xo"] = wgrad("xo_dw", o2, dy2)
    (dq2,), (dk2, dv2) = stage("xattn_bwd", f_xattn, [Tile(q2, D)], xa_pars, [o2048(BF16)], cts=[do2], dins=[o2048(BF16)])
    dh2 = dense_t("xq_dx", dq2, W["w_xq"])
    gw["w_xq"] = wgrad("xq_dw", h2, dq2)
    dkv = jnp.concatenate([dk2, dv2], axis=1).astype(BF16)
    dmn = mm("xkv_dx", _NT, [(dkv, W["w_xkv"], _bs((M, 1024), lambda i, j, k: (0, k)), _bs((None, 512, 1024), lambda i, j, k: (k, j, 0)))],
             S((M, D), F32), _bs((M, 512), lambda i, j, k: (0, j)), (1, D // 512, 4))
    gw["w_xkv"] = mm("xkv_dw", _TN, [(mn, dkv, _bs((M, D), lambda b, j, k: (0, 0)), _bs((M, 1024), lambda b, j, k: (0, b)))],
                     S((4, D, 1024), F32), _bs((None, D, 1024), lambda b, j, k: (b, 0, 0)), (4, 1, 1))
    hook("xattn", gw)
    _, (gs["mem_norm"],) = stage("mem_norm_bwd", f_prenorm, [Tile(mem, D)], [Par(sp["mem_norm"])], [Out(mem.shape, BF16, D)],
                                 tm=M, cts=[dmn], dins=[None])
    (dx0, dy1), dpp1 = stage("post1_bwd", f_post_pre, [Tile(x, D), Tile(y1, D)], pp1, [o2048(F32), o2048(BF16)],
                             cts=[dx1, dh2], dins=[o2048(F32), o2048(BF16)])
    gs["mix_post_norm"], gs["xa_pre_norm"] = dpp1

    dmerged = dense_t("mix_out_dx", dy1, W["w_mix_out"])
    gw["w_mix_out"] = wgrad("mix_out_dw", merged, dy1)
    pshape = (T, N_PROJ)
    (dP2, dya, dyb), _ = stage("merge_bwd", f_merge, merge_ins, [], [o2048(BF16)], cts=[dmerged],
                               dins=[Out(pshape, BF16, 2 * D, cb=C_GATES // (2 * D)), o2048(BF16), o2048(BF16)])
    d_on = dense_t("branch_a_dx", dya, W["w_branch_a"])
    gw["w_branch_a"] = wgrad("branch_a_dw", o_n, dya)
    dpooled = mm("branch_b_dx", _NT, [(dyb, W["w_branch_b"], _bs((tmm, 512), lambda i, j, k: (i, k)),
                                       _bs((None, 1024, 512), lambda i, j, k: (k, 0, 0)))],
                 S((T, 1024), F32), _bs((tmm, 1024), lambda i, j, k: (i, 0)), (nI, 1, 4))
    gw["w_branch_b"] = mm("branch_b_dw", _TN, [(pooled, dyb, _bs((tmm, 1024), lambda b, j, k: (k, 0)), _bs((tmm, 512), lambda b, j, k: (k, b)))],
                          S((4, 1024, 512), F32), _bs((None, 1024, 512), lambda b, j, k: (b, 0, 0)), (4, 1, nI))
    hook("mixer", gw)
    (do, dP2), (gs["gdn_norm"],) = stage("gnorm_bwd", f_gnorm, [Tile(o, D), z_tile], [Par(sp["gdn_norm"])], [o2048(BF16)],
                                         cts=[d_on], dins=[o2048(F32), Out(pshape, BF16, D, cb=C_Z // D, into=dP2)])
    dq, dk, dv, dbg = gdn_backward(qkv[0], qkv[1], qkv[2], bg, s0, do, GDN_SPAN_BWD)
    dcw = []
    for i, (n, dqq) in enumerate(zip("qkv", (dq, dk, dv))):
        (dP2,), (dc,) = stage("conv_%s_bwd" % n, f_heads[i], [Tile(P2, D, cb=i, halo=True)], [Par(cws[i])], [o2048(F32)],
                              cts=[dqq], dins=[Out(pshape, BF16, D, cb=i, into=dP2)])
        dcw.append(dc)
    gs["conv_qkv"] = jnp.concatenate(dcw, axis=1)
    (dP2,), (gs["a_log"], gs["dt_bias"]) = stage("bg_bwd", f_bg, [ba_tile], [Par(sp["a_log"]), Par(sp["dt_bias"])],
                                                 [Out((T, 128), F32, 128)], cts=[dbg],
                                                 dins=[Out(pshape, BF16, 512, cb=C_BA // 512, into=dP2)])
    (dP2,), (gs["pool_w"], gs["pool_scale"]) = stage("pool_bwd", f_pool, [p_tile], pool_pars, [Out((T, 1024), BF16, 1024)],
                                                     cts=[dpooled], dins=[Out(pshape, BF16, 1024, cb=C_POOL // 1024, into=dP2)])
    tk, ta, tt = 2304, 1152, min(2048, T)
    gw["w_in"] = mm("in_proj_dw", _TN, [(dP2, h1, _bs((tt, ta), lambda i, j, k: (k, i)), _bs((tt, D), lambda i, j, k: (k, 0)))],
                    S((N_PROJ, D), F32), _bs((ta, D), lambda i, j, k: (i, 0)), (N_PROJ // ta, 1, T // tt))
    hook("in_proj", gw)
    dh1 = mm("in_proj_dx", _NN, [(dP2, W["w_in"], _bs((tbig, tk), lambda i, j, k: (i, k)), _bs((tk, D), lambda i, j, k: (k, 0)))],
             S((T, D), F32), _bs((tbig, D), lambda i, j, k: (i, 0)), (T // tbig, 1, N_PROJ // tk))
    (grad_x,), (gs["mix_pre_norm"],) = stage("pre1_bwd", f_prenorm_res, [Tile(x, D)], [Par(sp["mix_pre_norm"])],
                                             [o2048(F32), o2048(BF16)], cts=[dx0, dh1], dins=[o2048(F32)])
    return loss, grad_x, gw, gs


W_IN_COLS = 13344
GROUPED = {"w_branch_b": 512, "w_xkv": 1024}
ROW_SHARDED = ("w_branch_a", "w_mix_out", "w_xq", "w_xo")


def shard_to_slab(name, w):
    if name == "w_in":
        return w.T.astype(BF16)
    if name == "w_up":
        return jnp.pad(w, ((0, 0), (0, FF_BLK - FF_COLS))).astype(BF16)
    return w.astype(BF16)


def slabs_to_weight(name, g):
    if name == "w_in":
        full = g.astype(F32).reshape(W_IN_COLS, D_MODEL)
        pad = jnp.zeros((N_PROJ - W_IN_COLS, D_MODEL), F32)
        return jnp.concatenate([full[0:8192], full[9248:13344], full[8224:9248], full[8192:8224], pad]).astype(BF16)
    if name == "w_down":
        z = jnp.zeros((2, FF_BLK - FF_COLS, D_MODEL), g.dtype)
        return jnp.concatenate([g.reshape(2, FF_COLS, D_MODEL), z], axis=1)
    if name in ROW_SHARDED:
        return g.reshape(D_MODEL, D_MODEL)
    return g


def grad_to_slabs(name, gw):
    if name == "w_in":
        g = gw["w_in"]
        return jnp.concatenate([g[0:8192], g[13312:13344], g[12288:13312], g[8192:12288]]).reshape(4, 3336, D_MODEL)
    if name == "w_up":
        return jnp.concatenate([gw["w_up_a"], gw["w_up_b"]], axis=0)
    if name == "w_down":
        return gw["w_down"][:, :FF_COLS].reshape(4, FF_COLS // 2, D_MODEL)
    if name in ROW_SHARDED:
        return gw[name].reshape(4, D_MODEL // 4, D_MODEL)
    return gw[name]


def slab_to_shard_grad(name, f):
    if name == "w_in":
        return f.T
    if name == "w_up":
        return f[:, :FF_COLS]
    return f


MESH = pl.DeviceIdType.MESH
ANY = pl.BlockSpec(memory_space=pl.ANY)


def _me():
    x, y, c = lax.axis_index("x"), lax.axis_index("y"), lax.axis_index("c")
    return x, y, c, 2 * x + y


def _chip_dev(t, c):
    return (t // 2, t % 2, c)


def _rcopy(src, dst, ssem, rsem, dev):
    return pltpu.make_async_remote_copy(src_ref=src, dst_ref=dst, send_sem=ssem, recv_sem=rsem, device_id=dev, device_id_type=MESH)


def _handshake_all(x, y, c):
    barrier = pltpu.get_barrier_semaphore()
    for dx in (0, 1):
        for dy in (0, 1):
            for dc in (0, 1):
                if dx or dy or dc:
                    pl.semaphore_signal(barrier, inc=1, device_id=((x + dx) % 2, (y + dy) % 2, (c + dc) % 2), device_id_type=MESH)
    pl.semaphore_wait(barrier, 7)


def _comm_call(body, name, operands, out_shape, sems, collective_id):
    if collective_id is not None:
        return pl.kernel(body, out_type=out_shape, mesh=plsc.ScalarSubcoreMesh(axis_name="seq", num_cores=1), name=name,
                         scratch_types=sems, compiler_params=pltpu.CompilerParams(collective_id=collective_id))(*operands)
    n_in, n_out = len(operands), len(out_shape)
    return pl.pallas_call(body, name=name, in_specs=[ANY] * n_in, out_specs=[ANY] * n_out, out_shape=out_shape,
                          scratch_shapes=sems)(*operands)


def gather_weights(slabs, name="gather_weights", collective_id=None):
    n = len(slabs)

    def body(*refs):
        src, dst = refs[:n], refs[n:2 * n]
        ici_s, ici_r, fwd_s, fwd_r = refs[2 * n:]
        x, y, c, s = _me()
        if collective_id is not None:
            _handshake_all(x, y, c)
        sender = c == s // 2
        sib = (x, y, 1 - c)
        for r in (1, 2, 3):
            @pl.when(sender)
            def _(r=r):
                for w in range(n):
                    _rcopy(src[w], dst[w].at[s], ici_s.at[w, r - 1], ici_r.at[w, r - 1], _chip_dev(s ^ r, c)).start()
        for r in (1, 2, 3):
            t = s ^ r
            here = c == t // 2

            @pl.when(here)
            def _(r=r, t=t):
                for w in range(n):
                    _rcopy(src[w], dst[w].at[t], ici_s.at[w, r - 1], ici_r.at[w, r - 1], sib).wait_recv()
                    _rcopy(dst[w].at[t], dst[w].at[t], fwd_s.at[w, r - 1], fwd_r.at[w, r - 1], sib).start()

            @pl.when(jnp.logical_not(here))
            def _(r=r, t=t):
                for w in range(n):
                    _rcopy(dst[w].at[t], dst[w].at[t], fwd_s.at[w, r - 1], fwd_r.at[w, r - 1], sib).wait_recv()
        for r in (1, 2, 3):
            t = s ^ r

            @pl.when(sender)
            def _(r=r):
                for w in range(n):
                    _rcopy(src[w], dst[w].at[s], ici_s.at[w, r - 1], ici_r.at[w, r - 1], sib).wait_send()

            @pl.when(c == t // 2)
            def _(r=r, t=t):
                for w in range(n):
                    _rcopy(dst[w].at[t], dst[w].at[t], fwd_s.at[w, r - 1], fwd_r.at[w, r - 1], sib).wait_send()

    out_shape = [jax.ShapeDtypeStruct((4,) + a.shape, a.dtype) for a in slabs]
    sems = [pltpu.SemaphoreType.DMA((n, 3))] * 4
    return _comm_call(body, name, slabs, out_shape, sems, collective_id)


def pair_exchange(g4, name="pair_exchange", collective_id=None):
    n = len(g4)

    def body(*refs):
        src, dst = refs[:n], refs[n:2 * n]
        ssem, rsem = refs[2 * n:]
        x, y, c, s = _me()
        if collective_id is not None:
            _handshake_all(x, y, c)
        cps = [_rcopy(src[w].at[pl.ds(2 * (1 - c), 2)], dst[w], ssem.at[w], rsem.at[w], (x, y, 1 - c)) for w in range(n)]
        for cp in cps:
            cp.start()
        for cp in cps:
            cp.wait()

    return _comm_call(body, name, g4, [jax.ShapeDtypeStruct((2,) + a.shape[1:], a.dtype) for a in g4],
                      [pltpu.SemaphoreType.DMA((n,))] * 2, collective_id)


def _col_tile(R, C):
    for tc in (512, 256, 128):
        if C % tc == 0 and R * tc * 4 <= (4 << 20):
            return tc
    return 128


def pair_add(name, g4, gsib, c):
    _, R, C = g4.shape
    tc = _col_tile(R, C)

    def body(c_ref, a_ref, b_ref, of_ref, ob_ref):
        v = a_ref[...] + b_ref[...]
        of_ref[...] = v
        ob_ref[...] = v.astype(BF16)

    blk = lambda f: pl.BlockSpec((None, R, tc), f)
    return _pcall(
        body, (c.reshape(1).astype(jnp.int32), g4, gsib), name=name, num_scalar_prefetch=1, grid=(2, C // tc),
        in_specs=[blk(lambda p, j, cr: (2 * cr[0] + p, 0, j)), blk(lambda p, j, cr: (p, 0, j))],
        out_specs=[blk(lambda p, j, cr: (p, 0, j)), blk(lambda p, j, cr: (p, 0, j))],
        out_shape=[jax.ShapeDtypeStruct((2, R, C), F32), jax.ShapeDtypeStruct((2, R, C), BF16)],
        compiler_params=_cparams(("arbitrary", "arbitrary")),
    )


def scatter_partials(rb, name="scatter_partials", collective_id=None):
    n = len(rb)

    def body(*refs):
        src, dst = refs[:n], refs[n:2 * n]
        ssem, rsem = refs[2 * n:]
        x, y, c, s = _me()
        if collective_id is not None:
            _handshake_all(x, y, c)
        for r in (1, 2, 3):
            t = s ^ r

            @pl.when(t // 2 == c)
            def _(r=r, t=t):
                for w in range(n):
                    _rcopy(src[w].at[t % 2], dst[w].at[s], ssem.at[w, r - 1], rsem.at[w, r - 1], _chip_dev(t, c)).start()
        for r in (1, 2, 3):
            t = s ^ r

            @pl.when(s // 2 == c)
            def _(r=r, t=t):
                for w in range(n):
                    _rcopy(src[w].at[0], dst[w].at[t], ssem.at[w, r - 1], rsem.at[w, r - 1], _chip_dev(t, c)).wait_recv()
        for r in (1, 2, 3):
            t = s ^ r

            @pl.when(t // 2 == c)
            def _(r=r, t=t):
                for w in range(n):
                    _rcopy(src[w].at[t % 2], dst[w].at[s], ssem.at[w, r - 1], rsem.at[w, r - 1], _chip_dev(t, c)).wait_send()

    return _comm_call(body, name, rb, [jax.ShapeDtypeStruct((4,) + a.shape[1:], a.dtype) for a in rb],
                      [pltpu.SemaphoreType.DMA((n, 3))] * 2, collective_id)


def final_sum(name, rf, recv, s):
    _, R, C = rf.shape
    tc = _col_tile(R, C)

    def body(s_ref, own_ref, r0_ref, r1_ref, r2_ref, o_ref):
        o_ref[...] = ((own_ref[...] + r0_ref[...].astype(F32)) + r1_ref[...].astype(F32)) + r2_ref[...].astype(F32)

    blk = lambda f: pl.BlockSpec((None, R, tc), f)
    other = lambda k: (lambda j, sr: (k + (k >= sr[0]).astype(jnp.int32), 0, j))
    return _pcall(
        body, (s.reshape(1).astype(jnp.int32), rf, recv, recv, recv), name=name, num_scalar_prefetch=1, grid=(C // tc,),
        in_specs=[blk(lambda j, sr: (sr[0] % 2, 0, j)), blk(other(0)), blk(other(1)), blk(other(2))],
        out_specs=pl.BlockSpec((R, tc), lambda j, sr: (0, j)), out_shape=jax.ShapeDtypeStruct((R, C), F32),
        compiler_params=_cparams(("arbitrary",)),
    )


def share_with_sibling(fs, name="share_with_sibling", collective_id=None):
    n = len(fs)

    def body(*refs):
        src, dst = refs[:n], refs[n:2 * n]
        ssem, rsem = refs[2 * n:]
        x, y, c, s = _me()
        if collective_id is not None:
            _handshake_all(x, y, c)
        sib = (x, y, 1 - c)

        @pl.when(s // 2 == c)
        def _():
            cps = [_rcopy(src[w], dst[w], ssem.at[w], rsem.at[w], sib) for w in range(n)]
            for cp in cps:
                cp.start()
            for cp in cps:
                cp.wait_send()

        @pl.when(s // 2 != c)
        def _():
            for w in range(n):
                _rcopy(src[w], dst[w], ssem.at[w], rsem.at[w], sib).wait_recv()

    return _comm_call(body, name, fs, [jax.ShapeDtypeStruct(a.shape, a.dtype) for a in fs],
                      [pltpu.SemaphoreType.DMA((n,))] * 2, collective_id)


def pick(name, mine, a, b):
    R, C = a.shape
    tc = _col_tile(R, C)

    def body(m_ref, a_ref, b_ref, o_ref):
        o_ref[...] = jnp.where(m_ref[0] != 0, a_ref[...], b_ref[...])

    blk = pl.BlockSpec((R, tc), lambda j, mr: (0, j))
    return _pcall(body, (mine.reshape(1).astype(jnp.int32), a, b), name=name, num_scalar_prefetch=1, grid=(C // tc,),
                  in_specs=[blk, blk], out_specs=blk, out_shape=jax.ShapeDtypeStruct((R, C), a.dtype),
                  compiler_params=_cparams(("arbitrary",)))


def allgather_rows(v):
    m_per, ncol = v.shape

    def body(x_ref, out_ref, send_sems, recv_sems, local_sem):
        x, y, c = lax.axis_index("x"), lax.axis_index("y"), lax.axis_index("c")
        me, sibling = (x, y, c), (x, y, 1 - c)
        chips = [(1 - x, y), (x, 1 - y), (1 - x, 1 - y)]

        def rows(px, py, pc):
            return out_ref.at[pl.ds((4 * px + 2 * py + pc) * m_per, m_per), :]

        def copy(k, block, to, src=None):
            return _rcopy(rows(*block) if src is None else src, rows(*block), send_sems.at[k], recv_sems.at[k], to)

        mine = pltpu.make_async_copy(x_ref, rows(*me), local_sem)
        mine.start()
        first = [copy(0, me, sibling, src=x_ref)]
        first += [copy(1 + j, me, (*chip, c), src=x_ref) for j, chip in enumerate(chips)]
        for cp in first:
            cp.start()
        passed = [copy(4 + j, (*chip, c), sibling) for j, chip in enumerate(chips)]
        for j, chip in enumerate(chips):
            copy(1 + j, (*chip, c), me).wait_recv()
            passed[j].start()
        copy(0, sibling, me).wait_recv()
        for j, chip in enumerate(chips):
            copy(4 + j, (*chip, 1 - c), me).wait_recv()
        for cp in first + passed:
            cp.wait_send()
        mine.wait()

    return pl.pallas_call(
        body, name="allgather_rows", out_shape=jax.ShapeDtypeStruct((8 * m_per, ncol), v.dtype),
        in_specs=[pl.BlockSpec(memory_space=pltpu.VMEM)], out_specs=pl.BlockSpec(memory_space=pltpu.VMEM),
        scratch_shapes=[pltpu.SemaphoreType.DMA((7,)), pltpu.SemaphoreType.DMA((7,)), pltpu.SemaphoreType.DMA],
        compiler_params=pltpu.CompilerParams(vmem_limit_bytes=V7X_VMEM_LIMIT),
    )(v)


def sum_blocks(name, a, nblk):
    m = a.shape[0] // nblk

    def body(a_ref, o_ref):
        acc = a_ref[pl.ds(0, m), :]
        for b in range(1, nblk):
            acc = acc + a_ref[pl.ds(b * m, m), :]
        o_ref[...] = acc

    return pl.pallas_call(body, name=name, out_shape=jax.ShapeDtypeStruct((m, a.shape[1]), a.dtype),
                          compiler_params=pltpu.CompilerParams(vmem_limit_bytes=V7X_VMEM_LIMIT))(a)


BIG = ("w_in", "w_branch_a", "w_branch_b", "w_mix_out", "w_xq", "w_xkv", "w_xo", "w_up", "w_down")
GAINS = ("mix_pre_norm", "gdn_norm", "pool_scale", "mix_post_norm", "xa_pre_norm", "mem_norm", "xa_post_norm",
         "ffn_pre_norm", "ffn_post_norm")
WEIGHTS = ("mix_pre_norm", "w_in", "conv_qkv", "a_log", "dt_bias", "gdn_norm", "pool_w", "pool_scale", "w_branch_a",
           "w_branch_b", "w_mix_out", "mix_post_norm", "xa_pre_norm", "mem_norm", "w_xq", "w_xkv", "w_xo", "xa_post_norm",
           "ffn_pre_norm", "w_up", "ffn_conv_w", "ffn_conv_b", "w_down", "ffn_post_norm")


RS_GROUPS = {"ffn": (("w_up", "w_down"), (3, 4, 5)), "xattn": (("w_xo", "w_xq", "w_xkv"), (6, 7, 8)),
             "mixer": (("w_mix_out", "w_branch_a", "w_branch_b"), (9, 10, 11)), "in_proj": (("w_in",), (12, 13, 14))}


def _rows128(vecs):
    flat = jnp.concatenate([v.reshape(-1) for v in vecs])
    m = -(-flat.shape[0] // 1024) * 8
    return jnp.pad(flat, (0, m * 128 - flat.shape[0])).reshape(m, 128)


def _unrows(a, shapes):
    flat, out, pos = a.reshape(-1), [], 0
    for sh in shapes:
        n = 1
        for d in sh:
            n *= d
        out.append(flat[pos:pos + n].reshape(sh))
        pos += n
    return out


def _lane128(v):
    return jnp.pad(v.reshape(1, GDN_HEADS), ((0, 0), (GDN_HEADS, 128 - 2 * GDN_HEADS)))


def kernel(x, mem, mix_pre_norm, w_in, conv_qkv, a_log, dt_bias, gdn_norm, pool_w, pool_scale, w_branch_a, w_branch_b, w_mix_out, mix_post_norm, xa_pre_norm, mem_norm, w_xq, w_xkv, w_xo, xa_post_norm, ffn_pre_norm, w_up, ffn_conv_w, ffn_conv_b, w_down, ffn_post_norm, loss_target, m_mix_pre_norm, m_w_in, m_conv_qkv, m_a_log, m_dt_bias, m_gdn_norm, m_pool_w, m_pool_scale, m_w_branch_a, m_w_branch_b, m_w_mix_out, m_mix_post_norm, m_xa_pre_norm, m_mem_norm, m_w_xq, m_w_xkv, m_w_xo, m_xa_post_norm, m_ffn_pre_norm, m_w_up, m_ffn_conv_w, m_ffn_conv_b, m_w_down, m_ffn_post_norm, v_mix_pre_norm, v_w_in, v_conv_qkv, v_a_log, v_dt_bias, v_gdn_norm, v_pool_w, v_pool_scale, v_w_branch_a, v_w_branch_b, v_w_mix_out, v_mix_post_norm, v_xa_pre_norm, v_mem_norm, v_w_xq, v_w_xkv, v_w_xo, v_xa_post_norm, v_ffn_pre_norm, v_w_up, v_ffn_conv_w, v_ffn_conv_b, v_w_down, v_ffn_post_norm):
    given = dict(locals())
    _LAST[0] = None
    w = {n: given[n][0] for n in WEIGHTS}
    cx, cy, cc = lax.axis_index("x"), lax.axis_index("y"), lax.axis_index("c")
    chip = 2 * cx + cy

    slabs = [shard_to_slab(n, w[n]) for n in BIG]
    first = gather_weights(slabs[:1], name="gather_w_in", collective_id=1)
    rest = gather_weights(slabs[1:], name="gather_rest", collective_id=2)
    stacks = [lax.dynamic_update_index_in_dim(g, sl, chip, 0) for g, sl in zip(list(first) + list(rest), slabs)]
    W = {n: slabs_to_weight(n, g) for n, g in zip(BIG, stacks)}
    sharded_small = (w["conv_qkv"], w["ffn_conv_w"], w["pool_w"])
    allv = allgather_rows(_rows128(sharded_small))
    per_chip = allv.reshape(8, -1)[0::2]
    parts = [_unrows(per_chip[t], [a.shape for a in sharded_small]) for t in range(4)]
    sp = {n: w[n].reshape(1, -1) for n in GAINS}
    sp["a_log"], sp["dt_bias"] = _lane128(w["a_log"]), _lane128(w["dt_bias"])
    sp["conv_qkv"] = jnp.concatenate([p[0] for p in parts], axis=1)
    sp["ffn_conv_w"] = jnp.pad(jnp.stack([p[1] for p in parts]), ((0, 0), (0, 0), (0, FF_BLK - FF_COLS)))
    sp["pool_w"] = jnp.concatenate([p[2] for p in parts], axis=1).reshape(4 * 256, 256)
    sp["ffn_conv_b"] = jnp.pad(w["ffn_conv_b"].reshape(4, 1, FF_COLS), ((0, 0), (0, 0), (0, FF_BLK - FF_COLS)))

    grads, todo = {}, []

    def rs_steps(names, ids, gw):
        g4 = [grad_to_slabs(n, gw) for n in names]
        tag = names[0]
        gsib = pair_exchange(g4, "pair_exchange_" + tag, ids[0])
        yield
        sums = [pair_add("pair_add_" + n, a, b, cc) for n, a, b in zip(names, g4, gsib)]
        recv = scatter_partials([sb for _, sb in sums], "scatter_partials_" + tag, ids[1])
        yield
        fin = [final_sum("final_sum_" + n, sf, rv, chip) for n, (sf, _), rv in zip(names, sums, recv)]
        got = share_with_sibling(fin, "share_" + tag, ids[2])
        yield
        for n, f, g in zip(names, fin, got):
            grads[n] = slab_to_shard_grad(n, pick("pick_" + n, (chip // 2 == cc), f, g))

    def advance():
        for it in list(todo):
            if next(it, "done") == "done":
                todo.remove(it)

    def hook(event, gw):
        new = rs_steps(*RS_GROUPS[event], gw)
        next(new)
        advance()
        todo.append(new)
        if event == "in_proj":
            next(new)

    loss, grad_x, gw, gs = local_step(x[0], mem[0], loss_target[0], W, sp, hook)

    small_names = GAINS + ("a_log", "dt_bias", "ffn_conv_b", "conv_qkv", "ffn_conv_w", "pool_w")
    vec = _rows128([gs[n] for n in small_names])
    total = sum_blocks("sum_small", allgather_rows(vec), 8)
    advance()
    loss = lax.psum(loss[0, 0], ("x", "y", "c"))
    tot = dict(zip(small_names, _unrows(total, [gs[n].shape for n in small_names])))
    for n in GAINS:
        grads[n] = tot[n].reshape(-1)
    grads["a_log"] = tot["a_log"][0, GDN_HEADS:2 * GDN_HEADS]
    grads["dt_bias"] = tot["dt_bias"][0, GDN_HEADS:2 * GDN_HEADS]
    grads["ffn_conv_b"] = tot["ffn_conv_b"][:, 0, :FF_COLS].reshape(-1)
    grads["conv_qkv"] = lax.dynamic_slice_in_dim(tot["conv_qkv"], chip * 1536, 1536, axis=1)
    grads["ffn_conv_w"] = lax.dynamic_index_in_dim(tot["ffn_conv_w"], chip, axis=0, keepdims=False)[:, :FF_COLS]
    grads["pool_w"] = lax.dynamic_slice_in_dim(tot["pool_w"].reshape(4, 256, 256), chip * 64, 64, axis=1)

    delta, new_m, new_v = {}, {}, {}

    def update(names):
        for n in names:
            shp = w[n].shape
            two = (lambda a: a.reshape(-1, shp[-1])) if len(shp) > 1 else (lambda a: a.reshape(1, -1))
            d, nm, nv = adamw("adamw_" + n, two(w[n]), two(grads[n]), two(given["m_" + n][0]), two(given["v_" + n][0]))
            delta[n], new_m[n], new_v[n] = (a.reshape((1,) + shp) for a in (d, nm, nv))

    update([n for n in WEIGHTS if n in grads])
    while todo:
        advance()
    update([n for n in WEIGHTS if n not in delta])
    out_g = [grads[n].reshape((1,) + w[n].shape) for n in WEIGHTS]
    return (loss, grad_x[None], *out_g, *[delta[n] for n in WEIGHTS], *[new_m[n] for n in WEIGHTS], *[new_v[n] for n in WEIGHTS])
```

```python
import functools

import jax
import jax.numpy as jnp
from jax import lax
from jax.experimental import pallas as pl
from jax.experimental.pallas import tpu as pltpu
from jax.experimental.pallas import tpu_sc as plsc

F32 = jnp.float32
BF16 = jnp.bfloat16

D_MODEL = 2048
CHUNK = 64
GDN_HEADS = 16
GDN_DK = 128
POOL_WINDOWS = (2, 4, 8, 16)
XA_HEADS = 4
XA_HEAD_DIM = D_MODEL // XA_HEADS
EPS = 1e-6
GDN_SPAN = 256
GDN_SPAN_BWD = 128
HALO = 16
V7X_VMEM_LIMIT = 56 * 1024 * 1024

ADAM_LR, ADAM_B1, ADAM_B2, ADAM_EPS, ADAM_WD, ADAM_STEP = 0.001, 0.9, 0.999, 1e-08, 0.01, 10

_NN = ((1,), (0,))
_NT = ((1,), (1,))
_TN = ((0,), (0,))


def _cparams(sem):
    return pltpu.CompilerParams(dimension_semantics=sem, vmem_limit_bytes=V7X_VMEM_LIMIT)


_LAST = [None]


def _pcall(body, operands, *, in_specs, out_specs, grid=(), num_scalar_prefetch=0, scratch_shapes=(), **kw):
    operands, in_specs = list(operands), list(in_specs)
    if _LAST[0] is not None and not any(o is _LAST[0] for o in operands):
        n = len(operands)
        inner = body

        def body(*refs):
            return inner(*refs[:n], *refs[n + 1:])

        operands.append(_LAST[0])
        in_specs.append(pl.BlockSpec(memory_space=pl.ANY))
    if num_scalar_prefetch:
        kw["grid_spec"] = pltpu.PrefetchScalarGridSpec(num_scalar_prefetch=num_scalar_prefetch, grid=grid, in_specs=in_specs,
                                                       out_specs=out_specs, scratch_shapes=list(scratch_shapes))
    else:
        kw.update(grid=grid, in_specs=in_specs, out_specs=out_specs, scratch_shapes=list(scratch_shapes))
    res = pl.pallas_call(body, **kw)(*operands)
    _LAST[0] = res[0] if isinstance(res, (list, tuple)) else res
    return res


def _dg(a, b, dims, prec=None):
    return lax.dot_general(a, b, (dims, ((), ())), precision=prec, preferred_element_type=F32)


def _make_dots(cast, prec):
    def raw(dims, a, b):
        return _dg(cast(a), cast(b), dims, prec)

    @jax.custom_vjp
    def nn(a, b):
        return raw(_NN, a, b)

    @jax.custom_vjp
    def nt(a, b):
        return raw(_NT, a, b)

    @jax.custom_vjp
    def tn(a, b):
        return raw(_TN, a, b)

    nn.defvjp(lambda a, b: (raw(_NN, a, b), (a, b)), lambda r, g: (nt(g, r[1]), tn(r[0], g)))
    nt.defvjp(lambda a, b: (raw(_NT, a, b), (a, b)), lambda r, g: (nn(g, r[1]), tn(g, r[0])))
    tn.defvjp(lambda a, b: (raw(_TN, a, b), (a, b)), lambda r, g: (nt(r[1], g), nn(r[0], g)))
    return nn, nt, tn


bdot_nn, bdot_nt, bdot_tn = _make_dots(lambda x: x.astype(BF16), None)
hdot_nn, hdot_nt, hdot_tn = _make_dots(lambda x: x, lax.Precision.HIGHEST)


@functools.partial(jax.custom_vjp, nondiff_argnums=(1,))
def shift_rows(x, k):
    return pltpu.roll(x, k, 0)


def _shift_rows_fwd(x, k):
    return pltpu.roll(x, k, 0), None


def _shift_rows_bwd(k, _, g):
    return (pltpu.roll(g, g.shape[0] - k, 0),)


shift_rows.defvjp(_shift_rows_fwd, _shift_rows_bwd)


@functools.partial(jax.custom_vjp, nondiff_argnums=(1,))
def drop_head(x, h):
    return x[h:]


def _drop_head_fwd(x, h):
    return x[h:], None


def _drop_head_bwd(h, _, g):
    return (jnp.concatenate([jnp.zeros((h,) + g.shape[1:], g.dtype), g], axis=0),)


drop_head.defvjp(_drop_head_fwd, _drop_head_bwd)


@functools.partial(jax.custom_vjp, nondiff_argnums=(1,))
def split_lanes(x, n):
    w = x.shape[-1] // n
    return tuple(x[:, i * w:(i + 1) * w] for i in range(n))


def _split_lanes_fwd(x, n):
    return split_lanes(x, n), None


def _split_lanes_bwd(n, _, gs):
    return (jnp.concatenate(list(gs), axis=-1),)


split_lanes.defvjp(_split_lanes_fwd, _split_lanes_bwd)


@functools.partial(jax.custom_vjp, nondiff_argnums=(1,))
def split_rows(x, n):
    h = x.shape[0] // n
    return tuple(x[i * h:(i + 1) * h] for i in range(n))


def _split_rows_fwd(x, n):
    return split_rows(x, n), None


def _split_rows_bwd(n, _, gs):
    return (jnp.concatenate(list(gs), axis=0),)


split_rows.defvjp(_split_rows_fwd, _split_rows_bwd)


def row_of(w, j):
    rid = lax.broadcasted_iota(jnp.int32, w.shape, 0)
    return jnp.sum(jnp.where(rid == j, w, 0.0), axis=0, keepdims=True)


def sigmoid(x):
    return 0.5 * jnp.tanh(0.5 * x) + 0.5


def silu(x):
    return x * sigmoid(x)


def softplus(x):
    return jnp.maximum(x, 0.0) + jnp.log(1.0 + jnp.exp(-jnp.abs(x)))


def rms(x, g):
    return x * lax.rsqrt(jnp.mean(x * x, axis=-1, keepdims=True) + EPS) * g


def mm(name, dims, pairs, out_shape, out_spec, grid):
    nk = grid[2]
    acc_in_out = nk > 1 and out_shape.dtype == F32
    npair = len(pairs)

    def body(*refs):
        o_ref = refs[2 * npair]
        part = None
        for p in range(npair):
            d = _dg(refs[2 * p][...].astype(BF16), refs[2 * p + 1][...].astype(BF16), dims)
            part = d if part is None else part + d
        if nk == 1:
            o_ref[...] = part.astype(o_ref.dtype)
            return
        acc = o_ref if acc_in_out else refs[2 * npair + 1]
        k = pl.program_id(2)

        @pl.when(k == 0)
        def _():
            acc[...] = part

        @pl.when(k > 0)
        def _():
            acc[...] += part

        if not acc_in_out:
            @pl.when(k == nk - 1)
            def _():
                o_ref[...] = acc[...].astype(o_ref.dtype)

    scratch = []
    if nk > 1 and not acc_in_out:
        scratch = [pltpu.VMEM(tuple(d for d in out_spec.block_shape if d is not None), F32)]
    in_specs, operands = [], []
    for a, b, a_spec, b_spec in pairs:
        in_specs += [a_spec, b_spec]
        operands += [a, b]
    return _pcall(body, operands, name=name, grid=grid, in_specs=in_specs, out_specs=out_spec, out_shape=out_shape,
                  scratch_shapes=scratch, compiler_params=_cparams(("parallel", "parallel", "arbitrary")))


def _bs(shape, fn):
    return pl.BlockSpec(shape, fn)


class Tile:
    def __init__(self, arr, w, cb=0, lead=None, halo=False):
        self.arr, self.w, self.cb, self.lead, self.halo = arr, w, cb, lead, halo


class Out:
    def __init__(self, shape, dtype, w, cb=0, lead=None, into=None):
        self.shape, self.dtype, self.w, self.cb, self.lead, self.into = shape, dtype, w, cb, lead, into


class Par:
    def __init__(self, arr, lead=None):
        self.arr, self.lead = arr, lead


def _spec(rows, w, cb, lead, tile_of):
    if lead is None:
        return pl.BlockSpec((rows, w), lambda o, i: (tile_of(i), cb))
    return pl.BlockSpec((None, rows, w), lambda o, i: (lead(o), tile_of(i), cb))


def _par_spec(p):
    if p.lead is None:
        return pl.BlockSpec(p.arr.shape, lambda o, i: (0, 0))
    return pl.BlockSpec((None,) + p.arr.shape[1:], lambda o, i: (p.lead(o), 0, 0))


def run_stage(name, fn, tm, ins, pars, outs, *, outer=1, cts=None, dins=None):
    T = ins[0].arr.shape[-2]
    nt = T // tm
    bwd = cts is not None
    any_halo = any(t.halo for t in ins)
    step_tile = (lambda i: nt - 1 - i) if bwd else (lambda i: i)
    hb = tm // HALO

    in_specs, operands = [], []
    for t in ins:
        if t.halo:
            in_specs.append(_spec(HALO, t.w, t.cb, t.lead, lambda i: jnp.maximum(step_tile(i) * hb - 1, 0)))
            operands.append(t.arr)
        in_specs.append(_spec(tm, t.w, t.cb, t.lead, step_tile))
        operands.append(t.arr)
    for p in pars:
        in_specs.append(_par_spec(p))
        operands.append(p.arr)
    n_in_refs = len(operands)

    out_descs = list(outs) if not bwd else [d for d in dins if d is not None]
    aliases = {}
    if bwd:
        for c, o in zip(cts, outs):
            in_specs.append(_spec(tm, o.w, o.cb, o.lead, step_tile))
            operands.append(c)
    n_ct = len(operands) - n_in_refs
    for k, o in enumerate(out_descs):
        if o.into is not None:
            aliases[len(operands)] = k
            in_specs.append(pl.BlockSpec(memory_space=pl.ANY))
            operands.append(o.into)
    out_specs = [_spec(tm, o.w, o.cb, o.lead, step_tile) for o in out_descs]
    out_shapes = [jax.ShapeDtypeStruct(o.shape, o.dtype) for o in out_descs]
    if bwd:
        for p in pars:
            out_specs.append(_par_spec(p))
            out_shapes.append(jax.ShapeDtypeStruct(p.arr.shape, F32))
    scratch = []
    if bwd and any_halo:
        scratch = [pltpu.VMEM((HALO, t.w), F32) for t, d in zip(ins, dins) if t.halo and d is not None]

    def body(*refs):
        i = pl.program_id(1)
        tile = step_tile(i)
        row0 = tile * tm
        pos = 0
        tiles = []
        for t in ins:
            if t.halo:
                prev = jnp.where(tile > 0, refs[pos][...].astype(F32), 0.0)
                tiles.append(jnp.concatenate([prev, refs[pos + 1][...].astype(F32)], axis=0))
                pos += 2
            else:
                tiles.append(refs[pos][...].astype(F32))
                pos += 1
        pvals = [refs[pos + k][...].astype(F32) for k in range(len(pars))]
        pos += len(pars)
        if not bwd:
            res = fn(tiles, pvals, row0)
            for o_ref, r in zip(refs[pos:], res):
                o_ref[...] = r.astype(o_ref.dtype)
            return
        ct_vals = [refs[pos + k][...].astype(F32) for k in range(n_ct)]
        pos += n_ct + len(aliases)
        _, vjp_fn = jax.vjp(lambda tt, pp: fn(tt, pp, row0), tiles, pvals)
        d_tiles, d_pars = vjp_fn(ct_vals)
        carries = list(refs[len(refs) - len(scratch):])
        for t, d, dt in zip(ins, dins, d_tiles):
            if d is None:
                continue
            o_ref = refs[pos]
            pos += 1
            if t.halo:
                carry = carries.pop(0)
                main = dt[HALO:]
                tail = main[tm - HALO:] + jnp.where(i > 0, carry[...], 0.0)
                o_ref[...] = jnp.concatenate([main[:tm - HALO], tail], axis=0).astype(o_ref.dtype)
                carry[...] = dt[:HALO]
            else:
                o_ref[...] = dt.astype(o_ref.dtype)
        for dp in d_pars:
            acc = refs[pos]
            pos += 1

            @pl.when(i == 0)
            def _(acc=acc, dp=dp):
                acc[...] = dp

            @pl.when(i > 0)
            def _(acc=acc, dp=dp):
                acc[...] += dp

    res = _pcall(body, operands, name=name, grid=(outer, nt), in_specs=in_specs, out_specs=out_specs, out_shape=out_shapes,
                 scratch_shapes=scratch, input_output_aliases=aliases, compiler_params=_cparams(("arbitrary", "arbitrary")))
    if not bwd:
        return list(res)
    n_d = len(out_descs)
    d_full, it = [], iter(res[:n_d])
    for d in dins:
        d_full.append(None if d is None else next(it))
    return d_full, list(res[n_d:])


def f_prenorm(t, p, row0):
    return [rms(t[0], p[0])]


def f_prenorm_res(t, p, row0):
    return [t[0], rms(t[0], p[0])]


def f_post_pre(t, p, row0):
    x, y = t
    x1 = x + rms(y, p[0])
    return [x1, rms(x1, p[1])]


def _causal_conv(x, w, taps):
    y = x * row_of(w, taps - 1)
    for j in range(taps - 1):
        y = y + shift_rows(x, taps - 1 - j) * row_of(w, j)
    return drop_head(y, HALO)


def _l2(x):
    return x * lax.rsqrt(jnp.sum(x * x, axis=-1, keepdims=True) + EPS)


def make_f_convhead(scale, normalise):
    def f(t, p, row0):
        y = silu(_causal_conv(t[0], p[0], 4))
        if not normalise:
            return [y]
        return [jnp.concatenate([_l2(c) * scale for c in split_lanes(y, GDN_HEADS)], axis=-1)]
    return f


def f_bg(t, p, row0):
    ba = split_lanes(t[0], 4)[0]
    alog, dtb = p
    lane = lax.broadcasted_iota(jnp.int32, ba.shape, 1)
    bg = jnp.where(lane < GDN_HEADS, sigmoid(ba), -jnp.exp(alog) * softplus(ba + dtb))
    return [jnp.where(lane < 2 * GDN_HEADS, bg, 0.0)]


def f_gnorm(t, p, row0):
    o, z = t
    po, pz = split_lanes(o, GDN_HEADS), split_lanes(z, GDN_HEADS)
    return [jnp.concatenate([rms(a, p[0]) * silu(b) for a, b in zip(po, pz)], axis=-1)]


def f_pool(t, p, row0):
    x = t[0]
    pw, psc = p
    tm = x.shape[0] - HALO
    tpos = (row0 + lax.broadcasted_iota(jnp.int32, (tm, 1), 0) + 1).astype(F32)
    outs = []
    for xg, wg, win in zip(split_lanes(x, 4), split_rows(pw, 4), POOL_WINDOWS):
        s, span = xg, 1
        while span < win:
            s = s + shift_rows(s, span)
            span *= 2
        mean = drop_head(s, HALO) / jnp.minimum(tpos, float(win))
        outs.append(bdot_nn(mean - drop_head(xg, HALO), wg))
    return [jnp.concatenate(outs, axis=-1) * psc]


def f_merge(t, p, row0):
    gates, ya, yb = t
    ga, gb = split_lanes(gates, 2)
    return [sigmoid(ga) * ya + sigmoid(gb) * yb]


def f_xattn(t, p, row0):
    k, v = p
    outs = []
    for qh, kh, vh in zip(split_lanes(t[0], XA_HEADS), split_lanes(k, XA_HEADS), split_lanes(v, XA_HEADS)):
        s = bdot_nt(qh, kh) * (XA_HEAD_DIM ** -0.5)
        s = s - jnp.max(s, axis=-1, keepdims=True)
        e = jnp.exp(s)
        outs.append(bdot_nn(e / jnp.sum(e, axis=-1, keepdims=True), vh))
    return [jnp.concatenate(outs, axis=-1)]


def f_convglu(t, p, row0):
    ua, ub = t
    cwa, cwb, ba, bb = p
    return [silu(_causal_conv(ua, cwa, 3) + ba) * (_causal_conv(ub, cwb, 3) + bb)]


_BNN = (((2,), (1,)), ((0,), (0,)))
_BNT = (((2,), (2,)), ((0,), (0,)))
_BTN = (((1,), (1,)), ((0,), (0,)))


def _make_batched_dots():
    def raw(dims, a, b):
        return lax.dot_general(a.astype(BF16), b.astype(BF16), dims, preferred_element_type=F32)

    @jax.custom_vjp
    def nn(a, b):
        return raw(_BNN, a, b)

    @jax.custom_vjp
    def nt(a, b):
        return raw(_BNT, a, b)

    @jax.custom_vjp
    def tn(a, b):
        return raw(_BTN, a, b)

    nn.defvjp(lambda a, b: (raw(_BNN, a, b), (a, b)), lambda r, g: (nt(g, r[1]), tn(r[0], g)))
    nt.defvjp(lambda a, b: (raw(_BNT, a, b), (a, b)), lambda r, g: (nn(g, r[1]), tn(g, r[0])))
    tn.defvjp(lambda a, b: (raw(_BTN, a, b), (a, b)), lambda r, g: (nt(r[1], g), nn(r[0], g)))
    return nn, nt, tn


bb_nn, bb_nt, bb_tn = _make_batched_dots()


def _inverse_correction(X):
    N, P = X, X
    for _ in range(5):
        P = bb_nn(P, P)
        N = N + P + bb_nn(N, P)
    return N


@jax.custom_vjp
def _saved_inverse_correction(X, N):
    return N


def _saved_inverse_fwd(X, N):
    return N, N


def _saved_inverse_bwd(N, G):
    t = G + bb_tn(N, G)
    return t + bb_nt(t, N), jnp.zeros_like(N)


_saved_inverse_correction.defvjp(_saved_inverse_fwd, _saved_inverse_bwd)


def _gdn_local(q, k, v, gcol, bcol, n_saved=None):
    C = CHUNK
    r = lax.broadcasted_iota(jnp.int32, (1, C, C), 1)
    c = lax.broadcasted_iota(jnp.int32, (1, C, C), 2)
    eye, incl, strict = r == c, r >= c, r > c
    grow = jnp.sum(jnp.where(eye, gcol, 0.0), axis=1, keepdims=True)
    Gcol = jnp.sum(jnp.where(incl, grow, 0.0), axis=2, keepdims=True)
    Grow = jnp.sum(jnp.where(eye, Gcol, 0.0), axis=1, keepdims=True)
    decay = jnp.where(incl, jnp.exp(jnp.where(incl, Gcol - Grow, 0.0)), 0.0)
    X = -jnp.where(strict, bcol * decay * bb_nt(k, k), 0.0)
    N = _inverse_correction(X) if n_saved is None else _saved_inverse_correction(X, n_saved)
    expg = jnp.exp(Gcol)
    rv, rk = bcol * v, (bcol * expg) * k
    u_v = rv + bb_nn(N, rv)
    w_k = rk + bb_nn(N, rk)
    attn = decay * bb_nt(q, k)
    rid = lax.broadcasted_iota(jnp.int32, (1, C, 1), 1)
    glast = jnp.sum(jnp.where(rid == C - 1, Gcol, 0.0), axis=1, keepdims=True)
    return (u_v, w_k, attn, q * expg, k * jnp.exp(glast - Gcol), jnp.exp(glast)), N


def _gdn_rec(u_v, w_k, attn, q_dec, k_dec, cd, S):
    u = u_v - bb_nn(w_k, S)
    o = bb_nn(q_dec, S) + bb_nn(attn, u)
    return o, cd * S + bb_tn(k_dec, u)


def _gdn_load(q_ref, k_ref, v_ref, bg_ref, nch):
    H = GDN_HEADS

    def batched(ref):
        return jnp.stack([ref[c * CHUNK:(c + 1) * CHUNK, h * GDN_DK:(h + 1) * GDN_DK] for c in range(nch) for h in range(H)])

    bg = bg_ref[...]
    lane = lax.broadcasted_iota(jnp.int32, bg.shape, 1)
    bcols = [jnp.sum(jnp.where(lane == h, bg, 0.0), axis=-1, keepdims=True) for h in range(H)]
    gcols = [jnp.sum(jnp.where(lane == H + h, bg, 0.0), axis=-1, keepdims=True) for h in range(H)]
    pick = lambda cols: jnp.stack([cols[h][c * CHUNK:(c + 1) * CHUNK] for c in range(nch) for h in range(H)])
    return batched(q_ref), batched(k_ref), batched(v_ref), pick(gcols), pick(bcols)


def _gdn_store(ref, val, nch):
    H = GDN_HEADS
    for c in range(nch):
        for h in range(H):
            ref[c * CHUNK:(c + 1) * CHUNK, h * GDN_DK:(h + 1) * GDN_DK] = val[c * H + h]


def gdn_forward(q, k, v, bg, span):
    T = q.shape[0]
    ns, nch, H = T // span, span // CHUNK, GDN_HEADS

    def body(q_ref, k_ref, v_ref, bg_ref, o_ref, s_ref, n_ref, state):
        @pl.when(pl.program_id(0) == 0)
        def _():
            state[...] = jnp.zeros(state.shape, F32)

        loc, n_ref[...] = _gdn_local(*_gdn_load(q_ref, k_ref, v_ref, bg_ref, nch))
        S = state[...]
        for c in range(nch):
            s_ref[c] = S
            o, S = _gdn_rec(*[a[c * H:(c + 1) * H] for a in loc], S)
            for h in range(H):
                o_ref[c * CHUNK:(c + 1) * CHUNK, h * GDN_DK:(h + 1) * GDN_DK] = o[h]
        state[...] = S

    wide = pl.BlockSpec((span, H * GDN_DK), lambda s: (s, 0))
    return _pcall(
        body, (q, k, v, bg), name="gdn_fwd", grid=(ns,),
        in_specs=[wide, wide, wide, pl.BlockSpec((span, 128), lambda s: (s, 0))],
        out_specs=[wide, pl.BlockSpec((nch, H, GDN_DK, GDN_DK), lambda s: (s, 0, 0, 0)),
                   pl.BlockSpec((nch * H, CHUNK, CHUNK), lambda s: (s, 0, 0))],
        out_shape=[jax.ShapeDtypeStruct((T, H * GDN_DK), F32), jax.ShapeDtypeStruct((T // CHUNK, H, GDN_DK, GDN_DK), F32),
                   jax.ShapeDtypeStruct((T // CHUNK * H, CHUNK, CHUNK), F32)],
        scratch_shapes=[pltpu.VMEM((H, GDN_DK, GDN_DK), F32)],
        compiler_params=_cparams(("arbitrary",)),
    )


def gdn_backward(q, k, v, bg, starts, ninv, do, span):
    T = q.shape[0]
    ns, nch, H = T // span, span // CHUNK, GDN_HEADS

    def body(q_ref, k_ref, v_ref, bg_ref, s_ref, n_ref, do_ref, dq_ref, dk_ref, dv_ref, dbg_ref, dstate):
        @pl.when(pl.program_id(0) == 0)
        def _():
            dstate[...] = jnp.zeros(dstate.shape, F32)

        (loc, _), vjp_loc = jax.vjp(_gdn_local, *_gdn_load(q_ref, k_ref, v_ref, bg_ref, nch), n_ref[...])
        dS = dstate[...]
        dloc = [None] * nch
        for c in reversed(range(nch)):
            _, vjp_rec = jax.vjp(_gdn_rec, *[a[c * H:(c + 1) * H] for a in loc], s_ref[c])
            do_c = jnp.stack([do_ref[c * CHUNK:(c + 1) * CHUNK, h * GDN_DK:(h + 1) * GDN_DK] for h in range(H)])
            *dloc[c], dS = vjp_rec((do_c, dS))
        dstate[...] = dS
        d_loc = tuple(jnp.concatenate([dloc[c][i] for c in range(nch)], axis=0) for i in range(6))
        dq, dk, dv, dg, db, _ = vjp_loc((d_loc, jnp.zeros((nch * H, CHUNK, CHUNK), F32)))
        _gdn_store(dq_ref, dq, nch)
        _gdn_store(dk_ref, dk, nch)
        _gdn_store(dv_ref, dv, nch)
        lane = lax.broadcasted_iota(jnp.int32, (CHUNK, 128), 1)
        for c in range(nch):
            acc = jnp.zeros((CHUNK, 128), F32)
            for h in range(H):
                acc = acc + jnp.where(lane == h, db[c * H + h], 0.0) + jnp.where(lane == H + h, dg[c * H + h], 0.0)
            dbg_ref[c * CHUNK:(c + 1) * CHUNK, :] = acc

    wide = pl.BlockSpec((span, H * GDN_DK), lambda i: (ns - 1 - i, 0))
    bgs = pl.BlockSpec((span, 128), lambda i: (ns - 1 - i, 0))
    return _pcall(
        body, (q, k, v, bg, starts, ninv, do), name="gdn_bwd", grid=(ns,),
        in_specs=[wide, wide, wide, bgs, pl.BlockSpec((nch, H, GDN_DK, GDN_DK), lambda i: (ns - 1 - i, 0, 0, 0)),
                  pl.BlockSpec((nch * H, CHUNK, CHUNK), lambda i: (ns - 1 - i, 0, 0)), wide],
        out_specs=[wide, wide, wide, bgs],
        out_shape=[jax.ShapeDtypeStruct((T, H * GDN_DK), F32)] * 3 + [jax.ShapeDtypeStruct((T, 128), F32)],
        scratch_shapes=[pltpu.VMEM((H, GDN_DK, GDN_DK), F32)],
        compiler_params=_cparams(("arbitrary",)),
    )


def loss_stage(x2, y3, tgt, g, tm):
    T, D = x2.shape
    nt = T // tm

    def body(x_ref, y_ref, t_ref, g_ref, loss_ref, dx_ref, dy_ref, dg_ref):
        i = pl.program_id(0)
        tgtv = t_ref[...]

        def f(x, y, gg):
            err = x + rms(y, gg) - tgtv
            return 0.5 * jnp.mean(err * err, axis=-1, keepdims=True)

        rows, vjp_fn = jax.vjp(f, x_ref[...], y_ref[...], g_ref[...])
        dx, dy, dg = vjp_fn(jnp.ones_like(rows))
        dx_ref[...] = dx
        dy_ref[...] = dy.astype(dy_ref.dtype)
        part = jnp.sum(rows, axis=0, keepdims=True)

        @pl.when(i == 0)
        def _():
            loss_ref[...] = part
            dg_ref[...] = dg

        @pl.when(i > 0)
        def _():
            loss_ref[...] += part
            dg_ref[...] += dg

    tile = pl.BlockSpec((tm, D), lambda i: (i, 0))
    gs = pl.BlockSpec((1, D), lambda i: (0, 0))
    return _pcall(
        body, (x2, y3, tgt, g), name="loss_head", grid=(nt,), in_specs=[tile, tile, tile, gs],
        out_specs=[pl.BlockSpec((1, 1), lambda i: (0, 0)), tile, tile, gs],
        out_shape=[jax.ShapeDtypeStruct((1, 1), F32), jax.ShapeDtypeStruct((T, D), F32),
                   jax.ShapeDtypeStruct((T, D), BF16), jax.ShapeDtypeStruct((1, D), F32)],
        compiler_params=_cparams(("arbitrary",)),
    )


def adamw(name, w, g, m, v):
    R, C = w.shape
    tr = R
    for cand in (256, 128, 64, 32, 16, 8):
        if R % cand == 0 and R > cand and cand * C * 4 <= (2 << 20):
            tr = cand
            break
    c1 = 1.0 / (1.0 - ADAM_B1 ** ADAM_STEP)
    c2 = 1.0 / (1.0 - ADAM_B2 ** ADAM_STEP)

    def body(w_ref, g_ref, m_ref, v_ref, d_ref, nm_ref, nv_ref):
        gg = g_ref[...]
        nm = ADAM_B1 * m_ref[...] + (1.0 - ADAM_B1) * gg
        nv = ADAM_B2 * v_ref[...] + (1.0 - ADAM_B2) * (gg * gg)
        d_ref[...] = -ADAM_LR * ((nm * c1) / (jnp.sqrt(nv * c2) + ADAM_EPS) + ADAM_WD * w_ref[...])
        nm_ref[...] = nm
        nv_ref[...] = nv

    spec = pl.BlockSpec((tr, C), lambda i: (i, 0))
    return _pcall(
        body, (w, g, m, v), name=name, grid=(R // tr,), in_specs=[spec] * 4, out_specs=[spec] * 3,
        out_shape=[jax.ShapeDtypeStruct((R, C), F32)] * 3, compiler_params=_cparams(("parallel",)),
    )


C_Z, C_GATES, C_POOL, C_BA, N_PROJ = 6144, 8192, 12288, 13312, 13824
FF_BLK = 2816
FF_COLS = 2752


def local_step(x, mem, tgt, W, sp, hook=lambda event, gw: None):
    T, D = x.shape
    tm = 256
    tmm = min(512, T)
    nI = T // tmm
    S = jax.ShapeDtypeStruct
    gw, gs = {}, {}

    def stage(name, fn, ins, pars, outs, **kw):
        return run_stage(name, fn, kw.pop("tm", tm), ins, pars, outs, **kw)

    def dense(name, a, w, out_dtype=F32, tn=2048):
        Tq, Kd = a.shape
        N = w.shape[1]
        tq = min(tmm, Tq)
        return mm(name, _NN, [(a, w, _bs((tq, Kd), lambda j, i, k: (i, 0)), _bs((Kd, tn), lambda j, i, k: (0, j)))],
                  S((Tq, N), out_dtype), _bs((tq, tn), lambda j, i, k: (i, j)), (N // tn, Tq // tq, 1))

    def dense_t(name, g, w, out_dtype=F32, tn=2048):
        Tq, N = g.shape
        Kd = w.shape[0]
        tq = min(tmm, Tq)
        return mm(name, _NT, [(g, w, _bs((tq, N), lambda j, i, k: (i, 0)), _bs((tn, N), lambda j, i, k: (j, 0)))],
                  S((Tq, Kd), out_dtype), _bs((tq, tn), lambda j, i, k: (i, j)), (Kd // tn, Tq // tq, 1))

    def wgrad(name, a, g, ta=1024, tn=2048):
        Tq, Kd = a.shape
        N = g.shape[1]
        tt = min(1024, Tq)
        return mm(name, _TN, [(a, g, _bs((tt, ta), lambda i, j, k: (k, i)), _bs((tt, tn), lambda i, j, k: (k, j)))],
                  S((Kd, N), F32), _bs((ta, tn), lambda i, j, k: (i, j)), (Kd // ta, N // tn, Tq // tt))

    o2048 = lambda dt: Out((T, D), dt, D)

    (h1,) = stage("pre1", f_prenorm, [Tile(x, D)], [Par(sp["mix_pre_norm"])], [o2048(BF16)])
    tnp = 1536
    P2 = mm("in_proj", _NT, [(h1, W["w_in"], _bs((tmm, D), lambda j, i, k: (i, 0)), _bs((tnp, D), lambda j, i, k: (j, 0)))],
            S((T, N_PROJ), F32), _bs((tmm, tnp), lambda j, i, k: (i, j)), (N_PROJ // tnp, nI, 1))
    cw = sp["conv_qkv"]
    cws = [cw[:, i * D:(i + 1) * D] for i in range(3)]
    f_heads = [make_f_convhead(GDN_DK ** -0.5, True), make_f_convhead(1.0, True), make_f_convhead(1.0, False)]
    qkv = [stage("conv_" + n, f_heads[i], [Tile(P2, D, cb=i, halo=True)], [Par(cws[i])], [o2048(F32)])[0]
           for i, n in enumerate("qkv")]
    ba_tile = Tile(P2, 512, cb=C_BA // 512)
    (bg,) = stage("bg", f_bg, [ba_tile], [Par(sp["a_log"]), Par(sp["dt_bias"])], [Out((T, 128), F32, 128)])
    o, s0, ninv = gdn_forward(qkv[0], qkv[1], qkv[2], bg, GDN_SPAN)
    z_tile = Tile(P2, D, cb=C_Z // D)
    (o_n,) = stage("gnorm", f_gnorm, [Tile(o, D), z_tile], [Par(sp["gdn_norm"])], [o2048(BF16)])
    y_a = dense("branch_a", o_n, W["w_branch_a"])
    p_tile = Tile(P2, 1024, cb=C_POOL // 1024, halo=True)
    pool_pars = [Par(sp["pool_w"]), Par(sp["pool_scale"])]
    (pooled,) = stage("pool", f_pool, [p_tile], pool_pars, [Out((T, 1024), BF16, 1024)])
    y_b = mm("branch_b", _NN, [(pooled, W["w_branch_b"], _bs((tmm, 1024), lambda j, i, k: (i, 0)),
                                _bs((None, 1024, 512), lambda j, i, k: (j, 0, 0)))],
             S((T, D), F32), _bs((tmm, 512), lambda j, i, k: (i, j)), (4, nI, 1))
    gate_tile = Tile(P2, 2 * D, cb=C_GATES // (2 * D))
    merge_ins = [gate_tile, Tile(y_a, D), Tile(y_b, D)]
    (merged,) = stage("merge", f_merge, merge_ins, [], [o2048(BF16)])
    y1 = dense("mix_out", merged, W["w_mix_out"])
    pp1 = [Par(sp["mix_post_norm"]), Par(sp["xa_pre_norm"])]
    x1, h2 = stage("post1", f_post_pre, [Tile(x, D), Tile(y1, D)], pp1, [o2048(F32), o2048(BF16)])

    q2 = dense("xq", h2, W["w_xq"], out_dtype=BF16)
    (mn,) = stage("mem_norm", f_prenorm, [Tile(mem, D)], [Par(sp["mem_norm"])], [Out(mem.shape, BF16, D)], tm=mem.shape[0])
    M = mem.shape[0]
    kv = mm("xkv", _NN, [(mn, W["w_xkv"], _bs((M, D), lambda j, i, k: (0, 0)), _bs((None, D, 1024), lambda j, i, k: (j, 0, 0)))],
            S((M, 2 * D), F32), _bs((M, 1024), lambda j, i, k: (0, j)), (4, 1, 1))
    k2, v2 = kv[:, :D], kv[:, D:]
    xa_pars = [Par(k2), Par(v2)]
    (o2,) = stage("xattn", f_xattn, [Tile(q2, D)], xa_pars, [o2048(BF16)])
    y2 = dense("xo", o2, W["w_xo"])
    pp2 = [Par(sp["xa_post_norm"]), Par(sp["ffn_pre_norm"])]
    x2, h3 = stage("post2", f_post_pre, [Tile(x1, D), Tile(y2, D)], pp2, [o2048(F32), o2048(BF16)])

    def up(name, off):
        return mm(name, _NN, [(h3, W["w_up"], _bs((tmm, D), lambda j, i, k: (i, 0)),
                               _bs((None, D, FF_BLK), lambda j, i, k: (j + off, 0, 0)))],
                  S((2, T, FF_BLK), F32), _bs((None, tmm, FF_BLK), lambda j, i, k: (j, i, 0)), (2, nI, 1))

    Ua, Ub = up("up_a", 0), up("up_b", 2)
    ffn_ins = [Tile(Ua, FF_BLK, lead=lambda o: o, halo=True), Tile(Ub, FF_BLK, lead=lambda o: o, halo=True)]
    ffn_pars = [Par(sp["ffn_conv_w"], lead=lambda o: o), Par(sp["ffn_conv_w"], lead=lambda o: o + 2),
                Par(sp["ffn_conv_b"], lead=lambda o: o), Par(sp["ffn_conv_b"], lead=lambda o: o + 2)]
    ffn_out = [Out((2, T, FF_BLK), BF16, FF_BLK, lead=lambda o: o)]
    (ff,) = stage("convglu", f_convglu, ffn_ins, ffn_pars, ffn_out, outer=2)
    y3 = mm("down", _NN, [(ff, W["w_down"], _bs((None, tmm, FF_BLK), lambda i, j, k: (k, i, 0)),
                           _bs((None, FF_BLK, D), lambda i, j, k: (k, 0, 0)))],
            S((T, D), F32), _bs((tmm, D), lambda i, j, k: (i, 0)), (nI, 1, 2))
    loss, dx2, dy3, gs["ffn_post_norm"] = loss_stage(x2, y3, tgt, sp["ffn_post_norm"], tm)

    dff = mm("down_dx", _NT, [(dy3, W["w_down"], _bs((tmm, D), lambda j, i, k: (i, 0)),
                               _bs((None, FF_BLK, D), lambda j, i, k: (j, 0, 0)))],
             S((2, T, FF_BLK), BF16), _bs((None, tmm, FF_BLK), lambda j, i, k: (j, i, 0)), (2, nI, 1))
    tbig = min(1024, T)
    gw["w_down"] = mm("down_dw", _TN, [(ff, dy3, _bs((None, tbig, FF_BLK), lambda b, j, k: (b, k, 0)),
                                        _bs((tbig, 1024), lambda b, j, k: (k, j)))],
                      S((2, FF_BLK, D), F32), _bs((None, FF_BLK, 1024), lambda b, j, k: (b, 0, j)), (2, D // 1024, T // tbig))
    dU_out = [Out((2, T, FF_BLK), BF16, FF_BLK, lead=lambda o: o), Out((2, T, FF_BLK), BF16, FF_BLK, lead=lambda o: o)]
    (dUa, dUb), dffn = stage("convglu_bwd", f_convglu, ffn_ins, ffn_pars, ffn_out, outer=2, cts=[dff], dins=dU_out)
    gs["ffn_conv_w"] = jnp.concatenate([dffn[0][:2], dffn[1][2:]], axis=0)
    gs["ffn_conv_b"] = jnp.concatenate([dffn[2][:2], dffn[3][2:]], axis=0)
    dh3 = mm("up_dx", _NT, [(dUa, W["w_up"], _bs((None, tmm, FF_BLK), lambda i, j, k: (k, i, 0)),
                             _bs((None, 1024, FF_BLK), lambda i, j, k: (k, j, 0))),
                            (dUb, W["w_up"], _bs((None, tmm, FF_BLK), lambda i, j, k: (k, i, 0)),
                             _bs((None, 1024, FF_BLK), lambda i, j, k: (k + 2, j, 0)))],
             S((T, D), F32), _bs((tmm, 1024), lambda i, j, k: (i, j)), (nI, D // 1024, 2))

    def up_dw(name, dU):
        tt = min(2048, T)
        return mm(name, _TN, [(h3, dU, _bs((tt, 512), lambda b, i, k: (k, i)), _bs((None, tt, FF_BLK), lambda b, i, k: (b, k, 0)))],
                  S((2, D, FF_BLK), F32), _bs((None, 512, FF_BLK), lambda b, i, k: (b, i, 0)), (2, D // 512, T // tt))

    gw["w_up_a"], gw["w_up_b"] = up_dw("up_dw_a", dUa), up_dw("up_dw_b", dUb)
    hook("ffn", gw)
    (dx1, dy2), dpp2 = stage("post2_bwd", f_post_pre, [Tile(x1, D), Tile(y2, D)], pp2, [o2048(F32), o2048(BF16)],
                             cts=[dx2, dh3], dins=[o2048(F32), o2048(BF16)])
    gs["xa_post_norm"], gs["ffn_pre_norm"] = dpp2

    do2 = dense_t("xo_dx", dy2, W["w_xo"])
    gw["w_xo"] = wgrad("xo_dw", o2, dy2)
    (dq2,), (dk2, dv2) = stage("xattn_bwd", f_xattn, [Tile(q2, D)], xa_pars, [o2048(BF16)], cts=[do2], dins=[o2048(BF16)])
    dh2 = dense_t("xq_dx", dq2, W["w_xq"])
    gw["w_xq"] = wgrad("xq_dw", h2, dq2)
    dkv = jnp.concatenate([dk2, dv2], axis=1).astype(BF16)
    dmn = mm("xkv_dx", _NT, [(dkv, W["w_xkv"], _bs((M, 1024), lambda i, j, k: (0, k)), _bs((None, 512, 1024), lambda i, j, k: (k, j, 0)))],
             S((M, D), F32), _bs((M, 512), lambda i, j, k: (0, j)), (1, D // 512, 4))
    gw["w_xkv"] = mm("xkv_dw", _TN, [(mn, dkv, _bs((M, D), lambda b, j, k: (0, 0)), _bs((M, 1024), lambda b, j, k: (0, b)))],
                     S((4, D, 1024), F32), _bs((None, D, 1024), lambda b, j, k: (b, 0, 0)), (4, 1, 1))
    hook("xattn", gw)
    _, (gs["mem_norm"],) = stage("mem_norm_bwd", f_prenorm, [Tile(mem, D)], [Par(sp["mem_norm"])], [Out(mem.shape, BF16, D)],
                                 tm=M, cts=[dmn], dins=[None])
    (dx0, dy1), dpp1 = stage("post1_bwd", f_post_pre, [Tile(x, D), Tile(y1, D)], pp1, [o2048(F32), o2048(BF16)],
                             cts=[dx1, dh2], dins=[o2048(F32), o2048(BF16)])
    gs["mix_post_norm"], gs["xa_pre_norm"] = dpp1

    dmerged = dense_t("mix_out_dx", dy1, W["w_mix_out"])
    gw["w_mix_out"] = wgrad("mix_out_dw", merged, dy1)
    pshape = (T, N_PROJ)
    (dP2, dya, dyb), _ = stage("merge_bwd", f_merge, merge_ins, [], [o2048(BF16)], cts=[dmerged],
                               dins=[Out(pshape, BF16, 2 * D, cb=C_GATES // (2 * D)), o2048(BF16), o2048(BF16)])
    d_on = dense_t("branch_a_dx", dya, W["w_branch_a"])
    gw["w_branch_a"] = wgrad("branch_a_dw", o_n, dya)
    dpooled = mm("branch_b_dx", _NT, [(dyb, W["w_branch_b"], _bs((tmm, 512), lambda i, j, k: (i, k)),
                                       _bs((None, 1024, 512), lambda i, j, k: (k, 0, 0)))],
                 S((T, 1024), F32), _bs((tmm, 1024), lambda i, j, k: (i, 0)), (nI, 1, 4))
    gw["w_branch_b"] = mm("branch_b_dw", _TN, [(pooled, dyb, _bs((tmm, 1024), lambda b, j, k: (k, 0)), _bs((tmm, 512), lambda b, j, k: (k, b)))],
                          S((4, 1024, 512), F32), _bs((None, 1024, 512), lambda b, j, k: (b, 0, 0)), (4, 1, nI))
    hook("mixer", gw)
    (do, dP2), (gs["gdn_norm"],) = stage("gnorm_bwd", f_gnorm, [Tile(o, D), z_tile], [Par(sp["gdn_norm"])], [o2048(BF16)],
                                         cts=[d_on], dins=[o2048(F32), Out(pshape, BF16, D, cb=C_Z // D, into=dP2)])
    dq, dk, dv, dbg = gdn_backward(qkv[0], qkv[1], qkv[2], bg, s0, ninv, do, GDN_SPAN_BWD)
    dcw = []
    for i, (n, dqq) in enumerate(zip("qkv", (dq, dk, dv))):
        (dP2,), (dc,) = stage("conv_%s_bwd" % n, f_heads[i], [Tile(P2, D, cb=i, halo=True)], [Par(cws[i])], [o2048(F32)],
                              cts=[dqq], dins=[Out(pshape, BF16, D, cb=i, into=dP2)])
        dcw.append(dc)
    gs["conv_qkv"] = jnp.concatenate(dcw, axis=1)
    (dP2,), (gs["a_log"], gs["dt_bias"]) = stage("bg_bwd", f_bg, [ba_tile], [Par(sp["a_log"]), Par(sp["dt_bias"])],
                                                 [Out((T, 128), F32, 128)], cts=[dbg],
                                                 dins=[Out(pshape, BF16, 512, cb=C_BA // 512, into=dP2)])
    (dP2,), (gs["pool_w"], gs["pool_scale"]) = stage("pool_bwd", f_pool, [p_tile], pool_pars, [Out((T, 1024), BF16, 1024)],
                                                     cts=[dpooled], dins=[Out(pshape, BF16, 1024, cb=C_POOL // 1024, into=dP2)])
    tk, ta, tt = 2304, 1152, min(2048, T)
    gw["w_in"] = mm("in_proj_dw", _TN, [(dP2, h1, _bs((tt, ta), lambda i, j, k: (k, i)), _bs((tt, D), lambda i, j, k: (k, 0)))],
                    S((N_PROJ, D), F32), _bs((ta, D), lambda i, j, k: (i, 0)), (N_PROJ // ta, 1, T // tt))
    hook("in_proj", gw)
    dh1 = mm("in_proj_dx", _NN, [(dP2, W["w_in"], _bs((tbig, tk), lambda i, j, k: (i, k)), _bs((tk, D), lambda i, j, k: (k, 0)))],
             S((T, D), F32), _bs((tbig, D), lambda i, j, k: (i, 0)), (T // tbig, 1, N_PROJ // tk))
    (grad_x,), (gs["mix_pre_norm"],) = stage("pre1_bwd", f_prenorm_res, [Tile(x, D)], [Par(sp["mix_pre_norm"])],
                                             [o2048(F32), o2048(BF16)], cts=[dx0, dh1], dins=[o2048(F32)])
    return loss, grad_x, gw, gs


W_IN_COLS = 13344
GROUPED = {"w_branch_b": 512, "w_xkv": 1024}
ROW_SHARDED = ("w_branch_a", "w_mix_out", "w_xq", "w_xo")


def shard_to_slab(name, w):
    if name == "w_in":
        return w.T.astype(BF16)
    if name == "w_up":
        return jnp.pad(w, ((0, 0), (0, FF_BLK - FF_COLS))).astype(BF16)
    return w.astype(BF16)


def slabs_to_weight(name, g):
    if name == "w_in":
        full = g.astype(F32).reshape(W_IN_COLS, D_MODEL)
        pad = jnp.zeros((N_PROJ - W_IN_COLS, D_MODEL), F32)
        return jnp.concatenate([full[0:8192], full[9248:13344], full[8224:9248], full[8192:8224], pad]).astype(BF16)
    if name == "w_down":
        z = jnp.zeros((2, FF_BLK - FF_COLS, D_MODEL), g.dtype)
        return jnp.concatenate([g.reshape(2, FF_COLS, D_MODEL), z], axis=1)
    if name in ROW_SHARDED:
        return g.reshape(D_MODEL, D_MODEL)
    return g


def grad_to_slabs(name, gw):
    if name == "w_in":
        g = gw["w_in"]
        return jnp.concatenate([g[0:8192], g[13312:13344], g[12288:13312], g[8192:12288]]).reshape(4, 3336, D_MODEL)
    if name == "w_up":
        return jnp.concatenate([gw["w_up_a"], gw["w_up_b"]], axis=0)
    if name == "w_down":
        return gw["w_down"][:, :FF_COLS].reshape(4, FF_COLS // 2, D_MODEL)
    if name in ROW_SHARDED:
        return gw[name].reshape(4, D_MODEL // 4, D_MODEL)
    return gw[name]


def slab_to_shard_grad(name, f):
    if name == "w_in":
        return f.T
    if name == "w_up":
        return f[:, :FF_COLS]
    return f


MESH = pl.DeviceIdType.MESH
ANY = pl.BlockSpec(memory_space=pl.ANY)


def _me():
    x, y, c = lax.axis_index("x"), lax.axis_index("y"), lax.axis_index("c")
    return x, y, c, 2 * x + y


def _chip_dev(t, c):
    return (t // 2, t % 2, c)


def _rcopy(src, dst, ssem, rsem, dev):
    return pltpu.make_async_remote_copy(src_ref=src, dst_ref=dst, send_sem=ssem, recv_sem=rsem, device_id=dev, device_id_type=MESH)


def _handshake_all(x, y, c):
    barrier = pltpu.get_barrier_semaphore()
    for dx in (0, 1):
        for dy in (0, 1):
            for dc in (0, 1):
                if dx or dy or dc:
                    pl.semaphore_signal(barrier, inc=1, device_id=((x + dx) % 2, (y + dy) % 2, (c + dc) % 2), device_id_type=MESH)
    pl.semaphore_wait(barrier, 7)


def _comm_call(body, name, operands, out_shape, sems, collective_id):
    if collective_id is not None:
        return pl.kernel(body, out_type=out_shape, mesh=plsc.ScalarSubcoreMesh(axis_name="seq", num_cores=1), name=name,
                         scratch_types=sems, compiler_params=pltpu.CompilerParams(collective_id=collective_id))(*operands)
    n_in, n_out = len(operands), len(out_shape)
    return pl.pallas_call(body, name=name, in_specs=[ANY] * n_in, out_specs=[ANY] * n_out, out_shape=out_shape,
                          scratch_shapes=sems)(*operands)


def gather_weights(slabs, name="gather_weights", collective_id=None):
    n = len(slabs)

    def body(*refs):
        src, dst = refs[:n], refs[n:2 * n]
        ici_s, ici_r, fwd_s, fwd_r = refs[2 * n:]
        x, y, c, s = _me()
        if collective_id is not None:
            _handshake_all(x, y, c)
        sender = c == s // 2
        sib = (x, y, 1 - c)
        for r in (1, 2, 3):
            @pl.when(sender)
            def _(r=r):
                for w in range(n):
                    _rcopy(src[w], dst[w].at[s], ici_s.at[w, r - 1], ici_r.at[w, r - 1], _chip_dev(s ^ r, c)).start()
        for r in (1, 2, 3):
            t = s ^ r
            here = c == t // 2

            @pl.when(here)
            def _(r=r, t=t):
                for w in range(n):
                    _rcopy(src[w], dst[w].at[t], ici_s.at[w, r - 1], ici_r.at[w, r - 1], sib).wait_recv()
                    _rcopy(dst[w].at[t], dst[w].at[t], fwd_s.at[w, r - 1], fwd_r.at[w, r - 1], sib).start()

            @pl.when(jnp.logical_not(here))
            def _(r=r, t=t):
                for w in range(n):
                    _rcopy(dst[w].at[t], dst[w].at[t], fwd_s.at[w, r - 1], fwd_r.at[w, r - 1], sib).wait_recv()
        for r in (1, 2, 3):
            t = s ^ r

            @pl.when(sender)
            def _(r=r):
                for w in range(n):
                    _rcopy(src[w], dst[w].at[s], ici_s.at[w, r - 1], ici_r.at[w, r - 1], sib).wait_send()

            @pl.when(c == t // 2)
            def _(r=r, t=t):
                for w in range(n):
                    _rcopy(dst[w].at[t], dst[w].at[t], fwd_s.at[w, r - 1], fwd_r.at[w, r - 1], sib).wait_send()

    out_shape = [jax.ShapeDtypeStruct((4,) + a.shape, a.dtype) for a in slabs]
    sems = [pltpu.SemaphoreType.DMA((n, 3))] * 4
    return _comm_call(body, name, slabs, out_shape, sems, collective_id)


def pair_exchange(g4, name="pair_exchange", collective_id=None):
    n = len(g4)

    def body(*refs):
        src, dst = refs[:n], refs[n:2 * n]
        ssem, rsem = refs[2 * n:]
        x, y, c, s = _me()
        if collective_id is not None:
            _handshake_all(x, y, c)
        cps = [_rcopy(src[w].at[pl.ds(2 * (1 - c), 2)], dst[w], ssem.at[w], rsem.at[w], (x, y, 1 - c)) for w in range(n)]
        for cp in cps:
            cp.start()
        for cp in cps:
            cp.wait()

    return _comm_call(body, name, g4, [jax.ShapeDtypeStruct((2,) + a.shape[1:], a.dtype) for a in g4],
                      [pltpu.SemaphoreType.DMA((n,))] * 2, collective_id)


def _col_tile(R, C):
    for tc in (512, 256, 128):
        if C % tc == 0 and R * tc * 4 <= (4 << 20):
            return tc
    return 128


def pair_add(name, g4, gsib, c):
    _, R, C = g4.shape
    tc = _col_tile(R, C)

    def body(c_ref, a_ref, b_ref, of_ref, ob_ref):
        v = a_ref[...] + b_ref[...]
        of_ref[...] = v
        ob_ref[...] = v.astype(BF16)

    blk = lambda f: pl.BlockSpec((None, R, tc), f)
    return _pcall(
        body, (c.reshape(1).astype(jnp.int32), g4, gsib), name=name, num_scalar_prefetch=1, grid=(2, C // tc),
        in_specs=[blk(lambda p, j, cr: (2 * cr[0] + p, 0, j)), blk(lambda p, j, cr: (p, 0, j))],
        out_specs=[blk(lambda p, j, cr: (p, 0, j)), blk(lambda p, j, cr: (p, 0, j))],
        out_shape=[jax.ShapeDtypeStruct((2, R, C), F32), jax.ShapeDtypeStruct((2, R, C), BF16)],
        compiler_params=_cparams(("arbitrary", "arbitrary")),
    )


def scatter_partials(rb, name="scatter_partials", collective_id=None):
    n = len(rb)

    def body(*refs):
        src, dst = refs[:n], refs[n:2 * n]
        ssem, rsem = refs[2 * n:]
        x, y, c, s = _me()
        if collective_id is not None:
            _handshake_all(x, y, c)
        for r in (1, 2, 3):
            t = s ^ r

            @pl.when(t // 2 == c)
            def _(r=r, t=t):
                for w in range(n):
                    _rcopy(src[w].at[t % 2], dst[w].at[s], ssem.at[w, r - 1], rsem.at[w, r - 1], _chip_dev(t, c)).start()
        for r in (1, 2, 3):
            t = s ^ r

            @pl.when(s // 2 == c)
            def _(r=r, t=t):
                for w in range(n):
                    _rcopy(src[w].at[0], dst[w].at[t], ssem.at[w, r - 1], rsem.at[w, r - 1], _chip_dev(t, c)).wait_recv()
        for r in (1, 2, 3):
            t = s ^ r

            @pl.when(t // 2 == c)
            def _(r=r, t=t):
                for w in range(n):
                    _rcopy(src[w].at[t % 2], dst[w].at[s], ssem.at[w, r - 1], rsem.at[w, r - 1], _chip_dev(t, c)).wait_send()

    return _comm_call(body, name, rb, [jax.ShapeDtypeStruct((4,) + a.shape[1:], a.dtype) for a in rb],
                      [pltpu.SemaphoreType.DMA((n, 3))] * 2, collective_id)


def final_sum(name, rf, recv, s):
    _, R, C = rf.shape
    tc = _col_tile(R, C)

    def body(s_ref, own_ref, r0_ref, r1_ref, r2_ref, o_ref):
        o_ref[...] = ((own_ref[...] + r0_ref[...].astype(F32)) + r1_ref[...].astype(F32)) + r2_ref[...].astype(F32)

    blk = lambda f: pl.BlockSpec((None, R, tc), f)
    other = lambda k: (lambda j, sr: (k + (k >= sr[0]).astype(jnp.int32), 0, j))
    return _pcall(
        body, (s.reshape(1).astype(jnp.int32), rf, recv, recv, recv), name=name, num_scalar_prefetch=1, grid=(C // tc,),
        in_specs=[blk(lambda j, sr: (sr[0] % 2, 0, j)), blk(other(0)), blk(other(1)), blk(other(2))],
        out_specs=pl.BlockSpec((R, tc), lambda j, sr: (0, j)), out_shape=jax.ShapeDtypeStruct((R, C), F32),
        compiler_params=_cparams(("arbitrary",)),
    )


def share_with_sibling(fs, name="share_with_sibling", collective_id=None):
    n = len(fs)

    def body(*refs):
        src, dst = refs[:n], refs[n:2 * n]
        ssem, rsem = refs[2 * n:]
        x, y, c, s = _me()
        if collective_id is not None:
            _handshake_all(x, y, c)
        sib = (x, y, 1 - c)

        @pl.when(s // 2 == c)
        def _():
            cps = [_rcopy(src[w], dst[w], ssem.at[w], rsem.at[w], sib) for w in range(n)]
            for cp in cps:
                cp.start()
            for cp in cps:
                cp.wait_send()

        @pl.when(s // 2 != c)
        def _():
            for w in range(n):
                _rcopy(src[w], dst[w], ssem.at[w], rsem.at[w], sib).wait_recv()

    return _comm_call(body, name, fs, [jax.ShapeDtypeStruct(a.shape, a.dtype) for a in fs],
                      [pltpu.SemaphoreType.DMA((n,))] * 2, collective_id)


def pick(name, mine, a, b):
    R, C = a.shape
    tc = _col_tile(R, C)

    def body(m_ref, a_ref, b_ref, o_ref):
        o_ref[...] = jnp.where(m_ref[0] != 0, a_ref[...], b_ref[...])

    blk = pl.BlockSpec((R, tc), lambda j, mr: (0, j))
    return _pcall(body, (mine.reshape(1).astype(jnp.int32), a, b), name=name, num_scalar_prefetch=1, grid=(C // tc,),
                  in_specs=[blk, blk], out_specs=blk, out_shape=jax.ShapeDtypeStruct((R, C), a.dtype),
                  compiler_params=_cparams(("arbitrary",)))


def allgather_rows(v):
    m_per, ncol = v.shape

    def body(x_ref, out_ref, send_sems, recv_sems, local_sem):
        x, y, c = lax.axis_index("x"), lax.axis_index("y"), lax.axis_index("c")
        me, sibling = (x, y, c), (x, y, 1 - c)
        chips = [(1 - x, y), (x, 1 - y), (1 - x, 1 - y)]

        def rows(px, py, pc):
            return out_ref.at[pl.ds((4 * px + 2 * py + pc) * m_per, m_per), :]

        def copy(k, block, to, src=None):
            return _rcopy(rows(*block) if src is None else src, rows(*block), send_sems.at[k], recv_sems.at[k], to)

        mine = pltpu.make_async_copy(x_ref, rows(*me), local_sem)
        mine.start()
        first = [copy(0, me, sibling, src=x_ref)]
        first += [copy(1 + j, me, (*chip, c), src=x_ref) for j, chip in enumerate(chips)]
        for cp in first:
            cp.start()
        passed = [copy(4 + j, (*chip, c), sibling) for j, chip in enumerate(chips)]
        for j, chip in enumerate(chips):
            copy(1 + j, (*chip, c), me).wait_recv()
            passed[j].start()
        copy(0, sibling, me).wait_recv()
        for j, chip in enumerate(chips):
            copy(4 + j, (*chip, 1 - c), me).wait_recv()
        for cp in first + passed:
            cp.wait_send()
        mine.wait()

    return pl.pallas_call(
        body, name="allgather_rows", out_shape=jax.ShapeDtypeStruct((8 * m_per, ncol), v.dtype),
        in_specs=[pl.BlockSpec(memory_space=pltpu.VMEM)], out_specs=pl.BlockSpec(memory_space=pltpu.VMEM),
        scratch_shapes=[pltpu.SemaphoreType.DMA((7,)), pltpu.SemaphoreType.DMA((7,)), pltpu.SemaphoreType.DMA],
        compiler_params=pltpu.CompilerParams(vmem_limit_bytes=V7X_VMEM_LIMIT),
    )(v)


def sum_blocks(name, a, nblk):
    m = a.shape[0] // nblk

    def body(a_ref, o_ref):
        acc = a_ref[pl.ds(0, m), :]
        for b in range(1, nblk):
            acc = acc + a_ref[pl.ds(b * m, m), :]
        o_ref[...] = acc

    return pl.pallas_call(body, name=name, out_shape=jax.ShapeDtypeStruct((m, a.shape[1]), a.dtype),
                          compiler_params=pltpu.CompilerParams(vmem_limit_bytes=V7X_VMEM_LIMIT))(a)


BIG = ("w_in", "w_branch_a", "w_branch_b", "w_mix_out", "w_xq", "w_xkv", "w_xo", "w_up", "w_down")
GAINS = ("mix_pre_norm", "gdn_norm", "pool_scale", "mix_post_norm", "xa_pre_norm", "mem_norm", "xa_post_norm",
         "ffn_pre_norm", "ffn_post_norm")
WEIGHTS = ("mix_pre_norm", "w_in", "conv_qkv", "a_log", "dt_bias", "gdn_norm", "pool_w", "pool_scale", "w_branch_a",
           "w_branch_b", "w_mix_out", "mix_post_norm", "xa_pre_norm", "mem_norm", "w_xq", "w_xkv", "w_xo", "xa_post_norm",
           "ffn_pre_norm", "w_up", "ffn_conv_w", "ffn_conv_b", "w_down", "ffn_post_norm")


RS_GROUPS = {"ffn": (("w_up", "w_down"), (3, 4, 5)), "xattn": (("w_xo", "w_xq", "w_xkv"), (6, 7, 8)),
             "mixer": (("w_mix_out", "w_branch_a", "w_branch_b"), (9, 10, 11)), "in_proj": (("w_in",), (12, 13, 14))}


def _rows128(vecs):
    flat = jnp.concatenate([v.reshape(-1) for v in vecs])
    m = -(-flat.shape[0] // 1024) * 8
    return jnp.pad(flat, (0, m * 128 - flat.shape[0])).reshape(m, 128)


def _unrows(a, shapes):
    flat, out, pos = a.reshape(-1), [], 0
    for sh in shapes:
        n = 1
        for d in sh:
            n *= d
        out.append(flat[pos:pos + n].reshape(sh))
        pos += n
    return out


def _lane128(v):
    return jnp.pad(v.reshape(1, GDN_HEADS), ((0, 0), (GDN_HEADS, 128 - 2 * GDN_HEADS)))


def kernel(x, mem, mix_pre_norm, w_in, conv_qkv, a_log, dt_bias, gdn_norm, pool_w, pool_scale, w_branch_a, w_branch_b, w_mix_out, mix_post_norm, xa_pre_norm, mem_norm, w_xq, w_xkv, w_xo, xa_post_norm, ffn_pre_norm, w_up, ffn_conv_w, ffn_conv_b, w_down, ffn_post_norm, loss_target, m_mix_pre_norm, m_w_in, m_conv_qkv, m_a_log, m_dt_bias, m_gdn_norm, m_pool_w, m_pool_scale, m_w_branch_a, m_w_branch_b, m_w_mix_out, m_mix_post_norm, m_xa_pre_norm, m_mem_norm, m_w_xq, m_w_xkv, m_w_xo, m_xa_post_norm, m_ffn_pre_norm, m_w_up, m_ffn_conv_w, m_ffn_conv_b, m_w_down, m_ffn_post_norm, v_mix_pre_norm, v_w_in, v_conv_qkv, v_a_log, v_dt_bias, v_gdn_norm, v_pool_w, v_pool_scale, v_w_branch_a, v_w_branch_b, v_w_mix_out, v_mix_post_norm, v_xa_pre_norm, v_mem_norm, v_w_xq, v_w_xkv, v_w_xo, v_xa_post_norm, v_ffn_pre_norm, v_w_up, v_ffn_conv_w, v_ffn_conv_b, v_w_down, v_ffn_post_norm):
    given = dict(locals())
    _LAST[0] = None
    w = {n: given[n][0] for n in WEIGHTS}
    cx, cy, cc = lax.axis_index("x"), lax.axis_index("y"), lax.axis_index("c")
    chip = 2 * cx + cy

    slabs = [shard_to_slab(n, w[n]) for n in BIG]
    first = gather_weights(slabs[:1], name="gather_w_in", collective_id=1)
    rest = gather_weights(slabs[1:], name="gather_rest", collective_id=2)
    stacks = [lax.dynamic_update_index_in_dim(g, sl, chip, 0) for g, sl in zip(list(first) + list(rest), slabs)]
    W = {n: slabs_to_weight(n, g) for n, g in zip(BIG, stacks)}
    sharded_small = (w["conv_qkv"], w["ffn_conv_w"], w["pool_w"])
    allv = allgather_rows(_rows128(sharded_small))
    per_chip = allv.reshape(8, -1)[0::2]
    parts = [_unrows(per_chip[t], [a.shape for a in sharded_small]) for t in range(4)]
    sp = {n: w[n].reshape(1, -1) for n in GAINS}
    sp["a_log"], sp["dt_bias"] = _lane128(w["a_log"]), _lane128(w["dt_bias"])
    sp["conv_qkv"] = jnp.concatenate([p[0] for p in parts], axis=1)
    sp["ffn_conv_w"] = jnp.pad(jnp.stack([p[1] for p in parts]), ((0, 0), (0, 0), (0, FF_BLK - FF_COLS)))
    sp["pool_w"] = jnp.concatenate([p[2] for p in parts], axis=1).reshape(4 * 256, 256)
    sp["ffn_conv_b"] = jnp.pad(w["ffn_conv_b"].reshape(4, 1, FF_COLS), ((0, 0), (0, 0), (0, FF_BLK - FF_COLS)))

    grads, todo = {}, []

    def rs_steps(names, ids, gw):
        g4 = [grad_to_slabs(n, gw) for n in names]
        tag = names[0]
        gsib = pair_exchange(g4, "pair_exchange_" + tag, ids[0])
        yield
        sums = [pair_add("pair_add_" + n, a, b, cc) for n, a, b in zip(names, g4, gsib)]
        recv = scatter_partials([sb for _, sb in sums], "scatter_partials_" + tag, ids[1])
        yield
        fin = [final_sum("final_sum_" + n, sf, rv, chip) for n, (sf, _), rv in zip(names, sums, recv)]
        got = share_with_sibling(fin, "share_" + tag, ids[2])
        yield
        for n, f, g in zip(names, fin, got):
            grads[n] = slab_to_shard_grad(n, pick("pick_" + n, (chip // 2 == cc), f, g))

    def advance():
        for it in list(todo):
            if next(it, "done") == "done":
                todo.remove(it)

    def hook(event, gw):
        new = rs_steps(*RS_GROUPS[event], gw)
        next(new)
        advance()
        todo.append(new)
        if event == "in_proj":
            next(new)

    loss, grad_x, gw, gs = local_step(x[0], mem[0], loss_target[0], W, sp, hook)

    small_names = GAINS + ("a_log", "dt_bias", "ffn_conv_b", "conv_qkv", "ffn_conv_w", "pool_w")
    vec = _rows128([gs[n] for n in small_names])
    total = sum_blocks("sum_small", allgather_rows(vec), 8)
    advance()
    loss = lax.psum(loss[0, 0], ("x", "y", "c"))
    tot = dict(zip(small_names, _unrows(total, [gs[n].shape for n in small_names])))
    for n in GAINS:
        grads[n] = tot[n].reshape(-1)
    grads["a_log"] = tot["a_log"][0, GDN_HEADS:2 * GDN_HEADS]
    grads["dt_bias"] = tot["dt_bias"][0, GDN_HEADS:2 * GDN_HEADS]
    grads["ffn_conv_b"] = tot["ffn_conv_b"][:, 0, :FF_COLS].reshape(-1)
    grads["conv_qkv"] = lax.dynamic_slice_in_dim(tot["conv_qkv"], chip * 1536, 1536, axis=1)
    grads["ffn_conv_w"] = lax.dynamic_index_in_dim(tot["ffn_conv_w"], chip, axis=0, keepdims=False)[:, :FF_COLS]
    grads["pool_w"] = lax.dynamic_slice_in_dim(tot["pool_w"].reshape(4, 256, 256), chip * 64, 64, axis=1)

    delta, new_m, new_v = {}, {}, {}

    def update(names):
        for n in names:
            shp = w[n].shape
            two = (lambda a: a.reshape(-1, shp[-1])) if len(shp) > 1 else (lambda a: a.reshape(1, -1))
            d, nm, nv = adamw("adamw_" + n, two(w[n]), two(grads[n]), two(given["m_" + n][0]), two(given["v_" + n][0]))
            delta[n], new_m[n], new_v[n] = (a.reshape((1,) + shp) for a in (d, nm, nv))

    update([n for n in WEIGHTS if n in grads])
    while todo:
        advance()
    update([n for n in WEIGHTS if n not in delta])
    out_g = [grads[n].reshape((1,) + w[n].shape) for n in WEIGHTS]
    return (loss, grad_x[None], *out_g, *[delta[n] for n in WEIGHTS], *[new_m[n] for n in WEIGHTS], *[new_v[n] for n in WEIGHTS])
```

```python
import functools

import jax
import jax.numpy as jnp
from jax import lax
from jax.experimental import pallas as pl
from jax.experimental.pallas import tpu as pltpu
from jax.experimental.pallas import tpu_sc as plsc

F32 = jnp.float32
BF16 = jnp.bfloat16

D_MODEL = 2048
CHUNK = 64
GDN_HEADS = 16
GDN_DK = 128
POOL_WINDOWS = (2, 4, 8, 16)
XA_HEADS = 4
XA_HEAD_DIM = D_MODEL // XA_HEADS
EPS = 1e-6
GDN_SPAN = 256
GDN_SPAN_BWD = 128
HALO = 16
V7X_VMEM_LIMIT = 56 * 1024 * 1024

ADAM_LR, ADAM_B1, ADAM_B2, ADAM_EPS, ADAM_WD, ADAM_STEP = 0.001, 0.9, 0.999, 1e-08, 0.01, 10

_NN = ((1,), (0,))
_NT = ((1,), (1,))
_TN = ((0,), (0,))


def _cparams(sem):
    return pltpu.CompilerParams(dimension_semantics=sem, vmem_limit_bytes=V7X_VMEM_LIMIT)


_LAST = [None]


def _pcall(body, operands, *, in_specs, out_specs, grid=(), num_scalar_prefetch=0, scratch_shapes=(), **kw):
    operands, in_specs = list(operands), list(in_specs)
    if _LAST[0] is not None and not any(o is _LAST[0] for o in operands):
        n = len(operands)
        inner = body

        def body(*refs):
            return inner(*refs[:n], *refs[n + 1:])

        operands.append(_LAST[0])
        in_specs.append(pl.BlockSpec(memory_space=pl.ANY))
    if num_scalar_prefetch:
        kw["grid_spec"] = pltpu.PrefetchScalarGridSpec(num_scalar_prefetch=num_scalar_prefetch, grid=grid, in_specs=in_specs,
                                                       out_specs=out_specs, scratch_shapes=list(scratch_shapes))
    else:
        kw.update(grid=grid, in_specs=in_specs, out_specs=out_specs, scratch_shapes=list(scratch_shapes))
    res = pl.pallas_call(body, **kw)(*operands)
    _LAST[0] = res[0] if isinstance(res, (list, tuple)) else res
    return res


def _dg(a, b, dims, prec=None):
    return lax.dot_general(a, b, (dims, ((), ())), precision=prec, preferred_element_type=F32)


def _make_dots(cast, prec):
    def raw(dims, a, b):
        return _dg(cast(a), cast(b), dims, prec)

    @jax.custom_vjp
    def nn(a, b):
        return raw(_NN, a, b)

    @jax.custom_vjp
    def nt(a, b):
        return raw(_NT, a, b)

    @jax.custom_vjp
    def tn(a, b):
        return raw(_TN, a, b)

    nn.defvjp(lambda a, b: (raw(_NN, a, b), (a, b)), lambda r, g: (nt(g, r[1]), tn(r[0], g)))
    nt.defvjp(lambda a, b: (raw(_NT, a, b), (a, b)), lambda r, g: (nn(g, r[1]), tn(g, r[0])))
    tn.defvjp(lambda a, b: (raw(_TN, a, b), (a, b)), lambda r, g: (nt(r[1], g), nn(r[0], g)))
    return nn, nt, tn


bdot_nn, bdot_nt, bdot_tn = _make_dots(lambda x: x.astype(BF16), None)
hdot_nn, hdot_nt, hdot_tn = _make_dots(lambda x: x, lax.Precision.HIGHEST)


@functools.partial(jax.custom_vjp, nondiff_argnums=(1,))
def shift_rows(x, k):
    return pltpu.roll(x, k, 0)


def _shift_rows_fwd(x, k):
    return pltpu.roll(x, k, 0), None


def _shift_rows_bwd(k, _, g):
    return (pltpu.roll(g, g.shape[0] - k, 0),)


shift_rows.defvjp(_shift_rows_fwd, _shift_rows_bwd)


@functools.partial(jax.custom_vjp, nondiff_argnums=(1,))
def drop_head(x, h):
    return x[h:]


def _drop_head_fwd(x, h):
    return x[h:], None


def _drop_head_bwd(h, _, g):
    return (jnp.concatenate([jnp.zeros((h,) + g.shape[1:], g.dtype), g], axis=0),)


drop_head.defvjp(_drop_head_fwd, _drop_head_bwd)


@functools.partial(jax.custom_vjp, nondiff_argnums=(1,))
def split_lanes(x, n):
    w = x.shape[-1] // n
    return tuple(x[:, i * w:(i + 1) * w] for i in range(n))


def _split_lanes_fwd(x, n):
    return split_lanes(x, n), None


def _split_lanes_bwd(n, _, gs):
    return (jnp.concatenate(list(gs), axis=-1),)


split_lanes.defvjp(_split_lanes_fwd, _split_lanes_bwd)


@functools.partial(jax.custom_vjp, nondiff_argnums=(1,))
def split_rows(x, n):
    h = x.shape[0] // n
    return tuple(x[i * h:(i + 1) * h] for i in range(n))


def _split_rows_fwd(x, n):
    return split_rows(x, n), None


def _split_rows_bwd(n, _, gs):
    return (jnp.concatenate(list(gs), axis=0),)


split_rows.defvjp(_split_rows_fwd, _split_rows_bwd)


def row_of(w, j):
    rid = lax.broadcasted_iota(jnp.int32, w.shape, 0)
    return jnp.sum(jnp.where(rid == j, w, 0.0), axis=0, keepdims=True)


def sigmoid(x):
    return 0.5 * jnp.tanh(0.5 * x) + 0.5


def silu(x):
    return x * sigmoid(x)


def softplus(x):
    return jnp.maximum(x, 0.0) + jnp.log(1.0 + jnp.exp(-jnp.abs(x)))


def rms(x, g):
    return x * lax.rsqrt(jnp.mean(x * x, axis=-1, keepdims=True) + EPS) * g


def mm(name, dims, pairs, out_shape, out_spec, grid):
    nk = grid[2]
    acc_in_out = nk > 1 and out_shape.dtype == F32
    npair = len(pairs)

    def body(*refs):
        o_ref = refs[2 * npair]
        part = None
        for p in range(npair):
            d = _dg(refs[2 * p][...].astype(BF16), refs[2 * p + 1][...].astype(BF16), dims)
            part = d if part is None else part + d
        if nk == 1:
            o_ref[...] = part.astype(o_ref.dtype)
            return
        acc = o_ref if acc_in_out else refs[2 * npair + 1]
        k = pl.program_id(2)

        @pl.when(k == 0)
        def _():
            acc[...] = part

        @pl.when(k > 0)
        def _():
            acc[...] += part

        if not acc_in_out:
            @pl.when(k == nk - 1)
            def _():
                o_ref[...] = acc[...].astype(o_ref.dtype)

    scratch = []
    if nk > 1 and not acc_in_out:
        scratch = [pltpu.VMEM(tuple(d for d in out_spec.block_shape if d is not None), F32)]
    in_specs, operands = [], []
    for a, b, a_spec, b_spec in pairs:
        in_specs += [a_spec, b_spec]
        operands += [a, b]
    return _pcall(body, operands, name=name, grid=grid, in_specs=in_specs, out_specs=out_spec, out_shape=out_shape,
                  scratch_shapes=scratch, compiler_params=_cparams(("parallel", "parallel", "arbitrary")))


def _bs(shape, fn):
    return pl.BlockSpec(shape, fn)


class Tile:
    def __init__(self, arr, w, cb=0, lead=None, halo=False):
        self.arr, self.w, self.cb, self.lead, self.halo = arr, w, cb, lead, halo


class Out:
    def __init__(self, shape, dtype, w, cb=0, lead=None, into=None):
        self.shape, self.dtype, self.w, self.cb, self.lead, self.into = shape, dtype, w, cb, lead, into


class Par:
    def __init__(self, arr, lead=None):
        self.arr, self.lead = arr, lead


def _spec(rows, w, cb, lead, tile_of):
    if lead is None:
        return pl.BlockSpec((rows, w), lambda o, i: (tile_of(i), cb))
    return pl.BlockSpec((None, rows, w), lambda o, i: (lead(o), tile_of(i), cb))


def _par_spec(p):
    if p.lead is None:
        return pl.BlockSpec(p.arr.shape, lambda o, i: (0, 0))
    return pl.BlockSpec((None,) + p.arr.shape[1:], lambda o, i: (p.lead(o), 0, 0))


def run_stage(name, fn, tm, ins, pars, outs, *, outer=1, cts=None, dins=None):
    T = ins[0].arr.shape[-2]
    nt = T // tm
    bwd = cts is not None
    any_halo = any(t.halo for t in ins)
    step_tile = (lambda i: nt - 1 - i) if bwd else (lambda i: i)
    hb = tm // HALO

    in_specs, operands = [], []
    for t in ins:
        if t.halo:
            in_specs.append(_spec(HALO, t.w, t.cb, t.lead, lambda i: jnp.maximum(step_tile(i) * hb - 1, 0)))
            operands.append(t.arr)
        in_specs.append(_spec(tm, t.w, t.cb, t.lead, step_tile))
        operands.append(t.arr)
    for p in pars:
        in_specs.append(_par_spec(p))
        operands.append(p.arr)
    n_in_refs = len(operands)

    out_descs = list(outs) if not bwd else [d for d in dins if d is not None]
    aliases = {}
    if bwd:
        for c, o in zip(cts, outs):
            in_specs.append(_spec(tm, o.w, o.cb, o.lead, step_tile))
            operands.append(c)
    n_ct = len(operands) - n_in_refs
    for k, o in enumerate(out_descs):
        if o.into is not None:
            aliases[len(operands)] = k
            in_specs.append(pl.BlockSpec(memory_space=pl.ANY))
            operands.append(o.into)
    out_specs = [_spec(tm, o.w, o.cb, o.lead, step_tile) for o in out_descs]
    out_shapes = [jax.ShapeDtypeStruct(o.shape, o.dtype) for o in out_descs]
    if bwd:
        for p in pars:
            out_specs.append(_par_spec(p))
            out_shapes.append(jax.ShapeDtypeStruct(p.arr.shape, F32))
    scratch = []
    if bwd and any_halo:
        scratch = [pltpu.VMEM((HALO, t.w), F32) for t, d in zip(ins, dins) if t.halo and d is not None]

    def body(*refs):
        i = pl.program_id(1)
        tile = step_tile(i)
        row0 = tile * tm
        pos = 0
        tiles = []
        for t in ins:
            if t.halo:
                prev = jnp.where(tile > 0, refs[pos][...].astype(F32), 0.0)
                tiles.append(jnp.concatenate([prev, refs[pos + 1][...].astype(F32)], axis=0))
                pos += 2
            else:
                tiles.append(refs[pos][...].astype(F32))
                pos += 1
        pvals = [refs[pos + k][...].astype(F32) for k in range(len(pars))]
        pos += len(pars)
        if not bwd:
            res = fn(tiles, pvals, row0)
            for o_ref, r in zip(refs[pos:], res):
                o_ref[...] = r.astype(o_ref.dtype)
            return
        ct_vals = [refs[pos + k][...].astype(F32) for k in range(n_ct)]
        pos += n_ct + len(aliases)
        _, vjp_fn = jax.vjp(lambda tt, pp: fn(tt, pp, row0), tiles, pvals)
        d_tiles, d_pars = vjp_fn(ct_vals)
        carries = list(refs[len(refs) - len(scratch):])
        for t, d, dt in zip(ins, dins, d_tiles):
            if d is None:
                continue
            o_ref = refs[pos]
            pos += 1
            if t.halo:
                carry = carries.pop(0)
                main = dt[HALO:]
                tail = main[tm - HALO:] + jnp.where(i > 0, carry[...], 0.0)
                o_ref[...] = jnp.concatenate([main[:tm - HALO], tail], axis=0).astype(o_ref.dtype)
                carry[...] = dt[:HALO]
            else:
                o_ref[...] = dt.astype(o_ref.dtype)
        for dp in d_pars:
            acc = refs[pos]
            pos += 1

            @pl.when(i == 0)
            def _(acc=acc, dp=dp):
                acc[...] = dp

            @pl.when(i > 0)
            def _(acc=acc, dp=dp):
                acc[...] += dp

    res = _pcall(body, operands, name=name, grid=(outer, nt), in_specs=in_specs, out_specs=out_specs, out_shape=out_shapes,
                 scratch_shapes=scratch, input_output_aliases=aliases, compiler_params=_cparams(("arbitrary", "arbitrary")))
    if not bwd:
        return list(res)
    n_d = len(out_descs)
    d_full, it = [], iter(res[:n_d])
    for d in dins:
        d_full.append(None if d is None else next(it))
    return d_full, list(res[n_d:])


def f_prenorm(t, p, row0):
    return [rms(t[0], p[0])]


def f_prenorm_res(t, p, row0):
    return [t[0], rms(t[0], p[0])]


def f_post_pre(t, p, row0):
    x, y = t
    x1 = x + rms(y, p[0])
    return [x1, rms(x1, p[1])]


def _causal_conv(x, w, taps):
    y = x * row_of(w, taps - 1)
    for j in range(taps - 1):
        y = y + shift_rows(x, taps - 1 - j) * row_of(w, j)
    return drop_head(y, HALO)


def _l2(x):
    return x * lax.rsqrt(jnp.sum(x * x, axis=-1, keepdims=True) + EPS)


def make_f_convhead(scale, normalise):
    def f(t, p, row0):
        y = silu(_causal_conv(t[0], p[0], 4))
        if not normalise:
            return [y]
        return [jnp.concatenate([_l2(c) * scale for c in split_lanes(y, GDN_HEADS)], axis=-1)]
    return f


def f_bg(t, p, row0):
    ba = split_lanes(t[0], 4)[0]
    alog, dtb = p
    lane = lax.broadcasted_iota(jnp.int32, ba.shape, 1)
    bg = jnp.where(lane < GDN_HEADS, sigmoid(ba), -jnp.exp(alog) * softplus(ba + dtb))
    return [jnp.where(lane < 2 * GDN_HEADS, bg, 0.0)]


def f_gnorm(t, p, row0):
    o, z = t
    po, pz = split_lanes(o, GDN_HEADS), split_lanes(z, GDN_HEADS)
    return [jnp.concatenate([rms(a, p[0]) * silu(b) for a, b in zip(po, pz)], axis=-1)]


def f_pool(t, p, row0):
    x = t[0]
    pw, psc = p
    tm = x.shape[0] - HALO
    tpos = (row0 + lax.broadcasted_iota(jnp.int32, (tm, 1), 0) + 1).astype(F32)
    outs = []
    for xg, wg, win in zip(split_lanes(x, 4), split_rows(pw, 4), POOL_WINDOWS):
        s, span = xg, 1
        while span < win:
            s = s + shift_rows(s, span)
            span *= 2
        mean = drop_head(s, HALO) / jnp.minimum(tpos, float(win))
        outs.append(bdot_nn(mean - drop_head(xg, HALO), wg))
    return [jnp.concatenate(outs, axis=-1) * psc]


def f_merge(t, p, row0):
    gates, ya, yb = t
    ga, gb = split_lanes(gates, 2)
    return [sigmoid(ga) * ya + sigmoid(gb) * yb]


def f_xattn(t, p, row0):
    k, v = p
    outs = []
    for qh, kh, vh in zip(split_lanes(t[0], XA_HEADS), split_lanes(k, XA_HEADS), split_lanes(v, XA_HEADS)):
        s = bdot_nt(qh, kh) * (XA_HEAD_DIM ** -0.5)
        s = s - jnp.max(s, axis=-1, keepdims=True)
        e = jnp.exp(s)
        outs.append(bdot_nn(e / jnp.sum(e, axis=-1, keepdims=True), vh))
    return [jnp.concatenate(outs, axis=-1)]


def f_convglu(t, p, row0):
    ua, ub = t
    cwa, cwb, ba, bb = p
    return [silu(_causal_conv(ua, cwa, 3) + ba) * (_causal_conv(ub, cwb, 3) + bb)]


_BNN = (((2,), (1,)), ((0,), (0,)))
_BNT = (((2,), (2,)), ((0,), (0,)))
_BTN = (((1,), (1,)), ((0,), (0,)))


def _make_batched_dots():
    def raw(dims, a, b):
        return lax.dot_general(a.astype(BF16), b.astype(BF16), dims, preferred_element_type=F32)

    @jax.custom_vjp
    def nn(a, b):
        return raw(_BNN, a, b)

    @jax.custom_vjp
    def nt(a, b):
        return raw(_BNT, a, b)

    @jax.custom_vjp
    def tn(a, b):
        return raw(_BTN, a, b)

    nn.defvjp(lambda a, b: (raw(_BNN, a, b), (a, b)), lambda r, g: (nt(g, r[1]), tn(r[0], g)))
    nt.defvjp(lambda a, b: (raw(_BNT, a, b), (a, b)), lambda r, g: (nn(g, r[1]), tn(g, r[0])))
    tn.defvjp(lambda a, b: (raw(_BTN, a, b), (a, b)), lambda r, g: (nt(r[1], g), nn(r[0], g)))
    return nn, nt, tn


bb_nn, bb_nt, bb_tn = _make_batched_dots()


def _inverse_correction(X):
    N, P = X, X
    for _ in range(5):
        P = bb_nn(P, P)
        N = N + P + bb_nn(N, P)
    return N


@jax.custom_vjp
def _saved_inverse_correction(X, N):
    return N


def _saved_inverse_fwd(X, N):
    return N, N


def _saved_inverse_bwd(N, G):
    t = G + bb_tn(N, G)
    return t + bb_nt(t, N), jnp.zeros_like(N)


_saved_inverse_correction.defvjp(_saved_inverse_fwd, _saved_inverse_bwd)


def _gdn_local(q, k, v, gcol, bcol, n_saved=None):
    C = CHUNK
    r = lax.broadcasted_iota(jnp.int32, (1, C, C), 1)
    c = lax.broadcasted_iota(jnp.int32, (1, C, C), 2)
    eye, incl, strict = r == c, r >= c, r > c
    grow = jnp.sum(jnp.where(eye, gcol, 0.0), axis=1, keepdims=True)
    Gcol = jnp.sum(jnp.where(incl, grow, 0.0), axis=2, keepdims=True)
    Grow = jnp.sum(jnp.where(eye, Gcol, 0.0), axis=1, keepdims=True)
    decay = jnp.where(incl, jnp.exp(jnp.where(incl, Gcol - Grow, 0.0)), 0.0)
    X = -jnp.where(strict, bcol * decay * bb_nt(k, k), 0.0)
    N = _inverse_correction(X) if n_saved is None else _saved_inverse_correction(X, n_saved)
    expg = jnp.exp(Gcol)
    rv, rk = bcol * v, (bcol * expg) * k
    u_v = rv + bb_nn(N, rv)
    w_k = rk + bb_nn(N, rk)
    attn = decay * bb_nt(q, k)
    rid = lax.broadcasted_iota(jnp.int32, (1, C, 1), 1)
    glast = jnp.sum(jnp.where(rid == C - 1, Gcol, 0.0), axis=1, keepdims=True)
    return (u_v, w_k, attn, q * expg, k * jnp.exp(glast - Gcol), jnp.exp(glast)), N


def _gdn_rec(u_v, w_k, attn, q_dec, k_dec, cd, S):
    u = u_v - bb_nn(w_k, S)
    o = bb_nn(q_dec, S) + bb_nn(attn, u)
    return o, cd * S + bb_tn(k_dec, u)


def _gdn_load(q_ref, k_ref, v_ref, bg_ref, nch):
    H = GDN_HEADS

    def batched(ref):
        return jnp.stack([ref[c * CHUNK:(c + 1) * CHUNK, h * GDN_DK:(h + 1) * GDN_DK] for c in range(nch) for h in range(H)])

    bg = bg_ref[...]
    lane = lax.broadcasted_iota(jnp.int32, bg.shape, 1)
    bcols = [jnp.sum(jnp.where(lane == h, bg, 0.0), axis=-1, keepdims=True) for h in range(H)]
    gcols = [jnp.sum(jnp.where(lane == H + h, bg, 0.0), axis=-1, keepdims=True) for h in range(H)]
    pick = lambda cols: jnp.stack([cols[h][c * CHUNK:(c + 1) * CHUNK] for c in range(nch) for h in range(H)])
    return batched(q_ref), batched(k_ref), batched(v_ref), pick(gcols), pick(bcols)


def _gdn_store(ref, val, nch):
    H = GDN_HEADS
    for c in range(nch):
        for h in range(H):
            ref[c * CHUNK:(c + 1) * CHUNK, h * GDN_DK:(h + 1) * GDN_DK] = val[c * H + h]


def gdn_forward(q, k, v, bg, span):
    T = q.shape[0]
    ns, nch, H = T // span, span // CHUNK, GDN_HEADS

    def body(q_ref, k_ref, v_ref, bg_ref, o_ref, s_ref, n_ref, state):
        @pl.when(pl.program_id(0) == 0)
        def _():
            state[...] = jnp.zeros(state.shape, F32)

        loc, n_ref[...] = _gdn_local(*_gdn_load(q_ref, k_ref, v_ref, bg_ref, nch))
        S = state[...]
        for c in range(nch):
            s_ref[c] = S
            o, S = _gdn_rec(*[a[c * H:(c + 1) * H] for a in loc], S)
            for h in range(H):
                o_ref[c * CHUNK:(c + 1) * CHUNK, h * GDN_DK:(h + 1) * GDN_DK] = o[h]
        state[...] = S

    wide = pl.BlockSpec((span, H * GDN_DK), lambda s: (s, 0))
    return _pcall(
        body, (q, k, v, bg), name="gdn_fwd", grid=(ns,),
        in_specs=[wide, wide, wide, pl.BlockSpec((span, 128), lambda s: (s, 0))],
        out_specs=[wide, pl.BlockSpec((nch, H, GDN_DK, GDN_DK), lambda s: (s, 0, 0, 0)),
                   pl.BlockSpec((nch * H, CHUNK, CHUNK), lambda s: (s, 0, 0))],
        out_shape=[jax.ShapeDtypeStruct((T, H * GDN_DK), F32), jax.ShapeDtypeStruct((T // CHUNK, H, GDN_DK, GDN_DK), F32),
                   jax.ShapeDtypeStruct((T // CHUNK * H, CHUNK, CHUNK), F32)],
        scratch_shapes=[pltpu.VMEM((H, GDN_DK, GDN_DK), F32)],
        compiler_params=_cparams(("arbitrary",)),
    )


def gdn_backward(q, k, v, bg, starts, ninv, do, span):
    T = q.shape[0]
    ns, nch, H = T // span, span // CHUNK, GDN_HEADS

    def body(q_ref, k_ref, v_ref, bg_ref, s_ref, n_ref, do_ref, dq_ref, dk_ref, dv_ref, dbg_ref, dstate):
        @pl.when(pl.program_id(0) == 0)
        def _():
            dstate[...] = jnp.zeros(dstate.shape, F32)

        (loc, _), vjp_loc = jax.vjp(_gdn_local, *_gdn_load(q_ref, k_ref, v_ref, bg_ref, nch), n_ref[...])
        dS = dstate[...]
        dloc = [None] * nch
        for c in reversed(range(nch)):
            _, vjp_rec = jax.vjp(_gdn_rec, *[a[c * H:(c + 1) * H] for a in loc], s_ref[c])
            do_c = jnp.stack([do_ref[c * CHUNK:(c + 1) * CHUNK, h * GDN_DK:(h + 1) * GDN_DK] for h in range(H)])
            *dloc[c], dS = vjp_rec((do_c, dS))
        dstate[...] = dS
        d_loc = tuple(jnp.concatenate([dloc[c][i] for c in range(nch)], axis=0) for i in range(6))
        dq, dk, dv, dg, db, _ = vjp_loc((d_loc, jnp.zeros((nch * H, CHUNK, CHUNK), F32)))
        _gdn_store(dq_ref, dq, nch)
        _gdn_store(dk_ref, dk, nch)
        _gdn_store(dv_ref, dv, nch)
        lane = lax.broadcasted_iota(jnp.int32, (CHUNK, 128), 1)
        for c in range(nch):
            acc = jnp.zeros((CHUNK, 128), F32)
            for h in range(H):
                acc = acc + jnp.where(lane == h, db[c * H + h], 0.0) + jnp.where(lane == H + h, dg[c * H + h], 0.0)
            dbg_ref[c * CHUNK:(c + 1) * CHUNK, :] = acc

    wide = pl.BlockSpec((span, H * GDN_DK), lambda i: (ns - 1 - i, 0))
    bgs = pl.BlockSpec((span, 128), lambda i: (ns - 1 - i, 0))
    return _pcall(
        body, (q, k, v, bg, starts, ninv, do), name="gdn_bwd", grid=(ns,),
        in_specs=[wide, wide, wide, bgs, pl.BlockSpec((nch, H, GDN_DK, GDN_DK), lambda i: (ns - 1 - i, 0, 0, 0)),
                  pl.BlockSpec((nch * H, CHUNK, CHUNK), lambda i: (ns - 1 - i, 0, 0)), wide],
        out_specs=[wide, wide, wide, bgs],
        out_shape=[jax.ShapeDtypeStruct((T, H * GDN_DK), F32)] * 3 + [jax.ShapeDtypeStruct((T, 128), F32)],
        scratch_shapes=[pltpu.VMEM((H, GDN_DK, GDN_DK), F32)],
        compiler_params=_cparams(("arbitrary",)),
    )


def loss_stage(x2, y3, tgt, g, tm):
    T, D = x2.shape
    nt = T // tm

    def body(x_ref, y_ref, t_ref, g_ref, loss_ref, dx_ref, dy_ref, dg_ref):
        i = pl.program_id(0)
        tgtv = t_ref[...]

        def f(x, y, gg):
            err = x + rms(y, gg) - tgtv
            return 0.5 * jnp.mean(err * err, axis=-1, keepdims=True)

        rows, vjp_fn = jax.vjp(f, x_ref[...], y_ref[...], g_ref[...])
        dx, dy, dg = vjp_fn(jnp.ones_like(rows))
        dx_ref[...] = dx
        dy_ref[...] = dy.astype(dy_ref.dtype)
        part = jnp.sum(rows, axis=0, keepdims=True)

        @pl.when(i == 0)
        def _():
            loss_ref[...] = part
            dg_ref[...] = dg

        @pl.when(i > 0)
        def _():
            loss_ref[...] += part
            dg_ref[...] += dg

    tile = pl.BlockSpec((tm, D), lambda i: (i, 0))
    gs = pl.BlockSpec((1, D), lambda i: (0, 0))
    return _pcall(
        body, (x2, y3, tgt, g), name="loss_head", grid=(nt,), in_specs=[tile, tile, tile, gs],
        out_specs=[pl.BlockSpec((1, 1), lambda i: (0, 0)), tile, tile, gs],
        out_shape=[jax.ShapeDtypeStruct((1, 1), F32), jax.ShapeDtypeStruct((T, D), F32),
                   jax.ShapeDtypeStruct((T, D), BF16), jax.ShapeDtypeStruct((1, D), F32)],
        compiler_params=_cparams(("arbitrary",)),
    )


def adamw(name, w, g, m, v):
    R, C = g.shape
    layered = w.ndim == 3
    tr = R
    for cand in (256, 128, 64, 32, 16, 8):
        if R % cand == 0 and R > cand and cand * C * 4 <= (2 << 20):
            tr = cand
            break
    c1 = 1.0 / (1.0 - ADAM_B1 ** ADAM_STEP)
    c2 = 1.0 / (1.0 - ADAM_B2 ** ADAM_STEP)

    def body(w_ref, g_ref, m_ref, v_ref, d_ref, nm_ref, nv_ref):
        gg = g_ref[...]
        nm = ADAM_B1 * m_ref[...] + (1.0 - ADAM_B1) * gg
        nv = ADAM_B2 * v_ref[...] + (1.0 - ADAM_B2) * (gg * gg)
        d_ref[...] = -ADAM_LR * ((nm * c1) / (jnp.sqrt(nv * c2) + ADAM_EPS) + ADAM_WD * w_ref[...])
        nm_ref[...] = nm
        nv_ref[...] = nv

    gspec = pl.BlockSpec((tr, C), lambda i: (i, 0))
    spec = pl.BlockSpec((None, tr, C), lambda i: (0, i, 0)) if layered else gspec
    steps = R // tr
    if tr < 64 and R > 512 and C % 128 == 0 and not layered:
        gspec = spec = pl.BlockSpec((R, 128), lambda i: (0, i))
        steps = C // 128
    return _pcall(
        body, (w, g, m, v), name=name, grid=(steps,), in_specs=[spec, gspec, spec, spec], out_specs=[spec] * 3,
        out_shape=[jax.ShapeDtypeStruct(w.shape, F32)] * 3, compiler_params=_cparams(("parallel",)),
    )


C_Z, C_GATES, C_POOL, C_BA, N_PROJ = 6144, 8192, 12288, 13312, 13824
FF_BLK = 2816
FF_COLS = 2752


def local_step(x, mem, tgt, W, sp, hook=lambda event, gw: None):
    T, D = x.shape
    tm = 256
    tmm = min(512, T)
    nI = T // tmm
    S = jax.ShapeDtypeStruct
    gw, gs = {}, {}

    def stage(name, fn, ins, pars, outs, **kw):
        return run_stage(name, fn, kw.pop("tm", tm), ins, pars, outs, **kw)

    def dense(name, a, w, out_dtype=F32, tn=2048):
        Tq, Kd = a.shape
        N = w.shape[1]
        tq = min(tmm, Tq)
        return mm(name, _NN, [(a, w, _bs((tq, Kd), lambda j, i, k: (i, 0)), _bs((Kd, tn), lambda j, i, k: (0, j)))],
                  S((Tq, N), out_dtype), _bs((tq, tn), lambda j, i, k: (i, j)), (N // tn, Tq // tq, 1))

    def dense_t(name, g, w, out_dtype=F32, tn=2048):
        Tq, N = g.shape
        Kd = w.shape[0]
        tq = min(tmm, Tq)
        return mm(name, _NT, [(g, w, _bs((tq, N), lambda j, i, k: (i, 0)), _bs((tn, N), lambda j, i, k: (j, 0)))],
                  S((Tq, Kd), out_dtype), _bs((tq, tn), lambda j, i, k: (i, j)), (Kd // tn, Tq // tq, 1))

    def wgrad(name, a, g, ta=1024, tn=2048):
        Tq, Kd = a.shape
        N = g.shape[1]
        tt = min(1024, Tq)
        return mm(name, _TN, [(a, g, _bs((tt, ta), lambda i, j, k: (k, i)), _bs((tt, tn), lambda i, j, k: (k, j)))],
                  S((Kd, N), F32), _bs((ta, tn), lambda i, j, k: (i, j)), (Kd // ta, N // tn, Tq // tt))

    o2048 = lambda dt: Out((T, D), dt, D)

    (h1,) = stage("pre1", f_prenorm, [Tile(x, D)], [Par(sp["mix_pre_norm"])], [o2048(BF16)])
    tnp = 1536
    P2 = mm("in_proj", _NT, [(h1, W["w_in"], _bs((tmm, D), lambda j, i, k: (i, 0)), _bs((tnp, D), lambda j, i, k: (j, 0)))],
            S((T, N_PROJ), F32), _bs((tmm, tnp), lambda j, i, k: (i, j)), (N_PROJ // tnp, nI, 1))
    cw = sp["conv_qkv"]
    cws = [cw[:, i * D:(i + 1) * D] for i in range(3)]
    f_heads = [make_f_convhead(GDN_DK ** -0.5, True), make_f_convhead(1.0, True), make_f_convhead(1.0, False)]
    qkv = [stage("conv_" + n, f_heads[i], [Tile(P2, D, cb=i, halo=True)], [Par(cws[i])], [o2048(F32)])[0]
           for i, n in enumerate("qkv")]
    ba_tile = Tile(P2, 512, cb=C_BA // 512)
    (bg,) = stage("bg", f_bg, [ba_tile], [Par(sp["a_log"]), Par(sp["dt_bias"])], [Out((T, 128), F32, 128)])
    o, s0, ninv = gdn_forward(qkv[0], qkv[1], qkv[2], bg, GDN_SPAN)
    z_tile = Tile(P2, D, cb=C_Z // D)
    (o_n,) = stage("gnorm", f_gnorm, [Tile(o, D), z_tile], [Par(sp["gdn_norm"])], [o2048(BF16)])
    y_a = dense("branch_a", o_n, W["w_branch_a"])
    p_tile = Tile(P2, 1024, cb=C_POOL // 1024, halo=True)
    pool_pars = [Par(sp["pool_w"]), Par(sp["pool_scale"])]
    (pooled,) = stage("pool", f_pool, [p_tile], pool_pars, [Out((T, 1024), BF16, 1024)])
    y_b = mm("branch_b", _NN, [(pooled, W["w_branch_b"], _bs((tmm, 1024), lambda j, i, k: (i, 0)),
                                _bs((None, 1024, 512), lambda j, i, k: (j, 0, 0)))],
             S((T, D), F32), _bs((tmm, 512), lambda j, i, k: (i, j)), (4, nI, 1))
    gate_tile = Tile(P2, 2 * D, cb=C_GATES // (2 * D))
    merge_ins = [gate_tile, Tile(y_a, D), Tile(y_b, D)]
    (merged,) = stage("merge", f_merge, merge_ins, [], [o2048(BF16)])
    y1 = dense("mix_out", merged, W["w_mix_out"])
    pp1 = [Par(sp["mix_post_norm"]), Par(sp["xa_pre_norm"])]
    x1, h2 = stage("post1", f_post_pre, [Tile(x, D), Tile(y1, D)], pp1, [o2048(F32), o2048(BF16)])

    q2 = dense("xq", h2, W["w_xq"], out_dtype=BF16)
    (mn,) = stage("mem_norm", f_prenorm, [Tile(mem, D)], [Par(sp["mem_norm"])], [Out(mem.shape, BF16, D)], tm=mem.shape[0])
    M = mem.shape[0]
    kv = mm("xkv", _NN, [(mn, W["w_xkv"], _bs((M, D), lambda j, i, k: (0, 0)), _bs((None, D, 1024), lambda j, i, k: (j, 0, 0)))],
            S((M, 2 * D), F32), _bs((M, 1024), lambda j, i, k: (0, j)), (4, 1, 1))
    k2, v2 = kv[:, :D], kv[:, D:]
    xa_pars = [Par(k2), Par(v2)]
    (o2,) = stage("xattn", f_xattn, [Tile(q2, D)], xa_pars, [o2048(BF16)])
    y2 = dense("xo", o2, W["w_xo"])
    pp2 = [Par(sp["xa_post_norm"]), Par(sp["ffn_pre_norm"])]
    x2, h3 = stage("post2", f_post_pre, [Tile(x1, D), Tile(y2, D)], pp2, [o2048(F32), o2048(BF16)])

    def up(name, off):
        return mm(name, _NN, [(h3, W["w_up"], _bs((tmm, D), lambda j, i, k: (i, 0)),
                               _bs((None, D, FF_BLK), lambda j, i, k: (j + off, 0, 0)))],
                  S((2, T, FF_BLK), F32), _bs((None, tmm, FF_BLK), lambda j, i, k: (j, i, 0)), (2, nI, 1))

    Ua, Ub = up("up_a", 0), up("up_b", 2)
    ffn_ins = [Tile(Ua, FF_BLK, lead=lambda o: o, halo=True), Tile(Ub, FF_BLK, lead=lambda o: o, halo=True)]
    ffn_pars = [Par(sp["ffn_conv_w"], lead=lambda o: o), Par(sp["ffn_conv_w"], lead=lambda o: o + 2),
                Par(sp["ffn_conv_b"], lead=lambda o: o), Par(sp["ffn_conv_b"], lead=lambda o: o + 2)]
    ffn_out = [Out((2, T, FF_BLK), BF16, FF_BLK, lead=lambda o: o)]
    (ff,) = stage("convglu", f_convglu, ffn_ins, ffn_pars, ffn_out, outer=2)
    y3 = mm("down", _NN, [(ff, W["w_down"], _bs((None, tmm, FF_BLK), lambda i, j, k: (k, i, 0)),
                           _bs((None, FF_BLK, D), lambda i, j, k: (k, 0, 0)))],
            S((T, D), F32), _bs((tmm, D), lambda i, j, k: (i, 0)), (nI, 1, 2))
    loss, dx2, dy3, gs["ffn_post_norm"] = loss_stage(x2, y3, tgt, sp["ffn_post_norm"], tm)

    dff = mm("down_dx", _NT, [(dy3, W["w_down"], _bs((tmm, D), lambda j, i, k: (i, 0)),
                               _bs((None, FF_BLK, D), lambda j, i, k: (j, 0, 0)))],
             S((2, T, FF_BLK), BF16), _bs((None, tmm, FF_BLK), lambda j, i, k: (j, i, 0)), (2, nI, 1))
    tbig = min(1024, T)
    gw["w_down"] = mm("down_dw", _TN, [(ff, dy3, _bs((None, tbig, FF_BLK), lambda b, j, k: (b, k, 0)),
                                        _bs((tbig, 1024), lambda b, j, k: (k, j)))],
                      S((2, FF_BLK, D), F32), _bs((None, FF_BLK, 1024), lambda b, j, k: (b, 0, j)), (2, D // 1024, T // tbig))
    dU_out = [Out((2, T, FF_BLK), BF16, FF_BLK, lead=lambda o: o), Out((2, T, FF_BLK), BF16, FF_BLK, lead=lambda o: o)]
    (dUa, dUb), dffn = stage("convglu_bwd", f_convglu, ffn_ins, ffn_pars, ffn_out, outer=2, cts=[dff], dins=dU_out)
    gs["ffn_conv_w"] = jnp.concatenate([dffn[0][:2], dffn[1][2:]], axis=0)
    gs["ffn_conv_b"] = jnp.concatenate([dffn[2][:2], dffn[3][2:]], axis=0)
    dh3 = mm("up_dx", _NT, [(dUa, W["w_up"], _bs((None, tmm, FF_BLK), lambda i, j, k: (k, i, 0)),
                             _bs((None, 1024, FF_BLK), lambda i, j, k: (k, j, 0))),
                            (dUb, W["w_up"], _bs((None, tmm, FF_BLK), lambda i, j, k: (k, i, 0)),
                             _bs((None, 1024, FF_BLK), lambda i, j, k: (k + 2, j, 0)))],
             S((T, D), F32), _bs((tmm, 1024), lambda i, j, k: (i, j)), (nI, D // 1024, 2))

    def up_dw(name, dU):
        tt = min(2048, T)
        return mm(name, _TN, [(h3, dU, _bs((tt, 512), lambda b, i, k: (k, i)), _bs((None, tt, FF_BLK), lambda b, i, k: (b, k, 0)))],
                  S((2, D, FF_BLK), F32), _bs((None, 512, FF_BLK), lambda b, i, k: (b, i, 0)), (2, D // 512, T // tt))

    gw["w_up_a"], gw["w_up_b"] = up_dw("up_dw_a", dUa), up_dw("up_dw_b", dUb)
    hook("ffn", gw)
    (dx1, dy2), dpp2 = stage("post2_bwd", f_post_pre, [Tile(x1, D), Tile(y2, D)], pp2, [o2048(F32), o2048(BF16)],
                             cts=[dx2, dh3], dins=[o2048(F32), o2048(BF16)])
    gs["xa_post_norm"], gs["ffn_pre_norm"] = dpp2

    do2 = dense_t("xo_dx", dy2, W["w_xo"])
    gw["w_xo"] = wgrad("xo_dw", o2, dy2)
    (dq2,), (dk2, dv2) = stage("xattn_bwd", f_xattn, [Tile(q2, D)], xa_pars, [o2048(BF16)], cts=[do2], dins=[o2048(BF16)])
    dh2 = dense_t("xq_dx", dq2, W["w_xq"])
    gw["w_xq"] = wgrad("xq_dw", h2, dq2)
    dkv = jnp.concatenate([dk2, dv2], axis=1).astype(BF16)
    dmn = mm("xkv_dx", _NT, [(dkv, W["w_xkv"], _bs((M, 1024), lambda i, j, k: (0, k)), _bs((None, 512, 1024), lambda i, j, k: (k, j, 0)))],
             S((M, D), F32), _bs((M, 512), lambda i, j, k: (0, j)), (1, D // 512, 4))
    gw["w_xkv"] = mm("xkv_dw", _TN, [(mn, dkv, _bs((M, D), lambda b, j, k: (0, 0)), _bs((M, 1024), lambda b, j, k: (0, b)))],
                     S((4, D, 1024), F32), _bs((None, D, 1024), lambda b, j, k: (b, 0, 0)), (4, 1, 1))
    hook("xattn", gw)
    _, (gs["mem_norm"],) = stage("mem_norm_bwd", f_prenorm, [Tile(mem, D)], [Par(sp["mem_norm"])], [Out(mem.shape, BF16, D)],
                                 tm=M, cts=[dmn], dins=[None])
    (dx0, dy1), dpp1 = stage("post1_bwd", f_post_pre, [Tile(x, D), Tile(y1, D)], pp1, [o2048(F32), o2048(BF16)],
                             cts=[dx1, dh2], dins=[o2048(F32), o2048(BF16)])
    gs["mix_post_norm"], gs["xa_pre_norm"] = dpp1

    dmerged = dense_t("mix_out_dx", dy1, W["w_mix_out"])
    gw["w_mix_out"] = wgrad("mix_out_dw", merged, dy1)
    pshape = (T, N_PROJ)
    (dP2, dya, dyb), _ = stage("merge_bwd", f_merge, merge_ins, [], [o2048(BF16)], cts=[dmerged],
                               dins=[Out(pshape, BF16, 2 * D, cb=C_GATES // (2 * D)), o2048(BF16), o2048(BF16)])
    d_on = dense_t("branch_a_dx", dya, W["w_branch_a"])
    gw["w_branch_a"] = wgrad("branch_a_dw", o_n, dya)
    dpooled = mm("branch_b_dx", _NT, [(dyb, W["w_branch_b"], _bs((tmm, 512), lambda i, j, k: (i, k)),
                                       _bs((None, 1024, 512), lambda i, j, k: (k, 0, 0)))],
                 S((T, 1024), F32), _bs((tmm, 1024), lambda i, j, k: (i, 0)), (nI, 1, 4))
    gw["w_branch_b"] = mm("branch_b_dw", _TN, [(pooled, dyb, _bs((tmm, 1024), lambda b, j, k: (k, 0)), _bs((tmm, 512), lambda b, j, k: (k, b)))],
                          S((4, 1024, 512), F32), _bs((None, 1024, 512), lambda b, j, k: (b, 0, 0)), (4, 1, nI))
    hook("mixer", gw)
    (do, dP2), (gs["gdn_norm"],) = stage("gnorm_bwd", f_gnorm, [Tile(o, D), z_tile], [Par(sp["gdn_norm"])], [o2048(BF16)],
                                         cts=[d_on], dins=[o2048(F32), Out(pshape, BF16, D, cb=C_Z // D, into=dP2)])
    dq, dk, dv, dbg = gdn_backward(qkv[0], qkv[1], qkv[2], bg, s0, ninv, do, GDN_SPAN_BWD)
    dcw = []
    for i, (n, dqq) in enumerate(zip("qkv", (dq, dk, dv))):
        (dP2,), (dc,) = stage("conv_%s_bwd" % n, f_heads[i], [Tile(P2, D, cb=i, halo=True)], [Par(cws[i])], [o2048(F32)],
                              cts=[dqq], dins=[Out(pshape, BF16, D, cb=i, into=dP2)])
        dcw.append(dc)
    gs["conv_qkv"] = jnp.concatenate(dcw, axis=1)
    (dP2,), (gs["a_log"], gs["dt_bias"]) = stage("bg_bwd", f_bg, [ba_tile], [Par(sp["a_log"]), Par(sp["dt_bias"])],
                                                 [Out((T, 128), F32, 128)], cts=[dbg],
                                                 dins=[Out(pshape, BF16, 512, cb=C_BA // 512, into=dP2)])
    (dP2,), (gs["pool_w"], gs["pool_scale"]) = stage("pool_bwd", f_pool, [p_tile], pool_pars, [Out((T, 1024), BF16, 1024)],
                                                     cts=[dpooled], dins=[Out(pshape, BF16, 1024, cb=C_POOL // 1024, into=dP2)])
    tk, ta, tt = 2304, 1152, min(2048, T)
    gw["w_in"] = mm("in_proj_dw", _TN, [(dP2, h1, _bs((tt, ta), lambda i, j, k: (k, i)), _bs((tt, D), lambda i, j, k: (k, 0)))],
                    S((N_PROJ, D), F32), _bs((ta, D), lambda i, j, k: (i, 0)), (N_PROJ // ta, 1, T // tt))
    hook("in_proj", gw)
    dh1 = mm("in_proj_dx", _NN, [(dP2, W["w_in"], _bs((tbig, tk), lambda i, j, k: (i, k)), _bs((tk, D), lambda i, j, k: (k, 0)))],
             S((T, D), F32), _bs((tbig, D), lambda i, j, k: (i, 0)), (T // tbig, 1, N_PROJ // tk))
    (grad_x,), (gs["mix_pre_norm"],) = stage("pre1_bwd", f_prenorm_res, [Tile(x, D)], [Par(sp["mix_pre_norm"])],
                                             [o2048(F32), o2048(BF16)], cts=[dx0, dh1], dins=[o2048(F32)])
    return loss, grad_x, gw, gs


W_IN_COLS = 13344
GROUPED = {"w_branch_b": 512, "w_xkv": 1024}
ROW_SHARDED = ("w_branch_a", "w_mix_out", "w_xq", "w_xo")


def shard_to_slab(name, w):
    if name == "w_in":
        return w.T.astype(BF16)
    if name == "w_up":
        return jnp.pad(w, ((0, 0), (0, FF_BLK - FF_COLS))).astype(BF16)
    return w.astype(BF16)


def slabs_to_weight(name, g):
    if name == "w_in":
        full = g.astype(F32).reshape(W_IN_COLS, D_MODEL)
        pad = jnp.zeros((N_PROJ - W_IN_COLS, D_MODEL), F32)
        return jnp.concatenate([full[0:8192], full[9248:13344], full[8224:9248], full[8192:8224], pad]).astype(BF16)
    if name == "w_down":
        z = jnp.zeros((2, FF_BLK - FF_COLS, D_MODEL), g.dtype)
        return jnp.concatenate([g.reshape(2, FF_COLS, D_MODEL), z], axis=1)
    if name in ROW_SHARDED:
        return g.reshape(D_MODEL, D_MODEL)
    return g


def grad_to_slabs(name, gw):
    if name == "w_in":
        g = gw["w_in"]
        return jnp.concatenate([g[0:8192], g[13312:13344], g[12288:13312], g[8192:12288]]).reshape(4, 3336, D_MODEL)
    if name == "w_up":
        return jnp.concatenate([gw["w_up_a"], gw["w_up_b"]], axis=0)
    if name == "w_down":
        return gw["w_down"][:, :FF_COLS].reshape(4, FF_COLS // 2, D_MODEL)
    if name in ROW_SHARDED:
        return gw[name].reshape(4, D_MODEL // 4, D_MODEL)
    return gw[name]


def slab_to_shard_grad(name, f):
    if name == "w_in":
        return f.T
    if name == "w_up":
        return f[:, :FF_COLS]
    return f


MESH = pl.DeviceIdType.MESH
ANY = pl.BlockSpec(memory_space=pl.ANY)


def _me():
    x, y, c = lax.axis_index("x"), lax.axis_index("y"), lax.axis_index("c")
    return x, y, c, 2 * x + y


def _chip_dev(t, c):
    return (t // 2, t % 2, c)


def _rcopy(src, dst, ssem, rsem, dev):
    return pltpu.make_async_remote_copy(src_ref=src, dst_ref=dst, send_sem=ssem, recv_sem=rsem, device_id=dev, device_id_type=MESH)


def _handshake_all(x, y, c):
    barrier = pltpu.get_barrier_semaphore()
    for dx in (0, 1):
        for dy in (0, 1):
            for dc in (0, 1):
                if dx or dy or dc:
                    pl.semaphore_signal(barrier, inc=1, device_id=((x + dx) % 2, (y + dy) % 2, (c + dc) % 2), device_id_type=MESH)
    pl.semaphore_wait(barrier, 7)


def _comm_call(body, name, operands, out_shape, sems, collective_id):
    if collective_id is not None:
        return pl.kernel(body, out_type=out_shape, mesh=plsc.ScalarSubcoreMesh(axis_name="seq", num_cores=1), name=name,
                         scratch_types=sems, compiler_params=pltpu.CompilerParams(collective_id=collective_id))(*operands)
    n_in, n_out = len(operands), len(out_shape)
    return pl.pallas_call(body, name=name, in_specs=[ANY] * n_in, out_specs=[ANY] * n_out, out_shape=out_shape,
                          scratch_shapes=sems)(*operands)


def gather_weights(slabs, name="gather_weights", collective_id=None):
    n = len(slabs)

    def body(*refs):
        src, dst = refs[:n], refs[n:2 * n]
        ici_s, ici_r, fwd_s, fwd_r = refs[2 * n:]
        x, y, c, s = _me()
        if collective_id is not None:
            _handshake_all(x, y, c)
        sender = c == s // 2
        sib = (x, y, 1 - c)
        for r in (1, 2, 3):
            @pl.when(sender)
            def _(r=r):
                for w in range(n):
                    _rcopy(src[w], dst[w].at[s], ici_s.at[w, r - 1], ici_r.at[w, r - 1], _chip_dev(s ^ r, c)).start()
        for r in (1, 2, 3):
            t = s ^ r
            here = c == t // 2

            @pl.when(here)
            def _(r=r, t=t):
                for w in range(n):
                    _rcopy(src[w], dst[w].at[t], ici_s.at[w, r - 1], ici_r.at[w, r - 1], sib).wait_recv()
                    _rcopy(dst[w].at[t], dst[w].at[t], fwd_s.at[w, r - 1], fwd_r.at[w, r - 1], sib).start()

            @pl.when(jnp.logical_not(here))
            def _(r=r, t=t):
                for w in range(n):
                    _rcopy(dst[w].at[t], dst[w].at[t], fwd_s.at[w, r - 1], fwd_r.at[w, r - 1], sib).wait_recv()
        for r in (1, 2, 3):
            t = s ^ r

            @pl.when(sender)
            def _(r=r):
                for w in range(n):
                    _rcopy(src[w], dst[w].at[s], ici_s.at[w, r - 1], ici_r.at[w, r - 1], sib).wait_send()

            @pl.when(c == t // 2)
            def _(r=r, t=t):
                for w in range(n):
                    _rcopy(dst[w].at[t], dst[w].at[t], fwd_s.at[w, r - 1], fwd_r.at[w, r - 1], sib).wait_send()

    out_shape = [jax.ShapeDtypeStruct((4,) + a.shape, a.dtype) for a in slabs]
    sems = [pltpu.SemaphoreType.DMA((n, 3))] * 4
    return _comm_call(body, name, slabs, out_shape, sems, collective_id)


def pair_exchange(g4, name="pair_exchange", collective_id=None):
    n = len(g4)

    def body(*refs):
        src, dst = refs[:n], refs[n:2 * n]
        ssem, rsem = refs[2 * n:]
        x, y, c, s = _me()
        if collective_id is not None:
            _handshake_all(x, y, c)
        cps = [_rcopy(src[w].at[pl.ds(2 * (1 - c), 2)], dst[w], ssem.at[w], rsem.at[w], (x, y, 1 - c)) for w in range(n)]
        for cp in cps:
            cp.start()
        for cp in cps:
            cp.wait()

    return _comm_call(body, name, g4, [jax.ShapeDtypeStruct((2,) + a.shape[1:], a.dtype) for a in g4],
                      [pltpu.SemaphoreType.DMA((n,))] * 2, collective_id)


def _col_tile(R, C):
    for tc in (512, 256, 128):
        if C % tc == 0 and R * tc * 4 <= (4 << 20):
            return tc
    return 128


def pair_add(name, g4, gsib, c):
    _, R, C = g4.shape
    tc = _col_tile(R, C)

    def body(c_ref, a_ref, b_ref, of_ref, ob_ref):
        v = a_ref[...] + b_ref[...]
        of_ref[...] = v
        ob_ref[...] = v.astype(BF16)

    blk = lambda f: pl.BlockSpec((None, R, tc), f)
    return _pcall(
        body, (c.reshape(1).astype(jnp.int32), g4, gsib), name=name, num_scalar_prefetch=1, grid=(2, C // tc),
        in_specs=[blk(lambda p, j, cr: (2 * cr[0] + p, 0, j)), blk(lambda p, j, cr: (p, 0, j))],
        out_specs=[blk(lambda p, j, cr: (p, 0, j)), blk(lambda p, j, cr: (p, 0, j))],
        out_shape=[jax.ShapeDtypeStruct((2, R, C), F32), jax.ShapeDtypeStruct((2, R, C), BF16)],
        compiler_params=_cparams(("arbitrary", "arbitrary")),
    )


def scatter_partials(rb, name="scatter_partials", collective_id=None):
    n = len(rb)

    def body(*refs):
        src, dst = refs[:n], refs[n:2 * n]
        ssem, rsem = refs[2 * n:]
        x, y, c, s = _me()
        if collective_id is not None:
            _handshake_all(x, y, c)
        for r in (1, 2, 3):
            t = s ^ r

            @pl.when(t // 2 == c)
            def _(r=r, t=t):
                for w in range(n):
                    _rcopy(src[w].at[t % 2], dst[w].at[s], ssem.at[w, r - 1], rsem.at[w, r - 1], _chip_dev(t, c)).start()
        for r in (1, 2, 3):
            t = s ^ r

            @pl.when(s // 2 == c)
            def _(r=r, t=t):
                for w in range(n):
                    _rcopy(src[w].at[0], dst[w].at[t], ssem.at[w, r - 1], rsem.at[w, r - 1], _chip_dev(t, c)).wait_recv()
        for r in (1, 2, 3):
            t = s ^ r

            @pl.when(t // 2 == c)
            def _(r=r, t=t):
                for w in range(n):
                    _rcopy(src[w].at[t % 2], dst[w].at[s], ssem.at[w, r - 1], rsem.at[w, r - 1], _chip_dev(t, c)).wait_send()

    return _comm_call(body, name, rb, [jax.ShapeDtypeStruct((4,) + a.shape[1:], a.dtype) for a in rb],
                      [pltpu.SemaphoreType.DMA((n, 3))] * 2, collective_id)


def final_sum(name, rf, recv, s):
    _, R, C = rf.shape
    tc = _col_tile(R, C)

    def body(s_ref, own_ref, r0_ref, r1_ref, r2_ref, o_ref):
        o_ref[...] = ((own_ref[...] + r0_ref[...].astype(F32)) + r1_ref[...].astype(F32)) + r2_ref[...].astype(F32)

    blk = lambda f: pl.BlockSpec((None, R, tc), f)
    other = lambda k: (lambda j, sr: (k + (k >= sr[0]).astype(jnp.int32), 0, j))
    return _pcall(
        body, (s.reshape(1).astype(jnp.int32), rf, recv, recv, recv), name=name, num_scalar_prefetch=1, grid=(C // tc,),
        in_specs=[blk(lambda j, sr: (sr[0] % 2, 0, j)), blk(other(0)), blk(other(1)), blk(other(2))],
        out_specs=pl.BlockSpec((R, tc), lambda j, sr: (0, j)), out_shape=jax.ShapeDtypeStruct((R, C), F32),
        compiler_params=_cparams(("arbitrary",)),
    )


def share_with_sibling(fs, name="share_with_sibling", collective_id=None):
    n = len(fs)

    def body(*refs):
        src, dst = refs[:n], refs[n:2 * n]
        ssem, rsem = refs[2 * n:]
        x, y, c, s = _me()
        if collective_id is not None:
            _handshake_all(x, y, c)
        sib = (x, y, 1 - c)

        @pl.when(s // 2 == c)
        def _():
            cps = [_rcopy(src[w], dst[w], ssem.at[w], rsem.at[w], sib) for w in range(n)]
            for cp in cps:
                cp.start()
            for cp in cps:
                cp.wait_send()

        @pl.when(s // 2 != c)
        def _():
            for w in range(n):
                _rcopy(src[w], dst[w], ssem.at[w], rsem.at[w], sib).wait_recv()

    return _comm_call(body, name, fs, [jax.ShapeDtypeStruct(a.shape, a.dtype) for a in fs],
                      [pltpu.SemaphoreType.DMA((n,))] * 2, collective_id)


def pick(name, mine, a, b):
    R, C = a.shape
    tc = _col_tile(R, C)

    def body(m_ref, a_ref, b_ref, o_ref):
        o_ref[...] = jnp.where(m_ref[0] != 0, a_ref[...], b_ref[...])

    blk = pl.BlockSpec((R, tc), lambda j, mr: (0, j))
    return _pcall(body, (mine.reshape(1).astype(jnp.int32), a, b), name=name, num_scalar_prefetch=1, grid=(C // tc,),
                  in_specs=[blk, blk], out_specs=blk, out_shape=jax.ShapeDtypeStruct((R, C), a.dtype),
                  compiler_params=_cparams(("arbitrary",)))


def allgather_rows(v):
    m_per, ncol = v.shape

    def body(x_ref, out_ref, send_sems, recv_sems, local_sem):
        x, y, c = lax.axis_index("x"), lax.axis_index("y"), lax.axis_index("c")
        me, sibling = (x, y, c), (x, y, 1 - c)
        chips = [(1 - x, y), (x, 1 - y), (1 - x, 1 - y)]

        def rows(px, py, pc):
            return out_ref.at[pl.ds((4 * px + 2 * py + pc) * m_per, m_per), :]

        def copy(k, block, to, src=None):
            return _rcopy(rows(*block) if src is None else src, rows(*block), send_sems.at[k], recv_sems.at[k], to)

        mine = pltpu.make_async_copy(x_ref, rows(*me), local_sem)
        mine.start()
        first = [copy(0, me, sibling, src=x_ref)]
        first += [copy(1 + j, me, (*chip, c), src=x_ref) for j, chip in enumerate(chips)]
        for cp in first:
            cp.start()
        passed = [copy(4 + j, (*chip, c), sibling) for j, chip in enumerate(chips)]
        for j, chip in enumerate(chips):
            copy(1 + j, (*chip, c), me).wait_recv()
            passed[j].start()
        copy(0, sibling, me).wait_recv()
        for j, chip in enumerate(chips):
            copy(4 + j, (*chip, 1 - c), me).wait_recv()
        for cp in first + passed:
            cp.wait_send()
        mine.wait()

    return pl.pallas_call(
        body, name="allgather_rows", out_shape=jax.ShapeDtypeStruct((8 * m_per, ncol), v.dtype),
        in_specs=[pl.BlockSpec(memory_space=pltpu.VMEM)], out_specs=pl.BlockSpec(memory_space=pltpu.VMEM),
        scratch_shapes=[pltpu.SemaphoreType.DMA((7,)), pltpu.SemaphoreType.DMA((7,)), pltpu.SemaphoreType.DMA],
        compiler_params=pltpu.CompilerParams(vmem_limit_bytes=V7X_VMEM_LIMIT),
    )(v)


def sum_blocks(name, a, nblk):
    m = a.shape[0] // nblk

    def body(a_ref, o_ref):
        acc = a_ref[pl.ds(0, m), :]
        for b in range(1, nblk):
            acc = acc + a_ref[pl.ds(b * m, m), :]
        o_ref[...] = acc

    return pl.pallas_call(body, name=name, out_shape=jax.ShapeDtypeStruct((m, a.shape[1]), a.dtype),
                          compiler_params=pltpu.CompilerParams(vmem_limit_bytes=V7X_VMEM_LIMIT))(a)


BIG = ("w_in", "w_branch_a", "w_branch_b", "w_mix_out", "w_xq", "w_xkv", "w_xo", "w_up", "w_down")
GAINS = ("mix_pre_norm", "gdn_norm", "pool_scale", "mix_post_norm", "xa_pre_norm", "mem_norm", "xa_post_norm",
         "ffn_pre_norm", "ffn_post_norm")
WEIGHTS = ("mix_pre_norm", "w_in", "conv_qkv", "a_log", "dt_bias", "gdn_norm", "pool_w", "pool_scale", "w_branch_a",
           "w_branch_b", "w_mix_out", "mix_post_norm", "xa_pre_norm", "mem_norm", "w_xq", "w_xkv", "w_xo", "xa_post_norm",
           "ffn_pre_norm", "w_up", "ffn_conv_w", "ffn_conv_b", "w_down", "ffn_post_norm")


RS_GROUPS = {"ffn": (("w_up", "w_down"), (3, 4, 5)), "xattn": (("w_xo", "w_xq", "w_xkv"), (6, 7, 8)),
             "mixer": (("w_mix_out", "w_branch_a", "w_branch_b"), (9, 10, 11)), "in_proj": (("w_in",), (12, 13, 14))}


def _rows128(vecs):
    flat = jnp.concatenate([v.reshape(-1) for v in vecs])
    m = -(-flat.shape[0] // 1024) * 8
    return jnp.pad(flat, (0, m * 128 - flat.shape[0])).reshape(m, 128)


def _unrows(a, shapes):
    flat, out, pos = a.reshape(-1), [], 0
    for sh in shapes:
        n = 1
        for d in sh:
            n *= d
        out.append(flat[pos:pos + n].reshape(sh))
        pos += n
    return out


def _lane128(v):
    return jnp.pad(v.reshape(1, GDN_HEADS), ((0, 0), (GDN_HEADS, 128 - 2 * GDN_HEADS)))


def kernel(x, mem, mix_pre_norm, w_in, conv_qkv, a_log, dt_bias, gdn_norm, pool_w, pool_scale, w_branch_a, w_branch_b, w_mix_out, mix_post_norm, xa_pre_norm, mem_norm, w_xq, w_xkv, w_xo, xa_post_norm, ffn_pre_norm, w_up, ffn_conv_w, ffn_conv_b, w_down, ffn_post_norm, loss_target, m_mix_pre_norm, m_w_in, m_conv_qkv, m_a_log, m_dt_bias, m_gdn_norm, m_pool_w, m_pool_scale, m_w_branch_a, m_w_branch_b, m_w_mix_out, m_mix_post_norm, m_xa_pre_norm, m_mem_norm, m_w_xq, m_w_xkv, m_w_xo, m_xa_post_norm, m_ffn_pre_norm, m_w_up, m_ffn_conv_w, m_ffn_conv_b, m_w_down, m_ffn_post_norm, v_mix_pre_norm, v_w_in, v_conv_qkv, v_a_log, v_dt_bias, v_gdn_norm, v_pool_w, v_pool_scale, v_w_branch_a, v_w_branch_b, v_w_mix_out, v_mix_post_norm, v_xa_pre_norm, v_mem_norm, v_w_xq, v_w_xkv, v_w_xo, v_xa_post_norm, v_ffn_pre_norm, v_w_up, v_ffn_conv_w, v_ffn_conv_b, v_w_down, v_ffn_post_norm):
    given = dict(locals())
    _LAST[0] = None
    w = {n: given[n][0] for n in WEIGHTS}
    cx, cy, cc = lax.axis_index("x"), lax.axis_index("y"), lax.axis_index("c")
    chip = 2 * cx + cy

    slabs = [shard_to_slab(n, w[n]) for n in BIG]
    first = gather_weights(slabs[:1], name="gather_w_in", collective_id=1)
    rest = gather_weights(slabs[1:], name="gather_rest", collective_id=2)
    stacks = [lax.dynamic_update_index_in_dim(g, sl, chip, 0) for g, sl in zip(list(first) + list(rest), slabs)]
    W = {n: slabs_to_weight(n, g) for n, g in zip(BIG, stacks)}
    sharded_small = (w["conv_qkv"], w["ffn_conv_w"], w["pool_w"])
    allv = allgather_rows(_rows128(sharded_small))
    per_chip = allv.reshape(8, -1)[0::2]
    parts = [_unrows(per_chip[t], [a.shape for a in sharded_small]) for t in range(4)]
    sp = {n: w[n].reshape(1, -1) for n in GAINS}
    sp["a_log"], sp["dt_bias"] = _lane128(w["a_log"]), _lane128(w["dt_bias"])
    sp["conv_qkv"] = jnp.concatenate([p[0] for p in parts], axis=1)
    sp["ffn_conv_w"] = jnp.pad(jnp.stack([p[1] for p in parts]), ((0, 0), (0, 0), (0, FF_BLK - FF_COLS)))
    sp["pool_w"] = jnp.concatenate([p[2] for p in parts], axis=1).reshape(4 * 256, 256)
    sp["ffn_conv_b"] = jnp.pad(w["ffn_conv_b"].reshape(4, 1, FF_COLS), ((0, 0), (0, 0), (0, FF_BLK - FF_COLS)))

    grads, summed, todo = {}, {}, []

    def rs_steps(names, ids, gw):
        g4 = [grad_to_slabs(n, gw) for n in names]
        tag = names[0]
        gsib = pair_exchange(g4, "pair_exchange_" + tag, ids[0])
        yield
        sums = [pair_add("pair_add_" + n, a, b, cc) for n, a, b in zip(names, g4, gsib)]
        recv = scatter_partials([sb for _, sb in sums], "scatter_partials_" + tag, ids[1])
        yield
        fin = [final_sum("final_sum_" + n, sf, rv, chip) for n, (sf, _), rv in zip(names, sums, recv)]
        got = share_with_sibling(fin, "share_" + tag, ids[2])
        yield
        for n, f, g in zip(names, fin, got):
            summed[n] = pick("pick_" + n, (chip // 2 == cc), f, g)
            grads[n] = slab_to_shard_grad(n, summed[n])

    def advance():
        for it in list(todo):
            if next(it, "done") == "done":
                todo.remove(it)

    def hook(event, gw):
        new = rs_steps(*RS_GROUPS[event], gw)
        next(new)
        advance()
        todo.append(new)
        if event == "in_proj":
            next(new)

    loss, grad_x, gw, gs = local_step(x[0], mem[0], loss_target[0], W, sp, hook)

    small_names = GAINS + ("a_log", "dt_bias", "ffn_conv_b", "conv_qkv", "ffn_conv_w", "pool_w")
    vec = _rows128([gs[n] for n in small_names])
    total = sum_blocks("sum_small", allgather_rows(vec), 8)
    advance()
    loss = lax.psum(loss[0, 0], ("x", "y", "c"))
    tot = dict(zip(small_names, _unrows(total, [gs[n].shape for n in small_names])))
    for n in GAINS:
        grads[n] = tot[n].reshape(-1)
    grads["a_log"] = tot["a_log"][0, GDN_HEADS:2 * GDN_HEADS]
    grads["dt_bias"] = tot["dt_bias"][0, GDN_HEADS:2 * GDN_HEADS]
    grads["ffn_conv_b"] = tot["ffn_conv_b"][:, 0, :FF_COLS].reshape(-1)
    grads["conv_qkv"] = lax.dynamic_slice_in_dim(tot["conv_qkv"], chip * 1536, 1536, axis=1)
    grads["ffn_conv_w"] = lax.dynamic_index_in_dim(tot["ffn_conv_w"], chip, axis=0, keepdims=False)[:, :FF_COLS]
    grads["pool_w"] = lax.dynamic_slice_in_dim(tot["pool_w"].reshape(4, 256, 256), chip * 64, 64, axis=1)

    delta, new_m, new_v = {}, {}, {}

    def update(names):
        for n in names:
            shp = w[n].shape
            if n in ("w_in", "w_up"):
                gt = summed[n] if n == "w_in" else grads[n].T
                res = adamw("adamw_" + n, w[n].T, gt, given["m_" + n][0].T, given["v_" + n][0].T)
                res = [a.T for a in res]
            elif len(shp) == 2:
                res = adamw("adamw_" + n, given[n], grads[n], given["m_" + n], given["v_" + n])
            else:
                two = (lambda a: a.reshape(-1, shp[-1])) if len(shp) > 1 else (lambda a: a.reshape(1, -1))
                res = adamw("adamw_" + n, two(w[n]), two(grads[n]), two(given["m_" + n][0]), two(given["v_" + n][0]))
            delta[n], new_m[n], new_v[n] = (a.reshape((1,) + shp) for a in res)

    update([n for n in WEIGHTS if n in grads])
    while todo:
        advance()
    update([n for n in WEIGHTS if n not in delta])
    out_g = [grads[n].reshape((1,) + w[n].shape) for n in WEIGHTS]
    return (loss, grad_x[None], *out_g, *[delta[n] for n in WEIGHTS], *[new_m[n] for n in WEIGHTS], *[new_v[n] for n in WEIGHTS])
```

```python
import functools

import jax
import jax.numpy as jnp
from jax import lax
from jax.experimental import pallas as pl
from jax.experimental.pallas import tpu as pltpu
from jax.experimental.pallas import tpu_sc as plsc

F32 = jnp.float32
BF16 = jnp.bfloat16

D_MODEL = 2048
CHUNK = 64
GDN_HEADS = 16
GDN_DK = 128
POOL_WINDOWS = (2, 4, 8, 16)
XA_HEADS = 4
XA_HEAD_DIM = D_MODEL // XA_HEADS
EPS = 1e-6
GDN_SPAN = 256
GDN_SPAN_BWD = 128
HALO = 16
V7X_VMEM_LIMIT = 56 * 1024 * 1024

ADAM_LR, ADAM_B1, ADAM_B2, ADAM_EPS, ADAM_WD, ADAM_STEP = 0.001, 0.9, 0.999, 1e-08, 0.01, 10

_NN = ((1,), (0,))
_NT = ((1,), (1,))
_TN = ((0,), (0,))


def _cparams(sem):
    return pltpu.CompilerParams(dimension_semantics=sem, vmem_limit_bytes=V7X_VMEM_LIMIT)


_LAST = [None]


def _pcall(body, operands, *, in_specs, out_specs, grid=(), num_scalar_prefetch=0, scratch_shapes=(), **kw):
    operands, in_specs = list(operands), list(in_specs)
    if _LAST[0] is not None and not any(o is _LAST[0] for o in operands):
        n = len(operands)
        inner = body

        def body(*refs):
            return inner(*refs[:n], *refs[n + 1:])

        operands.append(_LAST[0])
        in_specs.append(pl.BlockSpec(memory_space=pl.ANY))
    if num_scalar_prefetch:
        kw["grid_spec"] = pltpu.PrefetchScalarGridSpec(num_scalar_prefetch=num_scalar_prefetch, grid=grid, in_specs=in_specs,
                                                       out_specs=out_specs, scratch_shapes=list(scratch_shapes))
    else:
        kw.update(grid=grid, in_specs=in_specs, out_specs=out_specs, scratch_shapes=list(scratch_shapes))
    res = pl.pallas_call(body, **kw)(*operands)
    _LAST[0] = res[0] if isinstance(res, (list, tuple)) else res
    return res


def _dg(a, b, dims, prec=None):
    return lax.dot_general(a, b, (dims, ((), ())), precision=prec, preferred_element_type=F32)


def _make_dots(cast, prec):
    def raw(dims, a, b):
        return _dg(cast(a), cast(b), dims, prec)

    @jax.custom_vjp
    def nn(a, b):
        return raw(_NN, a, b)

    @jax.custom_vjp
    def nt(a, b):
        return raw(_NT, a, b)

    @jax.custom_vjp
    def tn(a, b):
        return raw(_TN, a, b)

    nn.defvjp(lambda a, b: (raw(_NN, a, b), (a, b)), lambda r, g: (nt(g, r[1]), tn(r[0], g)))
    nt.defvjp(lambda a, b: (raw(_NT, a, b), (a, b)), lambda r, g: (nn(g, r[1]), tn(g, r[0])))
    tn.defvjp(lambda a, b: (raw(_TN, a, b), (a, b)), lambda r, g: (nt(r[1], g), nn(r[0], g)))
    return nn, nt, tn


bdot_nn, bdot_nt, bdot_tn = _make_dots(lambda x: x.astype(BF16), None)
hdot_nn, hdot_nt, hdot_tn = _make_dots(lambda x: x, lax.Precision.HIGHEST)


@functools.partial(jax.custom_vjp, nondiff_argnums=(1,))
def shift_rows(x, k):
    return pltpu.roll(x, k, 0)


def _shift_rows_fwd(x, k):
    return pltpu.roll(x, k, 0), None


def _shift_rows_bwd(k, _, g):
    return (pltpu.roll(g, g.shape[0] - k, 0),)


shift_rows.defvjp(_shift_rows_fwd, _shift_rows_bwd)


@functools.partial(jax.custom_vjp, nondiff_argnums=(1,))
def drop_head(x, h):
    return x[h:]


def _drop_head_fwd(x, h):
    return x[h:], None


def _drop_head_bwd(h, _, g):
    return (jnp.concatenate([jnp.zeros((h,) + g.shape[1:], g.dtype), g], axis=0),)


drop_head.defvjp(_drop_head_fwd, _drop_head_bwd)


@functools.partial(jax.custom_vjp, nondiff_argnums=(1,))
def split_lanes(x, n):
    w = x.shape[-1] // n
    return tuple(x[:, i * w:(i + 1) * w] for i in range(n))


def _split_lanes_fwd(x, n):
    return split_lanes(x, n), None


def _split_lanes_bwd(n, _, gs):
    return (jnp.concatenate(list(gs), axis=-1),)


split_lanes.defvjp(_split_lanes_fwd, _split_lanes_bwd)


@functools.partial(jax.custom_vjp, nondiff_argnums=(1,))
def split_rows(x, n):
    h = x.shape[0] // n
    return tuple(x[i * h:(i + 1) * h] for i in range(n))


def _split_rows_fwd(x, n):
    return split_rows(x, n), None


def _split_rows_bwd(n, _, gs):
    return (jnp.concatenate(list(gs), axis=0),)


split_rows.defvjp(_split_rows_fwd, _split_rows_bwd)


def row_of(w, j):
    rid = lax.broadcasted_iota(jnp.int32, w.shape, 0)
    return jnp.sum(jnp.where(rid == j, w, 0.0), axis=0, keepdims=True)


def sigmoid(x):
    return 0.5 * jnp.tanh(0.5 * x) + 0.5


def silu(x):
    return x * sigmoid(x)


def softplus(x):
    return jnp.maximum(x, 0.0) + jnp.log(1.0 + jnp.exp(-jnp.abs(x)))


def rms(x, g):
    return x * lax.rsqrt(jnp.mean(x * x, axis=-1, keepdims=True) + EPS) * g


def mm(name, dims, pairs, out_shape, out_spec, grid):
    nk = grid[2]
    acc_in_out = nk > 1 and out_shape.dtype == F32
    npair = len(pairs)

    def body(*refs):
        o_ref = refs[2 * npair]
        part = None
        for p in range(npair):
            d = _dg(refs[2 * p][...].astype(BF16), refs[2 * p + 1][...].astype(BF16), dims)
            part = d if part is None else part + d
        if nk == 1:
            o_ref[...] = part.astype(o_ref.dtype)
            return
        acc = o_ref if acc_in_out else refs[2 * npair + 1]
        k = pl.program_id(2)

        @pl.when(k == 0)
        def _():
            acc[...] = part

        @pl.when(k > 0)
        def _():
            acc[...] += part

        if not acc_in_out:
            @pl.when(k == nk - 1)
            def _():
                o_ref[...] = acc[...].astype(o_ref.dtype)

    scratch = []
    if nk > 1 and not acc_in_out:
        scratch = [pltpu.VMEM(tuple(d for d in out_spec.block_shape if d is not None), F32)]
    in_specs, operands = [], []
    for a, b, a_spec, b_spec in pairs:
        in_specs += [a_spec, b_spec]
        operands += [a, b]
    return _pcall(body, operands, name=name, grid=grid, in_specs=in_specs, out_specs=out_spec, out_shape=out_shape,
                  scratch_shapes=scratch, compiler_params=_cparams(("parallel", "parallel", "arbitrary")))


def _bs(shape, fn):
    return pl.BlockSpec(shape, fn)


class Tile:
    def __init__(self, arr, w, cb=0, lead=None, halo=False):
        self.arr, self.w, self.cb, self.lead, self.halo = arr, w, cb, lead, halo


class Out:
    def __init__(self, shape, dtype, w, cb=0, lead=None, into=None):
        self.shape, self.dtype, self.w, self.cb, self.lead, self.into = shape, dtype, w, cb, lead, into


class Par:
    def __init__(self, arr, lead=None):
        self.arr, self.lead = arr, lead


def _spec(rows, w, cb, lead, tile_of):
    if lead is None:
        return pl.BlockSpec((rows, w), lambda o, i: (tile_of(i), cb))
    return pl.BlockSpec((None, rows, w), lambda o, i: (lead(o), tile_of(i), cb))


def _par_spec(p):
    if p.lead is None:
        return pl.BlockSpec(p.arr.shape, lambda o, i: (0, 0))
    return pl.BlockSpec((None,) + p.arr.shape[1:], lambda o, i: (p.lead(o), 0, 0))


def run_stage(name, fn, tm, ins, pars, outs, *, outer=1, cts=None, dins=None):
    T = ins[0].arr.shape[-2]
    nt = T // tm
    bwd = cts is not None
    any_halo = any(t.halo for t in ins)
    step_tile = (lambda i: nt - 1 - i) if bwd else (lambda i: i)
    hb = tm // HALO

    in_specs, operands = [], []
    for t in ins:
        if t.halo:
            in_specs.append(_spec(HALO, t.w, t.cb, t.lead, lambda i: jnp.maximum(step_tile(i) * hb - 1, 0)))
            operands.append(t.arr)
        in_specs.append(_spec(tm, t.w, t.cb, t.lead, step_tile))
        operands.append(t.arr)
    for p in pars:
        in_specs.append(_par_spec(p))
        operands.append(p.arr)
    n_in_refs = len(operands)

    out_descs = list(outs) if not bwd else [d for d in dins if d is not None]
    aliases = {}
    if bwd:
        for c, o in zip(cts, outs):
            in_specs.append(_spec(tm, o.w, o.cb, o.lead, step_tile))
            operands.append(c)
    n_ct = len(operands) - n_in_refs
    for k, o in enumerate(out_descs):
        if o.into is not None:
            aliases[len(operands)] = k
            in_specs.append(pl.BlockSpec(memory_space=pl.ANY))
            operands.append(o.into)
    out_specs = [_spec(tm, o.w, o.cb, o.lead, step_tile) for o in out_descs]
    out_shapes = [jax.ShapeDtypeStruct(o.shape, o.dtype) for o in out_descs]
    if bwd:
        for p in pars:
            out_specs.append(_par_spec(p))
            out_shapes.append(jax.ShapeDtypeStruct(p.arr.shape, F32))
    scratch = []
    if bwd and any_halo:
        scratch = [pltpu.VMEM((HALO, t.w), F32) for t, d in zip(ins, dins) if t.halo and d is not None]

    def body(*refs):
        i = pl.program_id(1)
        tile = step_tile(i)
        row0 = tile * tm
        pos = 0
        tiles = []
        for t in ins:
            if t.halo:
                prev = jnp.where(tile > 0, refs[pos][...].astype(F32), 0.0)
                tiles.append(jnp.concatenate([prev, refs[pos + 1][...].astype(F32)], axis=0))
                pos += 2
            else:
                tiles.append(refs[pos][...].astype(F32))
                pos += 1
        pvals = [refs[pos + k][...].astype(F32) for k in range(len(pars))]
        pos += len(pars)
        if not bwd:
            res = fn(tiles, pvals, row0)
            for o_ref, r in zip(refs[pos:], res):
                o_ref[...] = r.astype(o_ref.dtype)
            return
        ct_vals = [refs[pos + k][...].astype(F32) for k in range(n_ct)]
        pos += n_ct + len(aliases)
        _, vjp_fn = jax.vjp(lambda tt, pp: fn(tt, pp, row0), tiles, pvals)
        d_tiles, d_pars = vjp_fn(ct_vals)
        carries = list(refs[len(refs) - len(scratch):])
        for t, d, dt in zip(ins, dins, d_tiles):
            if d is None:
                continue
            o_ref = refs[pos]
            pos += 1
            if t.halo:
                carry = carries.pop(0)
                main = dt[HALO:]
                tail = main[tm - HALO:] + jnp.where(i > 0, carry[...], 0.0)
                o_ref[...] = jnp.concatenate([main[:tm - HALO], tail], axis=0).astype(o_ref.dtype)
                carry[...] = dt[:HALO]
            else:
                o_ref[...] = dt.astype(o_ref.dtype)
        for dp in d_pars:
            acc = refs[pos]
            pos += 1

            @pl.when(i == 0)
            def _(acc=acc, dp=dp):
                acc[...] = dp

            @pl.when(i > 0)
            def _(acc=acc, dp=dp):
                acc[...] += dp

    res = _pcall(body, operands, name=name, grid=(outer, nt), in_specs=in_specs, out_specs=out_specs, out_shape=out_shapes,
                 scratch_shapes=scratch, input_output_aliases=aliases, compiler_params=_cparams(("arbitrary", "arbitrary")))
    if not bwd:
        return list(res)
    n_d = len(out_descs)
    d_full, it = [], iter(res[:n_d])
    for d in dins:
        d_full.append(None if d is None else next(it))
    return d_full, list(res[n_d:])


def f_prenorm(t, p, row0):
    return [rms(t[0], p[0])]


def f_prenorm_res(t, p, row0):
    return [t[0], rms(t[0], p[0])]


def f_post_pre(t, p, row0):
    x, y = t
    x1 = x + rms(y, p[0])
    return [x1, rms(x1, p[1])]


def _causal_conv(x, w, taps):
    y = x * row_of(w, taps - 1)
    for j in range(taps - 1):
        y = y + shift_rows(x, taps - 1 - j) * row_of(w, j)
    return drop_head(y, HALO)


def _l2(x):
    return x * lax.rsqrt(jnp.sum(x * x, axis=-1, keepdims=True) + EPS)


def make_f_convhead(scale, normalise):
    def f(t, p, row0):
        y = silu(_causal_conv(t[0], p[0], 4))
        if not normalise:
            return [y]
        return [jnp.concatenate([_l2(c) * scale for c in split_lanes(y, GDN_HEADS)], axis=-1)]
    return f


def f_bg(t, p, row0):
    ba = split_lanes(t[0], 4)[0]
    alog, dtb = p
    lane = lax.broadcasted_iota(jnp.int32, ba.shape, 1)
    bg = jnp.where(lane < GDN_HEADS, sigmoid(ba), -jnp.exp(alog) * softplus(ba + dtb))
    return [jnp.where(lane < 2 * GDN_HEADS, bg, 0.0)]


def f_gnorm(t, p, row0):
    o, z = t
    po, pz = split_lanes(o, GDN_HEADS), split_lanes(z, GDN_HEADS)
    return [jnp.concatenate([rms(a, p[0]) * silu(b) for a, b in zip(po, pz)], axis=-1)]


def f_pool(t, p, row0):
    x = t[0]
    pw, psc = p
    tm = x.shape[0] - HALO
    tpos = (row0 + lax.broadcasted_iota(jnp.int32, (tm, 1), 0) + 1).astype(F32)
    outs = []
    for xg, wg, win in zip(split_lanes(x, 4), split_rows(pw, 4), POOL_WINDOWS):
        s, span = xg, 1
        while span < win:
            s = s + shift_rows(s, span)
            span *= 2
        mean = drop_head(s, HALO) / jnp.minimum(tpos, float(win))
        outs.append(bdot_nn(mean - drop_head(xg, HALO), wg))
    return [jnp.concatenate(outs, axis=-1) * psc]


def f_merge(t, p, row0):
    gates, ya, yb = t
    ga, gb = split_lanes(gates, 2)
    return [sigmoid(ga) * ya + sigmoid(gb) * yb]


def f_xattn(t, p, row0):
    k, v = p
    outs = []
    for qh, kh, vh in zip(split_lanes(t[0], XA_HEADS), split_lanes(k, XA_HEADS), split_lanes(v, XA_HEADS)):
        s = bdot_nt(qh, kh) * (XA_HEAD_DIM ** -0.5)
        s = s - jnp.max(s, axis=-1, keepdims=True)
        e = jnp.exp(s)
        outs.append(bdot_nn(e / jnp.sum(e, axis=-1, keepdims=True), vh))
    return [jnp.concatenate(outs, axis=-1)]


def f_convglu(t, p, row0):
    ua, ub = t
    cwa, cwb, ba, bb = p
    return [silu(_causal_conv(ua, cwa, 3) + ba) * (_causal_conv(ub, cwb, 3) + bb)]


_BNN = (((2,), (1,)), ((0,), (0,)))
_BNT = (((2,), (2,)), ((0,), (0,)))
_BTN = (((1,), (1,)), ((0,), (0,)))


def _make_batched_dots():
    def raw(dims, a, b):
        return lax.dot_general(a.astype(BF16), b.astype(BF16), dims, preferred_element_type=F32)

    @jax.custom_vjp
    def nn(a, b):
        return raw(_BNN, a, b)

    @jax.custom_vjp
    def nt(a, b):
        return raw(_BNT, a, b)

    @jax.custom_vjp
    def tn(a, b):
        return raw(_BTN, a, b)

    nn.defvjp(lambda a, b: (raw(_BNN, a, b), (a, b)), lambda r, g: (nt(g, r[1]), tn(r[0], g)))
    nt.defvjp(lambda a, b: (raw(_BNT, a, b), (a, b)), lambda r, g: (nn(g, r[1]), tn(g, r[0])))
    tn.defvjp(lambda a, b: (raw(_BTN, a, b), (a, b)), lambda r, g: (nt(r[1], g), nn(r[0], g)))
    return nn, nt, tn


bb_nn, bb_nt, bb_tn = _make_batched_dots()


def _inverse_correction(X):
    N, P = X, X
    for _ in range(5):
        P = bb_nn(P, P)
        N = N + P + bb_nn(N, P)
    return N


@jax.custom_vjp
def _saved_inverse_correction(X, N):
    return N


def _saved_inverse_fwd(X, N):
    return N, N


def _saved_inverse_bwd(N, G):
    t = G + bb_tn(N, G)
    return t + bb_nt(t, N), jnp.zeros_like(N)


_saved_inverse_correction.defvjp(_saved_inverse_fwd, _saved_inverse_bwd)


def _gdn_local(q, k, v, gcol, bcol, n_saved=None):
    C = CHUNK
    r = lax.broadcasted_iota(jnp.int32, (1, C, C), 1)
    c = lax.broadcasted_iota(jnp.int32, (1, C, C), 2)
    eye, incl, strict = r == c, r >= c, r > c
    grow = jnp.sum(jnp.where(eye, gcol, 0.0), axis=1, keepdims=True)
    Gcol = jnp.sum(jnp.where(incl, grow, 0.0), axis=2, keepdims=True)
    Grow = jnp.sum(jnp.where(eye, Gcol, 0.0), axis=1, keepdims=True)
    decay = jnp.where(incl, jnp.exp(jnp.where(incl, Gcol - Grow, 0.0)), 0.0)
    X = -jnp.where(strict, bcol * decay * bb_nt(k, k), 0.0)
    N = _inverse_correction(X) if n_saved is None else _saved_inverse_correction(X, n_saved)
    expg = jnp.exp(Gcol)
    rv, rk = bcol * v, (bcol * expg) * k
    u_v = rv + bb_nn(N, rv)
    w_k = rk + bb_nn(N, rk)
    attn = decay * bb_nt(q, k)
    rid = lax.broadcasted_iota(jnp.int32, (1, C, 1), 1)
    glast = jnp.sum(jnp.where(rid == C - 1, Gcol, 0.0), axis=1, keepdims=True)
    return (u_v, w_k, attn, q * expg, k * jnp.exp(glast - Gcol), jnp.exp(glast)), N


def _gdn_rec(u_v, w_k, attn, q_dec, k_dec, cd, S):
    u = u_v - bb_nn(w_k, S)
    o = bb_nn(q_dec, S) + bb_nn(attn, u)
    return o, cd * S + bb_tn(k_dec, u)


def _gdn_load(q_ref, k_ref, v_ref, bg_ref, nch):
    H = GDN_HEADS

    def batched(ref):
        return jnp.stack([ref[c * CHUNK:(c + 1) * CHUNK, h * GDN_DK:(h + 1) * GDN_DK] for c in range(nch) for h in range(H)])

    bg = bg_ref[...]
    lane = lax.broadcasted_iota(jnp.int32, bg.shape, 1)
    bcols = [jnp.sum(jnp.where(lane == h, bg, 0.0), axis=-1, keepdims=True) for h in range(H)]
    gcols = [jnp.sum(jnp.where(lane == H + h, bg, 0.0), axis=-1, keepdims=True) for h in range(H)]
    pick = lambda cols: jnp.stack([cols[h][c * CHUNK:(c + 1) * CHUNK] for c in range(nch) for h in range(H)])
    return batched(q_ref), batched(k_ref), batched(v_ref), pick(gcols), pick(bcols)


def _gdn_store(ref, val, nch):
    H = GDN_HEADS
    for c in range(nch):
        for h in range(H):
            ref[c * CHUNK:(c + 1) * CHUNK, h * GDN_DK:(h + 1) * GDN_DK] = val[c * H + h]


def gdn_forward(q, k, v, bg, span):
    T = q.shape[0]
    ns, nch, H = T // span, span // CHUNK, GDN_HEADS

    def body(q_ref, k_ref, v_ref, bg_ref, o_ref, s_ref, n_ref, state):
        @pl.when(pl.program_id(0) == 0)
        def _():
            state[...] = jnp.zeros(state.shape, F32)

        loc, n_ref[...] = _gdn_local(*_gdn_load(q_ref, k_ref, v_ref, bg_ref, nch))
        S = state[...]
        for c in range(nch):
            s_ref[c] = S
            o, S = _gdn_rec(*[a[c * H:(c + 1) * H] for a in loc], S)
            for h in range(H):
                o_ref[c * CHUNK:(c + 1) * CHUNK, h * GDN_DK:(h + 1) * GDN_DK] = o[h]
        state[...] = S

    wide = pl.BlockSpec((span, H * GDN_DK), lambda s: (s, 0))
    return _pcall(
        body, (q, k, v, bg), name="gdn_fwd", grid=(ns,),
        in_specs=[wide, wide, wide, pl.BlockSpec((span, 128), lambda s: (s, 0))],
        out_specs=[wide, pl.BlockSpec((nch, H, GDN_DK, GDN_DK), lambda s: (s, 0, 0, 0)),
                   pl.BlockSpec((nch * H, CHUNK, CHUNK), lambda s: (s, 0, 0))],
        out_shape=[jax.ShapeDtypeStruct((T, H * GDN_DK), F32), jax.ShapeDtypeStruct((T // CHUNK, H, GDN_DK, GDN_DK), F32),
                   jax.ShapeDtypeStruct((T // CHUNK * H, CHUNK, CHUNK), F32)],
        scratch_shapes=[pltpu.VMEM((H, GDN_DK, GDN_DK), F32)],
        compiler_params=_cparams(("arbitrary",)),
    )


def gdn_backward(q, k, v, bg, starts, ninv, do, span):
    T = q.shape[0]
    ns, nch, H = T // span, span // CHUNK, GDN_HEADS

    def body(q_ref, k_ref, v_ref, bg_ref, s_ref, n_ref, do_ref, dq_ref, dk_ref, dv_ref, dbg_ref, dstate):
        @pl.when(pl.program_id(0) == 0)
        def _():
            dstate[...] = jnp.zeros(dstate.shape, F32)

        (loc, _), vjp_loc = jax.vjp(_gdn_local, *_gdn_load(q_ref, k_ref, v_ref, bg_ref, nch), n_ref[...])
        dS = dstate[...]
        dloc = [None] * nch
        for c in reversed(range(nch)):
            _, vjp_rec = jax.vjp(_gdn_rec, *[a[c * H:(c + 1) * H] for a in loc], s_ref[c])
            do_c = jnp.stack([do_ref[c * CHUNK:(c + 1) * CHUNK, h * GDN_DK:(h + 1) * GDN_DK] for h in range(H)])
            *dloc[c], dS = vjp_rec((do_c, dS))
        dstate[...] = dS
        d_loc = tuple(jnp.concatenate([dloc[c][i] for c in range(nch)], axis=0) for i in range(6))
        dq, dk, dv, dg, db, _ = vjp_loc((d_loc, jnp.zeros((nch * H, CHUNK, CHUNK), F32)))
        _gdn_store(dq_ref, dq, nch)
        _gdn_store(dk_ref, dk, nch)
        _gdn_store(dv_ref, dv, nch)
        lane = lax.broadcasted_iota(jnp.int32, (CHUNK, 128), 1)
        for c in range(nch):
            acc = jnp.zeros((CHUNK, 128), F32)
            for h in range(H):
                acc = acc + jnp.where(lane == h, db[c * H + h], 0.0) + jnp.where(lane == H + h, dg[c * H + h], 0.0)
            dbg_ref[c * CHUNK:(c + 1) * CHUNK, :] = acc

    wide = pl.BlockSpec((span, H * GDN_DK), lambda i: (ns - 1 - i, 0))
    bgs = pl.BlockSpec((span, 128), lambda i: (ns - 1 - i, 0))
    return _pcall(
        body, (q, k, v, bg, starts, ninv, do), name="gdn_bwd", grid=(ns,),
        in_specs=[wide, wide, wide, bgs, pl.BlockSpec((nch, H, GDN_DK, GDN_DK), lambda i: (ns - 1 - i, 0, 0, 0)),
                  pl.BlockSpec((nch * H, CHUNK, CHUNK), lambda i: (ns - 1 - i, 0, 0)), wide],
        out_specs=[wide, wide, wide, bgs],
        out_shape=[jax.ShapeDtypeStruct((T, H * GDN_DK), F32)] * 3 + [jax.ShapeDtypeStruct((T, 128), F32)],
        scratch_shapes=[pltpu.VMEM((H, GDN_DK, GDN_DK), F32)],
        compiler_params=_cparams(("arbitrary",)),
    )


def loss_stage(x2, y3, tgt, g, tm):
    T, D = x2.shape
    nt = T // tm

    def body(x_ref, y_ref, t_ref, g_ref, loss_ref, dx_ref, dy_ref, dg_ref):
        i = pl.program_id(0)
        tgtv = t_ref[...]

        def f(x, y, gg):
            err = x + rms(y, gg) - tgtv
            return 0.5 * jnp.mean(err * err, axis=-1, keepdims=True)

        rows, vjp_fn = jax.vjp(f, x_ref[...], y_ref[...], g_ref[...])
        dx, dy, dg = vjp_fn(jnp.ones_like(rows))
        dx_ref[...] = dx
        dy_ref[...] = dy.astype(dy_ref.dtype)
        part = jnp.sum(rows, axis=0, keepdims=True)

        @pl.when(i == 0)
        def _():
            loss_ref[...] = part
            dg_ref[...] = dg

        @pl.when(i > 0)
        def _():
            loss_ref[...] += part
            dg_ref[...] += dg

    tile = pl.BlockSpec((tm, D), lambda i: (i, 0))
    gs = pl.BlockSpec((1, D), lambda i: (0, 0))
    return _pcall(
        body, (x2, y3, tgt, g), name="loss_head", grid=(nt,), in_specs=[tile, tile, tile, gs],
        out_specs=[pl.BlockSpec((1, 1), lambda i: (0, 0)), tile, tile, gs],
        out_shape=[jax.ShapeDtypeStruct((1, 1), F32), jax.ShapeDtypeStruct((T, D), F32),
                   jax.ShapeDtypeStruct((T, D), BF16), jax.ShapeDtypeStruct((1, D), F32)],
        compiler_params=_cparams(("arbitrary",)),
    )


def adamw(name, w, g, m, v):
    R, C = g.shape
    layered = w.ndim == 3
    tr = R
    for cand in (256, 128, 64, 32, 16, 8):
        if R % cand == 0 and R > cand and cand * C * 4 <= (2 << 20):
            tr = cand
            break
    c1 = 1.0 / (1.0 - ADAM_B1 ** ADAM_STEP)
    c2 = 1.0 / (1.0 - ADAM_B2 ** ADAM_STEP)

    def body(w_ref, g_ref, m_ref, v_ref, d_ref, nm_ref, nv_ref):
        gg = g_ref[...]
        nm = ADAM_B1 * m_ref[...] + (1.0 - ADAM_B1) * gg
        nv = ADAM_B2 * v_ref[...] + (1.0 - ADAM_B2) * (gg * gg)
        d_ref[...] = -ADAM_LR * ((nm * c1) / (jnp.sqrt(nv * c2) + ADAM_EPS) + ADAM_WD * w_ref[...])
        nm_ref[...] = nm
        nv_ref[...] = nv

    gspec = pl.BlockSpec((tr, C), lambda i: (i, 0))
    spec = pl.BlockSpec((None, tr, C), lambda i: (0, i, 0)) if layered else gspec
    steps = R // tr
    if tr < 64 and R > 512 and C % 128 == 0 and not layered:
        gspec = spec = pl.BlockSpec((R, 128), lambda i: (0, i))
        steps = C // 128
    return _pcall(
        body, (w, g, m, v), name=name, grid=(steps,), in_specs=[spec, gspec, spec, spec], out_specs=[spec] * 3,
        out_shape=[jax.ShapeDtypeStruct(w.shape, F32)] * 3, compiler_params=_cparams(("parallel",)),
    )


C_Z, C_GATES, C_POOL, C_BA, N_PROJ = 6144, 8192, 12288, 13312, 13824
FF_BLK = 2816
FF_COLS = 2752


def local_step(x, mem, tgt, W, sp, hook=lambda event, gw: None):
    T, D = x.shape
    tm = 256
    tmm = min(512, T)
    nI = T // tmm
    S = jax.ShapeDtypeStruct
    gw, gs = {}, {}

    def stage(name, fn, ins, pars, outs, **kw):
        return run_stage(name, fn, kw.pop("tm", tm), ins, pars, outs, **kw)

    def dense(name, a, w, out_dtype=F32, tn=2048):
        Tq, Kd = a.shape
        N = w.shape[1]
        tq = min(tmm, Tq)
        return mm(name, _NN, [(a, w, _bs((tq, Kd), lambda j, i, k: (i, 0)), _bs((Kd, tn), lambda j, i, k: (0, j)))],
                  S((Tq, N), out_dtype), _bs((tq, tn), lambda j, i, k: (i, j)), (N // tn, Tq // tq, 1))

    def dense_t(name, g, w, out_dtype=F32, tn=2048):
        Tq, N = g.shape
        Kd = w.shape[0]
        tq = min(tmm, Tq)
        return mm(name, _NT, [(g, w, _bs((tq, N), lambda j, i, k: (i, 0)), _bs((tn, N), lambda j, i, k: (j, 0)))],
                  S((Tq, Kd), out_dtype), _bs((tq, tn), lambda j, i, k: (i, j)), (Kd // tn, Tq // tq, 1))

    def wgrad(name, a, g, ta=1024, tn=2048):
        Tq, Kd = a.shape
        N = g.shape[1]
        tt = min(2048, Tq)
        return mm(name, _TN, [(a, g, _bs((tt, ta), lambda i, j, k: (k, i)), _bs((tt, tn), lambda i, j, k: (k, j)))],
                  S((Kd, N), F32), _bs((ta, tn), lambda i, j, k: (i, j)), (Kd // ta, N // tn, Tq // tt))

    o2048 = lambda dt: Out((T, D), dt, D)

    (h1,) = stage("pre1", f_prenorm, [Tile(x, D)], [Par(sp["mix_pre_norm"])], [o2048(BF16)])
    tnp = 1536
    P2 = mm("in_proj", _NT, [(h1, W["w_in"], _bs((tmm, D), lambda j, i, k: (i, 0)), _bs((tnp, D), lambda j, i, k: (j, 0)))],
            S((T, N_PROJ), F32), _bs((tmm, tnp), lambda j, i, k: (i, j)), (N_PROJ // tnp, nI, 1))
    cw = sp["conv_qkv"]
    cws = [cw[:, i * D:(i + 1) * D] for i in range(3)]
    f_heads = [make_f_convhead(GDN_DK ** -0.5, True), make_f_convhead(1.0, True), make_f_convhead(1.0, False)]
    qkv = [stage("conv_" + n, f_heads[i], [Tile(P2, D, cb=i, halo=True)], [Par(cws[i])], [o2048(F32)])[0]
           for i, n in enumerate("qkv")]
    ba_tile = Tile(P2, 512, cb=C_BA // 512)
    (bg,) = stage("bg", f_bg, [ba_tile], [Par(sp["a_log"]), Par(sp["dt_bias"])], [Out((T, 128), F32, 128)])
    o, s0, ninv = gdn_forward(qkv[0], qkv[1], qkv[2], bg, GDN_SPAN)
    z_tile = Tile(P2, D, cb=C_Z // D)
    (o_n,) = stage("gnorm", f_gnorm, [Tile(o, D), z_tile], [Par(sp["gdn_norm"])], [o2048(BF16)])
    y_a = dense("branch_a", o_n, W["w_branch_a"])
    p_tile = Tile(P2, 1024, cb=C_POOL // 1024, halo=True)
    pool_pars = [Par(sp["pool_w"]), Par(sp["pool_scale"])]
    (pooled,) = stage("pool", f_pool, [p_tile], pool_pars, [Out((T, 1024), BF16, 1024)])
    y_b = mm("branch_b", _NN, [(pooled, W["w_branch_b"], _bs((tmm, 1024), lambda j, i, k: (i, 0)),
                                _bs((None, 1024, 512), lambda j, i, k: (j, 0, 0)))],
             S((T, D), F32), _bs((tmm, 512), lambda j, i, k: (i, j)), (4, nI, 1))
    gate_tile = Tile(P2, 2 * D, cb=C_GATES // (2 * D))
    merge_ins = [gate_tile, Tile(y_a, D), Tile(y_b, D)]
    (merged,) = stage("merge", f_merge, merge_ins, [], [o2048(BF16)])
    y1 = dense("mix_out", merged, W["w_mix_out"])
    pp1 = [Par(sp["mix_post_norm"]), Par(sp["xa_pre_norm"])]
    x1, h2 = stage("post1", f_post_pre, [Tile(x, D), Tile(y1, D)], pp1, [o2048(F32), o2048(BF16)])

    q2 = dense("xq", h2, W["w_xq"], out_dtype=BF16)
    (mn,) = stage("mem_norm", f_prenorm, [Tile(mem, D)], [Par(sp["mem_norm"])], [Out(mem.shape, BF16, D)], tm=mem.shape[0])
    M = mem.shape[0]
    kv = mm("xkv", _NN, [(mn, W["w_xkv"], _bs((M, D), lambda j, i, k: (0, 0)), _bs((None, D, 1024), lambda j, i, k: (j, 0, 0)))],
            S((M, 2 * D), F32), _bs((M, 1024), lambda j, i, k: (0, j)), (4, 1, 1))
    k2, v2 = kv[:, :D], kv[:, D:]
    xa_pars = [Par(k2), Par(v2)]
    (o2,) = stage("xattn", f_xattn, [Tile(q2, D)], xa_pars, [o2048(BF16)])
    y2 = dense("xo", o2, W["w_xo"])
    pp2 = [Par(sp["xa_post_norm"]), Par(sp["ffn_pre_norm"])]
    x2, h3 = stage("post2", f_post_pre, [Tile(x1, D), Tile(y2, D)], pp2, [o2048(F32), o2048(BF16)])

    def up(name, off):
        return mm(name, _NN, [(h3, W["w_up"], _bs((tmm, D), lambda j, i, k: (i, 0)),
                               _bs((None, D, FF_BLK), lambda j, i, k: (j + off, 0, 0)))],
                  S((2, T, FF_BLK), F32), _bs((None, tmm, FF_BLK), lambda j, i, k: (j, i, 0)), (2, nI, 1))

    Ua, Ub = up("up_a", 0), up("up_b", 2)
    ffn_ins = [Tile(Ua, FF_BLK, lead=lambda o: o, halo=True), Tile(Ub, FF_BLK, lead=lambda o: o, halo=True)]
    ffn_pars = [Par(sp["ffn_conv_w"], lead=lambda o: o), Par(sp["ffn_conv_w"], lead=lambda o: o + 2),
                Par(sp["ffn_conv_b"], lead=lambda o: o), Par(sp["ffn_conv_b"], lead=lambda o: o + 2)]
    ffn_out = [Out((2, T, FF_BLK), BF16, FF_BLK, lead=lambda o: o)]
    (ff,) = stage("convglu", f_convglu, ffn_ins, ffn_pars, ffn_out, outer=2)
    y3 = mm("down", _NN, [(ff, W["w_down"], _bs((None, tmm, FF_BLK), lambda i, j, k: (k, i, 0)),
                           _bs((None, FF_BLK, D), lambda i, j, k: (k, 0, 0)))],
            S((T, D), F32), _bs((tmm, D), lambda i, j, k: (i, 0)), (nI, 1, 2))
    loss, dx2, dy3, gs["ffn_post_norm"] = loss_stage(x2, y3, tgt, sp["ffn_post_norm"], tm)

    dff = mm("down_dx", _NT, [(dy3, W["w_down"], _bs((tmm, D), lambda j, i, k: (i, 0)),
                               _bs((None, FF_BLK, D), lambda j, i, k: (j, 0, 0)))],
             S((2, T, FF_BLK), BF16), _bs((None, tmm, FF_BLK), lambda j, i, k: (j, i, 0)), (2, nI, 1))
    tbig = min(1024, T)
    gw["w_down"] = mm("down_dw", _TN, [(ff, dy3, _bs((None, tbig, FF_BLK), lambda b, j, k: (b, k, 0)),
                                        _bs((tbig, 1024), lambda b, j, k: (k, j)))],
                      S((2, FF_BLK, D), F32), _bs((None, FF_BLK, 1024), lambda b, j, k: (b, 0, j)), (2, D // 1024, T // tbig))
    dU_out = [Out((2, T, FF_BLK), BF16, FF_BLK, lead=lambda o: o), Out((2, T, FF_BLK), BF16, FF_BLK, lead=lambda o: o)]
    (dUa, dUb), dffn = stage("convglu_bwd", f_convglu, ffn_ins, ffn_pars, ffn_out, outer=2, cts=[dff], dins=dU_out)
    gs["ffn_conv_w"] = jnp.concatenate([dffn[0][:2], dffn[1][2:]], axis=0)
    gs["ffn_conv_b"] = jnp.concatenate([dffn[2][:2], dffn[3][2:]], axis=0)
    dh3 = mm("up_dx", _NT, [(dUa, W["w_up"], _bs((None, tmm, FF_BLK), lambda i, j, k: (k, i, 0)),
                             _bs((None, 1024, FF_BLK), lambda i, j, k: (k, j, 0))),
                            (dUb, W["w_up"], _bs((None, tmm, FF_BLK), lambda i, j, k: (k, i, 0)),
                             _bs((None, 1024, FF_BLK), lambda i, j, k: (k + 2, j, 0)))],
             S((T, D), F32), _bs((tmm, 1024), lambda i, j, k: (i, j)), (nI, D // 1024, 2))

    def up_dw(name, dU):
        tt = min(2048, T)
        return mm(name, _TN, [(h3, dU, _bs((tt, 512), lambda b, i, k: (k, i)), _bs((None, tt, FF_BLK), lambda b, i, k: (b, k, 0)))],
                  S((2, D, FF_BLK), F32), _bs((None, 512, FF_BLK), lambda b, i, k: (b, i, 0)), (2, D // 512, T // tt))

    gw["w_up_a"], gw["w_up_b"] = up_dw("up_dw_a", dUa), up_dw("up_dw_b", dUb)
    hook("ffn", gw)
    (dx1, dy2), dpp2 = stage("post2_bwd", f_post_pre, [Tile(x1, D), Tile(y2, D)], pp2, [o2048(F32), o2048(BF16)],
                             cts=[dx2, dh3], dins=[o2048(F32), o2048(BF16)])
    gs["xa_post_norm"], gs["ffn_pre_norm"] = dpp2

    do2 = dense_t("xo_dx", dy2, W["w_xo"])
    gw["w_xo"] = wgrad("xo_dw", o2, dy2)
    (dq2,), (dk2, dv2) = stage("xattn_bwd", f_xattn, [Tile(q2, D)], xa_pars, [o2048(BF16)], cts=[do2], dins=[o2048(BF16)])
    dh2 = dense_t("xq_dx", dq2, W["w_xq"])
    gw["w_xq"] = wgrad("xq_dw", h2, dq2)
    dkv = jnp.concatenate([dk2, dv2], axis=1).astype(BF16)
    dmn = mm("xkv_dx", _NT, [(dkv, W["w_xkv"], _bs((M, 1024), lambda i, j, k: (0, k)), _bs((None, 512, 1024), lambda i, j, k: (k, j, 0)))],
             S((M, D), F32), _bs((M, 512), lambda i, j, k: (0, j)), (1, D // 512, 4))
    gw["w_xkv"] = mm("xkv_dw", _TN, [(mn, dkv, _bs((M, D), lambda b, j, k: (0, 0)), _bs((M, 1024), lambda b, j, k: (0, b)))],
                     S((4, D, 1024), F32), _bs((None, D, 1024), lambda b, j, k: (b, 0, 0)), (4, 1, 1))
    hook("xattn", gw)
    _, (gs["mem_norm"],) = stage("mem_norm_bwd", f_prenorm, [Tile(mem, D)], [Par(sp["mem_norm"])], [Out(mem.shape, BF16, D)],
                                 tm=M, cts=[dmn], dins=[None])
    (dx0, dy1), dpp1 = stage("post1_bwd", f_post_pre, [Tile(x, D), Tile(y1, D)], pp1, [o2048(F32), o2048(BF16)],
                             cts=[dx1, dh2], dins=[o2048(F32), o2048(BF16)])
    gs["mix_post_norm"], gs["xa_pre_norm"] = dpp1

    dmerged = dense_t("mix_out_dx", dy1, W["w_mix_out"])
    gw["w_mix_out"] = wgrad("mix_out_dw", merged, dy1)
    pshape = (T, N_PROJ)
    (dP2, dya, dyb), _ = stage("merge_bwd", f_merge, merge_ins, [], [o2048(BF16)], cts=[dmerged],
                               dins=[Out(pshape, BF16, 2 * D, cb=C_GATES // (2 * D)), o2048(BF16), o2048(BF16)])
    d_on = dense_t("branch_a_dx", dya, W["w_branch_a"])
    gw["w_branch_a"] = wgrad("branch_a_dw", o_n, dya)
    dpooled = mm("branch_b_dx", _NT, [(dyb, W["w_branch_b"], _bs((tmm, 512), lambda i, j, k: (i, k)),
                                       _bs((None, 1024, 512), lambda i, j, k: (k, 0, 0)))],
                 S((T, 1024), F32), _bs((tmm, 1024), lambda i, j, k: (i, 0)), (nI, 1, 4))
    gw["w_branch_b"] = mm("branch_b_dw", _TN, [(pooled, dyb, _bs((tmm, 1024), lambda b, j, k: (k, 0)), _bs((tmm, 512), lambda b, j, k: (k, b)))],
                          S((4, 1024, 512), F32), _bs((None, 1024, 512), lambda b, j, k: (b, 0, 0)), (4, 1, nI))
    hook("mixer", gw)
    (do, dP2), (gs["gdn_norm"],) = stage("gnorm_bwd", f_gnorm, [Tile(o, D), z_tile], [Par(sp["gdn_norm"])], [o2048(BF16)],
                                         cts=[d_on], dins=[o2048(F32), Out(pshape, BF16, D, cb=C_Z // D, into=dP2)])
    dq, dk, dv, dbg = gdn_backward(qkv[0], qkv[1], qkv[2], bg, s0, ninv, do, GDN_SPAN_BWD)
    dcw = []
    for i, (n, dqq) in enumerate(zip("qkv", (dq, dk, dv))):
        (dP2,), (dc,) = stage("conv_%s_bwd" % n, f_heads[i], [Tile(P2, D, cb=i, halo=True)], [Par(cws[i])], [o2048(F32)],
                              cts=[dqq], dins=[Out(pshape, BF16, D, cb=i, into=dP2)])
        dcw.append(dc)
    gs["conv_qkv"] = jnp.concatenate(dcw, axis=1)
    (dP2,), (gs["a_log"], gs["dt_bias"]) = stage("bg_bwd", f_bg, [ba_tile], [Par(sp["a_log"]), Par(sp["dt_bias"])],
                                                 [Out((T, 128), F32, 128)], cts=[dbg],
                                                 dins=[Out(pshape, BF16, 512, cb=C_BA // 512, into=dP2)])
    (dP2,), (gs["pool_w"], gs["pool_scale"]) = stage("pool_bwd", f_pool, [p_tile], pool_pars, [Out((T, 1024), BF16, 1024)],
                                                     cts=[dpooled], dins=[Out(pshape, BF16, 1024, cb=C_POOL // 1024, into=dP2)])
    tk, ta, tt = 2304, 1152, min(2048, T)
    gw["w_in"] = mm("in_proj_dw", _TN, [(dP2, h1, _bs((tt, ta), lambda i, j, k: (k, i)), _bs((tt, D), lambda i, j, k: (k, 0)))],
                    S((N_PROJ, D), F32), _bs((ta, D), lambda i, j, k: (i, 0)), (N_PROJ // ta, 1, T // tt))
    hook("in_proj", gw)
    dh1 = mm("in_proj_dx", _NN, [(dP2, W["w_in"], _bs((tbig, tk), lambda i, j, k: (i, k)), _bs((tk, D), lambda i, j, k: (k, 0)))],
             S((T, D), F32), _bs((tbig, D), lambda i, j, k: (i, 0)), (T // tbig, 1, N_PROJ // tk))
    (grad_x,), (gs["mix_pre_norm"],) = stage("pre1_bwd", f_prenorm_res, [Tile(x, D)], [Par(sp["mix_pre_norm"])],
                                             [o2048(F32), o2048(BF16)], cts=[dx0, dh1], dins=[o2048(F32)])
    return loss, grad_x, gw, gs


W_IN_COLS = 13344
GROUPED = {"w_branch_b": 512, "w_xkv": 1024}
ROW_SHARDED = ("w_branch_a", "w_mix_out", "w_xq", "w_xo")


def shard_to_slab(name, w):
    if name == "w_in":
        return w.T.astype(BF16)
    if name == "w_up":
        return jnp.pad(w, ((0, 0), (0, FF_BLK - FF_COLS))).astype(BF16)
    return w.astype(BF16)


def slabs_to_weight(name, g):
    if name == "w_in":
        full = g.astype(F32).reshape(W_IN_COLS, D_MODEL)
        pad = jnp.zeros((N_PROJ - W_IN_COLS, D_MODEL), F32)
        return jnp.concatenate([full[0:8192], full[9248:13344], full[8224:9248], full[8192:8224], pad]).astype(BF16)
    if name == "w_down":
        z = jnp.zeros((2, FF_BLK - FF_COLS, D_MODEL), g.dtype)
        return jnp.concatenate([g.reshape(2, FF_COLS, D_MODEL), z], axis=1)
    if name in ROW_SHARDED:
        return g.reshape(D_MODEL, D_MODEL)
    return g


def grad_to_slabs(name, gw):
    if name == "w_in":
        g = gw["w_in"]
        return jnp.concatenate([g[0:8192], g[13312:13344], g[12288:13312], g[8192:12288]]).reshape(4, 3336, D_MODEL)
    if name == "w_up":
        return jnp.concatenate([gw["w_up_a"], gw["w_up_b"]], axis=0)
    if name == "w_down":
        return gw["w_down"][:, :FF_COLS].reshape(4, FF_COLS // 2, D_MODEL)
    if name in ROW_SHARDED:
        return gw[name].reshape(4, D_MODEL // 4, D_MODEL)
    return gw[name]


def slab_to_shard_grad(name, f):
    if name == "w_in":
        return f.T
    if name == "w_up":
        return f[:, :FF_COLS]
    return f


MESH = pl.DeviceIdType.MESH
ANY = pl.BlockSpec(memory_space=pl.ANY)


def _me():
    x, y, c = lax.axis_index("x"), lax.axis_index("y"), lax.axis_index("c")
    return x, y, c, 2 * x + y


def _chip_dev(t, c):
    return (t // 2, t % 2, c)


def _rcopy(src, dst, ssem, rsem, dev):
    return pltpu.make_async_remote_copy(src_ref=src, dst_ref=dst, send_sem=ssem, recv_sem=rsem, device_id=dev, device_id_type=MESH)


def _handshake_all(x, y, c):
    barrier = pltpu.get_barrier_semaphore()
    for dx in (0, 1):
        for dy in (0, 1):
            for dc in (0, 1):
                if dx or dy or dc:
                    pl.semaphore_signal(barrier, inc=1, device_id=((x + dx) % 2, (y + dy) % 2, (c + dc) % 2), device_id_type=MESH)
    pl.semaphore_wait(barrier, 7)


def _comm_call(body, name, operands, out_shape, sems, collective_id):
    if collective_id is not None:
        return pl.kernel(body, out_type=out_shape, mesh=plsc.ScalarSubcoreMesh(axis_name="seq", num_cores=1), name=name,
                         scratch_types=sems, compiler_params=pltpu.CompilerParams(collective_id=collective_id))(*operands)
    n_in, n_out = len(operands), len(out_shape)
    return pl.pallas_call(body, name=name, in_specs=[ANY] * n_in, out_specs=[ANY] * n_out, out_shape=out_shape,
                          scratch_shapes=sems)(*operands)


def gather_weights(slabs, name="gather_weights", collective_id=None):
    n = len(slabs)

    def body(*refs):
        src, dst = refs[:n], refs[n:2 * n]
        ici_s, ici_r, fwd_s, fwd_r = refs[2 * n:]
        x, y, c, s = _me()
        if collective_id is not None:
            _handshake_all(x, y, c)
        sender = c == s // 2
        sib = (x, y, 1 - c)
        for r in (1, 2, 3):
            @pl.when(sender)
            def _(r=r):
                for w in range(n):
                    _rcopy(src[w], dst[w].at[s], ici_s.at[w, r - 1], ici_r.at[w, r - 1], _chip_dev(s ^ r, c)).start()
        for r in (1, 2, 3):
            t = s ^ r
            here = c == t // 2

            @pl.when(here)
            def _(r=r, t=t):
                for w in range(n):
                    _rcopy(src[w], dst[w].at[t], ici_s.at[w, r - 1], ici_r.at[w, r - 1], sib).wait_recv()
                    _rcopy(dst[w].at[t], dst[w].at[t], fwd_s.at[w, r - 1], fwd_r.at[w, r - 1], sib).start()

            @pl.when(jnp.logical_not(here))
            def _(r=r, t=t):
                for w in range(n):
                    _rcopy(dst[w].at[t], dst[w].at[t], fwd_s.at[w, r - 1], fwd_r.at[w, r - 1], sib).wait_recv()
        for r in (1, 2, 3):
            t = s ^ r

            @pl.when(sender)
            def _(r=r):
                for w in range(n):
                    _rcopy(src[w], dst[w].at[s], ici_s.at[w, r - 1], ici_r.at[w, r - 1], sib).wait_send()

            @pl.when(c == t // 2)
            def _(r=r, t=t):
                for w in range(n):
                    _rcopy(dst[w].at[t], dst[w].at[t], fwd_s.at[w, r - 1], fwd_r.at[w, r - 1], sib).wait_send()

    out_shape = [jax.ShapeDtypeStruct((4,) + a.shape, a.dtype) for a in slabs]
    sems = [pltpu.SemaphoreType.DMA((n, 3))] * 4
    return _comm_call(body, name, slabs, out_shape, sems, collective_id)


def pair_exchange(g4, name="pair_exchange", collective_id=None):
    n = len(g4)

    def body(*refs):
        src, dst = refs[:n], refs[n:2 * n]
        ssem, rsem = refs[2 * n:]
        x, y, c, s = _me()
        if collective_id is not None:
            _handshake_all(x, y, c)
        cps = [_rcopy(src[w].at[pl.ds(2 * (1 - c), 2)], dst[w], ssem.at[w], rsem.at[w], (x, y, 1 - c)) for w in range(n)]
        for cp in cps:
            cp.start()
        for cp in cps:
            cp.wait()

    return _comm_call(body, name, g4, [jax.ShapeDtypeStruct((2,) + a.shape[1:], a.dtype) for a in g4],
                      [pltpu.SemaphoreType.DMA((n,))] * 2, collective_id)


def _col_tile(R, C):
    for tc in (512, 256, 128):
        if C % tc == 0 and R * tc * 4 <= (4 << 20):
            return tc
    return 128


def pair_add(name, g4, gsib, c):
    _, R, C = g4.shape
    tc = _col_tile(R, C)

    def body(c_ref, a_ref, b_ref, of_ref, ob_ref):
        v = a_ref[...] + b_ref[...]
        of_ref[...] = v
        ob_ref[...] = v.astype(BF16)

    blk = lambda f: pl.BlockSpec((None, R, tc), f)
    return _pcall(
        body, (c.reshape(1).astype(jnp.int32), g4, gsib), name=name, num_scalar_prefetch=1, grid=(2, C // tc),
        in_specs=[blk(lambda p, j, cr: (2 * cr[0] + p, 0, j)), blk(lambda p, j, cr: (p, 0, j))],
        out_specs=[blk(lambda p, j, cr: (p, 0, j)), blk(lambda p, j, cr: (p, 0, j))],
        out_shape=[jax.ShapeDtypeStruct((2, R, C), F32), jax.ShapeDtypeStruct((2, R, C), BF16)],
        compiler_params=_cparams(("arbitrary", "arbitrary")),
    )


def scatter_partials(rb, name="scatter_partials", collective_id=None):
    n = len(rb)

    def body(*refs):
        src, dst = refs[:n], refs[n:2 * n]
        ssem, rsem = refs[2 * n:]
        x, y, c, s = _me()
        if collective_id is not None:
            _handshake_all(x, y, c)
        for r in (1, 2, 3):
            t = s ^ r

            @pl.when(t // 2 == c)
            def _(r=r, t=t):
                for w in range(n):
                    _rcopy(src[w].at[t % 2], dst[w].at[s], ssem.at[w, r - 1], rsem.at[w, r - 1], _chip_dev(t, c)).start()
        for r in (1, 2, 3):
            t = s ^ r

            @pl.when(s // 2 == c)
            def _(r=r, t=t):
                for w in range(n):
                    _rcopy(src[w].at[0], dst[w].at[t], ssem.at[w, r - 1], rsem.at[w, r - 1], _chip_dev(t, c)).wait_recv()
        for r in (1, 2, 3):
            t = s ^ r

            @pl.when(t // 2 == c)
            def _(r=r, t=t):
                for w in range(n):
                    _rcopy(src[w].at[t % 2], dst[w].at[s], ssem.at[w, r - 1], rsem.at[w, r - 1], _chip_dev(t, c)).wait_send()

    return _comm_call(body, name, rb, [jax.ShapeDtypeStruct((4,) + a.shape[1:], a.dtype) for a in rb],
                      [pltpu.SemaphoreType.DMA((n, 3))] * 2, collective_id)


def final_sum(name, rf, recv, s):
    _, R, C = rf.shape
    tc = _col_tile(R, C)

    def body(s_ref, own_ref, r0_ref, r1_ref, r2_ref, o_ref):
        o_ref[...] = ((own_ref[...] + r0_ref[...].astype(F32)) + r1_ref[...].astype(F32)) + r2_ref[...].astype(F32)

    blk = lambda f: pl.BlockSpec((None, R, tc), f)
    other = lambda k: (lambda j, sr: (k + (k >= sr[0]).astype(jnp.int32), 0, j))
    return _pcall(
        body, (s.reshape(1).astype(jnp.int32), rf, recv, recv, recv), name=name, num_scalar_prefetch=1, grid=(C // tc,),
        in_specs=[blk(lambda j, sr: (sr[0] % 2, 0, j)), blk(other(0)), blk(other(1)), blk(other(2))],
        out_specs=pl.BlockSpec((R, tc), lambda j, sr: (0, j)), out_shape=jax.ShapeDtypeStruct((R, C), F32),
        compiler_params=_cparams(("arbitrary",)),
    )


def share_with_sibling(fs, name="share_with_sibling", collective_id=None):
    n = len(fs)

    def body(*refs):
        src, dst = refs[:n], refs[n:2 * n]
        ssem, rsem = refs[2 * n:]
        x, y, c, s = _me()
        if collective_id is not None:
            _handshake_all(x, y, c)
        sib = (x, y, 1 - c)

        @pl.when(s // 2 == c)
        def _():
            cps = [_rcopy(src[w], dst[w], ssem.at[w], rsem.at[w], sib) for w in range(n)]
            for cp in cps:
                cp.start()
            for cp in cps:
                cp.wait_send()

        @pl.when(s // 2 != c)
        def _():
            for w in range(n):
                _rcopy(src[w], dst[w], ssem.at[w], rsem.at[w], sib).wait_recv()

    return _comm_call(body, name, fs, [jax.ShapeDtypeStruct(a.shape, a.dtype) for a in fs],
                      [pltpu.SemaphoreType.DMA((n,))] * 2, collective_id)


def pick(name, mine, a, b):
    R, C = a.shape
    tc = _col_tile(R, C)

    def body(m_ref, a_ref, b_ref, o_ref):
        o_ref[...] = jnp.where(m_ref[0] != 0, a_ref[...], b_ref[...])

    blk = pl.BlockSpec((R, tc), lambda j, mr: (0, j))
    return _pcall(body, (mine.reshape(1).astype(jnp.int32), a, b), name=name, num_scalar_prefetch=1, grid=(C // tc,),
                  in_specs=[blk, blk], out_specs=blk, out_shape=jax.ShapeDtypeStruct((R, C), a.dtype),
                  compiler_params=_cparams(("arbitrary",)))


def allgather_rows(v):
    m_per, ncol = v.shape

    def body(x_ref, out_ref, send_sems, recv_sems, local_sem):
        x, y, c = lax.axis_index("x"), lax.axis_index("y"), lax.axis_index("c")
        me, sibling = (x, y, c), (x, y, 1 - c)
        chips = [(1 - x, y), (x, 1 - y), (1 - x, 1 - y)]

        def rows(px, py, pc):
            return out_ref.at[pl.ds((4 * px + 2 * py + pc) * m_per, m_per), :]

        def copy(k, block, to, src=None):
            return _rcopy(rows(*block) if src is None else src, rows(*block), send_sems.at[k], recv_sems.at[k], to)

        mine = pltpu.make_async_copy(x_ref, rows(*me), local_sem)
        mine.start()
        first = [copy(0, me, sibling, src=x_ref)]
        first += [copy(1 + j, me, (*chip, c), src=x_ref) for j, chip in enumerate(chips)]
        for cp in first:
            cp.start()
        passed = [copy(4 + j, (*chip, c), sibling) for j, chip in enumerate(chips)]
        for j, chip in enumerate(chips):
            copy(1 + j, (*chip, c), me).wait_recv()
            passed[j].start()
        copy(0, sibling, me).wait_recv()
        for j, chip in enumerate(chips):
            copy(4 + j, (*chip, 1 - c), me).wait_recv()
        for cp in first + passed:
            cp.wait_send()
        mine.wait()

    return pl.pallas_call(
        body, name="allgather_rows", out_shape=jax.ShapeDtypeStruct((8 * m_per, ncol), v.dtype),
        in_specs=[pl.BlockSpec(memory_space=pltpu.VMEM)], out_specs=pl.BlockSpec(memory_space=pltpu.VMEM),
        scratch_shapes=[pltpu.SemaphoreType.DMA((7,)), pltpu.SemaphoreType.DMA((7,)), pltpu.SemaphoreType.DMA],
        compiler_params=pltpu.CompilerParams(vmem_limit_bytes=V7X_VMEM_LIMIT),
    )(v)


def allgather_rows_seq(v, name, collective_id):
    m_per, ncol = v.shape

    def body(x_ref, out_ref, send_sems, recv_sems):
        x, y, c = lax.axis_index("x"), lax.axis_index("y"), lax.axis_index("c")
        _handshake_all(x, y, c)
        me, sibling = (x, y, c), (x, y, 1 - c)
        chips = [(1 - x, y), (x, 1 - y), (1 - x, 1 - y)]

        def rows(px, py, pc):
            return out_ref.at[pl.ds((4 * px + 2 * py + pc) * m_per, m_per), :]

        def copy(k, block, to, src=None):
            return _rcopy(rows(*block) if src is None else src, rows(*block), send_sems.at[k], recv_sems.at[k], to)

        first = [copy(0, me, sibling, src=x_ref)]
        first += [copy(1 + j, me, (*chip, c), src=x_ref) for j, chip in enumerate(chips)]
        for cp in first:
            cp.start()
        passed = [copy(4 + j, (*chip, c), sibling) for j, chip in enumerate(chips)]
        for j, chip in enumerate(chips):
            copy(1 + j, (*chip, c), me).wait_recv()
            passed[j].start()
        copy(0, sibling, me).wait_recv()
        for j, chip in enumerate(chips):
            copy(4 + j, (*chip, 1 - c), me).wait_recv()
        for cp in first + passed:
            cp.wait_send()

    return _comm_call(body, name, [v], [jax.ShapeDtypeStruct((8 * m_per, ncol), v.dtype)],
                      [pltpu.SemaphoreType.DMA((7,)), pltpu.SemaphoreType.DMA((7,))], collective_id)[0]


def sum_blocks(name, a, nblk):
    m = a.shape[0] // nblk

    def body(a_ref, o_ref):
        acc = a_ref[pl.ds(0, m), :]
        for b in range(1, nblk):
            acc = acc + a_ref[pl.ds(b * m, m), :]
        o_ref[...] = acc

    return pl.pallas_call(body, name=name, out_shape=jax.ShapeDtypeStruct((m, a.shape[1]), a.dtype),
                          compiler_params=pltpu.CompilerParams(vmem_limit_bytes=V7X_VMEM_LIMIT))(a)


BIG = ("w_in", "w_branch_a", "w_branch_b", "w_mix_out", "w_xq", "w_xkv", "w_xo", "w_up", "w_down")
GAINS = ("mix_pre_norm", "gdn_norm", "pool_scale", "mix_post_norm", "xa_pre_norm", "mem_norm", "xa_post_norm",
         "ffn_pre_norm", "ffn_post_norm")
WEIGHTS = ("mix_pre_norm", "w_in", "conv_qkv", "a_log", "dt_bias", "gdn_norm", "pool_w", "pool_scale", "w_branch_a",
           "w_branch_b", "w_mix_out", "mix_post_norm", "xa_pre_norm", "mem_norm", "w_xq", "w_xkv", "w_xo", "xa_post_norm",
           "ffn_pre_norm", "w_up", "ffn_conv_w", "ffn_conv_b", "w_down", "ffn_post_norm")


RS_GROUPS = {"ffn": (("w_up", "w_down"), (3, 4, 5)), "xattn": (("w_xo", "w_xq", "w_xkv"), (6, 7, 8)),
             "mixer": (("w_mix_out", "w_branch_a", "w_branch_b"), (9, 10, 11)), "in_proj": (("w_in",), (12, 13, 14))}


def _rows128(vecs):
    flat = jnp.concatenate([v.reshape(-1) for v in vecs])
    m = -(-flat.shape[0] // 1024) * 8
    return jnp.pad(flat, (0, m * 128 - flat.shape[0])).reshape(m, 128)


def _unrows(a, shapes):
    flat, out, pos = a.reshape(-1), [], 0
    for sh in shapes:
        n = 1
        for d in sh:
            n *= d
        out.append(flat[pos:pos + n].reshape(sh))
        pos += n
    return out


def _lane128(v):
    return jnp.pad(v.reshape(1, GDN_HEADS), ((0, 0), (GDN_HEADS, 128 - 2 * GDN_HEADS)))


def kernel(x, mem, mix_pre_norm, w_in, conv_qkv, a_log, dt_bias, gdn_norm, pool_w, pool_scale, w_branch_a, w_branch_b, w_mix_out, mix_post_norm, xa_pre_norm, mem_norm, w_xq, w_xkv, w_xo, xa_post_norm, ffn_pre_norm, w_up, ffn_conv_w, ffn_conv_b, w_down, ffn_post_norm, loss_target, m_mix_pre_norm, m_w_in, m_conv_qkv, m_a_log, m_dt_bias, m_gdn_norm, m_pool_w, m_pool_scale, m_w_branch_a, m_w_branch_b, m_w_mix_out, m_mix_post_norm, m_xa_pre_norm, m_mem_norm, m_w_xq, m_w_xkv, m_w_xo, m_xa_post_norm, m_ffn_pre_norm, m_w_up, m_ffn_conv_w, m_ffn_conv_b, m_w_down, m_ffn_post_norm, v_mix_pre_norm, v_w_in, v_conv_qkv, v_a_log, v_dt_bias, v_gdn_norm, v_pool_w, v_pool_scale, v_w_branch_a, v_w_branch_b, v_w_mix_out, v_mix_post_norm, v_xa_pre_norm, v_mem_norm, v_w_xq, v_w_xkv, v_w_xo, v_xa_post_norm, v_ffn_pre_norm, v_w_up, v_ffn_conv_w, v_ffn_conv_b, v_w_down, v_ffn_post_norm):
    given = dict(locals())
    _LAST[0] = None
    w = {n: given[n][0] for n in WEIGHTS}
    cx, cy, cc = lax.axis_index("x"), lax.axis_index("y"), lax.axis_index("c")
    chip = 2 * cx + cy

    slabs = [shard_to_slab(n, w[n]) for n in BIG]
    first = gather_weights(slabs[:1], name="gather_w_in", collective_id=1)
    rest = gather_weights(slabs[1:], name="gather_rest", collective_id=2)
    stacks = [lax.dynamic_update_index_in_dim(g, sl, chip, 0) for g, sl in zip(list(first) + list(rest), slabs)]
    W = {n: slabs_to_weight(n, g) for n, g in zip(BIG, stacks)}
    sharded_small = (w["conv_qkv"], w["ffn_conv_w"], w["pool_w"])
    allv = allgather_rows(_rows128(sharded_small))
    per_chip = allv.reshape(8, -1)[0::2]
    parts = [_unrows(per_chip[t], [a.shape for a in sharded_small]) for t in range(4)]
    sp = {n: w[n].reshape(1, -1) for n in GAINS}
    sp["a_log"], sp["dt_bias"] = _lane128(w["a_log"]), _lane128(w["dt_bias"])
    sp["conv_qkv"] = jnp.concatenate([p[0] for p in parts], axis=1)
    sp["ffn_conv_w"] = jnp.pad(jnp.stack([p[1] for p in parts]), ((0, 0), (0, 0), (0, FF_BLK - FF_COLS)))
    sp["pool_w"] = jnp.concatenate([p[2] for p in parts], axis=1).reshape(4 * 256, 256)
    sp["ffn_conv_b"] = jnp.pad(w["ffn_conv_b"].reshape(4, 1, FF_COLS), ((0, 0), (0, 0), (0, FF_BLK - FF_COLS)))

    grads, summed, todo = {}, {}, []

    def rs_steps(names, ids, gw):
        g4 = [grad_to_slabs(n, gw) for n in names]
        tag = names[0]
        gsib = pair_exchange(g4, "pair_exchange_" + tag, ids[0])
        yield
        sums = [pair_add("pair_add_" + n, a, b, cc) for n, a, b in zip(names, g4, gsib)]
        recv = scatter_partials([sb for _, sb in sums], "scatter_partials_" + tag, ids[1])
        yield
        fin = [final_sum("final_sum_" + n, sf, rv, chip) for n, (sf, _), rv in zip(names, sums, recv)]
        got = share_with_sibling(fin, "share_" + tag, ids[2])
        yield
        for n, f, g in zip(names, fin, got):
            summed[n] = pick("pick_" + n, (chip // 2 == cc), f, g)
            grads[n] = slab_to_shard_grad(n, summed[n])

    def advance():
        for it in list(todo):
            if next(it, "done") == "done":
                todo.remove(it)

    def hook(event, gw):
        new = rs_steps(*RS_GROUPS[event], gw)
        next(new)
        advance()
        todo.append(new)
        if event == "in_proj":
            next(new)

    loss, grad_x, gw, gs = local_step(x[0], mem[0], loss_target[0], W, sp, hook)

    delta, new_m, new_v = {}, {}, {}

    def update(names):
        for n in names:
            shp = w[n].shape
            if n in ("w_in", "w_up"):
                gt = summed[n] if n == "w_in" else grads[n].T
                res = adamw("adamw_" + n, w[n].T, gt, given["m_" + n][0].T, given["v_" + n][0].T)
                res = [a.T for a in res]
            elif len(shp) == 2:
                res = adamw("adamw_" + n, given[n], grads[n], given["m_" + n], given["v_" + n])
            else:
                two = (lambda a: a.reshape(-1, shp[-1])) if len(shp) > 1 else (lambda a: a.reshape(1, -1))
                res = adamw("adamw_" + n, two(w[n]), two(grads[n]), two(given["m_" + n][0]), two(given["v_" + n][0]))
            delta[n], new_m[n], new_v[n] = (a.reshape((1,) + shp) for a in res)

    advance()
    small_names = GAINS + ("a_log", "dt_bias", "ffn_conv_b", "conv_qkv", "ffn_conv_w", "pool_w")
    vec = _rows128([gs[n] for n in small_names])
    allv = allgather_rows_seq(vec, "allgather_small_grads", 15)
    loss = lax.psum(loss[0, 0], ("x", "y", "c"))
    update([n for n in WEIGHTS if n in grads])
    allv = lax.dynamic_update_slice_in_dim(allv, vec, (2 * chip + cc) * vec.shape[0], axis=0)
    total = sum_blocks("sum_small", allv, 8)
    tot = dict(zip(small_names, _unrows(total, [gs[n].shape for n in small_names])))
    for n in GAINS:
        grads[n] = tot[n].reshape(-1)
    grads["a_log"] = tot["a_log"][0, GDN_HEADS:2 * GDN_HEADS]
    grads["dt_bias"] = tot["dt_bias"][0, GDN_HEADS:2 * GDN_HEADS]
    grads["ffn_conv_b"] = tot["ffn_conv_b"][:, 0, :FF_COLS].reshape(-1)
    grads["conv_qkv"] = lax.dynamic_slice_in_dim(tot["conv_qkv"], chip * 1536, 1536, axis=1)
    grads["ffn_conv_w"] = lax.dynamic_index_in_dim(tot["ffn_conv_w"], chip, axis=0, keepdims=False)[:, :FF_COLS]
    grads["pool_w"] = lax.dynamic_slice_in_dim(tot["pool_w"].reshape(4, 256, 256), chip * 64, 64, axis=1)

    update([n for n in WEIGHTS if n in grads and n not in delta])
    while todo:
        advance()
    update([n for n in WEIGHTS if n not in delta])
    out_g = [grads[n].reshape((1,) + w[n].shape) for n in WEIGHTS]
    return (loss, grad_x[None], *out_g, *[delta[n] for n in WEIGHTS], *[new_m[n] for n in WEIGHTS], *[new_v[n] for n in WEIGHTS])
```

```python
import functools

import jax
import jax.numpy as jnp
from jax import lax
from jax.experimental import pallas as pl
from jax.experimental.pallas import tpu as pltpu
from jax.experimental.pallas import tpu_sc as plsc

F32 = jnp.float32
BF16 = jnp.bfloat16

D_MODEL = 2048
CHUNK = 64
GDN_HEADS = 16
GDN_DK = 128
POOL_WINDOWS = (2, 4, 8, 16)
XA_HEADS = 4
XA_HEAD_DIM = D_MODEL // XA_HEADS
EPS = 1e-6
GDN_SPAN = 256
GDN_SPAN_BWD = 128
HALO = 16
V7X_VMEM_LIMIT = 56 * 1024 * 1024

ADAM_LR, ADAM_B1, ADAM_B2, ADAM_EPS, ADAM_WD, ADAM_STEP = 0.001, 0.9, 0.999, 1e-08, 0.01, 10

_NN = ((1,), (0,))
_NT = ((1,), (1,))
_TN = ((0,), (0,))


def _cparams(sem):
    return pltpu.CompilerParams(dimension_semantics=sem, vmem_limit_bytes=V7X_VMEM_LIMIT)


_LAST = [None]


def _pcall(body, operands, *, in_specs, out_specs, grid=(), num_scalar_prefetch=0, scratch_shapes=(), **kw):
    operands, in_specs = list(operands), list(in_specs)
    if _LAST[0] is not None and not any(o is _LAST[0] for o in operands):
        n = len(operands)
        inner = body

        def body(*refs):
            return inner(*refs[:n], *refs[n + 1:])

        operands.append(_LAST[0])
        in_specs.append(pl.BlockSpec(memory_space=pl.ANY))
    if num_scalar_prefetch:
        kw["grid_spec"] = pltpu.PrefetchScalarGridSpec(num_scalar_prefetch=num_scalar_prefetch, grid=grid, in_specs=in_specs,
                                                       out_specs=out_specs, scratch_shapes=list(scratch_shapes))
    else:
        kw.update(grid=grid, in_specs=in_specs, out_specs=out_specs, scratch_shapes=list(scratch_shapes))
    res = pl.pallas_call(body, **kw)(*operands)
    _LAST[0] = res[0] if isinstance(res, (list, tuple)) else res
    return res


def _dg(a, b, dims, prec=None):
    return lax.dot_general(a, b, (dims, ((), ())), precision=prec, preferred_element_type=F32)


def _make_dots(cast, prec):
    def raw(dims, a, b):
        return _dg(cast(a), cast(b), dims, prec)

    @jax.custom_vjp
    def nn(a, b):
        return raw(_NN, a, b)

    @jax.custom_vjp
    def nt(a, b):
        return raw(_NT, a, b)

    @jax.custom_vjp
    def tn(a, b):
        return raw(_TN, a, b)

    nn.defvjp(lambda a, b: (raw(_NN, a, b), (a, b)), lambda r, g: (nt(g, r[1]), tn(r[0], g)))
    nt.defvjp(lambda a, b: (raw(_NT, a, b), (a, b)), lambda r, g: (nn(g, r[1]), tn(g, r[0])))
    tn.defvjp(lambda a, b: (raw(_TN, a, b), (a, b)), lambda r, g: (nt(r[1], g), nn(r[0], g)))
    return nn, nt, tn


bdot_nn, bdot_nt, bdot_tn = _make_dots(lambda x: x.astype(BF16), None)
hdot_nn, hdot_nt, hdot_tn = _make_dots(lambda x: x, lax.Precision.HIGHEST)


@functools.partial(jax.custom_vjp, nondiff_argnums=(1,))
def shift_rows(x, k):
    return pltpu.roll(x, k, 0)


def _shift_rows_fwd(x, k):
    return pltpu.roll(x, k, 0), None


def _shift_rows_bwd(k, _, g):
    return (pltpu.roll(g, g.shape[0] - k, 0),)


shift_rows.defvjp(_shift_rows_fwd, _shift_rows_bwd)


@functools.partial(jax.custom_vjp, nondiff_argnums=(1,))
def drop_head(x, h):
    return x[h:]


def _drop_head_fwd(x, h):
    return x[h:], None


def _drop_head_bwd(h, _, g):
    return (jnp.concatenate([jnp.zeros((h,) + g.shape[1:], g.dtype), g], axis=0),)


drop_head.defvjp(_drop_head_fwd, _drop_head_bwd)


@functools.partial(jax.custom_vjp, nondiff_argnums=(1,))
def split_lanes(x, n):
    w = x.shape[-1] // n
    return tuple(x[:, i * w:(i + 1) * w] for i in range(n))


def _split_lanes_fwd(x, n):
    return split_lanes(x, n), None


def _split_lanes_bwd(n, _, gs):
    return (jnp.concatenate(list(gs), axis=-1),)


split_lanes.defvjp(_split_lanes_fwd, _split_lanes_bwd)


@functools.partial(jax.custom_vjp, nondiff_argnums=(1,))
def split_rows(x, n):
    h = x.shape[0] // n
    return tuple(x[i * h:(i + 1) * h] for i in range(n))


def _split_rows_fwd(x, n):
    return split_rows(x, n), None


def _split_rows_bwd(n, _, gs):
    return (jnp.concatenate(list(gs), axis=0),)


split_rows.defvjp(_split_rows_fwd, _split_rows_bwd)


def row_of(w, j):
    rid = lax.broadcasted_iota(jnp.int32, w.shape, 0)
    return jnp.sum(jnp.where(rid == j, w, 0.0), axis=0, keepdims=True)


def sigmoid(x):
    return 0.5 * jnp.tanh(0.5 * x) + 0.5


def silu(x):
    return x * sigmoid(x)


def softplus(x):
    return jnp.maximum(x, 0.0) + jnp.log(1.0 + jnp.exp(-jnp.abs(x)))


def rms(x, g):
    return x * lax.rsqrt(jnp.mean(x * x, axis=-1, keepdims=True) + EPS) * g


def mm(name, dims, pairs, out_shape, out_spec, grid):
    nk = grid[2]
    acc_in_out = nk > 1 and out_shape.dtype == F32
    npair = len(pairs)

    def body(*refs):
        o_ref = refs[2 * npair]
        part = None
        for p in range(npair):
            d = _dg(refs[2 * p][...].astype(BF16), refs[2 * p + 1][...].astype(BF16), dims)
            part = d if part is None else part + d
        if nk == 1:
            o_ref[...] = part.astype(o_ref.dtype)
            return
        acc = o_ref if acc_in_out else refs[2 * npair + 1]
        k = pl.program_id(2)

        @pl.when(k == 0)
        def _():
            acc[...] = part

        @pl.when(k > 0)
        def _():
            acc[...] += part

        if not acc_in_out:
            @pl.when(k == nk - 1)
            def _():
                o_ref[...] = acc[...].astype(o_ref.dtype)

    scratch = []
    if nk > 1 and not acc_in_out:
        scratch = [pltpu.VMEM(tuple(d for d in out_spec.block_shape if d is not None), F32)]
    in_specs, operands = [], []
    for a, b, a_spec, b_spec in pairs:
        in_specs += [a_spec, b_spec]
        operands += [a, b]
    return _pcall(body, operands, name=name, grid=grid, in_specs=in_specs, out_specs=out_spec, out_shape=out_shape,
                  scratch_shapes=scratch, compiler_params=_cparams(("parallel", "parallel", "arbitrary")))


def _bs(shape, fn):
    return pl.BlockSpec(shape, fn)


class Tile:
    def __init__(self, arr, w, cb=0, lead=None, halo=False):
        self.arr, self.w, self.cb, self.lead, self.halo = arr, w, cb, lead, halo


class Out:
    def __init__(self, shape, dtype, w, cb=0, lead=None, into=None):
        self.shape, self.dtype, self.w, self.cb, self.lead, self.into = shape, dtype, w, cb, lead, into


class Par:
    def __init__(self, arr, lead=None):
        self.arr, self.lead = arr, lead


def _spec(rows, w, cb, lead, tile_of):
    if lead is None:
        return pl.BlockSpec((rows, w), lambda o, i: (tile_of(i), cb))
    return pl.BlockSpec((None, rows, w), lambda o, i: (lead(o), tile_of(i), cb))


def _par_spec(p):
    if p.lead is None:
        return pl.BlockSpec(p.arr.shape, lambda o, i: (0, 0))
    return pl.BlockSpec((None,) + p.arr.shape[1:], lambda o, i: (p.lead(o), 0, 0))


def run_stage(name, fn, tm, ins, pars, outs, *, outer=1, cts=None, dins=None):
    T = ins[0].arr.shape[-2]
    nt = T // tm
    bwd = cts is not None
    any_halo = any(t.halo for t in ins)
    step_tile = (lambda i: nt - 1 - i) if bwd else (lambda i: i)
    hb = tm // HALO

    in_specs, operands = [], []
    for t in ins:
        if t.halo:
            in_specs.append(_spec(HALO, t.w, t.cb, t.lead, lambda i: jnp.maximum(step_tile(i) * hb - 1, 0)))
            operands.append(t.arr)
        in_specs.append(_spec(tm, t.w, t.cb, t.lead, step_tile))
        operands.append(t.arr)
    for p in pars:
        in_specs.append(_par_spec(p))
        operands.append(p.arr)
    n_in_refs = len(operands)

    out_descs = list(outs) if not bwd else [d for d in dins if d is not None]
    aliases = {}
    if bwd:
        for c, o in zip(cts, outs):
            in_specs.append(_spec(tm, o.w, o.cb, o.lead, step_tile))
            operands.append(c)
    n_ct = len(operands) - n_in_refs
    for k, o in enumerate(out_descs):
        if o.into is not None:
            aliases[len(operands)] = k
            in_specs.append(pl.BlockSpec(memory_space=pl.ANY))
            operands.append(o.into)
    out_specs = [_spec(tm, o.w, o.cb, o.lead, step_tile) for o in out_descs]
    out_shapes = [jax.ShapeDtypeStruct(o.shape, o.dtype) for o in out_descs]
    if bwd:
        for p in pars:
            out_specs.append(_par_spec(p))
            out_shapes.append(jax.ShapeDtypeStruct(p.arr.shape, F32))
    scratch = []
    if bwd and any_halo:
        scratch = [pltpu.VMEM((HALO, t.w), F32) for t, d in zip(ins, dins) if t.halo and d is not None]

    def body(*refs):
        i = pl.program_id(1)
        tile = step_tile(i)
        row0 = tile * tm
        pos = 0
        tiles = []
        for t in ins:
            if t.halo:
                prev = jnp.where(tile > 0, refs[pos][...].astype(F32), 0.0)
                tiles.append(jnp.concatenate([prev, refs[pos + 1][...].astype(F32)], axis=0))
                pos += 2
            else:
                tiles.append(refs[pos][...].astype(F32))
                pos += 1
        pvals = [refs[pos + k][...].astype(F32) for k in range(len(pars))]
        pos += len(pars)
        if not bwd:
            res = fn(tiles, pvals, row0)
            for o_ref, r in zip(refs[pos:], res):
                o_ref[...] = r.astype(o_ref.dtype)
            return
        ct_vals = [refs[pos + k][...].astype(F32) for k in range(n_ct)]
        pos += n_ct + len(aliases)
        _, vjp_fn = jax.vjp(lambda tt, pp: fn(tt, pp, row0), tiles, pvals)
        d_tiles, d_pars = vjp_fn(ct_vals)
        carries = list(refs[len(refs) - len(scratch):])
        for t, d, dt in zip(ins, dins, d_tiles):
            if d is None:
                continue
            o_ref = refs[pos]
            pos += 1
            if t.halo:
                carry = carries.pop(0)
                main = dt[HALO:]
                tail = main[tm - HALO:] + jnp.where(i > 0, carry[...], 0.0)
                o_ref[...] = jnp.concatenate([main[:tm - HALO], tail], axis=0).astype(o_ref.dtype)
                carry[...] = dt[:HALO]
            else:
                o_ref[...] = dt.astype(o_ref.dtype)
        for dp in d_pars:
            acc = refs[pos]
            pos += 1

            @pl.when(i == 0)
            def _(acc=acc, dp=dp):
                acc[...] = dp

            @pl.when(i > 0)
            def _(acc=acc, dp=dp):
                acc[...] += dp

    res = _pcall(body, operands, name=name, grid=(outer, nt), in_specs=in_specs, out_specs=out_specs, out_shape=out_shapes,
                 scratch_shapes=scratch, input_output_aliases=aliases, compiler_params=_cparams(("arbitrary", "arbitrary")))
    if not bwd:
        return list(res)
    n_d = len(out_descs)
    d_full, it = [], iter(res[:n_d])
    for d in dins:
        d_full.append(None if d is None else next(it))
    return d_full, list(res[n_d:])


def f_prenorm(t, p, row0):
    return [rms(t[0], p[0])]


def f_prenorm_res(t, p, row0):
    return [t[0], rms(t[0], p[0])]


def f_post_pre(t, p, row0):
    x, y = t
    x1 = x + rms(y, p[0])
    return [x1, rms(x1, p[1])]


def _causal_conv(x, w, taps):
    y = x * row_of(w, taps - 1)
    for j in range(taps - 1):
        y = y + shift_rows(x, taps - 1 - j) * row_of(w, j)
    return drop_head(y, HALO)


def _l2(x):
    return x * lax.rsqrt(jnp.sum(x * x, axis=-1, keepdims=True) + EPS)


def make_f_convhead(scale, normalise):
    def f(t, p, row0):
        y = silu(_causal_conv(t[0], p[0], 4))
        if not normalise:
            return [y]
        return [jnp.concatenate([_l2(c) * scale for c in split_lanes(y, GDN_HEADS)], axis=-1)]
    return f


def f_bg(t, p, row0):
    ba = split_lanes(t[0], 4)[0]
    alog, dtb = p
    lane = lax.broadcasted_iota(jnp.int32, ba.shape, 1)
    bg = jnp.where(lane < GDN_HEADS, sigmoid(ba), -jnp.exp(alog) * softplus(ba + dtb))
    return [jnp.where(lane < 2 * GDN_HEADS, bg, 0.0)]


def f_gnorm(t, p, row0):
    o, z = t
    po, pz = split_lanes(o, GDN_HEADS), split_lanes(z, GDN_HEADS)
    return [jnp.concatenate([rms(a, p[0]) * silu(b) for a, b in zip(po, pz)], axis=-1)]


def f_pool(t, p, row0):
    x = t[0]
    pw, psc = p
    tm = x.shape[0] - HALO
    tpos = (row0 + lax.broadcasted_iota(jnp.int32, (tm, 1), 0) + 1).astype(F32)
    outs = []
    for xg, wg, win in zip(split_lanes(x, 4), split_rows(pw, 4), POOL_WINDOWS):
        s, span = xg, 1
        while span < win:
            s = s + shift_rows(s, span)
            span *= 2
        mean = drop_head(s, HALO) / jnp.minimum(tpos, float(win))
        outs.append(bdot_nn(mean - drop_head(xg, HALO), wg))
    return [jnp.concatenate(outs, axis=-1) * psc]


def f_merge(t, p, row0):
    gates, ya, yb = t
    ga, gb = split_lanes(gates, 2)
    return [sigmoid(ga) * ya + sigmoid(gb) * yb]


def f_xattn(t, p, row0):
    k, v = p
    outs = []
    for qh, kh, vh in zip(split_lanes(t[0], XA_HEADS), split_lanes(k, XA_HEADS), split_lanes(v, XA_HEADS)):
        s = bdot_nt(qh, kh) * (XA_HEAD_DIM ** -0.5)
        s = s - jnp.max(s, axis=-1, keepdims=True)
        e = jnp.exp(s)
        outs.append(bdot_nn(e / jnp.sum(e, axis=-1, keepdims=True), vh))
    return [jnp.concatenate(outs, axis=-1)]


def f_convglu(t, p, row0):
    ua, ub = t
    cwa, cwb, ba, bb = p
    return [silu(_causal_conv(ua, cwa, 3) + ba) * (_causal_conv(ub, cwb, 3) + bb)]


_BNN = (((2,), (1,)), ((0,), (0,)))
_BNT = (((2,), (2,)), ((0,), (0,)))
_BTN = (((1,), (1,)), ((0,), (0,)))


def _make_batched_dots():
    def raw(dims, a, b):
        return lax.dot_general(a.astype(BF16), b.astype(BF16), dims, preferred_element_type=F32)

    @jax.custom_vjp
    def nn(a, b):
        return raw(_BNN, a, b)

    @jax.custom_vjp
    def nt(a, b):
        return raw(_BNT, a, b)

    @jax.custom_vjp
    def tn(a, b):
        return raw(_BTN, a, b)

    nn.defvjp(lambda a, b: (raw(_BNN, a, b), (a, b)), lambda r, g: (nt(g, r[1]), tn(r[0], g)))
    nt.defvjp(lambda a, b: (raw(_BNT, a, b), (a, b)), lambda r, g: (nn(g, r[1]), tn(g, r[0])))
    tn.defvjp(lambda a, b: (raw(_BTN, a, b), (a, b)), lambda r, g: (nt(r[1], g), nn(r[0], g)))
    return nn, nt, tn


bb_nn, bb_nt, bb_tn = _make_batched_dots()


def _inverse_correction(X):
    N, P = X, X
    for _ in range(5):
        P = bb_nn(P, P)
        N = N + P + bb_nn(N, P)
    return N


@jax.custom_vjp
def _saved_inverse_correction(X, N):
    return N


def _saved_inverse_fwd(X, N):
    return N, N


def _saved_inverse_bwd(N, G):
    t = G + bb_tn(N, G)
    return t + bb_nt(t, N), jnp.zeros_like(N)


_saved_inverse_correction.defvjp(_saved_inverse_fwd, _saved_inverse_bwd)


def _gdn_local(q, k, v, gcol, bcol, n_saved=None):
    C = CHUNK
    r = lax.broadcasted_iota(jnp.int32, (1, C, C), 1)
    c = lax.broadcasted_iota(jnp.int32, (1, C, C), 2)
    eye, incl, strict = r == c, r >= c, r > c
    grow = jnp.sum(jnp.where(eye, gcol, 0.0), axis=1, keepdims=True)
    Gcol = jnp.sum(jnp.where(incl, grow, 0.0), axis=2, keepdims=True)
    Grow = jnp.sum(jnp.where(eye, Gcol, 0.0), axis=1, keepdims=True)
    decay = jnp.where(incl, jnp.exp(jnp.where(incl, Gcol - Grow, 0.0)), 0.0)
    X = -jnp.where(strict, bcol * decay * bb_nt(k, k), 0.0)
    N = _inverse_correction(X) if n_saved is None else _saved_inverse_correction(X, n_saved)
    expg = jnp.exp(Gcol)
    rv, rk = bcol * v, (bcol * expg) * k
    u_v = rv + bb_nn(N, rv)
    w_k = rk + bb_nn(N, rk)
    attn = decay * bb_nt(q, k)
    rid = lax.broadcasted_iota(jnp.int32, (1, C, 1), 1)
    glast = jnp.sum(jnp.where(rid == C - 1, Gcol, 0.0), axis=1, keepdims=True)
    return (u_v, w_k, attn, q * expg, k * jnp.exp(glast - Gcol), jnp.exp(glast)), N


def _gdn_rec(u_v, w_k, attn, q_dec, k_dec, cd, S):
    u = u_v - bb_nn(w_k, S)
    o = bb_nn(q_dec, S) + bb_nn(attn, u)
    return o, cd * S + bb_tn(k_dec, u)


def _gdn_load(q_ref, k_ref, v_ref, bg_ref, nch):
    H = GDN_HEADS

    def batched(ref):
        return jnp.stack([ref[c * CHUNK:(c + 1) * CHUNK, h * GDN_DK:(h + 1) * GDN_DK] for c in range(nch) for h in range(H)])

    bg = bg_ref[...]
    lane = lax.broadcasted_iota(jnp.int32, bg.shape, 1)
    bcols = [jnp.sum(jnp.where(lane == h, bg, 0.0), axis=-1, keepdims=True) for h in range(H)]
    gcols = [jnp.sum(jnp.where(lane == H + h, bg, 0.0), axis=-1, keepdims=True) for h in range(H)]
    pick = lambda cols: jnp.stack([cols[h][c * CHUNK:(c + 1) * CHUNK] for c in range(nch) for h in range(H)])
    return batched(q_ref), batched(k_ref), batched(v_ref), pick(gcols), pick(bcols)


def _gdn_store(ref, val, nch):
    H = GDN_HEADS
    for c in range(nch):
        for h in range(H):
            ref[c * CHUNK:(c + 1) * CHUNK, h * GDN_DK:(h + 1) * GDN_DK] = val[c * H + h]


def gdn_forward(q, k, v, bg, span):
    T = q.shape[0]
    ns, nch, H = T // span, span // CHUNK, GDN_HEADS

    def body(q_ref, k_ref, v_ref, bg_ref, o_ref, s_ref, n_ref, state):
        @pl.when(pl.program_id(0) == 0)
        def _():
            state[...] = jnp.zeros(state.shape, F32)

        loc, n_ref[...] = _gdn_local(*_gdn_load(q_ref, k_ref, v_ref, bg_ref, nch))
        S = state[...]
        for c in range(nch):
            s_ref[c] = S
            o, S = _gdn_rec(*[a[c * H:(c + 1) * H] for a in loc], S)
            for h in range(H):
                o_ref[c * CHUNK:(c + 1) * CHUNK, h * GDN_DK:(h + 1) * GDN_DK] = o[h]
        state[...] = S

    wide = pl.BlockSpec((span, H * GDN_DK), lambda s: (s, 0))
    return _pcall(
        body, (q, k, v, bg), name="gdn_fwd", grid=(ns,),
        in_specs=[wide, wide, wide, pl.BlockSpec((span, 128), lambda s: (s, 0))],
        out_specs=[wide, pl.BlockSpec((nch, H, GDN_DK, GDN_DK), lambda s: (s, 0, 0, 0)),
                   pl.BlockSpec((nch * H, CHUNK, CHUNK), lambda s: (s, 0, 0))],
        out_shape=[jax.ShapeDtypeStruct((T, H * GDN_DK), F32), jax.ShapeDtypeStruct((T // CHUNK, H, GDN_DK, GDN_DK), F32),
                   jax.ShapeDtypeStruct((T // CHUNK * H, CHUNK, CHUNK), F32)],
        scratch_shapes=[pltpu.VMEM((H, GDN_DK, GDN_DK), F32)],
        compiler_params=_cparams(("arbitrary",)),
    )


def gdn_backward(q, k, v, bg, starts, ninv, do, span):
    T = q.shape[0]
    ns, nch, H = T // span, span // CHUNK, GDN_HEADS

    def body(q_ref, k_ref, v_ref, bg_ref, s_ref, n_ref, do_ref, dq_ref, dk_ref, dv_ref, dbg_ref, dstate):
        @pl.when(pl.program_id(0) == 0)
        def _():
            dstate[...] = jnp.zeros(dstate.shape, F32)

        (loc, _), vjp_loc = jax.vjp(_gdn_local, *_gdn_load(q_ref, k_ref, v_ref, bg_ref, nch), n_ref[...])
        dS = dstate[...]
        dloc = [None] * nch
        for c in reversed(range(nch)):
            _, vjp_rec = jax.vjp(_gdn_rec, *[a[c * H:(c + 1) * H] for a in loc], s_ref[c])
            do_c = jnp.stack([do_ref[c * CHUNK:(c + 1) * CHUNK, h * GDN_DK:(h + 1) * GDN_DK] for h in range(H)])
            *dloc[c], dS = vjp_rec((do_c, dS))
        dstate[...] = dS
        d_loc = tuple(jnp.concatenate([dloc[c][i] for c in range(nch)], axis=0) for i in range(6))
        dq, dk, dv, dg, db, _ = vjp_loc((d_loc, jnp.zeros((nch * H, CHUNK, CHUNK), F32)))
        _gdn_store(dq_ref, dq, nch)
        _gdn_store(dk_ref, dk, nch)
        _gdn_store(dv_ref, dv, nch)
        lane = lax.broadcasted_iota(jnp.int32, (CHUNK, 128), 1)
        for c in range(nch):
            acc = jnp.zeros((CHUNK, 128), F32)
            for h in range(H):
                acc = acc + jnp.where(lane == h, db[c * H + h], 0.0) + jnp.where(lane == H + h, dg[c * H + h], 0.0)
            dbg_ref[c * CHUNK:(c + 1) * CHUNK, :] = acc

    wide = pl.BlockSpec((span, H * GDN_DK), lambda i: (ns - 1 - i, 0))
    bgs = pl.BlockSpec((span, 128), lambda i: (ns - 1 - i, 0))
    return _pcall(
        body, (q, k, v, bg, starts, ninv, do), name="gdn_bwd", grid=(ns,),
        in_specs=[wide, wide, wide, bgs, pl.BlockSpec((nch, H, GDN_DK, GDN_DK), lambda i: (ns - 1 - i, 0, 0, 0)),
                  pl.BlockSpec((nch * H, CHUNK, CHUNK), lambda i: (ns - 1 - i, 0, 0)), wide],
        out_specs=[wide, wide, wide, bgs],
        out_shape=[jax.ShapeDtypeStruct((T, H * GDN_DK), F32)] * 3 + [jax.ShapeDtypeStruct((T, 128), F32)],
        scratch_shapes=[pltpu.VMEM((H, GDN_DK, GDN_DK), F32)],
        compiler_params=_cparams(("arbitrary",)),
    )


def loss_stage(x2, y3, tgt, g, tm):
    T, D = x2.shape
    nt = T // tm

    def body(x_ref, y_ref, t_ref, g_ref, loss_ref, dx_ref, dy_ref, dg_ref):
        i = pl.program_id(0)
        tgtv = t_ref[...]

        def f(x, y, gg):
            err = x + rms(y, gg) - tgtv
            return 0.5 * jnp.mean(err * err, axis=-1, keepdims=True)

        rows, vjp_fn = jax.vjp(f, x_ref[...], y_ref[...], g_ref[...])
        dx, dy, dg = vjp_fn(jnp.ones_like(rows))
        dx_ref[...] = dx
        dy_ref[...] = dy.astype(dy_ref.dtype)
        part = jnp.sum(rows, axis=0, keepdims=True)

        @pl.when(i == 0)
        def _():
            loss_ref[...] = part
            dg_ref[...] = dg

        @pl.when(i > 0)
        def _():
            loss_ref[...] += part
            dg_ref[...] += dg

    tile = pl.BlockSpec((tm, D), lambda i: (i, 0))
    gs = pl.BlockSpec((1, D), lambda i: (0, 0))
    return _pcall(
        body, (x2, y3, tgt, g), name="loss_head", grid=(nt,), in_specs=[tile, tile, tile, gs],
        out_specs=[pl.BlockSpec((1, 1), lambda i: (0, 0)), tile, tile, gs],
        out_shape=[jax.ShapeDtypeStruct((1, 1), F32), jax.ShapeDtypeStruct((T, D), F32),
                   jax.ShapeDtypeStruct((T, D), BF16), jax.ShapeDtypeStruct((1, D), F32)],
        compiler_params=_cparams(("arbitrary",)),
    )


def adamw(name, w, g, m, v):
    R, C = g.shape
    layered = w.ndim == 3
    tr = R
    for cand in (256, 128, 64, 32, 16, 8):
        if R % cand == 0 and R > cand and cand * C * 4 <= (2 << 20):
            tr = cand
            break
    c1 = 1.0 / (1.0 - ADAM_B1 ** ADAM_STEP)
    c2 = 1.0 / (1.0 - ADAM_B2 ** ADAM_STEP)

    def body(w_ref, g_ref, m_ref, v_ref, d_ref, nm_ref, nv_ref):
        gg = g_ref[...]
        nm = ADAM_B1 * m_ref[...] + (1.0 - ADAM_B1) * gg
        nv = ADAM_B2 * v_ref[...] + (1.0 - ADAM_B2) * (gg * gg)
        d_ref[...] = -ADAM_LR * ((nm * c1) / (jnp.sqrt(nv * c2) + ADAM_EPS) + ADAM_WD * w_ref[...])
        nm_ref[...] = nm
        nv_ref[...] = nv

    gspec = pl.BlockSpec((tr, C), lambda i: (i, 0))
    spec = pl.BlockSpec((None, tr, C), lambda i: (0, i, 0)) if layered else gspec
    steps = R // tr
    if tr < 64 and R > 512 and C % 128 == 0 and not layered:
        gspec = spec = pl.BlockSpec((R, 128), lambda i: (0, i))
        steps = C // 128
    return _pcall(
        body, (w, g, m, v), name=name, grid=(steps,), in_specs=[spec, gspec, spec, spec], out_specs=[spec] * 3,
        out_shape=[jax.ShapeDtypeStruct(w.shape, F32)] * 3, compiler_params=_cparams(("parallel",)),
    )


C_Z, C_GATES, C_POOL, C_BA, N_PROJ = 6144, 8192, 12288, 13312, 13824
FF_BLK = 2816
FF_COLS = 2752


def local_step(x, mem, tgt, W, sp, hook=lambda event, gw: None):
    T, D = x.shape
    tm = 256
    tmm = min(512, T)
    nI = T // tmm
    S = jax.ShapeDtypeStruct
    gw, gs = {}, {}

    def stage(name, fn, ins, pars, outs, **kw):
        return run_stage(name, fn, kw.pop("tm", tm), ins, pars, outs, **kw)

    def dense(name, a, w, out_dtype=F32, tn=2048):
        Tq, Kd = a.shape
        N = w.shape[1]
        tq = min(tmm, Tq)
        return mm(name, _NN, [(a, w, _bs((tq, Kd), lambda j, i, k: (i, 0)), _bs((Kd, tn), lambda j, i, k: (0, j)))],
                  S((Tq, N), out_dtype), _bs((tq, tn), lambda j, i, k: (i, j)), (N // tn, Tq // tq, 1))

    def dense_t(name, g, w, out_dtype=F32, tn=2048):
        Tq, N = g.shape
        Kd = w.shape[0]
        tq = min(tmm, Tq)
        return mm(name, _NT, [(g, w, _bs((tq, N), lambda j, i, k: (i, 0)), _bs((tn, N), lambda j, i, k: (j, 0)))],
                  S((Tq, Kd), out_dtype), _bs((tq, tn), lambda j, i, k: (i, j)), (Kd // tn, Tq // tq, 1))

    def wgrad(name, a, g, ta=1024, tn=2048):
        Tq, Kd = a.shape
        N = g.shape[1]
        tt = min(2048, Tq)
        return mm(name, _TN, [(a, g, _bs((tt, ta), lambda i, j, k: (k, i)), _bs((tt, tn), lambda i, j, k: (k, j)))],
                  S((Kd, N), F32), _bs((ta, tn), lambda i, j, k: (i, j)), (Kd // ta, N // tn, Tq // tt))

    o2048 = lambda dt: Out((T, D), dt, D)

    (h1,) = stage("pre1", f_prenorm, [Tile(x, D)], [Par(sp["mix_pre_norm"])], [o2048(BF16)])
    tnp, tmp = 2304, min(1024, T)
    P2 = mm("in_proj", _NT, [(h1, W["w_in"], _bs((tmp, D), lambda j, i, k: (i, 0)), _bs((tnp, D), lambda j, i, k: (j, 0)))],
            S((T, N_PROJ), F32), _bs((tmp, tnp), lambda j, i, k: (i, j)), (N_PROJ // tnp, T // tmp, 1))
    cw = sp["conv_qkv"]
    cws = [cw[:, i * D:(i + 1) * D] for i in range(3)]
    f_heads = [make_f_convhead(GDN_DK ** -0.5, True), make_f_convhead(1.0, True), make_f_convhead(1.0, False)]
    qkv = [stage("conv_" + n, f_heads[i], [Tile(P2, D, cb=i, halo=True)], [Par(cws[i])], [o2048(F32)])[0]
           for i, n in enumerate("qkv")]
    ba_tile = Tile(P2, 512, cb=C_BA // 512)
    (bg,) = stage("bg", f_bg, [ba_tile], [Par(sp["a_log"]), Par(sp["dt_bias"])], [Out((T, 128), F32, 128)])
    o, s0, ninv = gdn_forward(qkv[0], qkv[1], qkv[2], bg, GDN_SPAN)
    z_tile = Tile(P2, D, cb=C_Z // D)
    (o_n,) = stage("gnorm", f_gnorm, [Tile(o, D), z_tile], [Par(sp["gdn_norm"])], [o2048(BF16)])
    y_a = dense("branch_a", o_n, W["w_branch_a"])
    p_tile = Tile(P2, 1024, cb=C_POOL // 1024, halo=True)
    pool_pars = [Par(sp["pool_w"]), Par(sp["pool_scale"])]
    (pooled,) = stage("pool", f_pool, [p_tile], pool_pars, [Out((T, 1024), BF16, 1024)])
    y_b = mm("branch_b", _NN, [(pooled, W["w_branch_b"], _bs((tmm, 1024), lambda j, i, k: (i, 0)),
                                _bs((None, 1024, 512), lambda j, i, k: (j, 0, 0)))],
             S((T, D), F32), _bs((tmm, 512), lambda j, i, k: (i, j)), (4, nI, 1))
    gate_tile = Tile(P2, 2 * D, cb=C_GATES // (2 * D))
    merge_ins = [gate_tile, Tile(y_a, D), Tile(y_b, D)]
    (merged,) = stage("merge", f_merge, merge_ins, [], [o2048(BF16)])
    y1 = dense("mix_out", merged, W["w_mix_out"])
    pp1 = [Par(sp["mix_post_norm"]), Par(sp["xa_pre_norm"])]
    x1, h2 = stage("post1", f_post_pre, [Tile(x, D), Tile(y1, D)], pp1, [o2048(F32), o2048(BF16)])

    q2 = dense("xq", h2, W["w_xq"], out_dtype=BF16)
    (mn,) = stage("mem_norm", f_prenorm, [Tile(mem, D)], [Par(sp["mem_norm"])], [Out(mem.shape, BF16, D)], tm=mem.shape[0])
    M = mem.shape[0]
    kv = mm("xkv", _NN, [(mn, W["w_xkv"], _bs((M, D), lambda j, i, k: (0, 0)), _bs((None, D, 1024), lambda j, i, k: (j, 0, 0)))],
            S((M, 2 * D), F32), _bs((M, 1024), lambda j, i, k: (0, j)), (4, 1, 1))
    k2, v2 = kv[:, :D], kv[:, D:]
    xa_pars = [Par(k2), Par(v2)]
    (o2,) = stage("xattn", f_xattn, [Tile(q2, D)], xa_pars, [o2048(BF16)])
    y2 = dense("xo", o2, W["w_xo"])
    pp2 = [Par(sp["xa_post_norm"]), Par(sp["ffn_pre_norm"])]
    x2, h3 = stage("post2", f_post_pre, [Tile(x1, D), Tile(y2, D)], pp2, [o2048(F32), o2048(BF16)])

    def up(name, off):
        return mm(name, _NN, [(h3, W["w_up"], _bs((tmm, D), lambda j, i, k: (i, 0)),
                               _bs((None, D, FF_BLK), lambda j, i, k: (j + off, 0, 0)))],
                  S((2, T, FF_BLK), F32), _bs((None, tmm, FF_BLK), lambda j, i, k: (j, i, 0)), (2, nI, 1))

    Ua, Ub = up("up_a", 0), up("up_b", 2)
    ffn_ins = [Tile(Ua, FF_BLK, lead=lambda o: o, halo=True), Tile(Ub, FF_BLK, lead=lambda o: o, halo=True)]
    ffn_pars = [Par(sp["ffn_conv_w"], lead=lambda o: o), Par(sp["ffn_conv_w"], lead=lambda o: o + 2),
                Par(sp["ffn_conv_b"], lead=lambda o: o), Par(sp["ffn_conv_b"], lead=lambda o: o + 2)]
    ffn_out = [Out((2, T, FF_BLK), BF16, FF_BLK, lead=lambda o: o)]
    (ff,) = stage("convglu", f_convglu, ffn_ins, ffn_pars, ffn_out, outer=2)
    y3 = mm("down", _NN, [(ff, W["w_down"], _bs((None, tmm, FF_BLK), lambda i, j, k: (k, i, 0)),
                           _bs((None, FF_BLK, D), lambda i, j, k: (k, 0, 0)))],
            S((T, D), F32), _bs((tmm, D), lambda i, j, k: (i, 0)), (nI, 1, 2))
    loss, dx2, dy3, gs["ffn_post_norm"] = loss_stage(x2, y3, tgt, sp["ffn_post_norm"], tm)

    dff = mm("down_dx", _NT, [(dy3, W["w_down"], _bs((tmm, D), lambda j, i, k: (i, 0)),
                               _bs((None, FF_BLK, D), lambda j, i, k: (j, 0, 0)))],
             S((2, T, FF_BLK), BF16), _bs((None, tmm, FF_BLK), lambda j, i, k: (j, i, 0)), (2, nI, 1))
    tbig = min(1024, T)
    gw["w_down"] = mm("down_dw", _TN, [(ff, dy3, _bs((None, tbig, FF_BLK), lambda b, j, k: (b, k, 0)),
                                        _bs((tbig, 1024), lambda b, j, k: (k, j)))],
                      S((2, FF_BLK, D), F32), _bs((None, FF_BLK, 1024), lambda b, j, k: (b, 0, j)), (2, D // 1024, T // tbig))
    dU_out = [Out((2, T, FF_BLK), BF16, FF_BLK, lead=lambda o: o), Out((2, T, FF_BLK), BF16, FF_BLK, lead=lambda o: o)]
    (dUa, dUb), dffn = stage("convglu_bwd", f_convglu, ffn_ins, ffn_pars, ffn_out, outer=2, cts=[dff], dins=dU_out)
    gs["ffn_conv_w"] = jnp.concatenate([dffn[0][:2], dffn[1][2:]], axis=0)
    gs["ffn_conv_b"] = jnp.concatenate([dffn[2][:2], dffn[3][2:]], axis=0)
    dh3 = mm("up_dx", _NT, [(dUa, W["w_up"], _bs((None, tmm, FF_BLK), lambda i, j, k: (k, i, 0)),
                             _bs((None, 1024, FF_BLK), lambda i, j, k: (k, j, 0))),
                            (dUb, W["w_up"], _bs((None, tmm, FF_BLK), lambda i, j, k: (k, i, 0)),
                             _bs((None, 1024, FF_BLK), lambda i, j, k: (k + 2, j, 0)))],
             S((T, D), F32), _bs((tmm, 1024), lambda i, j, k: (i, j)), (nI, D // 1024, 2))

    def up_dw(name, dU):
        tt = min(2048, T)
        return mm(name, _TN, [(h3, dU, _bs((tt, 512), lambda b, i, k: (k, i)), _bs((None, tt, FF_BLK), lambda b, i, k: (b, k, 0)))],
                  S((2, D, FF_BLK), F32), _bs((None, 512, FF_BLK), lambda b, i, k: (b, i, 0)), (2, D // 512, T // tt))

    gw["w_up_a"], gw["w_up_b"] = up_dw("up_dw_a", dUa), up_dw("up_dw_b", dUb)
    hook("ffn", gw)
    (dx1, dy2), dpp2 = stage("post2_bwd", f_post_pre, [Tile(x1, D), Tile(y2, D)], pp2, [o2048(F32), o2048(BF16)],
                             cts=[dx2, dh3], dins=[o2048(F32), o2048(BF16)])
    gs["xa_post_norm"], gs["ffn_pre_norm"] = dpp2

    do2 = dense_t("xo_dx", dy2, W["w_xo"])
    gw["w_xo"] = wgrad("xo_dw", o2, dy2)
    (dq2,), (dk2, dv2) = stage("xattn_bwd", f_xattn, [Tile(q2, D)], xa_pars, [o2048(BF16)], cts=[do2], dins=[o2048(BF16)])
    dh2 = dense_t("xq_dx", dq2, W["w_xq"])
    gw["w_xq"] = wgrad("xq_dw", h2, dq2)
    dkv = jnp.concatenate([dk2, dv2], axis=1).astype(BF16)
    dmn = mm("xkv_dx", _NT, [(dkv, W["w_xkv"], _bs((M, 1024), lambda i, j, k: (0, k)), _bs((None, 512, 1024), lambda i, j, k: (k, j, 0)))],
             S((M, D), F32), _bs((M, 512), lambda i, j, k: (0, j)), (1, D // 512, 4))
    gw["w_xkv"] = mm("xkv_dw", _TN, [(mn, dkv, _bs((M, D), lambda b, j, k: (0, 0)), _bs((M, 1024), lambda b, j, k: (0, b)))],
                     S((4, D, 1024), F32), _bs((None, D, 1024), lambda b, j, k: (b, 0, 0)), (4, 1, 1))
    hook("xattn", gw)
    _, (gs["mem_norm"],) = stage("mem_norm_bwd", f_prenorm, [Tile(mem, D)], [Par(sp["mem_norm"])], [Out(mem.shape, BF16, D)],
                                 tm=M, cts=[dmn], dins=[None])
    (dx0, dy1), dpp1 = stage("post1_bwd", f_post_pre, [Tile(x, D), Tile(y1, D)], pp1, [o2048(F32), o2048(BF16)],
                             cts=[dx1, dh2], dins=[o2048(F32), o2048(BF16)])
    gs["mix_post_norm"], gs["xa_pre_norm"] = dpp1

    dmerged = dense_t("mix_out_dx", dy1, W["w_mix_out"])
    gw["w_mix_out"] = wgrad("mix_out_dw", merged, dy1)
    pshape = (T, N_PROJ)
    (dP2, dya, dyb), _ = stage("merge_bwd", f_merge, merge_ins, [], [o2048(BF16)], cts=[dmerged],
                               dins=[Out(pshape, BF16, 2 * D, cb=C_GATES // (2 * D)), o2048(BF16), o2048(BF16)])
    d_on = dense_t("branch_a_dx", dya, W["w_branch_a"])
    gw["w_branch_a"] = wgrad("branch_a_dw", o_n, dya)
    dpooled = mm("branch_b_dx", _NT, [(dyb, W["w_branch_b"], _bs((tmm, 512), lambda i, j, k: (i, k)),
                                       _bs((None, 1024, 512), lambda i, j, k: (k, 0, 0)))],
                 S((T, 1024), F32), _bs((tmm, 1024), lambda i, j, k: (i, 0)), (nI, 1, 4))
    gw["w_branch_b"] = mm("branch_b_dw", _TN, [(pooled, dyb, _bs((tmm, 1024), lambda b, j, k: (k, 0)), _bs((tmm, 512), lambda b, j, k: (k, b)))],
                          S((4, 1024, 512), F32), _bs((None, 1024, 512), lambda b, j, k: (b, 0, 0)), (4, 1, nI))
    hook("mixer", gw)
    (do, dP2), (gs["gdn_norm"],) = stage("gnorm_bwd", f_gnorm, [Tile(o, D), z_tile], [Par(sp["gdn_norm"])], [o2048(BF16)],
                                         cts=[d_on], dins=[o2048(F32), Out(pshape, BF16, D, cb=C_Z // D, into=dP2)])
    dq, dk, dv, dbg = gdn_backward(qkv[0], qkv[1], qkv[2], bg, s0, ninv, do, GDN_SPAN_BWD)
    dcw = []
    for i, (n, dqq) in enumerate(zip("qkv", (dq, dk, dv))):
        (dP2,), (dc,) = stage("conv_%s_bwd" % n, f_heads[i], [Tile(P2, D, cb=i, halo=True)], [Par(cws[i])], [o2048(F32)],
                              cts=[dqq], dins=[Out(pshape, BF16, D, cb=i, into=dP2)])
        dcw.append(dc)
    gs["conv_qkv"] = jnp.concatenate(dcw, axis=1)
    (dP2,), (gs["a_log"], gs["dt_bias"]) = stage("bg_bwd", f_bg, [ba_tile], [Par(sp["a_log"]), Par(sp["dt_bias"])],
                                                 [Out((T, 128), F32, 128)], cts=[dbg],
                                                 dins=[Out(pshape, BF16, 512, cb=C_BA // 512, into=dP2)])
    (dP2,), (gs["pool_w"], gs["pool_scale"]) = stage("pool_bwd", f_pool, [p_tile], pool_pars, [Out((T, 1024), BF16, 1024)],
                                                     cts=[dpooled], dins=[Out(pshape, BF16, 1024, cb=C_POOL // 1024, into=dP2)])
    tk, ta, tt = 2304, 1152, min(2048, T)
    gw["w_in"] = mm("in_proj_dw", _TN, [(dP2, h1, _bs((tt, ta), lambda i, j, k: (k, i)), _bs((tt, D), lambda i, j, k: (k, 0)))],
                    S((N_PROJ, D), F32), _bs((ta, D), lambda i, j, k: (i, 0)), (N_PROJ // ta, 1, T // tt))
    hook("in_proj", gw)
    dh1 = mm("in_proj_dx", _NN, [(dP2, W["w_in"], _bs((tbig, tk), lambda i, j, k: (i, k)), _bs((tk, D), lambda i, j, k: (k, 0)))],
             S((T, D), F32), _bs((tbig, D), lambda i, j, k: (i, 0)), (T // tbig, 1, N_PROJ // tk))
    (grad_x,), (gs["mix_pre_norm"],) = stage("pre1_bwd", f_prenorm_res, [Tile(x, D)], [Par(sp["mix_pre_norm"])],
                                             [o2048(F32), o2048(BF16)], cts=[dx0, dh1], dins=[o2048(F32)])
    return loss, grad_x, gw, gs


W_IN_COLS = 13344
GROUPED = {"w_branch_b": 512, "w_xkv": 1024}
ROW_SHARDED = ("w_branch_a", "w_mix_out", "w_xq", "w_xo")


def shard_to_slab(name, w):
    if name == "w_in":
        return w.T.astype(BF16)
    if name == "w_up":
        return jnp.pad(w, ((0, 0), (0, FF_BLK - FF_COLS))).astype(BF16)
    return w.astype(BF16)


def slabs_to_weight(name, g):
    if name == "w_in":
        full = g.astype(F32).reshape(W_IN_COLS, D_MODEL)
        pad = jnp.zeros((N_PROJ - W_IN_COLS, D_MODEL), F32)
        return jnp.concatenate([full[0:8192], full[9248:13344], full[8224:9248], full[8192:8224], pad]).astype(BF16)
    if name == "w_down":
        z = jnp.zeros((2, FF_BLK - FF_COLS, D_MODEL), g.dtype)
        return jnp.concatenate([g.reshape(2, FF_COLS, D_MODEL), z], axis=1)
    if name in ROW_SHARDED:
        return g.reshape(D_MODEL, D_MODEL)
    return g


def grad_to_slabs(name, gw):
    if name == "w_in":
        g = gw["w_in"]
        return jnp.concatenate([g[0:8192], g[13312:13344], g[12288:13312], g[8192:12288]]).reshape(4, 3336, D_MODEL)
    if name == "w_up":
        return jnp.concatenate([gw["w_up_a"], gw["w_up_b"]], axis=0)
    if name == "w_down":
        return gw["w_down"][:, :FF_COLS].reshape(4, FF_COLS // 2, D_MODEL)
    if name in ROW_SHARDED:
        return gw[name].reshape(4, D_MODEL // 4, D_MODEL)
    return gw[name]


def slab_to_shard_grad(name, f):
    if name == "w_in":
        return f.T
    if name == "w_up":
        return f[:, :FF_COLS]
    return f


MESH = pl.DeviceIdType.MESH
ANY = pl.BlockSpec(memory_space=pl.ANY)


def _me():
    x, y, c = lax.axis_index("x"), lax.axis_index("y"), lax.axis_index("c")
    return x, y, c, 2 * x + y


def _chip_dev(t, c):
    return (t // 2, t % 2, c)


def _rcopy(src, dst, ssem, rsem, dev):
    return pltpu.make_async_remote_copy(src_ref=src, dst_ref=dst, send_sem=ssem, recv_sem=rsem, device_id=dev, device_id_type=MESH)


def _handshake_all(x, y, c):
    barrier = pltpu.get_barrier_semaphore()
    for dx in (0, 1):
        for dy in (0, 1):
            for dc in (0, 1):
                if dx or dy or dc:
                    pl.semaphore_signal(barrier, inc=1, device_id=((x + dx) % 2, (y + dy) % 2, (c + dc) % 2), device_id_type=MESH)
    pl.semaphore_wait(barrier, 7)


def _comm_call(body, name, operands, out_shape, sems, collective_id):
    if collective_id is not None:
        return pl.kernel(body, out_type=out_shape, mesh=plsc.ScalarSubcoreMesh(axis_name="seq", num_cores=1), name=name,
                         scratch_types=sems, compiler_params=pltpu.CompilerParams(collective_id=collective_id))(*operands)
    n_in, n_out = len(operands), len(out_shape)
    return pl.pallas_call(body, name=name, in_specs=[ANY] * n_in, out_specs=[ANY] * n_out, out_shape=out_shape,
                          scratch_shapes=sems)(*operands)


def gather_weights(slabs, name="gather_weights", collective_id=None):
    n = len(slabs)

    def body(*refs):
        src, dst = refs[:n], refs[n:2 * n]
        ici_s, ici_r, fwd_s, fwd_r = refs[2 * n:]
        x, y, c, s = _me()
        if collective_id is not None:
            _handshake_all(x, y, c)
        sender = c == s // 2
        sib = (x, y, 1 - c)
        for r in (1, 2, 3):
            @pl.when(sender)
            def _(r=r):
                for w in range(n):
                    _rcopy(src[w], dst[w].at[s], ici_s.at[w, r - 1], ici_r.at[w, r - 1], _chip_dev(s ^ r, c)).start()
        for r in (1, 2, 3):
            t = s ^ r
            here = c == t // 2

            @pl.when(here)
            def _(r=r, t=t):
                for w in range(n):
                    _rcopy(src[w], dst[w].at[t], ici_s.at[w, r - 1], ici_r.at[w, r - 1], sib).wait_recv()
                    _rcopy(dst[w].at[t], dst[w].at[t], fwd_s.at[w, r - 1], fwd_r.at[w, r - 1], sib).start()

            @pl.when(jnp.logical_not(here))
            def _(r=r, t=t):
                for w in range(n):
                    _rcopy(dst[w].at[t], dst[w].at[t], fwd_s.at[w, r - 1], fwd_r.at[w, r - 1], sib).wait_recv()
        for r in (1, 2, 3):
            t = s ^ r

            @pl.when(sender)
            def _(r=r):
                for w in range(n):
                    _rcopy(src[w], dst[w].at[s], ici_s.at[w, r - 1], ici_r.at[w, r - 1], sib).wait_send()

            @pl.when(c == t // 2)
            def _(r=r, t=t):
                for w in range(n):
                    _rcopy(dst[w].at[t], dst[w].at[t], fwd_s.at[w, r - 1], fwd_r.at[w, r - 1], sib).wait_send()

    out_shape = [jax.ShapeDtypeStruct((4,) + a.shape, a.dtype) for a in slabs]
    sems = [pltpu.SemaphoreType.DMA((n, 3))] * 4
    return _comm_call(body, name, slabs, out_shape, sems, collective_id)


def pair_exchange(g4, name="pair_exchange", collective_id=None):
    n = len(g4)

    def body(*refs):
        src, dst = refs[:n], refs[n:2 * n]
        ssem, rsem = refs[2 * n:]
        x, y, c, s = _me()
        if collective_id is not None:
            _handshake_all(x, y, c)
        cps = [_rcopy(src[w].at[pl.ds(2 * (1 - c), 2)], dst[w], ssem.at[w], rsem.at[w], (x, y, 1 - c)) for w in range(n)]
        for cp in cps:
            cp.start()
        for cp in cps:
            cp.wait()

    return _comm_call(body, name, g4, [jax.ShapeDtypeStruct((2,) + a.shape[1:], a.dtype) for a in g4],
                      [pltpu.SemaphoreType.DMA((n,))] * 2, collective_id)


def _col_tile(R, C):
    for tc in (512, 256, 128):
        if C % tc == 0 and R * tc * 4 <= (4 << 20):
            return tc
    return 128


def pair_add(name, g4, gsib, c):
    _, R, C = g4.shape
    tc = _col_tile(R, C)

    def body(c_ref, a_ref, b_ref, of_ref, ob_ref):
        v = a_ref[...] + b_ref[...]
        of_ref[...] = v
        ob_ref[...] = v.astype(BF16)

    blk = lambda f: pl.BlockSpec((None, R, tc), f)
    return _pcall(
        body, (c.reshape(1).astype(jnp.int32), g4, gsib), name=name, num_scalar_prefetch=1, grid=(2, C // tc),
        in_specs=[blk(lambda p, j, cr: (2 * cr[0] + p, 0, j)), blk(lambda p, j, cr: (p, 0, j))],
        out_specs=[blk(lambda p, j, cr: (p, 0, j)), blk(lambda p, j, cr: (p, 0, j))],
        out_shape=[jax.ShapeDtypeStruct((2, R, C), F32), jax.ShapeDtypeStruct((2, R, C), BF16)],
        compiler_params=_cparams(("arbitrary", "arbitrary")),
    )


def scatter_partials(rb, name="scatter_partials", collective_id=None):
    n = len(rb)

    def body(*refs):
        src, dst = refs[:n], refs[n:2 * n]
        ssem, rsem = refs[2 * n:]
        x, y, c, s = _me()
        if collective_id is not None:
            _handshake_all(x, y, c)
        for r in (1, 2, 3):
            t = s ^ r

            @pl.when(t // 2 == c)
            def _(r=r, t=t):
                for w in range(n):
                    _rcopy(src[w].at[t % 2], dst[w].at[s], ssem.at[w, r - 1], rsem.at[w, r - 1], _chip_dev(t, c)).start()
        for r in (1, 2, 3):
            t = s ^ r

            @pl.when(s // 2 == c)
            def _(r=r, t=t):
                for w in range(n):
                    _rcopy(src[w].at[0], dst[w].at[t], ssem.at[w, r - 1], rsem.at[w, r - 1], _chip_dev(t, c)).wait_recv()
        for r in (1, 2, 3):
            t = s ^ r

            @pl.when(t // 2 == c)
            def _(r=r, t=t):
                for w in range(n):
                    _rcopy(src[w].at[t % 2], dst[w].at[s], ssem.at[w, r - 1], rsem.at[w, r - 1], _chip_dev(t, c)).wait_send()

    return _comm_call(body, name, rb, [jax.ShapeDtypeStruct((4,) + a.shape[1:], a.dtype) for a in rb],
                      [pltpu.SemaphoreType.DMA((n, 3))] * 2, collective_id)


def final_sum(name, rf, recv, s):
    _, R, C = rf.shape
    tc = _col_tile(R, C)

    def body(s_ref, own_ref, r0_ref, r1_ref, r2_ref, o_ref):
        o_ref[...] = ((own_ref[...] + r0_ref[...].astype(F32)) + r1_ref[...].astype(F32)) + r2_ref[...].astype(F32)

    blk = lambda f: pl.BlockSpec((None, R, tc), f)
    other = lambda k: (lambda j, sr: (k + (k >= sr[0]).astype(jnp.int32), 0, j))
    return _pcall(
        body, (s.reshape(1).astype(jnp.int32), rf, recv, recv, recv), name=name, num_scalar_prefetch=1, grid=(C // tc,),
        in_specs=[blk(lambda j, sr: (sr[0] % 2, 0, j)), blk(other(0)), blk(other(1)), blk(other(2))],
        out_specs=pl.BlockSpec((R, tc), lambda j, sr: (0, j)), out_shape=jax.ShapeDtypeStruct((R, C), F32),
        compiler_params=_cparams(("arbitrary",)),
    )


def share_with_sibling(fs, name="share_with_sibling", collective_id=None):
    n = len(fs)

    def body(*refs):
        src, dst = refs[:n], refs[n:2 * n]
        ssem, rsem = refs[2 * n:]
        x, y, c, s = _me()
        if collective_id is not None:
            _handshake_all(x, y, c)
        sib = (x, y, 1 - c)

        @pl.when(s // 2 == c)
        def _():
            cps = [_rcopy(src[w], dst[w], ssem.at[w], rsem.at[w], sib) for w in range(n)]
            for cp in cps:
                cp.start()
            for cp in cps:
                cp.wait_send()

        @pl.when(s // 2 != c)
        def _():
            for w in range(n):
                _rcopy(src[w], dst[w], ssem.at[w], rsem.at[w], sib).wait_recv()

    return _comm_call(body, name, fs, [jax.ShapeDtypeStruct(a.shape, a.dtype) for a in fs],
                      [pltpu.SemaphoreType.DMA((n,))] * 2, collective_id)


def pick(name, mine, a, b):
    R, C = a.shape
    tc = _col_tile(R, C)

    def body(m_ref, a_ref, b_ref, o_ref):
        o_ref[...] = jnp.where(m_ref[0] != 0, a_ref[...], b_ref[...])

    blk = pl.BlockSpec((R, tc), lambda j, mr: (0, j))
    return _pcall(body, (mine.reshape(1).astype(jnp.int32), a, b), name=name, num_scalar_prefetch=1, grid=(C // tc,),
                  in_specs=[blk, blk], out_specs=blk, out_shape=jax.ShapeDtypeStruct((R, C), a.dtype),
                  compiler_params=_cparams(("arbitrary",)))


def allgather_rows(v):
    m_per, ncol = v.shape

    def body(x_ref, out_ref, send_sems, recv_sems, local_sem):
        x, y, c = lax.axis_index("x"), lax.axis_index("y"), lax.axis_index("c")
        me, sibling = (x, y, c), (x, y, 1 - c)
        chips = [(1 - x, y), (x, 1 - y), (1 - x, 1 - y)]

        def rows(px, py, pc):
            return out_ref.at[pl.ds((4 * px + 2 * py + pc) * m_per, m_per), :]

        def copy(k, block, to, src=None):
            return _rcopy(rows(*block) if src is None else src, rows(*block), send_sems.at[k], recv_sems.at[k], to)

        mine = pltpu.make_async_copy(x_ref, rows(*me), local_sem)
        mine.start()
        first = [copy(0, me, sibling, src=x_ref)]
        first += [copy(1 + j, me, (*chip, c), src=x_ref) for j, chip in enumerate(chips)]
        for cp in first:
            cp.start()
        passed = [copy(4 + j, (*chip, c), sibling) for j, chip in enumerate(chips)]
        for j, chip in enumerate(chips):
            copy(1 + j, (*chip, c), me).wait_recv()
            passed[j].start()
        copy(0, sibling, me).wait_recv()
        for j, chip in enumerate(chips):
            copy(4 + j, (*chip, 1 - c), me).wait_recv()
        for cp in first + passed:
            cp.wait_send()
        mine.wait()

    return pl.pallas_call(
        body, name="allgather_rows", out_shape=jax.ShapeDtypeStruct((8 * m_per, ncol), v.dtype),
        in_specs=[pl.BlockSpec(memory_space=pltpu.VMEM)], out_specs=pl.BlockSpec(memory_space=pltpu.VMEM),
        scratch_shapes=[pltpu.SemaphoreType.DMA((7,)), pltpu.SemaphoreType.DMA((7,)), pltpu.SemaphoreType.DMA],
        compiler_params=pltpu.CompilerParams(vmem_limit_bytes=V7X_VMEM_LIMIT),
    )(v)


def allgather_rows_seq(v, name, collective_id):
    m_per, ncol = v.shape

    def body(x_ref, out_ref, send_sems, recv_sems):
        x, y, c = lax.axis_index("x"), lax.axis_index("y"), lax.axis_index("c")
        _handshake_all(x, y, c)
        me, sibling = (x, y, c), (x, y, 1 - c)
        chips = [(1 - x, y), (x, 1 - y), (1 - x, 1 - y)]

        def rows(px, py, pc):
            return out_ref.at[pl.ds((4 * px + 2 * py + pc) * m_per, m_per), :]

        def copy(k, block, to, src=None):
            return _rcopy(rows(*block) if src is None else src, rows(*block), send_sems.at[k], recv_sems.at[k], to)

        first = [copy(0, me, sibling, src=x_ref)]
        first += [copy(1 + j, me, (*chip, c), src=x_ref) for j, chip in enumerate(chips)]
        for cp in first:
            cp.start()
        passed = [copy(4 + j, (*chip, c), sibling) for j, chip in enumerate(chips)]
        for j, chip in enumerate(chips):
            copy(1 + j, (*chip, c), me).wait_recv()
            passed[j].start()
        copy(0, sibling, me).wait_recv()
        for j, chip in enumerate(chips):
            copy(4 + j, (*chip, 1 - c), me).wait_recv()
        for cp in first + passed:
            cp.wait_send()

    return _comm_call(body, name, [v], [jax.ShapeDtypeStruct((8 * m_per, ncol), v.dtype)],
                      [pltpu.SemaphoreType.DMA((7,)), pltpu.SemaphoreType.DMA((7,))], collective_id)[0]


def sum_blocks(name, a, nblk):
    m = a.shape[0] // nblk

    def body(a_ref, o_ref):
        acc = a_ref[pl.ds(0, m), :]
        for b in range(1, nblk):
            acc = acc + a_ref[pl.ds(b * m, m), :]
        o_ref[...] = acc

    return pl.pallas_call(body, name=name, out_shape=jax.ShapeDtypeStruct((m, a.shape[1]), a.dtype),
                          compiler_params=pltpu.CompilerParams(vmem_limit_bytes=V7X_VMEM_LIMIT))(a)


BIG = ("w_in", "w_branch_a", "w_branch_b", "w_mix_out", "w_xq", "w_xkv", "w_xo", "w_up", "w_down")
GAINS = ("mix_pre_norm", "gdn_norm", "pool_scale", "mix_post_norm", "xa_pre_norm", "mem_norm", "xa_post_norm",
         "ffn_pre_norm", "ffn_post_norm")
WEIGHTS = ("mix_pre_norm", "w_in", "conv_qkv", "a_log", "dt_bias", "gdn_norm", "pool_w", "pool_scale", "w_branch_a",
           "w_branch_b", "w_mix_out", "mix_post_norm", "xa_pre_norm", "mem_norm", "w_xq", "w_xkv", "w_xo", "xa_post_norm",
           "ffn_pre_norm", "w_up", "ffn_conv_w", "ffn_conv_b", "w_down", "ffn_post_norm")


RS_GROUPS = {"ffn": (("w_up", "w_down"), (3, 4, 5)), "xattn": (("w_xo", "w_xq", "w_xkv"), (6, 7, 8)),
             "mixer": (("w_mix_out", "w_branch_a", "w_branch_b"), (9, 10, 11)), "in_proj": (("w_in",), (12, 13, 14))}


def _rows128(vecs):
    flat = jnp.concatenate([v.reshape(-1) for v in vecs])
    m = -(-flat.shape[0] // 1024) * 8
    return jnp.pad(flat, (0, m * 128 - flat.shape[0])).reshape(m, 128)


def _unrows(a, shapes):
    flat, out, pos = a.reshape(-1), [], 0
    for sh in shapes:
        n = 1
        for d in sh:
            n *= d
        out.append(flat[pos:pos + n].reshape(sh))
        pos += n
    return out


def _lane128(v):
    return jnp.pad(v.reshape(1, GDN_HEADS), ((0, 0), (GDN_HEADS, 128 - 2 * GDN_HEADS)))


def kernel(x, mem, mix_pre_norm, w_in, conv_qkv, a_log, dt_bias, gdn_norm, pool_w, pool_scale, w_branch_a, w_branch_b, w_mix_out, mix_post_norm, xa_pre_norm, mem_norm, w_xq, w_xkv, w_xo, xa_post_norm, ffn_pre_norm, w_up, ffn_conv_w, ffn_conv_b, w_down, ffn_post_norm, loss_target, m_mix_pre_norm, m_w_in, m_conv_qkv, m_a_log, m_dt_bias, m_gdn_norm, m_pool_w, m_pool_scale, m_w_branch_a, m_w_branch_b, m_w_mix_out, m_mix_post_norm, m_xa_pre_norm, m_mem_norm, m_w_xq, m_w_xkv, m_w_xo, m_xa_post_norm, m_ffn_pre_norm, m_w_up, m_ffn_conv_w, m_ffn_conv_b, m_w_down, m_ffn_post_norm, v_mix_pre_norm, v_w_in, v_conv_qkv, v_a_log, v_dt_bias, v_gdn_norm, v_pool_w, v_pool_scale, v_w_branch_a, v_w_branch_b, v_w_mix_out, v_mix_post_norm, v_xa_pre_norm, v_mem_norm, v_w_xq, v_w_xkv, v_w_xo, v_xa_post_norm, v_ffn_pre_norm, v_w_up, v_ffn_conv_w, v_ffn_conv_b, v_w_down, v_ffn_post_norm):
    given = dict(locals())
    _LAST[0] = None
    w = {n: given[n][0] for n in WEIGHTS}
    cx, cy, cc = lax.axis_index("x"), lax.axis_index("y"), lax.axis_index("c")
    chip = 2 * cx + cy

    slabs = [shard_to_slab(n, w[n]) for n in BIG]
    first = gather_weights(slabs[:1], name="gather_w_in", collective_id=1)
    rest = gather_weights(slabs[1:], name="gather_rest", collective_id=2)
    stacks = [lax.dynamic_update_index_in_dim(g, sl, chip, 0) for g, sl in zip(list(first) + list(rest), slabs)]
    W = {n: slabs_to_weight(n, g) for n, g in zip(BIG, stacks)}
    sharded_small = (w["conv_qkv"], w["ffn_conv_w"], w["pool_w"])
    allv = allgather_rows(_rows128(sharded_small))
    per_chip = allv.reshape(8, -1)[0::2]
    parts = [_unrows(per_chip[t], [a.shape for a in sharded_small]) for t in range(4)]
    sp = {n: w[n].reshape(1, -1) for n in GAINS}
    sp["a_log"], sp["dt_bias"] = _lane128(w["a_log"]), _lane128(w["dt_bias"])
    sp["conv_qkv"] = jnp.concatenate([p[0] for p in parts], axis=1)
    sp["ffn_conv_w"] = jnp.pad(jnp.stack([p[1] for p in parts]), ((0, 0), (0, 0), (0, FF_BLK - FF_COLS)))
    sp["pool_w"] = jnp.concatenate([p[2] for p in parts], axis=1).reshape(4 * 256, 256)
    sp["ffn_conv_b"] = jnp.pad(w["ffn_conv_b"].reshape(4, 1, FF_COLS), ((0, 0), (0, 0), (0, FF_BLK - FF_COLS)))

    grads, summed, todo = {}, {}, []

    def rs_steps(names, ids, gw):
        g4 = [grad_to_slabs(n, gw) for n in names]
        tag = names[0]
        gsib = pair_exchange(g4, "pair_exchange_" + tag, ids[0])
        yield
        sums = [pair_add("pair_add_" + n, a, b, cc) for n, a, b in zip(names, g4, gsib)]
        recv = scatter_partials([sb for _, sb in sums], "scatter_partials_" + tag, ids[1])
        yield
        fin = [final_sum("final_sum_" + n, sf, rv, chip) for n, (sf, _), rv in zip(names, sums, recv)]
        got = share_with_sibling(fin, "share_" + tag, ids[2])
        yield
        for n, f, g in zip(names, fin, got):
            summed[n] = pick("pick_" + n, (chip // 2 == cc), f, g)
            grads[n] = slab_to_shard_grad(n, summed[n])

    def advance():
        for it in list(todo):
            if next(it, "done") == "done":
                todo.remove(it)

    def hook(event, gw):
        new = rs_steps(*RS_GROUPS[event], gw)
        next(new)
        advance()
        todo.append(new)
        if event == "in_proj":
            next(new)

    loss, grad_x, gw, gs = local_step(x[0], mem[0], loss_target[0], W, sp, hook)

    delta, new_m, new_v = {}, {}, {}

    def update(names):
        for n in names:
            shp = w[n].shape
            if n in ("w_in", "w_up"):
                gt = summed[n] if n == "w_in" else grads[n].T
                res = adamw("adamw_" + n, w[n].T, gt, given["m_" + n][0].T, given["v_" + n][0].T)
                res = [a.T for a in res]
            elif len(shp) == 2:
                res = adamw("adamw_" + n, given[n], grads[n], given["m_" + n], given["v_" + n])
            else:
                two = (lambda a: a.reshape(-1, shp[-1])) if len(shp) > 1 else (lambda a: a.reshape(1, -1))
                res = adamw("adamw_" + n, two(w[n]), two(grads[n]), two(given["m_" + n][0]), two(given["v_" + n][0]))
            delta[n], new_m[n], new_v[n] = (a.reshape((1,) + shp) for a in res)

    advance()
    small_names = GAINS + ("a_log", "dt_bias", "ffn_conv_b", "conv_qkv", "ffn_conv_w", "pool_w")
    vec = _rows128([gs[n] for n in small_names])
    allv = allgather_rows_seq(vec, "allgather_small_grads", 15)
    loss = lax.psum(loss[0, 0], ("x", "y", "c"))
    update([n for n in WEIGHTS if n in grads])
    allv = lax.dynamic_update_slice_in_dim(allv, vec, (2 * chip + cc) * vec.shape[0], axis=0)
    total = sum_blocks("sum_small", allv, 8)
    tot = dict(zip(small_names, _unrows(total, [gs[n].shape for n in small_names])))
    for n in GAINS:
        grads[n] = tot[n].reshape(-1)
    grads["a_log"] = tot["a_log"][0, GDN_HEADS:2 * GDN_HEADS]
    grads["dt_bias"] = tot["dt_bias"][0, GDN_HEADS:2 * GDN_HEADS]
    grads["ffn_conv_b"] = tot["ffn_conv_b"][:, 0, :FF_COLS].reshape(-1)
    grads["conv_qkv"] = lax.dynamic_slice_in_dim(tot["conv_qkv"], chip * 1536, 1536, axis=1)
    grads["ffn_conv_w"] = lax.dynamic_index_in_dim(tot["ffn_conv_w"], chip, axis=0, keepdims=False)[:, :FF_COLS]
    grads["pool_w"] = lax.dynamic_slice_in_dim(tot["pool_w"].reshape(4, 256, 256), chip * 64, 64, axis=1)

    update([n for n in WEIGHTS if n in grads and n not in delta])
    while todo:
        advance()
    update([n for n in WEIGHTS if n not in delta])
    out_g = [grads[n].reshape((1,) + w[n].shape) for n in WEIGHTS]
    return (loss, grad_x[None], *out_g, *[delta[n] for n in WEIGHTS], *[new_m[n] for n in WEIGHTS], *[new_v[n] for n in WEIGHTS])
```

```python
import functools

import jax
import jax.numpy as jnp
from jax import lax
from jax.experimental import pallas as pl
from jax.experimental.pallas import tpu as pltpu
from jax.experimental.pallas import tpu_sc as plsc

F32 = jnp.float32
BF16 = jnp.bfloat16

D_MODEL = 2048
CHUNK = 64
GDN_HEADS = 16
GDN_DK = 128
POOL_WINDOWS = (2, 4, 8, 16)
XA_HEADS = 4
XA_HEAD_DIM = D_MODEL // XA_HEADS
EPS = 1e-6
GDN_SPAN = 256
GDN_SPAN_BWD = 128
HALO = 16
V7X_VMEM_LIMIT = 56 * 1024 * 1024

ADAM_LR, ADAM_B1, ADAM_B2, ADAM_EPS, ADAM_WD, ADAM_STEP = 0.001, 0.9, 0.999, 1e-08, 0.01, 10

_NN = ((1,), (0,))
_NT = ((1,), (1,))
_TN = ((0,), (0,))


def _cparams(sem):
    return pltpu.CompilerParams(dimension_semantics=sem, vmem_limit_bytes=V7X_VMEM_LIMIT)


_LAST = [None]


def _pcall(body, operands, *, in_specs, out_specs, grid=(), num_scalar_prefetch=0, scratch_shapes=(), **kw):
    operands, in_specs = list(operands), list(in_specs)
    if _LAST[0] is not None and not any(o is _LAST[0] for o in operands):
        n = len(operands)
        inner = body

        def body(*refs):
            return inner(*refs[:n], *refs[n + 1:])

        operands.append(_LAST[0])
        in_specs.append(pl.BlockSpec(memory_space=pl.ANY))
    if num_scalar_prefetch:
        kw["grid_spec"] = pltpu.PrefetchScalarGridSpec(num_scalar_prefetch=num_scalar_prefetch, grid=grid, in_specs=in_specs,
                                                       out_specs=out_specs, scratch_shapes=list(scratch_shapes))
    else:
        kw.update(grid=grid, in_specs=in_specs, out_specs=out_specs, scratch_shapes=list(scratch_shapes))
    res = pl.pallas_call(body, **kw)(*operands)
    _LAST[0] = res[0] if isinstance(res, (list, tuple)) else res
    return res


def _dg(a, b, dims, prec=None):
    return lax.dot_general(a, b, (dims, ((), ())), precision=prec, preferred_element_type=F32)


def _make_dots(cast, prec):
    def raw(dims, a, b):
        return _dg(cast(a), cast(b), dims, prec)

    @jax.custom_vjp
    def nn(a, b):
        return raw(_NN, a, b)

    @jax.custom_vjp
    def nt(a, b):
        return raw(_NT, a, b)

    @jax.custom_vjp
    def tn(a, b):
        return raw(_TN, a, b)

    nn.defvjp(lambda a, b: (raw(_NN, a, b), (a, b)), lambda r, g: (nt(g, r[1]), tn(r[0], g)))
    nt.defvjp(lambda a, b: (raw(_NT, a, b), (a, b)), lambda r, g: (nn(g, r[1]), tn(g, r[0])))
    tn.defvjp(lambda a, b: (raw(_TN, a, b), (a, b)), lambda r, g: (nt(r[1], g), nn(r[0], g)))
    return nn, nt, tn


bdot_nn, bdot_nt, bdot_tn = _make_dots(lambda x: x.astype(BF16), None)
hdot_nn, hdot_nt, hdot_tn = _make_dots(lambda x: x, lax.Precision.HIGHEST)


@functools.partial(jax.custom_vjp, nondiff_argnums=(1,))
def shift_rows(x, k):
    return pltpu.roll(x, k, 0)


def _shift_rows_fwd(x, k):
    return pltpu.roll(x, k, 0), None


def _shift_rows_bwd(k, _, g):
    return (pltpu.roll(g, g.shape[0] - k, 0),)


shift_rows.defvjp(_shift_rows_fwd, _shift_rows_bwd)


@functools.partial(jax.custom_vjp, nondiff_argnums=(1,))
def drop_head(x, h):
    return x[h:]


def _drop_head_fwd(x, h):
    return x[h:], None


def _drop_head_bwd(h, _, g):
    return (jnp.concatenate([jnp.zeros((h,) + g.shape[1:], g.dtype), g], axis=0),)


drop_head.defvjp(_drop_head_fwd, _drop_head_bwd)


@functools.partial(jax.custom_vjp, nondiff_argnums=(1,))
def split_lanes(x, n):
    w = x.shape[-1] // n
    return tuple(x[:, i * w:(i + 1) * w] for i in range(n))


def _split_lanes_fwd(x, n):
    return split_lanes(x, n), None


def _split_lanes_bwd(n, _, gs):
    return (jnp.concatenate(list(gs), axis=-1),)


split_lanes.defvjp(_split_lanes_fwd, _split_lanes_bwd)


@functools.partial(jax.custom_vjp, nondiff_argnums=(1,))
def split_rows(x, n):
    h = x.shape[0] // n
    return tuple(x[i * h:(i + 1) * h] for i in range(n))


def _split_rows_fwd(x, n):
    return split_rows(x, n), None


def _split_rows_bwd(n, _, gs):
    return (jnp.concatenate(list(gs), axis=0),)


split_rows.defvjp(_split_rows_fwd, _split_rows_bwd)


def row_of(w, j):
    rid = lax.broadcasted_iota(jnp.int32, w.shape, 0)
    return jnp.sum(jnp.where(rid == j, w, 0.0), axis=0, keepdims=True)


def sigmoid(x):
    return 0.5 * jnp.tanh(0.5 * x) + 0.5


def silu(x):
    return x * sigmoid(x)


def softplus(x):
    return jnp.maximum(x, 0.0) + jnp.log(1.0 + jnp.exp(-jnp.abs(x)))


def rms(x, g):
    return x * lax.rsqrt(jnp.mean(x * x, axis=-1, keepdims=True) + EPS) * g


def mm(name, dims, pairs, out_shape, out_spec, grid):
    nk = grid[2]
    acc_in_out = nk > 1 and out_shape.dtype == F32
    npair = len(pairs)

    def body(*refs):
        o_ref = refs[2 * npair]
        part = None
        for p in range(npair):
            d = _dg(refs[2 * p][...].astype(BF16), refs[2 * p + 1][...].astype(BF16), dims)
            part = d if part is None else part + d
        if nk == 1:
            o_ref[...] = part.astype(o_ref.dtype)
            return
        acc = o_ref if acc_in_out else refs[2 * npair + 1]
        k = pl.program_id(2)

        @pl.when(k == 0)
        def _():
            acc[...] = part

        @pl.when(k > 0)
        def _():
            acc[...] += part

        if not acc_in_out:
            @pl.when(k == nk - 1)
            def _():
                o_ref[...] = acc[...].astype(o_ref.dtype)

    scratch = []
    if nk > 1 and not acc_in_out:
        scratch = [pltpu.VMEM(tuple(d for d in out_spec.block_shape if d is not None), F32)]
    in_specs, operands = [], []
    for a, b, a_spec, b_spec in pairs:
        in_specs += [a_spec, b_spec]
        operands += [a, b]
    return _pcall(body, operands, name=name, grid=grid, in_specs=in_specs, out_specs=out_spec, out_shape=out_shape,
                  scratch_shapes=scratch, compiler_params=_cparams(("parallel", "parallel", "arbitrary")))


def _bs(shape, fn):
    return pl.BlockSpec(shape, fn)


class Tile:
    def __init__(self, arr, w, cb=0, lead=None, halo=False):
        self.arr, self.w, self.cb, self.lead, self.halo = arr, w, cb, lead, halo


class Out:
    def __init__(self, shape, dtype, w, cb=0, lead=None, into=None):
        self.shape, self.dtype, self.w, self.cb, self.lead, self.into = shape, dtype, w, cb, lead, into


class Par:
    def __init__(self, arr, lead=None):
        self.arr, self.lead = arr, lead


def _spec(rows, w, cb, lead, tile_of):
    if lead is None:
        return pl.BlockSpec((rows, w), lambda o, i: (tile_of(i), cb))
    return pl.BlockSpec((None, rows, w), lambda o, i: (lead(o), tile_of(i), cb))


def _par_spec(p):
    if p.lead is None:
        return pl.BlockSpec(p.arr.shape, lambda o, i: (0, 0))
    return pl.BlockSpec((None,) + p.arr.shape[1:], lambda o, i: (p.lead(o), 0, 0))


def run_stage(name, fn, tm, ins, pars, outs, *, outer=1, cts=None, dins=None):
    T = ins[0].arr.shape[-2]
    nt = T // tm
    bwd = cts is not None
    any_halo = any(t.halo for t in ins)
    step_tile = (lambda i: nt - 1 - i) if bwd else (lambda i: i)
    hb = tm // HALO

    in_specs, operands = [], []
    for t in ins:
        if t.halo:
            in_specs.append(_spec(HALO, t.w, t.cb, t.lead, lambda i: jnp.maximum(step_tile(i) * hb - 1, 0)))
            operands.append(t.arr)
        in_specs.append(_spec(tm, t.w, t.cb, t.lead, step_tile))
        operands.append(t.arr)
    for p in pars:
        in_specs.append(_par_spec(p))
        operands.append(p.arr)
    n_in_refs = len(operands)

    out_descs = list(outs) if not bwd else [d for d in dins if d is not None]
    aliases = {}
    if bwd:
        for c, o in zip(cts, outs):
            in_specs.append(_spec(tm, o.w, o.cb, o.lead, step_tile))
            operands.append(c)
    n_ct = len(operands) - n_in_refs
    for k, o in enumerate(out_descs):
        if o.into is not None:
            aliases[len(operands)] = k
            in_specs.append(pl.BlockSpec(memory_space=pl.ANY))
            operands.append(o.into)
    out_specs = [_spec(tm, o.w, o.cb, o.lead, step_tile) for o in out_descs]
    out_shapes = [jax.ShapeDtypeStruct(o.shape, o.dtype) for o in out_descs]
    if bwd:
        for p in pars:
            out_specs.append(_par_spec(p))
            out_shapes.append(jax.ShapeDtypeStruct(p.arr.shape, F32))
    scratch = []
    if bwd and any_halo:
        scratch = [pltpu.VMEM((HALO, t.w), F32) for t, d in zip(ins, dins) if t.halo and d is not None]

    def body(*refs):
        i = pl.program_id(1)
        tile = step_tile(i)
        row0 = tile * tm
        pos = 0
        tiles = []
        for t in ins:
            if t.halo:
                prev = jnp.where(tile > 0, refs[pos][...].astype(F32), 0.0)
                tiles.append(jnp.concatenate([prev, refs[pos + 1][...].astype(F32)], axis=0))
                pos += 2
            else:
                tiles.append(refs[pos][...].astype(F32))
                pos += 1
        pvals = [refs[pos + k][...].astype(F32) for k in range(len(pars))]
        pos += len(pars)
        if not bwd:
            res = fn(tiles, pvals, row0)
            for o_ref, r in zip(refs[pos:], res):
                o_ref[...] = r.astype(o_ref.dtype)
            return
        ct_vals = [refs[pos + k][...].astype(F32) for k in range(n_ct)]
        pos += n_ct + len(aliases)
        _, vjp_fn = jax.vjp(lambda tt, pp: fn(tt, pp, row0), tiles, pvals)
        d_tiles, d_pars = vjp_fn(ct_vals)
        carries = list(refs[len(refs) - len(scratch):])
        for t, d, dt in zip(ins, dins, d_tiles):
            if d is None:
                continue
            o_ref = refs[pos]
            pos += 1
            if t.halo:
                carry = carries.pop(0)
                main = dt[HALO:]
                tail = main[tm - HALO:] + jnp.where(i > 0, carry[...], 0.0)
                o_ref[...] = jnp.concatenate([main[:tm - HALO], tail], axis=0).astype(o_ref.dtype)
                carry[...] = dt[:HALO]
            else:
                o_ref[...] = dt.astype(o_ref.dtype)
        for dp in d_pars:
            acc = refs[pos]
            pos += 1

            @pl.when(i == 0)
            def _(acc=acc, dp=dp):
                acc[...] = dp

            @pl.when(i > 0)
            def _(acc=acc, dp=dp):
                acc[...] += dp

    res = _pcall(body, operands, name=name, grid=(outer, nt), in_specs=in_specs, out_specs=out_specs, out_shape=out_shapes,
                 scratch_shapes=scratch, input_output_aliases=aliases, compiler_params=_cparams(("arbitrary", "arbitrary")))
    if not bwd:
        return list(res)
    n_d = len(out_descs)
    d_full, it = [], iter(res[:n_d])
    for d in dins:
        d_full.append(None if d is None else next(it))
    return d_full, list(res[n_d:])


def f_prenorm(t, p, row0):
    return [rms(t[0], p[0])]


def f_prenorm_res(t, p, row0):
    return [t[0], rms(t[0], p[0])]


def f_post_pre(t, p, row0):
    x, y = t
    x1 = x + rms(y, p[0])
    return [x1, rms(x1, p[1])]


def _causal_conv(x, w, taps):
    y = x * row_of(w, taps - 1)
    for j in range(taps - 1):
        y = y + shift_rows(x, taps - 1 - j) * row_of(w, j)
    return drop_head(y, HALO)


def _l2(x):
    return x * lax.rsqrt(jnp.sum(x * x, axis=-1, keepdims=True) + EPS)


def make_f_convhead(scale, normalise):
    def f(t, p, row0):
        y = silu(_causal_conv(t[0], p[0], 4))
        if not normalise:
            return [y]
        return [jnp.concatenate([_l2(c) * scale for c in split_lanes(y, GDN_HEADS)], axis=-1)]
    return f


def f_bg(t, p, row0):
    ba = split_lanes(t[0], 4)[0]
    alog, dtb = p
    lane = lax.broadcasted_iota(jnp.int32, ba.shape, 1)
    bg = jnp.where(lane < GDN_HEADS, sigmoid(ba), -jnp.exp(alog) * softplus(ba + dtb))
    return [jnp.where(lane < 2 * GDN_HEADS, bg, 0.0)]


def f_gnorm(t, p, row0):
    o, z = t
    po, pz = split_lanes(o, GDN_HEADS), split_lanes(z, GDN_HEADS)
    return [jnp.concatenate([rms(a, p[0]) * silu(b) for a, b in zip(po, pz)], axis=-1)]


def f_pool(t, p, row0):
    x = t[0]
    pw, psc = p
    tm = x.shape[0] - HALO
    tpos = (row0 + lax.broadcasted_iota(jnp.int32, (tm, 1), 0) + 1).astype(F32)
    outs = []
    for xg, wg, win in zip(split_lanes(x, 4), split_rows(pw, 4), POOL_WINDOWS):
        s, span = xg, 1
        while span < win:
            s = s + shift_rows(s, span)
            span *= 2
        mean = drop_head(s, HALO) / jnp.minimum(tpos, float(win))
        outs.append(bdot_nn(mean - drop_head(xg, HALO), wg))
    return [jnp.concatenate(outs, axis=-1) * psc]


def f_merge(t, p, row0):
    gates, ya, yb = t
    ga, gb = split_lanes(gates, 2)
    return [sigmoid(ga) * ya + sigmoid(gb) * yb]


def f_xattn(t, p, row0):
    k, v = p
    outs = []
    for qh, kh, vh in zip(split_lanes(t[0], XA_HEADS), split_lanes(k, XA_HEADS), split_lanes(v, XA_HEADS)):
        s = bdot_nt(qh, kh) * (XA_HEAD_DIM ** -0.5)
        s = s - jnp.max(s, axis=-1, keepdims=True)
        e = jnp.exp(s)
        outs.append(bdot_nn(e / jnp.sum(e, axis=-1, keepdims=True), vh))
    return [jnp.concatenate(outs, axis=-1)]


def f_convglu(t, p, row0):
    ua, ub = t
    cwa, cwb, ba, bb = p
    return [silu(_causal_conv(ua, cwa, 3) + ba) * (_causal_conv(ub, cwb, 3) + bb)]


_BNN = (((2,), (1,)), ((0,), (0,)))
_BNT = (((2,), (2,)), ((0,), (0,)))
_BTN = (((1,), (1,)), ((0,), (0,)))


def _make_batched_dots():
    def raw(dims, a, b):
        return lax.dot_general(a.astype(BF16), b.astype(BF16), dims, preferred_element_type=F32)

    @jax.custom_vjp
    def nn(a, b):
        return raw(_BNN, a, b)

    @jax.custom_vjp
    def nt(a, b):
        return raw(_BNT, a, b)

    @jax.custom_vjp
    def tn(a, b):
        return raw(_BTN, a, b)

    nn.defvjp(lambda a, b: (raw(_BNN, a, b), (a, b)), lambda r, g: (nt(g, r[1]), tn(r[0], g)))
    nt.defvjp(lambda a, b: (raw(_BNT, a, b), (a, b)), lambda r, g: (nn(g, r[1]), tn(g, r[0])))
    tn.defvjp(lambda a, b: (raw(_BTN, a, b), (a, b)), lambda r, g: (nt(r[1], g), nn(r[0], g)))
    return nn, nt, tn


bb_nn, bb_nt, bb_tn = _make_batched_dots()


def _inverse_correction(X):
    N, P = X, X
    for _ in range(5):
        P = bb_nn(P, P)
        N = N + P + bb_nn(N, P)
    return N


@jax.custom_vjp
def _saved_inverse_correction(X, N):
    return N


def _saved_inverse_fwd(X, N):
    return N, N


def _saved_inverse_bwd(N, G):
    t = G + bb_tn(N, G)
    return t + bb_nt(t, N), jnp.zeros_like(N)


_saved_inverse_correction.defvjp(_saved_inverse_fwd, _saved_inverse_bwd)


def _gdn_local(q, k, v, gcol, bcol, n_saved=None):
    C = CHUNK
    r = lax.broadcasted_iota(jnp.int32, (1, C, C), 1)
    c = lax.broadcasted_iota(jnp.int32, (1, C, C), 2)
    eye, incl, strict = r == c, r >= c, r > c
    grow = jnp.sum(jnp.where(eye, gcol, 0.0), axis=1, keepdims=True)
    Gcol = jnp.sum(jnp.where(incl, grow, 0.0), axis=2, keepdims=True)
    Grow = jnp.sum(jnp.where(eye, Gcol, 0.0), axis=1, keepdims=True)
    decay = jnp.where(incl, jnp.exp(jnp.where(incl, Gcol - Grow, 0.0)), 0.0)
    X = -jnp.where(strict, bcol * decay * bb_nt(k, k), 0.0)
    N = _inverse_correction(X) if n_saved is None else _saved_inverse_correction(X, n_saved)
    expg = jnp.exp(Gcol)
    rv, rk = bcol * v, (bcol * expg) * k
    u_v = rv + bb_nn(N, rv)
    w_k = rk + bb_nn(N, rk)
    attn = decay * bb_nt(q, k)
    rid = lax.broadcasted_iota(jnp.int32, (1, C, 1), 1)
    glast = jnp.sum(jnp.where(rid == C - 1, Gcol, 0.0), axis=1, keepdims=True)
    return (u_v, w_k, attn, q * expg, k * jnp.exp(glast - Gcol), jnp.exp(glast)), N


def _gdn_rec(u_v, w_k, attn, q_dec, k_dec, cd, S):
    u = u_v - bb_nn(w_k, S)
    o = bb_nn(q_dec, S) + bb_nn(attn, u)
    return o, cd * S + bb_tn(k_dec, u)


def _gdn_load(q_ref, k_ref, v_ref, bg_ref, nch):
    H = GDN_HEADS

    def batched(ref):
        return jnp.stack([ref[c * CHUNK:(c + 1) * CHUNK, h * GDN_DK:(h + 1) * GDN_DK] for c in range(nch) for h in range(H)])

    bg = bg_ref[...]
    lane = lax.broadcasted_iota(jnp.int32, bg.shape, 1)
    bcols = [jnp.sum(jnp.where(lane == h, bg, 0.0), axis=-1, keepdims=True) for h in range(H)]
    gcols = [jnp.sum(jnp.where(lane == H + h, bg, 0.0), axis=-1, keepdims=True) for h in range(H)]
    pick = lambda cols: jnp.stack([cols[h][c * CHUNK:(c + 1) * CHUNK] for c in range(nch) for h in range(H)])
    return batched(q_ref), batched(k_ref), batched(v_ref), pick(gcols), pick(bcols)


def _gdn_store(ref, val, nch):
    H = GDN_HEADS
    for c in range(nch):
        for h in range(H):
            ref[c * CHUNK:(c + 1) * CHUNK, h * GDN_DK:(h + 1) * GDN_DK] = val[c * H + h]


def gdn_forward(q, k, v, bg, span):
    T = q.shape[0]
    ns, nch, H = T // span, span // CHUNK, GDN_HEADS

    def body(q_ref, k_ref, v_ref, bg_ref, o_ref, s_ref, n_ref, state):
        @pl.when(pl.program_id(0) == 0)
        def _():
            state[...] = jnp.zeros(state.shape, F32)

        loc, n_ref[...] = _gdn_local(*_gdn_load(q_ref, k_ref, v_ref, bg_ref, nch))
        S = state[...]
        for c in range(nch):
            s_ref[c] = S
            o, S = _gdn_rec(*[a[c * H:(c + 1) * H] for a in loc], S)
            for h in range(H):
                o_ref[c * CHUNK:(c + 1) * CHUNK, h * GDN_DK:(h + 1) * GDN_DK] = o[h]
        state[...] = S

    wide = pl.BlockSpec((span, H * GDN_DK), lambda s: (s, 0))
    return _pcall(
        body, (q, k, v, bg), name="gdn_fwd", grid=(ns,),
        in_specs=[wide, wide, wide, pl.BlockSpec((span, 128), lambda s: (s, 0))],
        out_specs=[wide, pl.BlockSpec((nch, H, GDN_DK, GDN_DK), lambda s: (s, 0, 0, 0)),
                   pl.BlockSpec((nch * H, CHUNK, CHUNK), lambda s: (s, 0, 0))],
        out_shape=[jax.ShapeDtypeStruct((T, H * GDN_DK), F32), jax.ShapeDtypeStruct((T // CHUNK, H, GDN_DK, GDN_DK), F32),
                   jax.ShapeDtypeStruct((T // CHUNK * H, CHUNK, CHUNK), F32)],
        scratch_shapes=[pltpu.VMEM((H, GDN_DK, GDN_DK), F32)],
        compiler_params=_cparams(("arbitrary",)),
    )


def gdn_backward(q, k, v, bg, starts, ninv, do, span):
    T = q.shape[0]
    ns, nch, H = T // span, span // CHUNK, GDN_HEADS

    def body(q_ref, k_ref, v_ref, bg_ref, s_ref, n_ref, do_ref, dq_ref, dk_ref, dv_ref, dbg_ref, dstate):
        @pl.when(pl.program_id(0) == 0)
        def _():
            dstate[...] = jnp.zeros(dstate.shape, F32)

        (loc, _), vjp_loc = jax.vjp(_gdn_local, *_gdn_load(q_ref, k_ref, v_ref, bg_ref, nch), n_ref[...])
        dS = dstate[...]
        dloc = [None] * nch
        for c in reversed(range(nch)):
            _, vjp_rec = jax.vjp(_gdn_rec, *[a[c * H:(c + 1) * H] for a in loc], s_ref[c])
            do_c = jnp.stack([do_ref[c * CHUNK:(c + 1) * CHUNK, h * GDN_DK:(h + 1) * GDN_DK] for h in range(H)])
            *dloc[c], dS = vjp_rec((do_c, dS))
        dstate[...] = dS
        d_loc = tuple(jnp.concatenate([dloc[c][i] for c in range(nch)], axis=0) for i in range(6))
        dq, dk, dv, dg, db, _ = vjp_loc((d_loc, jnp.zeros((nch * H, CHUNK, CHUNK), F32)))
        _gdn_store(dq_ref, dq, nch)
        _gdn_store(dk_ref, dk, nch)
        _gdn_store(dv_ref, dv, nch)
        lane = lax.broadcasted_iota(jnp.int32, (CHUNK, 128), 1)
        for c in range(nch):
            acc = jnp.zeros((CHUNK, 128), F32)
            for h in range(H):
                acc = acc + jnp.where(lane == h, db[c * H + h], 0.0) + jnp.where(lane == H + h, dg[c * H + h], 0.0)
            dbg_ref[c * CHUNK:(c + 1) * CHUNK, :] = acc

    wide = pl.BlockSpec((span, H * GDN_DK), lambda i: (ns - 1 - i, 0))
    bgs = pl.BlockSpec((span, 128), lambda i: (ns - 1 - i, 0))
    return _pcall(
        body, (q, k, v, bg, starts, ninv, do), name="gdn_bwd", grid=(ns,),
        in_specs=[wide, wide, wide, bgs, pl.BlockSpec((nch, H, GDN_DK, GDN_DK), lambda i: (ns - 1 - i, 0, 0, 0)),
                  pl.BlockSpec((nch * H, CHUNK, CHUNK), lambda i: (ns - 1 - i, 0, 0)), wide],
        out_specs=[wide, wide, wide, bgs],
        out_shape=[jax.ShapeDtypeStruct((T, H * GDN_DK), F32)] * 3 + [jax.ShapeDtypeStruct((T, 128), F32)],
        scratch_shapes=[pltpu.VMEM((H, GDN_DK, GDN_DK), F32)],
        compiler_params=_cparams(("arbitrary",)),
    )


def loss_stage(x2, y3, tgt, g, tm):
    T, D = x2.shape
    nt = T // tm

    def body(x_ref, y_ref, t_ref, g_ref, loss_ref, dx_ref, dy_ref, dg_ref):
        i = pl.program_id(0)
        tgtv = t_ref[...]

        def f(x, y, gg):
            err = x + rms(y, gg) - tgtv
            return 0.5 * jnp.mean(err * err, axis=-1, keepdims=True)

        rows, vjp_fn = jax.vjp(f, x_ref[...], y_ref[...], g_ref[...])
        dx, dy, dg = vjp_fn(jnp.ones_like(rows))
        dx_ref[...] = dx
        dy_ref[...] = dy.astype(dy_ref.dtype)
        part = jnp.sum(rows, axis=0, keepdims=True)

        @pl.when(i == 0)
        def _():
            loss_ref[...] = part
            dg_ref[...] = dg

        @pl.when(i > 0)
        def _():
            loss_ref[...] += part
            dg_ref[...] += dg

    tile = pl.BlockSpec((tm, D), lambda i: (i, 0))
    gs = pl.BlockSpec((1, D), lambda i: (0, 0))
    return _pcall(
        body, (x2, y3, tgt, g), name="loss_head", grid=(nt,), in_specs=[tile, tile, tile, gs],
        out_specs=[pl.BlockSpec((1, 1), lambda i: (0, 0)), tile, tile, gs],
        out_shape=[jax.ShapeDtypeStruct((1, 1), F32), jax.ShapeDtypeStruct((T, D), F32),
                   jax.ShapeDtypeStruct((T, D), BF16), jax.ShapeDtypeStruct((1, D), F32)],
        compiler_params=_cparams(("arbitrary",)),
    )


def adamw_pick(name, mine, w, ga, gb, m, v):
    R, C = ga.shape
    tr = R
    for cand in (256, 128, 64, 32, 16, 8):
        if R % cand == 0 and R > cand and cand * C * 4 <= (2 << 20):
            tr = cand
            break
    c1 = 1.0 / (1.0 - ADAM_B1 ** ADAM_STEP)
    c2 = 1.0 / (1.0 - ADAM_B2 ** ADAM_STEP)

    def body(f_ref, w_ref, a_ref, b_ref, m_ref, v_ref, d_ref, nm_ref, nv_ref, g_ref):
        gg = jnp.where(f_ref[0] != 0, a_ref[...], b_ref[...])
        nm = ADAM_B1 * m_ref[...] + (1.0 - ADAM_B1) * gg
        nv = ADAM_B2 * v_ref[...] + (1.0 - ADAM_B2) * (gg * gg)
        d_ref[...] = -ADAM_LR * ((nm * c1) / (jnp.sqrt(nv * c2) + ADAM_EPS) + ADAM_WD * w_ref[...])
        nm_ref[...] = nm
        nv_ref[...] = nv
        g_ref[...] = gg

    gspec = pl.BlockSpec((tr, C), lambda i, f: (i, 0))
    spec = pl.BlockSpec((None, tr, C), lambda i, f: (0, i, 0))
    return _pcall(
        body, (mine.reshape(1).astype(jnp.int32), w, ga, gb, m, v), name=name, num_scalar_prefetch=1, grid=(R // tr,),
        in_specs=[spec, gspec, gspec, spec, spec], out_specs=[spec, spec, spec, gspec],
        out_shape=[jax.ShapeDtypeStruct(w.shape, F32)] * 3 + [jax.ShapeDtypeStruct((R, C), F32)],
        compiler_params=_cparams(("arbitrary",)),
    )


def adamw(name, w, g, m, v):
    R, C = g.shape
    layered = w.ndim == 3
    tr = R
    for cand in (256, 128, 64, 32, 16, 8):
        if R % cand == 0 and R > cand and cand * C * 4 <= (2 << 20):
            tr = cand
            break
    c1 = 1.0 / (1.0 - ADAM_B1 ** ADAM_STEP)
    c2 = 1.0 / (1.0 - ADAM_B2 ** ADAM_STEP)

    def body(w_ref, g_ref, m_ref, v_ref, d_ref, nm_ref, nv_ref):
        gg = g_ref[...]
        nm = ADAM_B1 * m_ref[...] + (1.0 - ADAM_B1) * gg
        nv = ADAM_B2 * v_ref[...] + (1.0 - ADAM_B2) * (gg * gg)
        d_ref[...] = -ADAM_LR * ((nm * c1) / (jnp.sqrt(nv * c2) + ADAM_EPS) + ADAM_WD * w_ref[...])
        nm_ref[...] = nm
        nv_ref[...] = nv

    gspec = pl.BlockSpec((tr, C), lambda i: (i, 0))
    spec = pl.BlockSpec((None, tr, C), lambda i: (0, i, 0)) if layered else gspec
    steps = R // tr
    if tr < 64 and R > 512 and C % 128 == 0 and not layered:
        gspec = spec = pl.BlockSpec((R, 128), lambda i: (0, i))
        steps = C // 128
    return _pcall(
        body, (w, g, m, v), name=name, grid=(steps,), in_specs=[spec, gspec, spec, spec], out_specs=[spec] * 3,
        out_shape=[jax.ShapeDtypeStruct(w.shape, F32)] * 3, compiler_params=_cparams(("parallel",)),
    )


C_Z, C_GATES, C_POOL, C_BA, N_PROJ = 6144, 8192, 12288, 13312, 13824
FF_BLK = 2816
FF_COLS = 2752


def local_step(x, mem, tgt, W, sp, hook=lambda event, gw: None):
    T, D = x.shape
    tm = 256
    tmm = min(512, T)
    nI = T // tmm
    S = jax.ShapeDtypeStruct
    gw, gs = {}, {}

    def stage(name, fn, ins, pars, outs, **kw):
        return run_stage(name, fn, kw.pop("tm", tm), ins, pars, outs, **kw)

    def dense(name, a, w, out_dtype=F32, tn=2048):
        Tq, Kd = a.shape
        N = w.shape[1]
        tq = min(tmm, Tq)
        return mm(name, _NN, [(a, w, _bs((tq, Kd), lambda j, i, k: (i, 0)), _bs((Kd, tn), lambda j, i, k: (0, j)))],
                  S((Tq, N), out_dtype), _bs((tq, tn), lambda j, i, k: (i, j)), (N // tn, Tq // tq, 1))

    def dense_t(name, g, w, out_dtype=F32, tn=2048):
        Tq, N = g.shape
        Kd = w.shape[0]
        tq = min(tmm, Tq)
        return mm(name, _NT, [(g, w, _bs((tq, N), lambda j, i, k: (i, 0)), _bs((tn, N), lambda j, i, k: (j, 0)))],
                  S((Tq, Kd), out_dtype), _bs((tq, tn), lambda j, i, k: (i, j)), (Kd // tn, Tq // tq, 1))

    def wgrad(name, a, g, ta=1024, tn=2048):
        Tq, Kd = a.shape
        N = g.shape[1]
        tt = min(2048, Tq)
        return mm(name, _TN, [(a, g, _bs((tt, ta), lambda i, j, k: (k, i)), _bs((tt, tn), lambda i, j, k: (k, j)))],
                  S((Kd, N), F32), _bs((ta, tn), lambda i, j, k: (i, j)), (Kd // ta, N // tn, Tq // tt))

    o2048 = lambda dt: Out((T, D), dt, D)

    (h1,) = stage("pre1", f_prenorm, [Tile(x, D)], [Par(sp["mix_pre_norm"])], [o2048(BF16)])
    tnp, tmp = 2304, min(1024, T)
    P2 = mm("in_proj", _NT, [(h1, W["w_in"], _bs((tmp, D), lambda j, i, k: (i, 0)), _bs((tnp, D), lambda j, i, k: (j, 0)))],
            S((T, N_PROJ), F32), _bs((tmp, tnp), lambda j, i, k: (i, j)), (N_PROJ // tnp, T // tmp, 1))
    cw = sp["conv_qkv"]
    cws = [cw[:, i * D:(i + 1) * D] for i in range(3)]
    f_heads = [make_f_convhead(GDN_DK ** -0.5, True), make_f_convhead(1.0, True), make_f_convhead(1.0, False)]
    qkv = [stage("conv_" + n, f_heads[i], [Tile(P2, D, cb=i, halo=True)], [Par(cws[i])], [o2048(F32)])[0]
           for i, n in enumerate("qkv")]
    ba_tile = Tile(P2, 512, cb=C_BA // 512)
    (bg,) = stage("bg", f_bg, [ba_tile], [Par(sp["a_log"]), Par(sp["dt_bias"])], [Out((T, 128), F32, 128)])
    o, s0, ninv = gdn_forward(qkv[0], qkv[1], qkv[2], bg, GDN_SPAN)
    z_tile = Tile(P2, D, cb=C_Z // D)
    (o_n,) = stage("gnorm", f_gnorm, [Tile(o, D), z_tile], [Par(sp["gdn_norm"])], [o2048(BF16)])
    y_a = dense("branch_a", o_n, W["w_branch_a"])
    p_tile = Tile(P2, 1024, cb=C_POOL // 1024, halo=True)
    pool_pars = [Par(sp["pool_w"]), Par(sp["pool_scale"])]
    (pooled,) = stage("pool", f_pool, [p_tile], pool_pars, [Out((T, 1024), BF16, 1024)])
    y_b = mm("branch_b", _NN, [(pooled, W["w_branch_b"], _bs((tmm, 1024), lambda j, i, k: (i, 0)),
                                _bs((None, 1024, 512), lambda j, i, k: (j, 0, 0)))],
             S((T, D), F32), _bs((tmm, 512), lambda j, i, k: (i, j)), (4, nI, 1))
    gate_tile = Tile(P2, 2 * D, cb=C_GATES // (2 * D))
    merge_ins = [gate_tile, Tile(y_a, D), Tile(y_b, D)]
    (merged,) = stage("merge", f_merge, merge_ins, [], [o2048(BF16)])
    y1 = dense("mix_out", merged, W["w_mix_out"])
    pp1 = [Par(sp["mix_post_norm"]), Par(sp["xa_pre_norm"])]
    x1, h2 = stage("post1", f_post_pre, [Tile(x, D), Tile(y1, D)], pp1, [o2048(F32), o2048(BF16)])

    q2 = dense("xq", h2, W["w_xq"], out_dtype=BF16)
    (mn,) = stage("mem_norm", f_prenorm, [Tile(mem, D)], [Par(sp["mem_norm"])], [Out(mem.shape, BF16, D)], tm=mem.shape[0])
    M = mem.shape[0]
    kv = mm("xkv", _NN, [(mn, W["w_xkv"], _bs((M, D), lambda j, i, k: (0, 0)), _bs((None, D, 1024), lambda j, i, k: (j, 0, 0)))],
            S((M, 2 * D), F32), _bs((M, 1024), lambda j, i, k: (0, j)), (4, 1, 1))
    k2, v2 = kv[:, :D], kv[:, D:]
    xa_pars = [Par(k2), Par(v2)]
    (o2,) = stage("xattn", f_xattn, [Tile(q2, D)], xa_pars, [o2048(BF16)])
    y2 = dense("xo", o2, W["w_xo"])
    pp2 = [Par(sp["xa_post_norm"]), Par(sp["ffn_pre_norm"])]
    x2, h3 = stage("post2", f_post_pre, [Tile(x1, D), Tile(y2, D)], pp2, [o2048(F32), o2048(BF16)])

    def up(name, off):
        return mm(name, _NN, [(h3, W["w_up"], _bs((tmm, D), lambda j, i, k: (i, 0)),
                               _bs((None, D, FF_BLK), lambda j, i, k: (j + off, 0, 0)))],
                  S((2, T, FF_BLK), F32), _bs((None, tmm, FF_BLK), lambda j, i, k: (j, i, 0)), (2, nI, 1))

    Ua, Ub = up("up_a", 0), up("up_b", 2)
    ffn_ins = [Tile(Ua, FF_BLK, lead=lambda o: o, halo=True), Tile(Ub, FF_BLK, lead=lambda o: o, halo=True)]
    ffn_pars = [Par(sp["ffn_conv_w"], lead=lambda o: o), Par(sp["ffn_conv_w"], lead=lambda o: o + 2),
                Par(sp["ffn_conv_b"], lead=lambda o: o), Par(sp["ffn_conv_b"], lead=lambda o: o + 2)]
    ffn_out = [Out((2, T, FF_BLK), BF16, FF_BLK, lead=lambda o: o)]
    (ff,) = stage("convglu", f_convglu, ffn_ins, ffn_pars, ffn_out, outer=2)
    y3 = mm("down", _NN, [(ff, W["w_down"], _bs((None, tmm, FF_BLK), lambda i, j, k: (k, i, 0)),
                           _bs((None, FF_BLK, D), lambda i, j, k: (k, 0, 0)))],
            S((T, D), F32), _bs((tmm, D), lambda i, j, k: (i, 0)), (nI, 1, 2))
    loss, dx2, dy3, gs["ffn_post_norm"] = loss_stage(x2, y3, tgt, sp["ffn_post_norm"], tm)

    dff = mm("down_dx", _NT, [(dy3, W["w_down"], _bs((tmm, D), lambda j, i, k: (i, 0)),
                               _bs((None, FF_BLK, D), lambda j, i, k: (j, 0, 0)))],
             S((2, T, FF_BLK), BF16), _bs((None, tmm, FF_BLK), lambda j, i, k: (j, i, 0)), (2, nI, 1))
    tbig = min(1024, T)
    gw["w_down"] = mm("down_dw", _TN, [(ff, dy3, _bs((None, tbig, FF_BLK), lambda b, j, k: (b, k, 0)),
                                        _bs((tbig, 1024), lambda b, j, k: (k, j)))],
                      S((2, FF_BLK, D), F32), _bs((None, FF_BLK, 1024), lambda b, j, k: (b, 0, j)), (2, D // 1024, T // tbig))
    dU_out = [Out((2, T, FF_BLK), BF16, FF_BLK, lead=lambda o: o), Out((2, T, FF_BLK), BF16, FF_BLK, lead=lambda o: o)]
    (dUa, dUb), dffn = stage("convglu_bwd", f_convglu, ffn_ins, ffn_pars, ffn_out, outer=2, cts=[dff], dins=dU_out)
    gs["ffn_conv_w"] = jnp.concatenate([dffn[0][:2], dffn[1][2:]], axis=0)
    gs["ffn_conv_b"] = jnp.concatenate([dffn[2][:2], dffn[3][2:]], axis=0)
    dh3 = mm("up_dx", _NT, [(dUa, W["w_up"], _bs((None, tmm, FF_BLK), lambda i, j, k: (k, i, 0)),
                             _bs((None, 1024, FF_BLK), lambda i, j, k: (k, j, 0))),
                            (dUb, W["w_up"], _bs((None, tmm, FF_BLK), lambda i, j, k: (k, i, 0)),
                             _bs((None, 1024, FF_BLK), lambda i, j, k: (k + 2, j, 0)))],
             S((T, D), F32), _bs((tmm, 1024), lambda i, j, k: (i, j)), (nI, D // 1024, 2))

    def up_dw(name, dU):
        tt = min(2048, T)
        return mm(name, _TN, [(h3, dU, _bs((tt, 512), lambda b, i, k: (k, i)), _bs((None, tt, FF_BLK), lambda b, i, k: (b, k, 0)))],
                  S((2, D, FF_BLK), F32), _bs((None, 512, FF_BLK), lambda b, i, k: (b, i, 0)), (2, D // 512, T // tt))

    gw["w_up_a"], gw["w_up_b"] = up_dw("up_dw_a", dUa), up_dw("up_dw_b", dUb)
    hook("ffn", gw)
    (dx1, dy2), dpp2 = stage("post2_bwd", f_post_pre, [Tile(x1, D), Tile(y2, D)], pp2, [o2048(F32), o2048(BF16)],
                             cts=[dx2, dh3], dins=[o2048(F32), o2048(BF16)])
    gs["xa_post_norm"], gs["ffn_pre_norm"] = dpp2

    do2 = dense_t("xo_dx", dy2, W["w_xo"])
    gw["w_xo"] = wgrad("xo_dw", o2, dy2)
    (dq2,), (dk2, dv2) = stage("xattn_bwd", f_xattn, [Tile(q2, D)], xa_pars, [o2048(BF16)], cts=[do2], dins=[o2048(BF16)])
    dh2 = dense_t("xq_dx", dq2, W["w_xq"])
    gw["w_xq"] = wgrad("xq_dw", h2, dq2)
    dkv = jnp.concatenate([dk2, dv2], axis=1).astype(BF16)
    dmn = mm("xkv_dx", _NT, [(dkv, W["w_xkv"], _bs((M, 1024), lambda i, j, k: (0, k)), _bs((None, 512, 1024), lambda i, j, k: (k, j, 0)))],
             S((M, D), F32), _bs((M, 512), lambda i, j, k: (0, j)), (1, D // 512, 4))
    gw["w_xkv"] = mm("xkv_dw", _TN, [(mn, dkv, _bs((M, D), lambda b, j, k: (0, 0)), _bs((M, 1024), lambda b, j, k: (0, b)))],
                     S((4, D, 1024), F32), _bs((None, D, 1024), lambda b, j, k: (b, 0, 0)), (4, 1, 1))
    hook("xattn", gw)
    _, (gs["mem_norm"],) = stage("mem_norm_bwd", f_prenorm, [Tile(mem, D)], [Par(sp["mem_norm"])], [Out(mem.shape, BF16, D)],
                                 tm=M, cts=[dmn], dins=[None])
    (dx0, dy1), dpp1 = stage("post1_bwd", f_post_pre, [Tile(x, D), Tile(y1, D)], pp1, [o2048(F32), o2048(BF16)],
                             cts=[dx1, dh2], dins=[o2048(F32), o2048(BF16)])
    gs["mix_post_norm"], gs["xa_pre_norm"] = dpp1

    dmerged = dense_t("mix_out_dx", dy1, W["w_mix_out"])
    gw["w_mix_out"] = wgrad("mix_out_dw", merged, dy1)
    pshape = (T, N_PROJ)
    (dP2, dya, dyb), _ = stage("merge_bwd", f_merge, merge_ins, [], [o2048(BF16)], cts=[dmerged],
                               dins=[Out(pshape, BF16, 2 * D, cb=C_GATES // (2 * D)), o2048(BF16), o2048(BF16)])
    d_on = dense_t("branch_a_dx", dya, W["w_branch_a"])
    gw["w_branch_a"] = wgrad("branch_a_dw", o_n, dya)
    dpooled = mm("branch_b_dx", _NT, [(dyb, W["w_branch_b"], _bs((tmm, 512), lambda i, j, k: (i, k)),
                                       _bs((None, 1024, 512), lambda i, j, k: (k, 0, 0)))],
                 S((T, 1024), F32), _bs((tmm, 1024), lambda i, j, k: (i, 0)), (nI, 1, 4))
    gw["w_branch_b"] = mm("branch_b_dw", _TN, [(pooled, dyb, _bs((tmm, 1024), lambda b, j, k: (k, 0)), _bs((tmm, 512), lambda b, j, k: (k, b)))],
                          S((4, 1024, 512), F32), _bs((None, 1024, 512), lambda b, j, k: (b, 0, 0)), (4, 1, nI))
    hook("mixer", gw)
    (do, dP2), (gs["gdn_norm"],) = stage("gnorm_bwd", f_gnorm, [Tile(o, D), z_tile], [Par(sp["gdn_norm"])], [o2048(BF16)],
                                         cts=[d_on], dins=[o2048(F32), Out(pshape, BF16, D, cb=C_Z // D, into=dP2)])
    dq, dk, dv, dbg = gdn_backward(qkv[0], qkv[1], qkv[2], bg, s0, ninv, do, GDN_SPAN_BWD)
    dcw = []
    for i, (n, dqq) in enumerate(zip("qkv", (dq, dk, dv))):
        (dP2,), (dc,) = stage("conv_%s_bwd" % n, f_heads[i], [Tile(P2, D, cb=i, halo=True)], [Par(cws[i])], [o2048(F32)],
                              cts=[dqq], dins=[Out(pshape, BF16, D, cb=i, into=dP2)])
        dcw.append(dc)
    gs["conv_qkv"] = jnp.concatenate(dcw, axis=1)
    (dP2,), (gs["a_log"], gs["dt_bias"]) = stage("bg_bwd", f_bg, [ba_tile], [Par(sp["a_log"]), Par(sp["dt_bias"])],
                                                 [Out((T, 128), F32, 128)], cts=[dbg],
                                                 dins=[Out(pshape, BF16, 512, cb=C_BA // 512, into=dP2)])
    (dP2,), (gs["pool_w"], gs["pool_scale"]) = stage("pool_bwd", f_pool, [p_tile], pool_pars, [Out((T, 1024), BF16, 1024)],
                                                     cts=[dpooled], dins=[Out(pshape, BF16, 1024, cb=C_POOL // 1024, into=dP2)])
    tk, ta, tt = 2304, 1152, min(2048, T)
    gw["w_in"] = mm("in_proj_dw", _TN, [(dP2, h1, _bs((tt, ta), lambda i, j, k: (k, i)), _bs((tt, D), lambda i, j, k: (k, 0)))],
                    S((N_PROJ, D), F32), _bs((ta, D), lambda i, j, k: (i, 0)), (N_PROJ // ta, 1, T // tt))
    hook("in_proj", gw)
    dh1 = mm("in_proj_dx", _NN, [(dP2, W["w_in"], _bs((tbig, tk), lambda i, j, k: (i, k)), _bs((tk, D), lambda i, j, k: (k, 0)))],
             S((T, D), F32), _bs((tbig, D), lambda i, j, k: (i, 0)), (T // tbig, 1, N_PROJ // tk))
    (grad_x,), (gs["mix_pre_norm"],) = stage("pre1_bwd", f_prenorm_res, [Tile(x, D)], [Par(sp["mix_pre_norm"])],
                                             [o2048(F32), o2048(BF16)], cts=[dx0, dh1], dins=[o2048(F32)])
    return loss, grad_x, gw, gs


W_IN_COLS = 13344
GROUPED = {"w_branch_b": 512, "w_xkv": 1024}
ROW_SHARDED = ("w_branch_a", "w_mix_out", "w_xq", "w_xo")


def shard_to_slab(name, w):
    if name == "w_in":
        return w.T.astype(BF16)
    if name == "w_up":
        return jnp.pad(w, ((0, 0), (0, FF_BLK - FF_COLS))).astype(BF16)
    return w.astype(BF16)


def slabs_to_weight(name, g):
    if name == "w_in":
        full = g.astype(F32).reshape(W_IN_COLS, D_MODEL)
        pad = jnp.zeros((N_PROJ - W_IN_COLS, D_MODEL), F32)
        return jnp.concatenate([full[0:8192], full[9248:13344], full[8224:9248], full[8192:8224], pad]).astype(BF16)
    if name == "w_down":
        z = jnp.zeros((2, FF_BLK - FF_COLS, D_MODEL), g.dtype)
        return jnp.concatenate([g.reshape(2, FF_COLS, D_MODEL), z], axis=1)
    if name in ROW_SHARDED:
        return g.reshape(D_MODEL, D_MODEL)
    return g


def grad_to_slabs(name, gw):
    if name == "w_in":
        g = gw["w_in"]
        return jnp.concatenate([g[0:8192], g[13312:13344], g[12288:13312], g[8192:12288]]).reshape(4, 3336, D_MODEL)
    if name == "w_up":
        return jnp.concatenate([gw["w_up_a"], gw["w_up_b"]], axis=0)
    if name == "w_down":
        return gw["w_down"][:, :FF_COLS].reshape(4, FF_COLS // 2, D_MODEL)
    if name in ROW_SHARDED:
        return gw[name].reshape(4, D_MODEL // 4, D_MODEL)
    return gw[name]


def slab_to_shard_grad(name, f):
    if name == "w_in":
        return f.T
    if name == "w_up":
        return f[:, :FF_COLS]
    return f


MESH = pl.DeviceIdType.MESH
ANY = pl.BlockSpec(memory_space=pl.ANY)


def _me():
    x, y, c = lax.axis_index("x"), lax.axis_index("y"), lax.axis_index("c")
    return x, y, c, 2 * x + y


def _chip_dev(t, c):
    return (t // 2, t % 2, c)


def _rcopy(src, dst, ssem, rsem, dev):
    return pltpu.make_async_remote_copy(src_ref=src, dst_ref=dst, send_sem=ssem, recv_sem=rsem, device_id=dev, device_id_type=MESH)


def _handshake_all(x, y, c):
    barrier = pltpu.get_barrier_semaphore()
    for dx in (0, 1):
        for dy in (0, 1):
            for dc in (0, 1):
                if dx or dy or dc:
                    pl.semaphore_signal(barrier, inc=1, device_id=((x + dx) % 2, (y + dy) % 2, (c + dc) % 2), device_id_type=MESH)
    pl.semaphore_wait(barrier, 7)


def _comm_call(body, name, operands, out_shape, sems, collective_id):
    if collective_id is not None:
        return pl.kernel(body, out_type=out_shape, mesh=plsc.ScalarSubcoreMesh(axis_name="seq", num_cores=1), name=name,
                         scratch_types=sems, compiler_params=pltpu.CompilerParams(collective_id=collective_id))(*operands)
    n_in, n_out = len(operands), len(out_shape)
    return pl.pallas_call(body, name=name, in_specs=[ANY] * n_in, out_specs=[ANY] * n_out, out_shape=out_shape,
                          scratch_shapes=sems)(*operands)


def gather_weights(slabs, name="gather_weights", collective_id=None):
    n = len(slabs)

    def body(*refs):
        src, dst = refs[:n], refs[n:2 * n]
        ici_s, ici_r, fwd_s, fwd_r = refs[2 * n:]
        x, y, c, s = _me()
        if collective_id is not None:
            _handshake_all(x, y, c)
        sender = c == s // 2
        sib = (x, y, 1 - c)
        for r in (1, 2, 3):
            @pl.when(sender)
            def _(r=r):
                for w in range(n):
                    _rcopy(src[w], dst[w].at[s], ici_s.at[w, r - 1], ici_r.at[w, r - 1], _chip_dev(s ^ r, c)).start()
        for r in (1, 2, 3):
            t = s ^ r
            here = c == t // 2

            @pl.when(here)
            def _(r=r, t=t):
                for w in range(n):
                    _rcopy(src[w], dst[w].at[t], ici_s.at[w, r - 1], ici_r.at[w, r - 1], sib).wait_recv()
                    _rcopy(dst[w].at[t], dst[w].at[t], fwd_s.at[w, r - 1], fwd_r.at[w, r - 1], sib).start()

            @pl.when(jnp.logical_not(here))
            def _(r=r, t=t):
                for w in range(n):
                    _rcopy(dst[w].at[t], dst[w].at[t], fwd_s.at[w, r - 1], fwd_r.at[w, r - 1], sib).wait_recv()
        for r in (1, 2, 3):
            t = s ^ r

            @pl.when(sender)
            def _(r=r):
                for w in range(n):
                    _rcopy(src[w], dst[w].at[s], ici_s.at[w, r - 1], ici_r.at[w, r - 1], sib).wait_send()

            @pl.when(c == t // 2)
            def _(r=r, t=t):
                for w in range(n):
                    _rcopy(dst[w].at[t], dst[w].at[t], fwd_s.at[w, r - 1], fwd_r.at[w, r - 1], sib).wait_send()

    out_shape = [jax.ShapeDtypeStruct((4,) + a.shape, a.dtype) for a in slabs]
    sems = [pltpu.SemaphoreType.DMA((n, 3))] * 4
    return _comm_call(body, name, slabs, out_shape, sems, collective_id)


def pair_exchange(g4, name="pair_exchange", collective_id=None):
    n = len(g4)

    def body(*refs):
        src, dst = refs[:n], refs[n:2 * n]
        ssem, rsem = refs[2 * n:]
        x, y, c, s = _me()
        if collective_id is not None:
            _handshake_all(x, y, c)
        cps = [_rcopy(src[w].at[pl.ds(2 * (1 - c), 2)], dst[w], ssem.at[w], rsem.at[w], (x, y, 1 - c)) for w in range(n)]
        for cp in cps:
            cp.start()
        for cp in cps:
            cp.wait()

    return _comm_call(body, name, g4, [jax.ShapeDtypeStruct((2,) + a.shape[1:], a.dtype) for a in g4],
                      [pltpu.SemaphoreType.DMA((n,))] * 2, collective_id)


def _col_tile(R, C):
    for tc in (512, 256, 128):
        if C % tc == 0 and R * tc * 4 <= (4 << 20):
            return tc
    return 128


def pair_add(name, g4, gsib, c):
    _, R, C = g4.shape
    tc = _col_tile(R, C)

    def body(c_ref, a_ref, b_ref, of_ref, ob_ref):
        v = a_ref[...] + b_ref[...]
        of_ref[...] = v
        ob_ref[...] = v.astype(BF16)

    blk = lambda f: pl.BlockSpec((None, R, tc), f)
    return _pcall(
        body, (c.reshape(1).astype(jnp.int32), g4, gsib), name=name, num_scalar_prefetch=1, grid=(2, C // tc),
        in_specs=[blk(lambda p, j, cr: (2 * cr[0] + p, 0, j)), blk(lambda p, j, cr: (p, 0, j))],
        out_specs=[blk(lambda p, j, cr: (p, 0, j)), blk(lambda p, j, cr: (p, 0, j))],
        out_shape=[jax.ShapeDtypeStruct((2, R, C), F32), jax.ShapeDtypeStruct((2, R, C), BF16)],
        compiler_params=_cparams(("arbitrary", "arbitrary")),
    )


def scatter_partials(rb, name="scatter_partials", collective_id=None):
    n = len(rb)

    def body(*refs):
        src, dst = refs[:n], refs[n:2 * n]
        ssem, rsem = refs[2 * n:]
        x, y, c, s = _me()
        if collective_id is not None:
            _handshake_all(x, y, c)
        for r in (1, 2, 3):
            t = s ^ r

            @pl.when(t // 2 == c)
            def _(r=r, t=t):
                for w in range(n):
                    _rcopy(src[w].at[t % 2], dst[w].at[s], ssem.at[w, r - 1], rsem.at[w, r - 1], _chip_dev(t, c)).start()
        for r in (1, 2, 3):
            t = s ^ r

            @pl.when(s // 2 == c)
            def _(r=r, t=t):
                for w in range(n):
                    _rcopy(src[w].at[0], dst[w].at[t], ssem.at[w, r - 1], rsem.at[w, r - 1], _chip_dev(t, c)).wait_recv()
        for r in (1, 2, 3):
            t = s ^ r

            @pl.when(t // 2 == c)
            def _(r=r, t=t):
                for w in range(n):
                    _rcopy(src[w].at[t % 2], dst[w].at[s], ssem.at[w, r - 1], rsem.at[w, r - 1], _chip_dev(t, c)).wait_send()

    return _comm_call(body, name, rb, [jax.ShapeDtypeStruct((4,) + a.shape[1:], a.dtype) for a in rb],
                      [pltpu.SemaphoreType.DMA((n, 3))] * 2, collective_id)


def final_sum(name, rf, recv, s):
    _, R, C = rf.shape
    tc = _col_tile(R, C)

    def body(s_ref, own_ref, r0_ref, r1_ref, r2_ref, o_ref):
        o_ref[...] = ((own_ref[...] + r0_ref[...].astype(F32)) + r1_ref[...].astype(F32)) + r2_ref[...].astype(F32)

    blk = lambda f: pl.BlockSpec((None, R, tc), f)
    other = lambda k: (lambda j, sr: (k + (k >= sr[0]).astype(jnp.int32), 0, j))
    return _pcall(
        body, (s.reshape(1).astype(jnp.int32), rf, recv, recv, recv), name=name, num_scalar_prefetch=1, grid=(C // tc,),
        in_specs=[blk(lambda j, sr: (sr[0] % 2, 0, j)), blk(other(0)), blk(other(1)), blk(other(2))],
        out_specs=pl.BlockSpec((R, tc), lambda j, sr: (0, j)), out_shape=jax.ShapeDtypeStruct((R, C), F32),
        compiler_params=_cparams(("arbitrary",)),
    )


def share_with_sibling(fs, name="share_with_sibling", collective_id=None):
    n = len(fs)

    def body(*refs):
        src, dst = refs[:n], refs[n:2 * n]
        ssem, rsem = refs[2 * n:]
        x, y, c, s = _me()
        if collective_id is not None:
            _handshake_all(x, y, c)
        sib = (x, y, 1 - c)

        @pl.when(s // 2 == c)
        def _():
            cps = [_rcopy(src[w], dst[w], ssem.at[w], rsem.at[w], sib) for w in range(n)]
            for cp in cps:
                cp.start()
            for cp in cps:
                cp.wait_send()

        @pl.when(s // 2 != c)
        def _():
            for w in range(n):
                _rcopy(src[w], dst[w], ssem.at[w], rsem.at[w], sib).wait_recv()

    return _comm_call(body, name, fs, [jax.ShapeDtypeStruct(a.shape, a.dtype) for a in fs],
                      [pltpu.SemaphoreType.DMA((n,))] * 2, collective_id)


def pick(name, mine, a, b):
    R, C = a.shape
    tc = _col_tile(R, C)

    def body(m_ref, a_ref, b_ref, o_ref):
        o_ref[...] = jnp.where(m_ref[0] != 0, a_ref[...], b_ref[...])

    blk = pl.BlockSpec((R, tc), lambda j, mr: (0, j))
    return _pcall(body, (mine.reshape(1).astype(jnp.int32), a, b), name=name, num_scalar_prefetch=1, grid=(C // tc,),
                  in_specs=[blk, blk], out_specs=blk, out_shape=jax.ShapeDtypeStruct((R, C), a.dtype),
                  compiler_params=_cparams(("arbitrary",)))


def allgather_rows(v):
    m_per, ncol = v.shape

    def body(x_ref, out_ref, send_sems, recv_sems, local_sem):
        x, y, c = lax.axis_index("x"), lax.axis_index("y"), lax.axis_index("c")
        me, sibling = (x, y, c), (x, y, 1 - c)
        chips = [(1 - x, y), (x, 1 - y), (1 - x, 1 - y)]

        def rows(px, py, pc):
            return out_ref.at[pl.ds((4 * px + 2 * py + pc) * m_per, m_per), :]

        def copy(k, block, to, src=None):
            return _rcopy(rows(*block) if src is None else src, rows(*block), send_sems.at[k], recv_sems.at[k], to)

        mine = pltpu.make_async_copy(x_ref, rows(*me), local_sem)
        mine.start()
        first = [copy(0, me, sibling, src=x_ref)]
        first += [copy(1 + j, me, (*chip, c), src=x_ref) for j, chip in enumerate(chips)]
        for cp in first:
            cp.start()
        passed = [copy(4 + j, (*chip, c), sibling) for j, chip in enumerate(chips)]
        for j, chip in enumerate(chips):
            copy(1 + j, (*chip, c), me).wait_recv()
            passed[j].start()
        copy(0, sibling, me).wait_recv()
        for j, chip in enumerate(chips):
            copy(4 + j, (*chip, 1 - c), me).wait_recv()
        for cp in first + passed:
            cp.wait_send()
        mine.wait()

    return pl.pallas_call(
        body, name="allgather_rows", out_shape=jax.ShapeDtypeStruct((8 * m_per, ncol), v.dtype),
        in_specs=[pl.BlockSpec(memory_space=pltpu.VMEM)], out_specs=pl.BlockSpec(memory_space=pltpu.VMEM),
        scratch_shapes=[pltpu.SemaphoreType.DMA((7,)), pltpu.SemaphoreType.DMA((7,)), pltpu.SemaphoreType.DMA],
        compiler_params=pltpu.CompilerParams(vmem_limit_bytes=V7X_VMEM_LIMIT),
    )(v)


def allgather_rows_seq(v, name, collective_id):
    m_per, ncol = v.shape

    def body(x_ref, out_ref, send_sems, recv_sems):
        x, y, c = lax.axis_index("x"), lax.axis_index("y"), lax.axis_index("c")
        _handshake_all(x, y, c)
        me, sibling = (x, y, c), (x, y, 1 - c)
        chips = [(1 - x, y), (x, 1 - y), (1 - x, 1 - y)]

        def rows(px, py, pc):
            return out_ref.at[pl.ds((4 * px + 2 * py + pc) * m_per, m_per), :]

        def copy(k, block, to, src=None):
            return _rcopy(rows(*block) if src is None else src, rows(*block), send_sems.at[k], recv_sems.at[k], to)

        first = [copy(0, me, sibling, src=x_ref)]
        first += [copy(1 + j, me, (*chip, c), src=x_ref) for j, chip in enumerate(chips)]
        for cp in first:
            cp.start()
        passed = [copy(4 + j, (*chip, c), sibling) for j, chip in enumerate(chips)]
        for j, chip in enumerate(chips):
            copy(1 + j, (*chip, c), me).wait_recv()
            passed[j].start()
        copy(0, sibling, me).wait_recv()
        for j, chip in enumerate(chips):
            copy(4 + j, (*chip, 1 - c), me).wait_recv()
        for cp in first + passed:
            cp.wait_send()

    return _comm_call(body, name, [v], [jax.ShapeDtypeStruct((8 * m_per, ncol), v.dtype)],
                      [pltpu.SemaphoreType.DMA((7,)), pltpu.SemaphoreType.DMA((7,))], collective_id)[0]


def sum_blocks(name, a, nblk):
    m = a.shape[0] // nblk

    def body(a_ref, o_ref):
        acc = a_ref[pl.ds(0, m), :]
        for b in range(1, nblk):
            acc = acc + a_ref[pl.ds(b * m, m), :]
        o_ref[...] = acc

    return pl.pallas_call(body, name=name, out_shape=jax.ShapeDtypeStruct((m, a.shape[1]), a.dtype),
                          compiler_params=pltpu.CompilerParams(vmem_limit_bytes=V7X_VMEM_LIMIT))(a)


BIG = ("w_in", "w_branch_a", "w_branch_b", "w_mix_out", "w_xq", "w_xkv", "w_xo", "w_up", "w_down")
GAINS = ("mix_pre_norm", "gdn_norm", "pool_scale", "mix_post_norm", "xa_pre_norm", "mem_norm", "xa_post_norm",
         "ffn_pre_norm", "ffn_post_norm")
WEIGHTS = ("mix_pre_norm", "w_in", "conv_qkv", "a_log", "dt_bias", "gdn_norm", "pool_w", "pool_scale", "w_branch_a",
           "w_branch_b", "w_mix_out", "mix_post_norm", "xa_pre_norm", "mem_norm", "w_xq", "w_xkv", "w_xo", "xa_post_norm",
           "ffn_pre_norm", "w_up", "ffn_conv_w", "ffn_conv_b", "w_down", "ffn_post_norm")


RS_GROUPS = {"ffn": (("w_up", "w_down"), (3, 4, 5)), "xattn": (("w_xo", "w_xq", "w_xkv"), (6, 7, 8)),
             "mixer": (("w_mix_out", "w_branch_a", "w_branch_b"), (9, 10, 11)), "in_proj": (("w_in",), (12, 13, 14))}


def _rows128(vecs):
    flat = jnp.concatenate([v.reshape(-1) for v in vecs])
    m = -(-flat.shape[0] // 1024) * 8
    return jnp.pad(flat, (0, m * 128 - flat.shape[0])).reshape(m, 128)


def _unrows(a, shapes):
    flat, out, pos = a.reshape(-1), [], 0
    for sh in shapes:
        n = 1
        for d in sh:
            n *= d
        out.append(flat[pos:pos + n].reshape(sh))
        pos += n
    return out


def _lane128(v):
    return jnp.pad(v.reshape(1, GDN_HEADS), ((0, 0), (GDN_HEADS, 128 - 2 * GDN_HEADS)))


def kernel(x, mem, mix_pre_norm, w_in, conv_qkv, a_log, dt_bias, gdn_norm, pool_w, pool_scale, w_branch_a, w_branch_b, w_mix_out, mix_post_norm, xa_pre_norm, mem_norm, w_xq, w_xkv, w_xo, xa_post_norm, ffn_pre_norm, w_up, ffn_conv_w, ffn_conv_b, w_down, ffn_post_norm, loss_target, m_mix_pre_norm, m_w_in, m_conv_qkv, m_a_log, m_dt_bias, m_gdn_norm, m_pool_w, m_pool_scale, m_w_branch_a, m_w_branch_b, m_w_mix_out, m_mix_post_norm, m_xa_pre_norm, m_mem_norm, m_w_xq, m_w_xkv, m_w_xo, m_xa_post_norm, m_ffn_pre_norm, m_w_up, m_ffn_conv_w, m_ffn_conv_b, m_w_down, m_ffn_post_norm, v_mix_pre_norm, v_w_in, v_conv_qkv, v_a_log, v_dt_bias, v_gdn_norm, v_pool_w, v_pool_scale, v_w_branch_a, v_w_branch_b, v_w_mix_out, v_mix_post_norm, v_xa_pre_norm, v_mem_norm, v_w_xq, v_w_xkv, v_w_xo, v_xa_post_norm, v_ffn_pre_norm, v_w_up, v_ffn_conv_w, v_ffn_conv_b, v_w_down, v_ffn_post_norm):
    given = dict(locals())
    _LAST[0] = None
    w = {n: given[n][0] for n in WEIGHTS}
    cx, cy, cc = lax.axis_index("x"), lax.axis_index("y"), lax.axis_index("c")
    chip = 2 * cx + cy

    slabs = [shard_to_slab(n, w[n]) for n in BIG]
    first = gather_weights(slabs[:1], name="gather_w_in", collective_id=1)
    rest = gather_weights(slabs[1:], name="gather_rest", collective_id=2)
    stacks = [lax.dynamic_update_index_in_dim(g, sl, chip, 0) for g, sl in zip(list(first) + list(rest), slabs)]
    W = {n: slabs_to_weight(n, g) for n, g in zip(BIG, stacks)}
    sharded_small = (w["conv_qkv"], w["ffn_conv_w"], w["pool_w"])
    allv = allgather_rows(_rows128(sharded_small))
    per_chip = allv.reshape(8, -1)[0::2]
    parts = [_unrows(per_chip[t], [a.shape for a in sharded_small]) for t in range(4)]
    sp = {n: w[n].reshape(1, -1) for n in GAINS}
    sp["a_log"], sp["dt_bias"] = _lane128(w["a_log"]), _lane128(w["dt_bias"])
    sp["conv_qkv"] = jnp.concatenate([p[0] for p in parts], axis=1)
    sp["ffn_conv_w"] = jnp.pad(jnp.stack([p[1] for p in parts]), ((0, 0), (0, 0), (0, FF_BLK - FF_COLS)))
    sp["pool_w"] = jnp.concatenate([p[2] for p in parts], axis=1).reshape(4 * 256, 256)
    sp["ffn_conv_b"] = jnp.pad(w["ffn_conv_b"].reshape(4, 1, FF_COLS), ((0, 0), (0, 0), (0, FF_BLK - FF_COLS)))

    grads, summed, pending, todo = {}, {}, {}, []

    def rs_steps(names, ids, gw):
        g4 = [grad_to_slabs(n, gw) for n in names]
        tag = names[0]
        gsib = pair_exchange(g4, "pair_exchange_" + tag, ids[0])
        yield
        sums = [pair_add("pair_add_" + n, a, b, cc) for n, a, b in zip(names, g4, gsib)]
        recv = scatter_partials([sb for _, sb in sums], "scatter_partials_" + tag, ids[1])
        yield
        fin = [final_sum("final_sum_" + n, sf, rv, chip) for n, (sf, _), rv in zip(names, sums, recv)]
        got = share_with_sibling(fin, "share_" + tag, ids[2])
        yield
        for n, f, g in zip(names, fin, got):
            if n in ("w_in", "w_up"):
                summed[n] = pick("pick_" + n, (chip // 2 == cc), f, g)
                grads[n] = slab_to_shard_grad(n, summed[n])
            else:
                pending[n] = (f, g)

    def advance():
        for it in list(todo):
            if next(it, "done") == "done":
                todo.remove(it)

    def hook(event, gw):
        new = rs_steps(*RS_GROUPS[event], gw)
        next(new)
        advance()
        todo.append(new)
        if event == "in_proj":
            next(new)

    loss, grad_x, gw, gs = local_step(x[0], mem[0], loss_target[0], W, sp, hook)

    delta, new_m, new_v = {}, {}, {}

    def update(names):
        for n in names:
            shp = w[n].shape
            if n in pending:
                *res, grads[n] = adamw_pick("adamw_" + n, (chip // 2 == cc), given[n], *pending.pop(n), given["m_" + n], given["v_" + n])
            elif n in ("w_in", "w_up"):
                gt = summed[n] if n == "w_in" else grads[n].T
                res = adamw("adamw_" + n, w[n].T, gt, given["m_" + n][0].T, given["v_" + n][0].T)
                res = [a.T for a in res]
            elif len(shp) == 2:
                res = adamw("adamw_" + n, given[n], grads[n], given["m_" + n], given["v_" + n])
            else:
                two = (lambda a: a.reshape(-1, shp[-1])) if len(shp) > 1 else (lambda a: a.reshape(1, -1))
                res = adamw("adamw_" + n, two(w[n]), two(grads[n]), two(given["m_" + n][0]), two(given["v_" + n][0]))
            delta[n], new_m[n], new_v[n] = (a.reshape((1,) + shp) for a in res)

    advance()
    small_names = GAINS + ("a_log", "dt_bias", "ffn_conv_b", "conv_qkv", "ffn_conv_w", "pool_w")
    vec = _rows128([gs[n] for n in small_names])
    allv = allgather_rows_seq(vec, "allgather_small_grads", 15)
    loss = lax.psum(loss[0, 0], ("x", "y", "c"))
    update([n for n in WEIGHTS if n in grads or n in pending])
    allv = lax.dynamic_update_slice_in_dim(allv, vec, (2 * chip + cc) * vec.shape[0], axis=0)
    total = sum_blocks("sum_small", allv, 8)
    tot = dict(zip(small_names, _unrows(total, [gs[n].shape for n in small_names])))
    for n in GAINS:
        grads[n] = tot[n].reshape(-1)
    grads["a_log"] = tot["a_log"][0, GDN_HEADS:2 * GDN_HEADS]
    grads["dt_bias"] = tot["dt_bias"][0, GDN_HEADS:2 * GDN_HEADS]
    grads["ffn_conv_b"] = tot["ffn_conv_b"][:, 0, :FF_COLS].reshape(-1)
    grads["conv_qkv"] = lax.dynamic_slice_in_dim(tot["conv_qkv"], chip * 1536, 1536, axis=1)
    grads["ffn_conv_w"] = lax.dynamic_index_in_dim(tot["ffn_conv_w"], chip, axis=0, keepdims=False)[:, :FF_COLS]
    grads["pool_w"] = lax.dynamic_slice_in_dim(tot["pool_w"].reshape(4, 256, 256), chip * 64, 64, axis=1)

    update([n for n in WEIGHTS if n in grads and n not in delta])
    while todo:
        advance()
    update([n for n in WEIGHTS if n not in delta])
    out_g = [grads[n].reshape((1,) + w[n].shape) for n in WEIGHTS]
    return (loss, grad_x[None], *out_g, *[delta[n] for n in WEIGHTS], *[new_m[n] for n in WEIGHTS], *[new_v[n] for n in WEIGHTS])
```
